```python
import math
import jax
import jax.numpy as jnp
from jax import lax
import numpy as np


D_MODEL = 2048
BATCH = 1
SEQ = 16384
DEPTH = 1

RWKV_HEADS = 16
RWKV_HEAD_DIM = 64
RWKV_WIDTH = RWKV_HEADS * RWKV_HEAD_DIM
DECAY_LORA = 96
ICLR_LORA = 96
GATE_LORA = 256
DIFF_HEADS = 8
DIFF_HEAD_DIM = 64
DIFF_V_DIM = 2 * DIFF_HEAD_DIM
DIFF_QK_COLS = DIFF_HEADS * 2 * DIFF_HEAD_DIM
DIFF_WIDTH = DIFF_HEADS * DIFF_V_DIM
Q_BLOCK = 128
N_BRANCHES = 2
RWKV_COLS = 3 * RWKV_WIDTH + DECAY_LORA + ICLR_LORA + GATE_LORA
DIFF_COLS = 2 * DIFF_QK_COLS + DIFF_WIDTH
GATE_COLS = N_BRANCHES * D_MODEL
IN_COLS = RWKV_COLS + DIFF_COLS + GATE_COLS
PEER_HEADS = 8
PEER_QUERY_DIM = 256
PEER_HALF = PEER_QUERY_DIM // 2
N_KEYS = 128
N_EXPERTS = N_KEYS * N_KEYS
PEER_TOPK = 16
PEER_CHUNK = 128
NORM_EPS = 1e-6
LN_X_EPS = 64e-5
SUBLN_EPS = 1e-5
NEG_INF = -1e30

kernel_name = 'hybrid_rwkv7_diffattn_peer'


def rmsnorm(x, g, eps=NORM_EPS):
    x32 = x.astype(jnp.float32)
    y = x32 * lax.rsqrt(jnp.mean(x32 * x32, axis=-1, keepdims=True) + eps)
    return (y * g.astype(jnp.float32)).astype(x.dtype)


def lambda_init_fn(layer):
    return 0.8 - 0.6 * math.exp(-0.3 * layer)


def wkv7_scan(r, decay, k, v, kk, a):
    B, S, H, N = r.shape

    def step(state, inp):
        r_t, w_t, k_t, v_t, kk_t, a_t = inp
        sa = jnp.einsum('bhvk,bhk->bhv', state, -kk_t)
        state = (state * w_t[:, :, None, :]
                 + sa[..., None] * (kk_t * a_t)[:, :, None, :]
                 + v_t[..., None] * k_t[:, :, None, :])
        y_t = jnp.einsum('bhvk,bhk->bhv', state, r_t)
        return state, y_t

    xs = (jnp.moveaxis(r, 1, 0), jnp.moveaxis(decay, 1, 0), jnp.moveaxis(k, 1, 0),
          jnp.moveaxis(v, 1, 0), jnp.moveaxis(kk, 1, 0), jnp.moveaxis(a, 1, 0))
    state0 = jnp.zeros((B, H, N, N), jnp.float32)
    _, ys = lax.scan(step, state0, xs)
    return jnp.moveaxis(ys, 0, 1)


def rwkv7_time_mix(p, mu, w0, w_decay_up, a0, w_iclr_up, w_gate_up, k_k, k_a, r_k, lnx_g, lnx_b):
    B, S, _ = p.shape
    H, N = RWKV_HEADS, RWKV_HEAD_DIM
    prev = jnp.pad(p, ((0, 0), (1, 0), (0, 0)))[:, :-1]
    p = p + (prev - p) * mu
    splits = [RWKV_WIDTH, 2 * RWKV_WIDTH, 3 * RWKV_WIDTH,
              3 * RWKV_WIDTH + DECAY_LORA, 3 * RWKV_WIDTH + DECAY_LORA + ICLR_LORA]
    r, k, v, xw, xa, xg = jnp.split(p, splits, axis=-1)
    w_log = -jax.nn.softplus(-(w0 + jnp.tanh(xw) @ w_decay_up)) - 0.5
    decay = jnp.exp(-jnp.exp(w_log.astype(jnp.float32)))
    a = jax.nn.sigmoid(a0 + xa @ w_iclr_up)
    g = jax.nn.sigmoid(xg) @ w_gate_up
    kk = (k * k_k).reshape(B, S, H, N).astype(jnp.float32)
    kk = kk / jnp.maximum(jnp.sqrt(jnp.sum(kk * kk, axis=-1, keepdims=True)), 1e-12)
    k = k * (1.0 + (a - 1.0) * k_a)
    r_h = r.reshape(B, S, H, N).astype(jnp.float32)
    k_h = k.reshape(B, S, H, N).astype(jnp.float32)
    v_h = v.reshape(B, S, H, N).astype(jnp.float32)
    a_h = a.reshape(B, S, H, N).astype(jnp.float32)
    y = wkv7_scan(r_h, decay.reshape(B, S, H, N), k_h, v_h, kk, a_h)
    mean = jnp.mean(y, axis=-1, keepdims=True)
    var = jnp.mean(jnp.square(y - mean), axis=-1, keepdims=True)
    y = ((y - mean) * lax.rsqrt(var + LN_X_EPS)).reshape(B, S, RWKV_WIDTH)
    y = y * lnx_g.astype(jnp.float32) + lnx_b.astype(jnp.float32)
    bonus = jnp.sum(r_h * k_h * r_k.astype(jnp.float32), axis=-1, keepdims=True) * v_h
    out = (y + bonus.reshape(B, S, RWKV_WIDTH)) * g.astype(jnp.float32)
    return out.astype(p.dtype)


def diff_attention(p, lam_q1, lam_k1, lam_q2, lam_k2, subln_g, lambda_init):
    B, S, _ = p.shape
    H, d = DIFF_HEADS, DIFF_HEAD_DIM
    q, k, v = jnp.split(p, [DIFF_QK_COLS, 2 * DIFF_QK_COLS], axis=-1)
    q = q.reshape(B, S, H, 2, d).transpose(3, 0, 2, 1, 4)
    k = k.reshape(B, S, H, 2, d).transpose(3, 0, 2, 1, 4)
    v = v.reshape(B, S, H, 2 * d).transpose(0, 2, 1, 3)
    lam = (jnp.exp(jnp.sum(lam_q1.astype(jnp.float32) * lam_k1.astype(jnp.float32)))
           - jnp.exp(jnp.sum(lam_q2.astype(jnp.float32) * lam_k2.astype(jnp.float32)))
           + lambda_init)
    scale = d ** -0.5
    n_blocks = S // Q_BLOCK
    q_blocks = q.reshape(2, B, H, n_blocks, Q_BLOCK, d).transpose(3, 0, 1, 2, 4, 5)
    key_pos = jnp.arange(S)

    def attend(args):
        q_blk, blk = args
        q_pos = blk * Q_BLOCK + jnp.arange(Q_BLOCK)
        causal = key_pos[None, :] <= q_pos[:, None]
        s = jnp.einsum('mbhqd,mbhkd->mbhqk', q_blk, k).astype(jnp.float32) * scale
        prob = jax.nn.softmax(jnp.where(causal, s, NEG_INF), axis=-1)
        attn = prob[0] - lam * prob[1]
        return jnp.einsum('bhqk,bhkd->bhqd', attn.astype(v.dtype), v)

    o = lax.map(attend, (q_blocks, jnp.arange(n_blocks)))
    o = o.transpose(1, 0, 3, 2, 4).reshape(B, S, H, 2 * d)
    o = rmsnorm(o, subln_g, SUBLN_EPS) * (1.0 - lambda_init)
    return o.reshape(B, S, DIFF_WIDTH)


def peer_ffn(h, wq, sub_keys, u_tab, v_tab):
    B, S, D = h.shape
    n_chunks = (B * S) // PEER_CHUNK
    xc = h.reshape(n_chunks, PEER_CHUNK, D)

    def chunk(xb):
        q = (xb @ wq).reshape(PEER_CHUNK, PEER_HEADS, 2, PEER_HALF)
        s = jnp.einsum('chpd,hpnd->chpn', q, sub_keys).astype(jnp.float32)
        top_s, top_i = lax.top_k(s, PEER_TOPK)
        cand_s = top_s[:, :, 0, :, None] + top_s[:, :, 1, None, :]
        cand_i = top_i[:, :, 0, :, None] * N_KEYS + top_i[:, :, 1, None, :]
        cand_s = cand_s.reshape(PEER_CHUNK, PEER_HEADS, PEER_TOPK * PEER_TOPK)
        cand_i = cand_i.reshape(PEER_CHUNK, PEER_HEADS, PEER_TOPK * PEER_TOPK)
        best_s, pos = lax.top_k(cand_s, PEER_TOPK)
        idx = jnp.take_along_axis(cand_i, pos, axis=-1)
        gate = jax.nn.softmax(best_s, axis=-1)
        u = jnp.take(u_tab, idx, axis=0)
        act = jax.nn.gelu(jnp.einsum('cd,chkd->chk', xb, u), approximate=False)
        vv = jnp.take(v_tab, idx, axis=0)
        return jnp.einsum('chk,chkd->cd', (gate * act.astype(jnp.float32)).astype(vv.dtype), vv)

    return lax.map(chunk, xc).reshape(B, S, D)


def setup_inputs(seed: int = 0) -> dict:
    key = jax.random.key(seed)
    ks = jax.random.split(key, 32)
    f32 = jnp.float32
    L = DEPTH

    def nrm(k, shape, scale):
        return jax.random.normal(k, shape, f32) * scale

    return {
        'x': nrm(ks[0], (BATCH, SEQ, D_MODEL), 1.0),
        'norm1_g': 1.0 + nrm(ks[1], (L, D_MODEL), 0.02),
        'w_in': nrm(ks[2], (L, D_MODEL, IN_COLS), D_MODEL ** -0.5),
        'shift_mu': jax.random.uniform(ks[3], (L, RWKV_COLS), f32),
        'rwkv_w0': jax.random.uniform(ks[4], (L, RWKV_WIDTH), f32, -5.5, -0.5),
        'w_decay_up': nrm(ks[5], (L, DECAY_LORA, RWKV_WIDTH), 0.5 * DECAY_LORA ** -0.5),
        'rwkv_a0': nrm(ks[6], (L, RWKV_WIDTH), 0.1),
        'w_iclr_up': nrm(ks[7], (L, ICLR_LORA, RWKV_WIDTH), ICLR_LORA ** -0.5),
        'w_gate_up': nrm(ks[8], (L, GATE_LORA, RWKV_WIDTH), GATE_LORA ** -0.5),
        'k_k': 0.85 + nrm(ks[9], (L, RWKV_WIDTH), 0.05),
        'k_a': 1.0 + nrm(ks[10], (L, RWKV_WIDTH), 0.05),
        'r_k': nrm(ks[11], (L, RWKV_HEADS, RWKV_HEAD_DIM), 0.1),
        'lnx_g': 1.0 + nrm(ks[12], (L, RWKV_WIDTH), 0.02),
        'lnx_b': nrm(ks[13], (L, RWKV_WIDTH), 0.02),
        'lam_q1': nrm(ks[14], (L, DIFF_HEAD_DIM), 0.1),
        'lam_k1': nrm(ks[15], (L, DIFF_HEAD_DIM), 0.1),
        'lam_q2': nrm(ks[16], (L, DIFF_HEAD_DIM), 0.1),
        'lam_k2': nrm(ks[17], (L, DIFF_HEAD_DIM), 0.1),
        'subln_g': 1.0 + nrm(ks[18], (L, DIFF_V_DIM), 0.02),
        'w_proj_a': nrm(ks[19], (L, RWKV_WIDTH, D_MODEL), RWKV_WIDTH ** -0.5),
        'w_proj_b': nrm(ks[20], (L, DIFF_WIDTH, D_MODEL), DIFF_WIDTH ** -0.5),
        'w_out': nrm(ks[21], (L, D_MODEL, D_MODEL), D_MODEL ** -0.5),
        'norm2_g': 1.0 + nrm(ks[22], (L, D_MODEL), 0.02),
        'peer_wq': nrm(ks[23], (L, D_MODEL, PEER_HEADS * PEER_QUERY_DIM), D_MODEL ** -0.5),
        'peer_sub_keys': nrm(ks[24], (L, PEER_HEADS, 2, N_KEYS, PEER_HALF), PEER_HALF ** -0.5),
        'peer_u': nrm(ks[25], (L, N_EXPERTS, D_MODEL), D_MODEL ** -0.5),
        'peer_v': nrm(ks[26], (L, N_EXPERTS, D_MODEL), PEER_HEADS ** -0.5),
        'final_g': 1.0 + nrm(ks[27], (D_MODEL,), 0.02),
    }


def reference(x, norm1_g, w_in, shift_mu, rwkv_w0, w_decay_up, rwkv_a0, w_iclr_up, w_gate_up,
              k_k, k_a, r_k, lnx_g, lnx_b, lam_q1, lam_k1, lam_q2, lam_k2, subln_g,
              w_proj_a, w_proj_b, w_out, norm2_g, peer_wq, peer_sub_keys, peer_u, peer_v,
              final_g):
    B, S, D = x.shape
    h = x
    for l in range(DEPTH):
        xn = rmsnorm(h, norm1_g[l])
        proj = xn @ w_in[l]
        p_rwkv, p_diff, p_gate = jnp.split(proj, [RWKV_COLS, RWKV_COLS + DIFF_COLS], axis=-1)
        y_a = rwkv7_time_mix(p_rwkv, shift_mu[l], rwkv_w0[l], w_decay_up[l], rwkv_a0[l],
                             w_iclr_up[l], w_gate_up[l], k_k[l], k_a[l], r_k[l],
                             lnx_g[l], lnx_b[l])
        y_b = diff_attention(p_diff, lam_q1[l], lam_k1[l], lam_q2[l], lam_k2[l], subln_g[l],
                             lambda_init_fn(l))
        gates = jax.nn.sigmoid(p_gate).reshape(B, S, N_BRANCHES, D)
        merged = gates[:, :, 0, :] * (y_a @ w_proj_a[l]) + gates[:, :, 1, :] * (y_b @ w_proj_b[l])
        h = h + merged @ w_out[l]
        h = h + peer_ffn(rmsnorm(h, norm2_g[l]), peer_wq[l], peer_sub_keys[l],
                         peer_u[l], peer_v[l])
    return rmsnorm(h, final_g)
```

```python
import functools
import math

import jax
import jax.numpy as jnp
from jax import lax
from jax.experimental import pallas as pl
from jax.experimental.pallas import tpu as pltpu

F32 = jnp.float32
BF16 = jnp.bfloat16
HIGHEST = lax.Precision.HIGHEST

LANES = 128
SUBLANES = 8

D_MODEL = 2048
RWKV_HEADS = 16
RWKV_HEAD_DIM = 64
RWKV_WIDTH = RWKV_HEADS * RWKV_HEAD_DIM
DECAY_LORA = 96
ICLR_LORA = 96
GATE_LORA = 256
LORA_PAD = 128
RWKV_COLS = 3 * RWKV_WIDTH + DECAY_LORA + ICLR_LORA + GATE_LORA
RWKV_COLS_PAD = 3 * RWKV_WIDTH + 2 * LORA_PAD + GATE_LORA
DIFF_HEADS = 8
DIFF_HEAD_DIM = 64
DIFF_V_DIM = 2 * DIFF_HEAD_DIM
DIFF_WIDTH = DIFF_HEADS * DIFF_V_DIM
DIFF_COLS = 3 * DIFF_WIDTH
GATE_COLS = 2 * D_MODEL
PEER_HEADS = 8
PEER_HALF = 128
N_KEYS = 128
N_EXPERTS = N_KEYS * N_KEYS
PEER_TOPK = 16
NORM_EPS = 1e-6
LN_X_EPS = 64e-5
SUBLN_EPS = 1e-5
NEG_INF = -1e30
RWKV_CHUNK = 64


def _dot(a, b, precision=None):
    return jnp.dot(a, b, preferred_element_type=F32, precision=precision)


def _dot_nt(a, b, precision=None):
    return lax.dot_general(a, b, (((1,), (1,)), ((), ())), preferred_element_type=F32,
                           precision=precision)


def _const_spec(shape):
    nd = len(shape)
    return pl.BlockSpec(shape, lambda *_: (0,) * nd)


def _rmsnorm_kernel(x_ref, g_ref, o_ref, *, eps):
    x = x_ref[...]
    y = x * lax.rsqrt(jnp.mean(x * x, axis=-1, keepdims=True) + eps) * g_ref[...]
    o_ref[...] = y.astype(o_ref.dtype)


def _rmsnorm(x, g, eps, out_dtype, tm):
    s, d = x.shape
    return pl.pallas_call(
        functools.partial(_rmsnorm_kernel, eps=eps),
        grid=(s // tm,),
        in_specs=[pl.BlockSpec((tm, d), lambda i: (i, 0)), _const_spec((1, d))],
        out_specs=pl.BlockSpec((tm, d), lambda i: (i, 0)),
        out_shape=jax.ShapeDtypeStruct((s, d), out_dtype),
        compiler_params=pltpu.CompilerParams(dimension_semantics=("parallel",)),
        name="rmsnorm",
    )(x, g.reshape(1, d))


def _mm_kernel(x_ref, w_ref, o_ref):
    o_ref[...] = _dot(x_ref[...], w_ref[...]).astype(o_ref.dtype)


def _matmul(x, w, out_dtype, tm, tn, name):
    s, k = x.shape
    n = w.shape[1]
    return pl.pallas_call(
        _mm_kernel,
        grid=(s // tm, n // tn),
        in_specs=[pl.BlockSpec((tm, k), lambda i, j: (i, 0)),
                  pl.BlockSpec((k, tn), lambda i, j: (0, j))],
        out_specs=pl.BlockSpec((tm, tn), lambda i, j: (i, j)),
        out_shape=jax.ShapeDtypeStruct((s, n), out_dtype),
        compiler_params=pltpu.CompilerParams(dimension_semantics=("parallel", "parallel")),
        name=name,
    )(x, w)


def _mm_split_kernel(xh_ref, xl_ref, wh_ref, wl_ref, o_ref):
    xh = xh_ref[...]
    o_ref[...] = _dot(xh, wh_ref[...]) + (_dot(xh, wl_ref[...]) + _dot(xl_ref[...], wh_ref[...]))


def _matmul_split(xh, xl, wh, wl, tm, tn, name):
    s, k = xh.shape
    n = wh.shape[1]
    xspec = pl.BlockSpec((tm, k), lambda i, j: (i, 0))
    wspec = pl.BlockSpec((k, tn), lambda i, j: (0, j))
    return pl.pallas_call(
        _mm_split_kernel,
        grid=(s // tm, n // tn),
        in_specs=[xspec, xspec, wspec, wspec],
        out_specs=pl.BlockSpec((tm, tn), lambda i, j: (i, j)),
        out_shape=jax.ShapeDtypeStruct((s, n), F32),
        compiler_params=pltpu.CompilerParams(dimension_semantics=("parallel", "parallel")),
        name=name,
    )(xh, xl, wh, wl)


def _head_sum(x, bd):
    tiles = [_dot(x[:, c * LANES:(c + 1) * LANES], bd, HIGHEST)
             for c in range(x.shape[1] // LANES)]
    return jnp.concatenate(tiles, axis=1)


def _softplus(x):
    return jnp.maximum(x, 0.0) + jnp.log1p(jnp.exp(-jnp.abs(x)))


def _rwkv_kernel(p_ref, pprev_ref, mu_ref, w0_ref, wd_ref, a0_ref, wa_ref, wg_ref, kk_ref,
                 ka_ref, rk_ref, lng_ref, lnb_ref, o_ref, state_ref, y_ref):
    L = RWKV_CHUNK
    N = RWKV_HEAD_DIM
    W = RWKV_WIDTH
    step = pl.program_id(0)

    @pl.when(step == 0)
    def _():
        state_ref[...] = jnp.zeros_like(state_ref)

    row = lax.broadcasted_iota(jnp.int32, (L, 1), 0)
    carry_on = jnp.where(step == 0, 0.0, 1.0)

    def shifted(c0, c1):
        p = p_ref[:, c0:c1]
        last = pprev_ref[SUBLANES - 1:SUBLANES, c0:c1] * carry_on
        prev = jnp.where(row == 0, last, pltpu.roll(p, 1, axis=0))
        return p + (prev - p) * mu_ref[:, c0:c1]

    r = shifted(0, W)
    k = shifted(W, 2 * W)
    v = shifted(2 * W, 3 * W)
    xw = shifted(3 * W, 3 * W + LORA_PAD)
    xa = shifted(3 * W + LORA_PAD, 3 * W + 2 * LORA_PAD)
    xg = shifted(3 * W + 2 * LORA_PAD, 3 * W + 2 * LORA_PAD + GATE_LORA)

    z = w0_ref[...] + _dot(jnp.tanh(xw), wd_ref[...], HIGHEST)
    w_log = -_softplus(-z) - 0.5
    lw = -jnp.exp(w_log)
    a = jax.nn.sigmoid(a0_ref[...] + _dot(xa, wa_ref[...], HIGHEST))
    g = _dot(jax.nn.sigmoid(xg), wg_ref[...], HIGHEST)

    lane_i = lax.broadcasted_iota(jnp.int32, (LANES, LANES), 0) // N
    lane_j = lax.broadcasted_iota(jnp.int32, (LANES, LANES), 1) // N
    bd = jnp.where(lane_i == lane_j, 1.0, 0.0).astype(F32)

    kk = k * kk_ref[...]
    kk = kk / jnp.maximum(jnp.sqrt(_head_sum(kk * kk, bd)), 1e-12)
    k = k * (1.0 + (a - 1.0) * ka_ref[...])

    ti = lax.broadcasted_iota(jnp.int32, (L, L), 0)
    tj = lax.broadcasted_iota(jnp.int32, (L, L), 1)
    incl = jnp.where(tj <= ti, 1.0, 0.0).astype(F32)
    strict = jnp.where(tj < ti, 1.0, 0.0).astype(F32)
    eye = jnp.where(tj == ti, 1.0, 0.0).astype(F32)

    cum = _dot(incl, lw, HIGHEST)
    cum_last = cum[L - 1:L, :]
    e_cum = jnp.exp(cum)
    e_inv = jnp.exp(-cum)
    e_tail = jnp.exp(cum_last - cum)
    r_t = r * e_cum
    a_t = -kk * jnp.exp(cum - lw)
    b = kk * a
    b_t = b * e_inv
    k_t = k * e_inv
    b_w = b * e_tail
    k_w = k * e_tail
    w_last = jnp.exp(cum_last)

    for h in range(RWKV_HEADS):
        sl = slice(h * N, (h + 1) * N)
        zt = state_ref[h]
        a_h, r_h, v_h = a_t[:, sl], r_t[:, sl], v[:, sl]
        m = _dot_nt(jnp.concatenate([a_h, r_h], axis=0),
                    jnp.concatenate([b_t[:, sl], k_t[:, sl]], axis=0), HIGHEST)
        a_ab = m[:L, :L] * strict
        a_ak = m[:L, L:] * strict
        a_rb = m[L:, :L] * incl
        a_rk = m[L:, L:] * incl
        pw = a_ab
        inv = eye + a_ab
        for _ in range(5):
            pw = _dot(pw, pw, HIGHEST)
            inv = inv + _dot(inv, pw, HIGHEST)
        x = _dot(jnp.concatenate([a_h, a_ak], axis=1),
                 jnp.concatenate([zt, v_h], axis=0), HIGHEST)
        u = _dot(inv, x, HIGHEST)
        bk_t = _dot_nt(eye, jnp.concatenate([b_w[:, sl], k_w[:, sl]], axis=0), HIGHEST)
        lhs = jnp.concatenate([jnp.concatenate([r_h, a_rb, a_rk], axis=1),
                               jnp.concatenate([eye * w_last[:, sl], bk_t], axis=1)], axis=0)
        yz = _dot(lhs, jnp.concatenate([zt, u, v_h], axis=0), HIGHEST)
        y_ref[:, sl] = yz[:L]
        state_ref[h] = yz[L:]

    y = y_ref[...]
    mean = _head_sum(y, bd) * (1.0 / N)
    yc = y - mean
    var = _head_sum(yc * yc, bd) * (1.0 / N)
    yn = yc * lax.rsqrt(var + LN_X_EPS) * lng_ref[...] + lnb_ref[...]
    bonus = _head_sum(r * k * rk_ref[...], bd) * v
    o_ref[...] = ((yn + bonus) * g).astype(o_ref.dtype)


def _rwkv_time_mix(p, mu, w0, wd, a0, wa, wg, k_k, k_a, r_k, lnx_g, lnx_b):
    s = p.shape[0]
    L = RWKV_CHUNK
    W = RWKV_WIDTH
    row = lambda v: v.reshape(1, -1)
    consts = [row(mu), row(w0), wd, row(a0), wa, wg, row(k_k), row(k_a), row(r_k), row(lnx_g),
              row(lnx_b)]
    return pl.pallas_call(
        _rwkv_kernel,
        grid=(s // L,),
        in_specs=[pl.BlockSpec((L, RWKV_COLS_PAD), lambda i: (i, 0)),
                  pl.BlockSpec((SUBLANES, RWKV_COLS_PAD),
                               lambda i: (jnp.maximum(i * (L // SUBLANES) - 1, 0), 0))]
                 + [_const_spec(c.shape) for c in consts],
        out_specs=pl.BlockSpec((L, W), lambda i: (i, 0)),
        out_shape=jax.ShapeDtypeStruct((s, W), BF16),
        scratch_shapes=[pltpu.VMEM((RWKV_HEADS, RWKV_HEAD_DIM, RWKV_HEAD_DIM), F32),
                        pltpu.VMEM((L, W), F32)],
        compiler_params=pltpu.CompilerParams(dimension_semantics=("arbitrary",)),
        name="rwkv7",
    )(p, p, *consts)


def _diffattn_kernel(q_ref, k_ref, v_ref, lq1_ref, lk1_ref, lq2_ref, lk2_ref, g_ref, o_ref,
                     m_ref, l_ref, acc_ref, *, tq, tk, lambda_init):
    qi = pl.program_id(1)
    ki = pl.program_id(2)
    last = (qi * tq + tq - 1) // tk

    @pl.when(ki == 0)
    def _():
        m_ref[...] = jnp.full_like(m_ref, NEG_INF)
        l_ref[...] = jnp.zeros_like(l_ref)
        acc_ref[...] = jnp.zeros_like(acc_ref)

    @pl.when(ki <= last)
    def _():
        q = q_ref[...] * (DIFF_HEAD_DIM ** -0.5)
        k = k_ref[...]
        v = v_ref[...]
        lane = lax.broadcasted_iota(jnp.int32, q.shape, 1)
        halves = (jnp.where(lane < DIFF_HEAD_DIM, q, jnp.zeros_like(q)),
                  jnp.where(lane >= DIFF_HEAD_DIM, q, jnp.zeros_like(q)))
        q_pos = qi * tq + lax.broadcasted_iota(jnp.int32, (tq, tk), 0)
        k_pos = ki * tk + lax.broadcasted_iota(jnp.int32, (tq, tk), 1)
        causal = k_pos <= q_pos
        for idx, qh in enumerate(halves):
            s = jnp.where(causal, _dot_nt(qh, k), NEG_INF)
            m_prev = m_ref[idx]
            m_new = jnp.maximum(m_prev, jnp.max(s, axis=-1, keepdims=True))
            alpha = jnp.exp(m_prev - m_new)
            p = jnp.exp(s - m_new)
            l_ref[idx] = alpha * l_ref[idx] + jnp.sum(p, axis=-1, keepdims=True)
            acc_ref[idx] = alpha * acc_ref[idx] + _dot(p.astype(BF16), v)
            m_ref[idx] = m_new

    @pl.when(ki == last)
    def _():
        lam = (jnp.exp(jnp.sum(lq1_ref[...] * lk1_ref[...], axis=-1, keepdims=True))
               - jnp.exp(jnp.sum(lq2_ref[...] * lk2_ref[...], axis=-1, keepdims=True))
               + lambda_init)
        o = acc_ref[0] / l_ref[0] - lam * (acc_ref[1] / l_ref[1])
        o = o * lax.rsqrt(jnp.mean(o * o, axis=-1, keepdims=True) + SUBLN_EPS) * g_ref[...]
        o_ref[...] = (o * (1.0 - lambda_init)).astype(o_ref.dtype)


def _diff_attention(p_diff, lq1, lk1, lq2, lk2, subln_g, lambda_init, tq, tk):
    s = p_diff.shape[0]
    H = DIFF_HEADS
    row = lambda v: v.reshape(1, -1)

    def kv_block(qi, ki):
        return jnp.minimum(ki, (qi * tq + tq - 1) // tk)

    lam_specs = [_const_spec((1, DIFF_HEAD_DIM))] * 4
    return pl.pallas_call(
        functools.partial(_diffattn_kernel, tq=tq, tk=tk, lambda_init=lambda_init),
        grid=(H, s // tq, s // tk),
        in_specs=[pl.BlockSpec((tq, DIFF_V_DIM), lambda h, qi, ki: (qi, h)),
                  pl.BlockSpec((tk, DIFF_V_DIM), lambda h, qi, ki: (kv_block(qi, ki), H + h)),
                  pl.BlockSpec((tk, DIFF_V_DIM), lambda h, qi, ki: (kv_block(qi, ki), 2 * H + h))]
                 + lam_specs + [_const_spec((1, DIFF_V_DIM))],
        out_specs=pl.BlockSpec((tq, DIFF_V_DIM), lambda h, qi, ki: (qi, h)),
        out_shape=jax.ShapeDtypeStruct((s, DIFF_WIDTH), BF16),
        scratch_shapes=[pltpu.VMEM((2, tq, 1), F32), pltpu.VMEM((2, tq, 1), F32),
                        pltpu.VMEM((2, tq, DIFF_V_DIM), F32)],
        compiler_params=pltpu.CompilerParams(
            dimension_semantics=("parallel", "parallel", "arbitrary")),
        name="diff_attention",
    )(p_diff, p_diff, p_diff, row(lq1), row(lk1), row(lq2), row(lk2), row(subln_g))


def _merge_kernel(ya_ref, yb_ref, gate_ref, x_ref, wa_ref, wb_ref, wo_ref, g2_ref,
                  h_ref, hh_ref, hl_ref):
    pa = _dot(ya_ref[...], wa_ref[...])
    pb = _dot(yb_ref[...], wb_ref[...])
    ga = jax.nn.sigmoid(gate_ref[:, :D_MODEL].astype(F32))
    gb = jax.nn.sigmoid(gate_ref[:, D_MODEL:].astype(F32))
    merged = ga * pa + gb * pb
    h = x_ref[...] + _dot(merged.astype(BF16), wo_ref[...])
    h_ref[...] = h
    hn = h * lax.rsqrt(jnp.mean(h * h, axis=-1, keepdims=True) + NORM_EPS) * g2_ref[...]
    hi = hn.astype(BF16)
    hh_ref[...] = hi
    hl_ref[...] = (hn - hi.astype(F32)).astype(BF16)


def _merge(ya, yb, gate, x, wa, wb, wo, g2, tm):
    s = x.shape[0]
    D = D_MODEL
    rows = lambda w: pl.BlockSpec((tm, w), lambda i: (i, 0))
    single = lambda shape: pl.BlockSpec(shape, lambda i: (0, 0), pipeline_mode=pl.Buffered(1))
    return pl.pallas_call(
        _merge_kernel,
        grid=(s // tm,),
        in_specs=[rows(RWKV_WIDTH), rows(DIFF_WIDTH), rows(GATE_COLS), rows(D),
                  single((RWKV_WIDTH, D)), single((DIFF_WIDTH, D)), single((D, D)),
                  _const_spec((1, D))],
        out_specs=[rows(D), rows(D), rows(D)],
        out_shape=[jax.ShapeDtypeStruct((s, D), F32), jax.ShapeDtypeStruct((s, D), BF16),
                   jax.ShapeDtypeStruct((s, D), BF16)],
        compiler_params=pltpu.CompilerParams(dimension_semantics=("parallel",),
                                             vmem_limit_bytes=56 * 1024 * 1024),
        name="merge_out_proj",
    )(ya, yb, gate, x, wa, wb, wo, g2.reshape(1, D))


def _cmp_exchange(xs, i, l, descending):
    hi = jnp.maximum(xs[i], xs[l])
    lo = jnp.minimum(xs[i], xs[l])
    xs[i], xs[l] = (hi, lo) if descending else (lo, hi)


def _bitonic_merge_desc(xs):
    xs = list(xs)
    n = len(xs)
    j = n // 2
    while j >= 1:
        for i in range(n):
            l = i ^ j
            if l > i:
                _cmp_exchange(xs, i, l, True)
        j //= 2
    return xs


def _bitonic_sort_desc(xs):
    xs = list(xs)
    n = len(xs)
    k = 2
    while k <= n:
        j = k // 2
        while j >= 1:
            for i in range(n):
                l = i ^ j
                if l > i:
                    _cmp_exchange(xs, i, l, (i & k) == 0)
            j //= 2
        k *= 2
    return xs


def _merge_top(a, b):
    n = len(a)
    return _bitonic_merge_desc([jnp.maximum(a[i], b[n - 1 - i]) for i in range(n)])


def _top16_over_rows(s):
    groups = [s[g * SUBLANES:(g + 1) * SUBLANES, :] for g in range(s.shape[0] // SUBLANES)]
    top = _bitonic_sort_desc(groups)
    for shift in (4, 2, 1):
        top = _merge_top(top, [pltpu.roll(x, shift, axis=0) for x in top])
    return top


def _peer_score_kernel(q_ref, keys_ref, r2_ref, e2_ref, n_ref, d_ref):
    K = PEER_TOPK
    T = q_ref.shape[0]
    H = PEER_HEADS
    scores = []
    tops = []
    for hp in range(2 * H):
        s = _dot_nt(keys_ref[hp], q_ref[:, hp * PEER_HALF:(hp + 1) * PEER_HALF], HIGHEST)
        scores.append(s)
        tops.append(_top16_over_rows(s))
    sub = lax.broadcasted_iota(jnp.int32, (SUBLANES, T), 0)

    def by_head(p, i):
        out = tops[p][i]
        for h in range(1, H):
            out = jnp.where(sub == h, tops[2 * h + p][i], out)
        return out

    aa = [by_head(0, i) for i in range(K)]
    bb = [by_head(1, i) for i in range(K)]
    cands = [aa[i] + bb[j] for i in range(K) for j in range(K) if (i + 1) * (j + 1) <= K]
    cands += [jnp.full_like(cands[0], -jnp.inf)] * (-len(cands) % K)
    best = _bitonic_sort_desc(cands[:K])
    for c in range(K, len(cands), K):
        best = _merge_top(best, _bitonic_sort_desc(cands[c:c + K]))
    thr = best[K - 1]
    zsum = jnp.zeros_like(thr)
    for t in best:
        zsum = zsum + jnp.exp(t - best[0])
    inv_z = 1.0 / zsum
    for h in range(H):
        hs = slice(h, h + 1)
        s1, s2 = scores[2 * h], scores[2 * h + 1]
        thr_h = thr[hs]
        cnt = jnp.zeros_like(s1)
        rank = jnp.zeros_like(s2)
        for m in range(K):
            cnt = cnt + jnp.where(s1 + bb[m][hs] >= thr_h, 1.0, 0.0)
            rank = rank + jnp.where(bb[m][hs] > s2, 1.0, 0.0)
        n_ref[h] = cnt
        r2_ref[h] = rank
        d_ref[h] = jnp.exp(s1 - aa[0][hs]) * inv_z[hs]
        e2_ref[h] = jnp.exp(s2 - bb[0][hs])


def _peer_scores(q, keys, tt):
    s = q.shape[0]
    H = PEER_HEADS
    out = jax.ShapeDtypeStruct((H, N_KEYS, s), F32)
    ospec = pl.BlockSpec((H, N_KEYS, tt), lambda i: (0, 0, i))
    return pl.pallas_call(
        _peer_score_kernel,
        grid=(s // tt,),
        in_specs=[pl.BlockSpec((tt, 2 * H * PEER_HALF), lambda i: (i, 0)),
                  _const_spec((2 * H, N_KEYS, PEER_HALF))],
        out_specs=[ospec] * 4,
        out_shape=[out] * 4,
        compiler_params=pltpu.CompilerParams(dimension_semantics=("parallel",)),
        name="peer_scores",
    )(q, keys)


def _peer_expert_kernel(hn_ref, h1_ref, r2_ref, e2_ref, n_ref, d_ref, u_ref, vt_ref, fg_ref,
                        o_ref, acc_ref, ga_ref, *, row_slab):
    e = pl.program_id(1)
    eb = u_ref.shape[0]
    ni = eb // N_KEYS

    @pl.when(e == 0)
    def _():
        acc_ref[...] = jnp.zeros_like(acc_ref)

    pre = _dot_nt(u_ref[...], hn_ref[...])
    act = 0.5 * pre * (1.0 + lax.erf(pre * math.sqrt(0.5)))
    for ii in range(ni):
        i = e * ni + ii
        for jb in range(N_KEYS // row_slab):
            rows = slice(jb * row_slab, (jb + 1) * row_slab)
            gate = None
            for h in range(PEER_HEADS):
                picked = r2_ref[h, rows, :] < n_ref[h, pl.ds(i, 1), :]
                term = jnp.where(picked, e2_ref[h, rows, :], 0.0) * d_ref[h, pl.ds(i, 1), :]
                gate = term if gate is None else gate + term
            lo = ii * N_KEYS + jb * row_slab
            ga_ref[lo:lo + row_slab, :] = (gate * act[lo:lo + row_slab, :]).astype(BF16)
    acc_ref[...] += _dot(vt_ref[...], ga_ref[...])

    @pl.when(e == pl.num_programs(1) - 1)
    def _():
        h = h1_ref[...] + acc_ref[...].T
        o_ref[...] = h * lax.rsqrt(jnp.mean(h * h, axis=-1, keepdims=True) + NORM_EPS) * fg_ref[...]


def _peer_experts(hn, h1, r2, e2, n, d, u, vt, final_g, tt, eb):
    s = hn.shape[0]
    D = D_MODEL
    H = PEER_HEADS
    tok = lambda i, e: (i, 0)
    sel = pl.BlockSpec((H, N_KEYS, tt), lambda i, e: (0, 0, i), pipeline_mode=pl.Buffered(1))
    return pl.pallas_call(
        functools.partial(_peer_expert_kernel, row_slab=16),
        grid=(s // tt, N_EXPERTS // eb),
        in_specs=[pl.BlockSpec((tt, D), tok), pl.BlockSpec((tt, D), tok), sel, sel, sel, sel,
                  pl.BlockSpec((eb, D), lambda i, e: (e, 0)),
                  pl.BlockSpec((D, eb), lambda i, e: (0, e)),
                  pl.BlockSpec((1, D), lambda i, e: (0, 0))],
        out_specs=pl.BlockSpec((tt, D), tok),
        out_shape=jax.ShapeDtypeStruct((s, D), F32),
        scratch_shapes=[pltpu.VMEM((D, tt), F32), pltpu.VMEM((eb, tt), BF16)],
        compiler_params=pltpu.CompilerParams(dimension_semantics=("parallel", "arbitrary"),
                                             vmem_limit_bytes=56 * 1024 * 1024),
        name="peer_experts",
    )(hn, h1, r2, e2, n, d, u, vt, final_g.reshape(1, D))


def _pad_rows(w, rows):
    return jnp.pad(w, ((0, rows - w.shape[0]), (0, 0)))


def _split_bf16(w):
    hi = w.astype(BF16)
    return hi, (w - hi.astype(F32)).astype(BF16)


def _layer(h, norm1_g, w_in, shift_mu, rwkv_w0, w_decay_up, rwkv_a0, w_iclr_up, w_gate_up,
           k_k, k_a, r_k, lnx_g, lnx_b, lam_q1, lam_k1, lam_q2, lam_k2, subln_g, w_proj_a,
           w_proj_b, w_out, norm2_g, peer_wq, peer_sub_keys, peer_u, peer_v, out_g, lambda_init):
    s = h.shape[0]
    W = RWKV_WIDTH
    tm = min(s, 512)

    c0, c1, c2 = 3 * W, 3 * W + DECAY_LORA, 3 * W + DECAY_LORA + ICLR_LORA
    pad_cols = lambda m, n: jnp.pad(m, ((0, 0), (0, n - m.shape[1])))
    w_rwkv = jnp.concatenate([w_in[:, :c0], pad_cols(w_in[:, c0:c1], LORA_PAD),
                              pad_cols(w_in[:, c1:c2], LORA_PAD), w_in[:, c2:RWKV_COLS]], axis=1)
    mu2 = shift_mu.reshape(1, -1)
    mu = jnp.concatenate([mu2[:, :c0], pad_cols(mu2[:, c0:c1], LORA_PAD),
                          pad_cols(mu2[:, c1:c2], LORA_PAD), mu2[:, c2:]], axis=1)
    w_diff = w_in[:, RWKV_COLS:RWKV_COLS + DIFF_COLS]
    w_gate = w_in[:, RWKV_COLS + DIFF_COLS:]

    xn = _rmsnorm(h, norm1_g, NORM_EPS, BF16, tm)
    p_rwkv = _matmul(xn, w_rwkv.astype(BF16), F32, tm, 512, "in_proj_rwkv")
    p_diff = _matmul(xn, w_diff.astype(BF16), BF16, tm, 512, "in_proj_diff")
    p_gate = _matmul(xn, w_gate.astype(BF16), BF16, tm, 512, "in_proj_gate")

    y_a = _rwkv_time_mix(p_rwkv, mu, rwkv_w0, _pad_rows(w_decay_up, LORA_PAD), rwkv_a0,
                         _pad_rows(w_iclr_up, LORA_PAD), w_gate_up, k_k, k_a, r_k, lnx_g, lnx_b)
    y_b = _diff_attention(p_diff, lam_q1, lam_k1, lam_q2, lam_k2, subln_g, lambda_init,
                          min(s, 512), min(s, 512))
    h1, hn_hi, hn_lo = _merge(y_a, y_b, p_gate, h, w_proj_a.astype(BF16), w_proj_b.astype(BF16),
                              w_out.astype(BF16), norm2_g, min(s, 256))

    wq_hi, wq_lo = _split_bf16(peer_wq)
    q = _matmul_split(hn_hi, hn_lo, wq_hi, wq_lo, tm, 512, "peer_query")
    keys = peer_sub_keys.reshape(2 * PEER_HEADS, N_KEYS, PEER_HALF)
    r2, e2, n, d = _peer_scores(q, keys, min(s, 256))
    return _peer_experts(hn_hi, h1, r2, e2, n, d, peer_u.astype(BF16), peer_v.T.astype(BF16),
                         out_g, min(s, 512), 512)


def kernel(x, norm1_g, w_in, shift_mu, rwkv_w0, w_decay_up, rwkv_a0, w_iclr_up, w_gate_up, k_k, k_a, r_k, lnx_g, lnx_b, lam_q1, lam_k1, lam_q2, lam_k2, subln_g, w_proj_a, w_proj_b, w_out, norm2_g, peer_wq, peer_sub_keys, peer_u, peer_v, final_g):
    B, S, D = x.shape
    assert B == 1 and D == D_MODEL and norm1_g.shape[0] == 1
    lambda_init = 0.8 - 0.6 * math.exp(-0.3 * 0)
    out = _layer(x[0], norm1_g[0], w_in[0], shift_mu[0], rwkv_w0[0], w_decay_up[0], rwkv_a0[0],
                 w_iclr_up[0], w_gate_up[0], k_k[0], k_a[0], r_k[0].reshape(-1), lnx_g[0],
                 lnx_b[0], lam_q1[0], lam_k1[0], lam_q2[0], lam_k2[0], subln_g[0], w_proj_a[0],
                 w_proj_b[0], w_out[0], norm2_g[0], peer_wq[0], peer_sub_keys[0], peer_u[0],
                 peer_v[0], final_g, lambda_init)
    return out[None]
```

```python
import functools
import math

import jax
import jax.numpy as jnp
from jax import lax
from jax.experimental import pallas as pl
from jax.experimental.pallas import tpu as pltpu

F32 = jnp.float32
BF16 = jnp.bfloat16
HIGHEST = lax.Precision.HIGHEST

LANES = 128
SUBLANES = 8
BF16_SUBLANES = 16

D_MODEL = 2048
RWKV_HEADS = 16
RWKV_HEAD_DIM = 64
RWKV_WIDTH = RWKV_HEADS * RWKV_HEAD_DIM
DECAY_LORA = 96
ICLR_LORA = 96
GATE_LORA = 256
LORA_PAD = 128
RWKV_COLS = 3 * RWKV_WIDTH + DECAY_LORA + ICLR_LORA + GATE_LORA
RWKV_COLS_PAD = 3 * RWKV_WIDTH + 2 * LORA_PAD + GATE_LORA
RWKV_CHUNK = 64
RWKV_GROUP = 4
RWKV_GROUP_LANES = RWKV_GROUP * RWKV_HEAD_DIM
DIFF_HEADS = 8
DIFF_HEAD_DIM = 64
DIFF_V_DIM = 2 * DIFF_HEAD_DIM
DIFF_WIDTH = DIFF_HEADS * DIFF_V_DIM
DIFF_COLS = 3 * DIFF_WIDTH
ATTN_BLOCK = 512
GATE_COLS = 2 * D_MODEL
PEER_HEADS = 8
PEER_HALF = 128
N_KEYS = 128
N_EXPERTS = N_KEYS * N_KEYS
PEER_TOPK = 16
NORM_EPS = 1e-6
LN_X_EPS = 64e-5
SUBLN_EPS = 1e-5
NEG_INF = -1e30


def _dot(a, b, precision=None):
    return jnp.dot(a, b, preferred_element_type=F32, precision=precision)


def _dot_nt(a, b, precision=None):
    return lax.dot_general(a, b, (((1,), (1,)), ((), ())), preferred_element_type=F32,
                           precision=precision)


def _const_spec(shape):
    nd = len(shape)
    return pl.BlockSpec(shape, lambda *_: (0,) * nd)


def _split(x):
    hi = x.astype(BF16)
    return hi, (x - hi.astype(F32)).astype(BF16)


def _dot3(a, b, nt=False):
    f = _dot_nt if nt else _dot
    ah, al = _split(a)
    bh, bl = _split(b)
    return f(ah, bh) + (f(ah, bl) + f(al, bh))


def _rmsnorm_kernel(x_ref, g_ref, o_ref, *, eps):
    x = x_ref[...]
    y = x * lax.rsqrt(jnp.mean(x * x, axis=-1, keepdims=True) + eps) * g_ref[...]
    o_ref[...] = y.astype(o_ref.dtype)


def _rmsnorm(x, g, eps, out_dtype, tm):
    s, d = x.shape
    return pl.pallas_call(
        functools.partial(_rmsnorm_kernel, eps=eps),
        grid=(s // tm,),
        in_specs=[pl.BlockSpec((tm, d), lambda i: (i, 0)), _const_spec((1, d))],
        out_specs=pl.BlockSpec((tm, d), lambda i: (i, 0)),
        out_shape=jax.ShapeDtypeStruct((s, d), out_dtype),
        compiler_params=pltpu.CompilerParams(dimension_semantics=("parallel",)),
        name="rmsnorm",
    )(x, g.reshape(1, d))


def _mm_kernel(x_ref, w_ref, o_ref):
    o_ref[...] = _dot(x_ref[...], w_ref[...]).astype(o_ref.dtype)


def _matmul(x, w, out_dtype, tm, tn, name):
    s, k = x.shape
    n = w.shape[1]
    return pl.pallas_call(
        _mm_kernel,
        grid=(s // tm, n // tn),
        in_specs=[pl.BlockSpec((tm, k), lambda i, j: (i, 0)),
                  pl.BlockSpec((k, tn), lambda i, j: (0, j))],
        out_specs=pl.BlockSpec((tm, tn), lambda i, j: (i, j)),
        out_shape=jax.ShapeDtypeStruct((s, n), out_dtype),
        compiler_params=pltpu.CompilerParams(dimension_semantics=("parallel", "parallel")),
        name=name,
    )(x, w)


def _mm_nt_kernel(w_ref, x_ref, o_ref):
    o_ref[...] = _dot_nt(w_ref[...], x_ref[...]).astype(o_ref.dtype)


def _matmul_nt(w, x, out_dtype, tn, tm, name):
    n, k = w.shape
    s = x.shape[0]
    return pl.pallas_call(
        _mm_nt_kernel,
        grid=(s // tm, n // tn),
        in_specs=[pl.BlockSpec((tn, k), lambda i, j: (j, 0)),
                  pl.BlockSpec((tm, k), lambda i, j: (i, 0))],
        out_specs=pl.BlockSpec((tn, tm), lambda i, j: (j, i)),
        out_shape=jax.ShapeDtypeStruct((n, s), out_dtype),
        compiler_params=pltpu.CompilerParams(dimension_semantics=("parallel", "parallel")),
        name=name,
    )(w, x)


def _mm_split_kernel(xh_ref, xl_ref, wh_ref, wl_ref, o_ref):
    xh = xh_ref[...]
    o_ref[...] = _dot(xh, wh_ref[...]) + (_dot(xh, wl_ref[...]) + _dot(xl_ref[...], wh_ref[...]))


def _matmul_split(xh, xl, wh, wl, tm, tn, name):
    s, k = xh.shape
    n = wh.shape[1]
    xspec = pl.BlockSpec((tm, k), lambda i, j: (i, 0))
    wspec = pl.BlockSpec((k, tn), lambda i, j: (0, j))
    return pl.pallas_call(
        _mm_split_kernel,
        grid=(s // tm, n // tn),
        in_specs=[xspec, xspec, wspec, wspec],
        out_specs=pl.BlockSpec((tm, tn), lambda i, j: (i, j)),
        out_shape=jax.ShapeDtypeStruct((s, n), F32),
        compiler_params=pltpu.CompilerParams(dimension_semantics=("parallel", "parallel")),
        name=name,
    )(xh, xl, wh, wl)


def _head_sum(x, ones_bd):
    xh, xl = _split(x)
    tiles = []
    for c in range(x.shape[1] // LANES):
        cs = slice(c * LANES, (c + 1) * LANES)
        tiles.append(_dot(xh[:, cs], ones_bd) + _dot(xl[:, cs], ones_bd))
    return jnp.concatenate(tiles, axis=1)


def _softplus(x):
    return jnp.maximum(x, 0.0) + jnp.log1p(jnp.exp(-jnp.abs(x)))


def _rwkv_kernel(p_ref, pprev_ref, mu_ref, w0_ref, wd_ref, a0_ref, wa_ref, wg_ref, kk_ref,
                 ka_ref, rk_ref, lng_ref, lnb_ref, o_ref, state_ref, y_ref):
    L = RWKV_CHUNK
    N = RWKV_HEAD_DIM
    W = RWKV_WIDTH
    G = RWKV_GROUP
    GL = RWKV_GROUP_LANES
    step = pl.program_id(0)

    @pl.when(step == 0)
    def _():
        state_ref[...] = jnp.zeros_like(state_ref)

    row = lax.broadcasted_iota(jnp.int32, (L, 1), 0)
    carry_on = jnp.where(step == 0, 0.0, 1.0)

    def shifted(c0, c1):
        p = p_ref[:, c0:c1]
        last = pprev_ref[SUBLANES - 1:SUBLANES, c0:c1] * carry_on
        prev = jnp.where(row == 0, last, pltpu.roll(p, 1, axis=0))
        return p + (prev - p) * mu_ref[:, c0:c1]

    r = shifted(0, W)
    k = shifted(W, 2 * W)
    v = shifted(2 * W, 3 * W)
    xw = shifted(3 * W, 3 * W + LORA_PAD)
    xa = shifted(3 * W + LORA_PAD, 3 * W + 2 * LORA_PAD)
    xg = shifted(3 * W + 2 * LORA_PAD, 3 * W + 2 * LORA_PAD + GATE_LORA)

    z = w0_ref[...] + _dot3(jnp.tanh(xw), wd_ref[...])
    w_log = -_softplus(-z) - 0.5
    lw = -jnp.exp(w_log)
    a = jax.nn.sigmoid(a0_ref[...] + _dot3(xa, wa_ref[...]))
    g = _dot3(jax.nn.sigmoid(xg), wg_ref[...])

    lane_i = lax.broadcasted_iota(jnp.int32, (LANES, LANES), 0) // N
    lane_j = lax.broadcasted_iota(jnp.int32, (LANES, LANES), 1) // N
    ones_bd = jnp.where(lane_i == lane_j, 1.0, 0.0).astype(BF16)

    kk = k * kk_ref[...]
    kk = kk / jnp.maximum(jnp.sqrt(_head_sum(kk * kk, ones_bd)), 1e-12)
    k = k * (1.0 + (a - 1.0) * ka_ref[...])

    ti = lax.broadcasted_iota(jnp.int32, (L, L), 0)
    tj = lax.broadcasted_iota(jnp.int32, (L, L), 1)
    tril = jnp.where(tj <= ti, 1.0, 0.0).astype(F32)
    gi = lax.broadcasted_iota(jnp.int32, (L, GL), 0)
    gj = lax.broadcasted_iota(jnp.int32, (L, GL), 1) % L
    incl = gj <= gi
    strict = gj < gi
    eye4 = jnp.where(gj == gi, 1.0, 0.0).astype(F32)
    bi = lax.broadcasted_iota(jnp.int32, (GL, GL), 0) // N
    bj = lax.broadcasted_iota(jnp.int32, (GL, GL), 1) // N
    bd_mask = bi == bj

    def bd(x):
        return jnp.where(bd_mask, jnp.concatenate([x] * G, axis=0), 0.0)

    cum = _dot(tril, lw, HIGHEST)
    cum_last = cum[L - 1:L, :]
    e_inv = jnp.exp(-cum)
    e_tail = jnp.exp(cum_last - cum)
    r_t = r * jnp.exp(cum)
    a_t = -kk * jnp.exp(cum - lw)
    b = kk * a
    b_t = b * e_inv
    k_t = k * e_inv
    b_w = b * e_tail
    k_w = k * e_tail
    w_last = jnp.exp(cum_last)

    for grp in range(RWKV_HEADS // G):
        gs = slice(grp * GL, (grp + 1) * GL)
        a_g, r_g = a_t[:, gs], r_t[:, gs]
        bd_z = bd(state_ref[grp])
        bd_v = bd(v[:, gs])
        m = _dot3(jnp.concatenate([a_g, r_g], axis=0),
                  jnp.concatenate([bd(b_t[:, gs]), bd(k_t[:, gs])], axis=0), nt=True)
        a_ab = jnp.where(strict, m[:L, :GL], 0.0)
        a_ak = jnp.where(strict, m[:L, GL:], 0.0)
        a_rb = jnp.where(incl, m[L:, :GL], 0.0)
        a_rk = jnp.where(incl, m[L:, GL:], 0.0)
        pw = _dot3(a_ab, bd(a_ab))
        inv = eye4 + a_ab
        for _ in range(4):
            both = _dot3(jnp.concatenate([pw, inv], axis=0), bd(pw))
            pw, inv = both[:L], inv + both[L:]
        inv = inv + _dot3(inv, bd(pw))
        x = _dot3(jnp.concatenate([a_g, a_ak], axis=1), jnp.concatenate([bd_z, bd_v], axis=0))
        u = _dot3(inv, bd(x))
        bk = _dot3(eye4, jnp.concatenate([bd(b_w[:, gs]), bd(k_w[:, gs])], axis=0), nt=True)
        lhs = jnp.concatenate([jnp.concatenate([r_g, a_rb, a_rk], axis=1),
                               jnp.concatenate([eye4 * w_last[:, gs], bk], axis=1)], axis=0)
        yz = _dot3(lhs, jnp.concatenate([bd_z, bd(u), bd_v], axis=0))
        y_ref[:, gs] = yz[:L]
        state_ref[grp] = yz[L:]

    y = y_ref[...]
    mean = _head_sum(y, ones_bd) * (1.0 / N)
    yc = y - mean
    var = _head_sum(yc * yc, ones_bd) * (1.0 / N)
    yn = yc * lax.rsqrt(var + LN_X_EPS) * lng_ref[...] + lnb_ref[...]
    bonus = _head_sum(r * k * rk_ref[...], ones_bd) * v
    o_ref[...] = ((yn + bonus) * g).astype(o_ref.dtype)


def _rwkv_time_mix(p, mu, w0, wd, a0, wa, wg, k_k, k_a, r_k, lnx_g, lnx_b):
    s = p.shape[0]
    L = RWKV_CHUNK
    W = RWKV_WIDTH
    row = lambda v: v.reshape(1, -1)
    consts = [row(mu), row(w0), wd, row(a0), wa, wg, row(k_k), row(k_a), row(r_k), row(lnx_g),
              row(lnx_b)]
    return pl.pallas_call(
        _rwkv_kernel,
        grid=(s // L,),
        in_specs=[pl.BlockSpec((L, RWKV_COLS_PAD), lambda i: (i, 0)),
                  pl.BlockSpec((SUBLANES, RWKV_COLS_PAD),
                               lambda i: (jnp.maximum(i * (L // SUBLANES) - 1, 0), 0))]
                 + [_const_spec(c.shape) for c in consts],
        out_specs=pl.BlockSpec((L, W), lambda i: (i, 0)),
        out_shape=jax.ShapeDtypeStruct((s, W), BF16),
        scratch_shapes=[pltpu.VMEM((RWKV_HEADS // RWKV_GROUP, RWKV_HEAD_DIM, RWKV_GROUP_LANES),
                                   F32),
                        pltpu.VMEM((L, W), F32)],
        compiler_params=pltpu.CompilerParams(dimension_semantics=("arbitrary",)),
        name="rwkv7",
    )(p, p, *consts)


def _diffattn_kernel(q_ref, k_ref, vt_ref, lq1_ref, lk1_ref, lq2_ref, lk2_ref, g_ref, o_ref,
                     sa_ref, sb_ref, m_ref, acc_ref, *, lambda_init):
    T = ATTN_BLOCK
    DV = DIFF_V_DIM
    qi = pl.program_id(1)
    q = q_ref[...]
    lane = lax.broadcasted_iota(jnp.int32, q.shape, 1)
    zero = jnp.zeros_like(q)
    q_halves = (jnp.where(lane < DIFF_HEAD_DIM, q, zero), jnp.where(lane >= DIFF_HEAD_DIM, q, zero))
    ones = jnp.ones((BF16_SUBLANES, T), BF16)
    causal = (lax.broadcasted_iota(jnp.int32, (T, T), 0)
              <= lax.broadcasted_iota(jnp.int32, (T, T), 1))

    def scores(j, s_ref):
        kb = k_ref[pl.ds(pl.multiple_of(j * T, T), T), :]
        for idx in range(2):
            s_ref[idx] = _dot_nt(kb, q_halves[idx])

    def absorb(j, s_ref, masked):
        vt = vt_ref[:, pl.ds(pl.multiple_of(j * T, T), T)]
        vext = jnp.concatenate([vt, ones], axis=0)
        for idx in range(2):
            sc = jnp.where(causal, s_ref[idx], NEG_INF) if masked else s_ref[idx]
            m_old = m_ref[idx]
            m_new = jnp.maximum(m_old, jnp.max(sc, axis=0, keepdims=True))
            sc = jnp.where(causal, s_ref[idx], NEG_INF) if masked else s_ref[idx]
            p = jnp.exp2(sc - m_new).astype(BF16)
            acc_ref[idx] = jnp.exp2(m_old - m_new) * acc_ref[idx] + _dot(vext, p)
            m_ref[idx] = m_new

    m_ref[...] = jnp.full_like(m_ref, NEG_INF)
    acc_ref[...] = jnp.zeros_like(acc_ref)
    scores(0, sa_ref)

    @pl.loop(0, qi // 2)
    def _(i):
        scores(2 * i + 1, sb_ref)
        absorb(2 * i, sa_ref, False)
        scores(2 * i + 2, sa_ref)
        absorb(2 * i + 1, sb_ref, False)

    @pl.when(qi % 2 == 1)
    def _():
        scores(qi, sb_ref)
        absorb(qi - 1, sa_ref, False)
        absorb(qi, sb_ref, True)

    @pl.when(qi % 2 == 0)
    def _():
        absorb(qi, sa_ref, True)

    lam = (jnp.exp(jnp.sum(lq1_ref[...] * lk1_ref[...], axis=-1, keepdims=True))
           - jnp.exp(jnp.sum(lq2_ref[...] * lk2_ref[...], axis=-1, keepdims=True))
           + lambda_init)
    o = (acc_ref[0, :DV, :] / acc_ref[0, DV:DV + 1, :]
         - lam * (acc_ref[1, :DV, :] / acc_ref[1, DV:DV + 1, :]))
    o = o * lax.rsqrt(jnp.mean(o * o, axis=0, keepdims=True) + SUBLN_EPS) * g_ref[...]
    o_ref[...] = (o * (1.0 - lambda_init)).T.astype(o_ref.dtype)


def _diff_attention(qk, vt, lq1, lk1, lq2, lk2, subln_g, lambda_init):
    s = qk.shape[0]
    H = DIFF_HEADS
    T = ATTN_BLOCK
    DV = DIFF_V_DIM
    row = lambda v: v.reshape(1, -1)
    lam_specs = [_const_spec((1, DIFF_HEAD_DIM))] * 4
    return pl.pallas_call(
        functools.partial(_diffattn_kernel, lambda_init=lambda_init),
        grid=(H, s // T),
        in_specs=[pl.BlockSpec((T, DV), lambda h, qi: (qi, h)),
                  pl.BlockSpec((s, DV), lambda h, qi: (0, H + h)),
                  pl.BlockSpec((DV, s), lambda h, qi: (h, 0))]
                 + lam_specs + [_const_spec((DV, 1))],
        out_specs=pl.BlockSpec((T, DV), lambda h, qi: (qi, h)),
        out_shape=jax.ShapeDtypeStruct((s, DIFF_WIDTH), BF16),
        scratch_shapes=[pltpu.VMEM((2, T, T), F32), pltpu.VMEM((2, T, T), F32),
                        pltpu.VMEM((2, 1, T), F32), pltpu.VMEM((2, DV + BF16_SUBLANES, T), F32)],
        compiler_params=pltpu.CompilerParams(dimension_semantics=("parallel", "arbitrary"),
                                             vmem_limit_bytes=40 * 1024 * 1024),
        name="diff_attention",
    )(qk, qk, vt, row(lq1), row(lk1), row(lq2), row(lk2), subln_g.reshape(DV, 1))


def _merge_kernel(ya_ref, yb_ref, gate_ref, x_ref, wa_ref, wb_ref, wo_ref, g2_ref,
                  h_ref, hh_ref, hl_ref, ht_ref):
    pa = _dot(ya_ref[...], wa_ref[...])
    pb = _dot(yb_ref[...], wb_ref[...])
    ga = jax.nn.sigmoid(gate_ref[:, :D_MODEL].astype(F32))
    gb = jax.nn.sigmoid(gate_ref[:, D_MODEL:].astype(F32))
    merged = ga * pa + gb * pb
    h = x_ref[...] + _dot(merged.astype(BF16), wo_ref[...])
    h_ref[...] = h
    hn = h * lax.rsqrt(jnp.mean(h * h, axis=-1, keepdims=True) + NORM_EPS) * g2_ref[...]
    hi = hn.astype(BF16)
    hh_ref[...] = hi
    hl_ref[...] = (hn - hi.astype(F32)).astype(BF16)
    ht_ref[...] = hn.T.astype(BF16)


def _merge(ya, yb, gate, x, wa, wb, wo, g2, tm):
    s = x.shape[0]
    D = D_MODEL
    rows = lambda w: pl.BlockSpec((tm, w), lambda i: (i, 0))
    single = lambda shape: pl.BlockSpec(shape, lambda i: (0, 0), pipeline_mode=pl.Buffered(1))
    return pl.pallas_call(
        _merge_kernel,
        grid=(s // tm,),
        in_specs=[rows(RWKV_WIDTH), rows(DIFF_WIDTH), rows(GATE_COLS), rows(D),
                  single((RWKV_WIDTH, D)), single((DIFF_WIDTH, D)), single((D, D)),
                  _const_spec((1, D))],
        out_specs=[rows(D), rows(D), rows(D), pl.BlockSpec((D, tm), lambda i: (0, i))],
        out_shape=[jax.ShapeDtypeStruct((s, D), F32), jax.ShapeDtypeStruct((s, D), BF16),
                   jax.ShapeDtypeStruct((s, D), BF16), jax.ShapeDtypeStruct((D, s), BF16)],
        compiler_params=pltpu.CompilerParams(dimension_semantics=("parallel",),
                                             vmem_limit_bytes=56 * 1024 * 1024),
        name="merge_out_proj",
    )(ya, yb, gate, x, wa, wb, wo, g2.reshape(1, D))


def _cmp_exchange(xs, i, l, descending):
    hi = jnp.maximum(xs[i], xs[l])
    lo = jnp.minimum(xs[i], xs[l])
    xs[i], xs[l] = (hi, lo) if descending else (lo, hi)


def _bitonic_merge_desc(xs):
    xs = list(xs)
    n = len(xs)
    j = n // 2
    while j >= 1:
        for i in range(n):
            l = i ^ j
            if l > i:
                _cmp_exchange(xs, i, l, True)
        j //= 2
    return xs


def _bitonic_sort_desc(xs):
    xs = list(xs)
    n = len(xs)
    k = 2
    while k <= n:
        j = k // 2
        while j >= 1:
            for i in range(n):
                l = i ^ j
                if l > i:
                    _cmp_exchange(xs, i, l, (i & k) == 0)
            j //= 2
        k *= 2
    return xs


def _merge_top(a, b):
    n = len(a)
    return _bitonic_merge_desc([jnp.maximum(a[i], b[n - 1 - i]) for i in range(n)])


def _top16_over_rows(s):
    groups = [s[g * SUBLANES:(g + 1) * SUBLANES, :] for g in range(s.shape[0] // SUBLANES)]
    top = _bitonic_sort_desc(groups)
    for shift in (4, 2, 1):
        top = _merge_top(top, [pltpu.roll(x, shift, axis=0) for x in top])
    return top


def _peer_score_kernel(q_ref, keys_ref, r2_ref, e2_ref, n_ref, d_ref):
    K = PEER_TOPK
    T = q_ref.shape[0]
    H = PEER_HEADS
    scores = []
    tops = []
    for hp in range(2 * H):
        s = _dot_nt(keys_ref[hp], q_ref[:, hp * PEER_HALF:(hp + 1) * PEER_HALF], HIGHEST)
        scores.append(s)
        tops.append(_top16_over_rows(s))
    sub = lax.broadcasted_iota(jnp.int32, (SUBLANES, T), 0)

    def by_head(p, i):
        out = tops[p][i]
        for h in range(1, H):
            out = jnp.where(sub == h, tops[2 * h + p][i], out)
        return out

    aa = [by_head(0, i) for i in range(K)]
    bb = [by_head(1, i) for i in range(K)]
    cands = [aa[i] + bb[j] for i in range(K) for j in range(K) if (i + 1) * (j + 1) <= K]
    cands += [jnp.full_like(cands[0], -jnp.inf)] * (-len(cands) % K)
    best = _bitonic_sort_desc(cands[:K])
    for c in range(K, len(cands), K):
        best = _merge_top(best, _bitonic_sort_desc(cands[c:c + K]))
    thr = best[K - 1]
    zsum = jnp.zeros_like(thr)
    for t in best:
        zsum = zsum + jnp.exp(t - best[0])
    inv_z = 1.0 / zsum
    for h in range(H):
        hs = slice(h, h + 1)
        s1, s2 = scores[2 * h], scores[2 * h + 1]
        thr_h = thr[hs]
        cnt = jnp.zeros_like(s1)
        rank = jnp.zeros_like(s2)
        for m in range(K):
            cnt = cnt + jnp.where(s1 + bb[m][hs] >= thr_h, 1.0, 0.0)
            rank = rank + jnp.where(bb[m][hs] > s2, 1.0, 0.0)
        n_ref[h] = cnt
        r2_ref[h] = rank
        d_ref[h] = jnp.exp(s1 - aa[0][hs]) * inv_z[hs]
        e2_ref[h] = jnp.exp(s2 - bb[0][hs])


def _peer_scores(q, keys, tt):
    s = q.shape[0]
    H = PEER_HEADS
    out = jax.ShapeDtypeStruct((H, N_KEYS, s), F32)
    ospec = pl.BlockSpec((H, N_KEYS, tt), lambda i: (0, 0, i))
    return pl.pallas_call(
        _peer_score_kernel,
        grid=(s // tt,),
        in_specs=[pl.BlockSpec((tt, 2 * H * PEER_HALF), lambda i: (i, 0)),
                  _const_spec((2 * H, N_KEYS, PEER_HALF))],
        out_specs=[ospec] * 4,
        out_shape=[out] * 4,
        compiler_params=pltpu.CompilerParams(dimension_semantics=("parallel",)),
        name="peer_scores",
    )(q, keys)


def _peer_expert_kernel(hnt_ref, h1_ref, r2_ref, e2_ref, n_ref, d_ref, u_ref, vt_ref, fg_ref,
                        o_ref, acc_ref):
    e = pl.program_id(1)
    eb = u_ref.shape[0]
    tt = hnt_ref.shape[1]
    ni = eb // N_KEYS
    slab = BF16_SUBLANES
    strip = 2 * LANES

    @pl.when(e == 0)
    def _():
        acc_ref[...] = jnp.zeros_like(acc_ref)

    groups = []
    for ii in range(ni):
        i = e * ni + ii
        rows = slice(ii * N_KEYS, (ii + 1) * N_KEYS)
        pre = _dot(u_ref[rows, :], hnt_ref[...])
        act = 0.5 * pre * (1.0 + lax.erf(pre * math.sqrt(0.5)))
        strips = []
        for t0 in range(0, tt, strip):
            ts = slice(t0, t0 + strip)
            nb = [jnp.broadcast_to(n_ref[h, pl.ds(i, 1), ts], (slab, strip))
                  for h in range(PEER_HEADS)]
            db = [jnp.broadcast_to(d_ref[h, pl.ds(i, 1), ts], (slab, strip))
                  for h in range(PEER_HEADS)]
            slabs = []
            for j0 in range(0, N_KEYS, slab):
                js = slice(j0, j0 + slab)
                gate = None
                for h in range(PEER_HEADS):
                    term = jnp.where(r2_ref[h, js, ts] < nb[h], e2_ref[h, js, ts], 0.0) * db[h]
                    gate = term if gate is None else gate + term
                slabs.append((gate * act[js, ts]).astype(BF16))
            strips.append(jnp.concatenate(slabs, axis=0))
        groups.append(jnp.concatenate(strips, axis=1))
    acc_ref[...] += _dot(vt_ref[...], jnp.concatenate(groups, axis=0))

    @pl.when(e == pl.num_programs(1) - 1)
    def _():
        h = h1_ref[...] + acc_ref[...].T
        o_ref[...] = h * lax.rsqrt(jnp.mean(h * h, axis=-1, keepdims=True) + NORM_EPS) * fg_ref[...]


def _peer_experts(hnt, h1, r2, e2, n, d, u, vt, final_g, tt, eb):
    s = hnt.shape[1]
    D = D_MODEL
    H = PEER_HEADS
    sel = pl.BlockSpec((H, N_KEYS, tt), lambda i, e: (0, 0, i), pipeline_mode=pl.Buffered(1))
    return pl.pallas_call(
        _peer_expert_kernel,
        grid=(s // tt, N_EXPERTS // eb),
        in_specs=[pl.BlockSpec((D, tt), lambda i, e: (0, i)),
                  pl.BlockSpec((tt, D), lambda i, e: (i, 0), pipeline_mode=pl.Buffered(1)),
                  sel, sel, sel, sel,
                  pl.BlockSpec((eb, D), lambda i, e: (e, 0)),
                  pl.BlockSpec((D, eb), lambda i, e: (0, e)),
                  pl.BlockSpec((1, D), lambda i, e: (0, 0))],
        out_specs=pl.BlockSpec((tt, D), lambda i, e: (i, 0)),
        out_shape=jax.ShapeDtypeStruct((s, D), F32),
        scratch_shapes=[pltpu.VMEM((D, tt), F32)],
        compiler_params=pltpu.CompilerParams(dimension_semantics=("parallel", "arbitrary"),
                                             vmem_limit_bytes=56 * 1024 * 1024),
        name="peer_experts",
    )(hnt, h1, r2, e2, n, d, u, vt, final_g.reshape(1, D))


def _pad_rows(w, rows):
    return jnp.pad(w, ((0, rows - w.shape[0]), (0, 0)))


def _split_bf16(w):
    hi = w.astype(BF16)
    return hi, (w - hi.astype(F32)).astype(BF16)


def _layer(h, norm1_g, w_in, shift_mu, rwkv_w0, w_decay_up, rwkv_a0, w_iclr_up, w_gate_up,
           k_k, k_a, r_k, lnx_g, lnx_b, lam_q1, lam_k1, lam_q2, lam_k2, subln_g, w_proj_a,
           w_proj_b, w_out, norm2_g, peer_wq, peer_sub_keys, peer_u, peer_v, out_g, lambda_init):
    s = h.shape[0]
    W = RWKV_WIDTH
    tm = min(s, 512)

    c0, c1, c2 = 3 * W, 3 * W + DECAY_LORA, 3 * W + DECAY_LORA + ICLR_LORA
    pad_cols = lambda m, n: jnp.pad(m, ((0, 0), (0, n - m.shape[1])))
    w_rwkv = jnp.concatenate([w_in[:, :c0], pad_cols(w_in[:, c0:c1], LORA_PAD),
                              pad_cols(w_in[:, c1:c2], LORA_PAD), w_in[:, c2:RWKV_COLS]], axis=1)
    mu2 = shift_mu.reshape(1, -1)
    mu = jnp.concatenate([mu2[:, :c0], pad_cols(mu2[:, c0:c1], LORA_PAD),
                          pad_cols(mu2[:, c1:c2], LORA_PAD), mu2[:, c2:]], axis=1)
    d0 = RWKV_COLS
    q_scale = DIFF_HEAD_DIM ** -0.5 * math.log2(math.e)
    w_qk = jnp.concatenate([w_in[:, d0:d0 + DIFF_WIDTH] * q_scale,
                            w_in[:, d0 + DIFF_WIDTH:d0 + 2 * DIFF_WIDTH]], axis=1)
    w_vt = w_in[:, d0 + 2 * DIFF_WIDTH:d0 + DIFF_COLS].T
    w_gate = w_in[:, d0 + DIFF_COLS:]

    xn = _rmsnorm(h, norm1_g, NORM_EPS, BF16, tm)
    p_rwkv = _matmul(xn, w_rwkv.astype(BF16), F32, tm, 512, "in_proj_rwkv")
    p_qk = _matmul(xn, w_qk.astype(BF16), BF16, tm, 512, "in_proj_qk")
    p_vt = _matmul_nt(w_vt.astype(BF16), xn, BF16, 256, tm, "in_proj_vt")
    p_gate = _matmul(xn, w_gate.astype(BF16), BF16, tm, 512, "in_proj_gate")

    y_a = _rwkv_time_mix(p_rwkv, mu, rwkv_w0, _pad_rows(w_decay_up, LORA_PAD), rwkv_a0,
                         _pad_rows(w_iclr_up, LORA_PAD), w_gate_up, k_k, k_a, r_k, lnx_g, lnx_b)
    y_b = _diff_attention(p_qk, p_vt, lam_q1, lam_k1, lam_q2, lam_k2, subln_g, lambda_init)
    h1, hn_hi, hn_lo, hn_t = _merge(y_a, y_b, p_gate, h, w_proj_a.astype(BF16),
                                    w_proj_b.astype(BF16), w_out.astype(BF16), norm2_g,
                                    min(s, 256))

    wq_hi, wq_lo = _split_bf16(peer_wq)
    q = _matmul_split(hn_hi, hn_lo, wq_hi, wq_lo, tm, 512, "peer_query")
    keys = peer_sub_keys.reshape(2 * PEER_HEADS, N_KEYS, PEER_HALF)
    r2, e2, n, d = _peer_scores(q, keys, min(s, 256))
    return _peer_experts(hn_t, h1, r2, e2, n, d, peer_u.astype(BF16), peer_v.T.astype(BF16),
                         out_g, min(s, 512), 1024)


def kernel(x, norm1_g, w_in, shift_mu, rwkv_w0, w_decay_up, rwkv_a0, w_iclr_up, w_gate_up, k_k, k_a, r_k, lnx_g, lnx_b, lam_q1, lam_k1, lam_q2, lam_k2, subln_g, w_proj_a, w_proj_b, w_out, norm2_g, peer_wq, peer_sub_keys, peer_u, peer_v, final_g):
    B, S, D = x.shape
    assert B == 1 and D == D_MODEL and norm1_g.shape[0] == 1
    lambda_init = 0.8 - 0.6 * math.exp(-0.3 * 0)
    out = _layer(x[0], norm1_g[0], w_in[0], shift_mu[0], rwkv_w0[0], w_decay_up[0], rwkv_a0[0],
                 w_iclr_up[0], w_gate_up[0], k_k[0], k_a[0], r_k[0].reshape(-1), lnx_g[0],
                 lnx_b[0], lam_q1[0], lam_k1[0], lam_q2[0], lam_k2[0], subln_g[0], w_proj_a[0],
                 w_proj_b[0], w_out[0], norm2_g[0], peer_wq[0], peer_sub_keys[0], peer_u[0],
                 peer_v[0], final_g, lambda_init)
    return out[None]
```

```python
import functools
import math

import jax
import jax.numpy as jnp
from jax import lax
from jax.experimental import pallas as pl
from jax.experimental.pallas import tpu as pltpu

F32 = jnp.float32
BF16 = jnp.bfloat16
HIGHEST = lax.Precision.HIGHEST

LANES = 128
SUBLANES = 8
BF16_SUBLANES = 16

D_MODEL = 2048
RWKV_HEADS = 16
RWKV_HEAD_DIM = 64
RWKV_WIDTH = RWKV_HEADS * RWKV_HEAD_DIM
DECAY_LORA = 96
ICLR_LORA = 96
GATE_LORA = 256
LORA_PAD = 128
RWKV_COLS = 3 * RWKV_WIDTH + DECAY_LORA + ICLR_LORA + GATE_LORA
RWKV_COLS_PAD = 3 * RWKV_WIDTH + 2 * LORA_PAD + GATE_LORA
RWKV_CHUNK = 64
RWKV_BLOCK = 128
RWKV_GROUP = 4
RWKV_GROUP_LANES = RWKV_GROUP * RWKV_HEAD_DIM
DIFF_HEADS = 8
DIFF_HEAD_DIM = 64
DIFF_V_DIM = 2 * DIFF_HEAD_DIM
DIFF_WIDTH = DIFF_HEADS * DIFF_V_DIM
DIFF_COLS = 3 * DIFF_WIDTH
ATTN_BLOCK = 512
GATE_COLS = 2 * D_MODEL
PEER_HEADS = 8
PEER_HALF = 128
N_KEYS = 128
N_EXPERTS = N_KEYS * N_KEYS
PEER_TOPK = 16
NORM_EPS = 1e-6
LN_X_EPS = 64e-5
SUBLN_EPS = 1e-5
NEG_INF = -1e30


def _dot(a, b, precision=None):
    return jnp.dot(a, b, preferred_element_type=F32, precision=precision)


def _dot_nt(a, b, precision=None):
    return lax.dot_general(a, b, (((1,), (1,)), ((), ())), preferred_element_type=F32,
                           precision=precision)


def _const_spec(shape):
    nd = len(shape)
    return pl.BlockSpec(shape, lambda *_: (0,) * nd)


def _split(x):
    hi = x.astype(BF16)
    return hi, (x - hi.astype(F32)).astype(BF16)


def _dot3(a, b, nt=False):
    f = _dot_nt if nt else _dot
    ah, al = _split(a)
    bh, bl = _split(b)
    return f(ah, bh) + (f(ah, bl) + f(al, bh))


def _rmsnorm_kernel(x_ref, g_ref, o_ref, *, eps):
    x = x_ref[...]
    y = x * lax.rsqrt(jnp.mean(x * x, axis=-1, keepdims=True) + eps) * g_ref[...]
    o_ref[...] = y.astype(o_ref.dtype)


def _rmsnorm(x, g, eps, out_dtype, tm):
    s, d = x.shape
    return pl.pallas_call(
        functools.partial(_rmsnorm_kernel, eps=eps),
        grid=(s // tm,),
        in_specs=[pl.BlockSpec((tm, d), lambda i: (i, 0)), _const_spec((1, d))],
        out_specs=pl.BlockSpec((tm, d), lambda i: (i, 0)),
        out_shape=jax.ShapeDtypeStruct((s, d), out_dtype),
        compiler_params=pltpu.CompilerParams(dimension_semantics=("parallel",)),
        name="rmsnorm",
    )(x, g.reshape(1, d))


def _mm_kernel(x_ref, w_ref, o_ref):
    o_ref[...] = _dot(x_ref[...], w_ref[...]).astype(o_ref.dtype)


def _matmul(x, w, out_dtype, tm, tn, name):
    s, k = x.shape
    n = w.shape[1]
    return pl.pallas_call(
        _mm_kernel,
        grid=(s // tm, n // tn),
        in_specs=[pl.BlockSpec((tm, k), lambda i, j: (i, 0)),
                  pl.BlockSpec((k, tn), lambda i, j: (0, j))],
        out_specs=pl.BlockSpec((tm, tn), lambda i, j: (i, j)),
        out_shape=jax.ShapeDtypeStruct((s, n), out_dtype),
        compiler_params=pltpu.CompilerParams(dimension_semantics=("parallel", "parallel")),
        name=name,
    )(x, w)


def _mm_nt_kernel(w_ref, x_ref, o_ref):
    o_ref[...] = _dot_nt(w_ref[...], x_ref[...]).astype(o_ref.dtype)


def _matmul_nt(w, x, out_dtype, tn, tm, name):
    n, k = w.shape
    s = x.shape[0]
    return pl.pallas_call(
        _mm_nt_kernel,
        grid=(s // tm, n // tn),
        in_specs=[pl.BlockSpec((tn, k), lambda i, j: (j, 0)),
                  pl.BlockSpec((tm, k), lambda i, j: (i, 0))],
        out_specs=pl.BlockSpec((tn, tm), lambda i, j: (j, i)),
        out_shape=jax.ShapeDtypeStruct((n, s), out_dtype),
        compiler_params=pltpu.CompilerParams(dimension_semantics=("parallel", "parallel")),
        name=name,
    )(w, x)


def _mm_split_kernel(xh_ref, xl_ref, wh_ref, wl_ref, o_ref):
    xh = xh_ref[...]
    o_ref[...] = _dot(xh, wh_ref[...]) + (_dot(xh, wl_ref[...]) + _dot(xl_ref[...], wh_ref[...]))


def _matmul_split(xh, xl, wh, wl, tm, tn, name):
    s, k = xh.shape
    n = wh.shape[1]
    xspec = pl.BlockSpec((tm, k), lambda i, j: (i, 0))
    wspec = pl.BlockSpec((k, tn), lambda i, j: (0, j))
    return pl.pallas_call(
        _mm_split_kernel,
        grid=(s // tm, n // tn),
        in_specs=[xspec, xspec, wspec, wspec],
        out_specs=pl.BlockSpec((tm, tn), lambda i, j: (i, j)),
        out_shape=jax.ShapeDtypeStruct((s, n), F32),
        compiler_params=pltpu.CompilerParams(dimension_semantics=("parallel", "parallel")),
        name=name,
    )(xh, xl, wh, wl)


def _head_sum(x, ones_bd):
    xh, xl = _split(x)
    tiles = []
    for c in range(x.shape[1] // LANES):
        cs = slice(c * LANES, (c + 1) * LANES)
        tiles.append(_dot(xh[:, cs], ones_bd) + _dot(xl[:, cs], ones_bd))
    return jnp.concatenate(tiles, axis=1)


def _softplus(x):
    return jnp.maximum(x, 0.0) + jnp.log1p(jnp.exp(-jnp.abs(x)))


def _rwkv_kernel(p_ref, pprev_ref, mu_ref, w0_ref, wd_ref, a0_ref, wa_ref, wg_ref, kk_ref,
                 ka_ref, rk_ref, lng_ref, lnb_ref, o_ref, state_ref, y_ref):
    L = RWKV_CHUNK
    N = RWKV_HEAD_DIM
    W = RWKV_WIDTH
    G = RWKV_GROUP
    GL = RWKV_GROUP_LANES
    step = pl.program_id(0)

    @pl.when(step == 0)
    def _():
        state_ref[...] = jnp.zeros_like(state_ref)

    TB = p_ref.shape[0]
    row = lax.broadcasted_iota(jnp.int32, (TB, 1), 0)
    carry_on = jnp.where(step == 0, 0.0, 1.0)

    def shifted(c0, c1):
        p = p_ref[:, c0:c1]
        last = pprev_ref[SUBLANES - 1:SUBLANES, c0:c1] * carry_on
        prev = jnp.where(row == 0, last, pltpu.roll(p, 1, axis=0))
        return p + (prev - p) * mu_ref[:, c0:c1]

    r = shifted(0, W)
    k = shifted(W, 2 * W)
    v = shifted(2 * W, 3 * W)
    xw = shifted(3 * W, 3 * W + LORA_PAD)
    xa = shifted(3 * W + LORA_PAD, 3 * W + 2 * LORA_PAD)
    xg = shifted(3 * W + 2 * LORA_PAD, 3 * W + 2 * LORA_PAD + GATE_LORA)

    z = w0_ref[...] + _dot3(jnp.tanh(xw), wd_ref[...])
    w_log = -_softplus(-z) - 0.5
    lw = -jnp.exp(w_log)
    a = jax.nn.sigmoid(a0_ref[...] + _dot3(xa, wa_ref[...]))
    g = _dot3(jax.nn.sigmoid(xg), wg_ref[...])

    lane_i = lax.broadcasted_iota(jnp.int32, (LANES, LANES), 0) // N
    lane_j = lax.broadcasted_iota(jnp.int32, (LANES, LANES), 1) // N
    ones_bd = jnp.where(lane_i == lane_j, 1.0, 0.0).astype(BF16)

    kk = k * kk_ref[...]
    kk = kk / jnp.maximum(jnp.sqrt(_head_sum(kk * kk, ones_bd)), 1e-12)
    k = k * (1.0 + (a - 1.0) * ka_ref[...])

    ti = lax.broadcasted_iota(jnp.int32, (TB, TB), 0)
    tj = lax.broadcasted_iota(jnp.int32, (TB, TB), 1)
    tril = jnp.where((tj <= ti) & (tj // L == ti // L), 1.0, 0.0).astype(F32)
    gi = lax.broadcasted_iota(jnp.int32, (L, GL), 0)
    gj = lax.broadcasted_iota(jnp.int32, (L, GL), 1) % L
    incl = gj <= gi
    strict = gj < gi
    eye4 = jnp.where(gj == gi, 1.0, 0.0).astype(F32)
    bi = lax.broadcasted_iota(jnp.int32, (GL, GL), 0) // N
    bj = lax.broadcasted_iota(jnp.int32, (GL, GL), 1) // N
    bd_mask = bi == bj

    def bd(x):
        return jnp.where(bd_mask, jnp.concatenate([x] * G, axis=0), jnp.zeros((), x.dtype))

    cum = _dot(tril, lw, HIGHEST)
    e_inv = jnp.exp(-cum)
    r_t = (r * jnp.exp(cum)).astype(BF16)
    a_t = (-kk * jnp.exp(cum - lw)).astype(BF16)
    b = kk * a
    b_t = (b * e_inv).astype(BF16)
    k_t = (k * e_inv).astype(BF16)
    v_b = v.astype(BF16)
    eye4_b = eye4.astype(BF16)
    n_grp = RWKV_HEADS // G
    states = [state_ref[grp] for grp in range(n_grp)]

    n_chunks = TB // L
    pairs = [(c, grp) for c in range(n_chunks) for grp in range(n_grp)]
    rows_of = lambda c: slice(c * L, (c + 1) * L)
    lanes_of = lambda grp: slice(grp * GL, (grp + 1) * GL)
    cum_last = [cum[(c + 1) * L - 1:(c + 1) * L, :] for c in range(n_chunks)]

    a_ab, a_ak, a_rb, a_rk = {}, {}, {}, {}
    for c, grp in pairs:
        rs, gs = rows_of(c), lanes_of(grp)
        m = _dot_nt(jnp.concatenate([a_t[rs, gs], r_t[rs, gs]], axis=0),
                    jnp.concatenate([bd(b_t[rs, gs]), bd(k_t[rs, gs])], axis=0))
        a_ab[c, grp] = jnp.where(strict, m[:L, :GL], 0.0)
        a_ak[c, grp] = jnp.where(strict, m[:L, GL:], 0.0).astype(BF16)
        a_rb[c, grp] = jnp.where(incl, m[L:, :GL], 0.0).astype(BF16)
        a_rk[c, grp] = jnp.where(incl, m[L:, GL:], 0.0).astype(BF16)

    inv = {p: eye4 + a_ab[p] for p in pairs}
    pw = {}
    for p in pairs:
        pw_b = a_ab[p].astype(BF16)
        pw[p] = _dot(pw_b, bd(pw_b))
    for _ in range(4):
        for p in pairs:
            pw_b = pw[p].astype(BF16)
            both = _dot(jnp.concatenate([pw_b, inv[p].astype(BF16)], axis=0), bd(pw_b))
            pw[p], inv[p] = both[:L], inv[p] + both[L:]
    for p in pairs:
        inv[p] = (inv[p] + _dot(inv[p].astype(BF16), bd(pw[p].astype(BF16)))).astype(BF16)

    bk = {}
    for c in range(n_chunks):
        rs = rows_of(c)
        e_tail = jnp.exp(cum_last[c] - cum[rs])
        b_w = (b[rs] * e_tail).astype(BF16)
        k_w = (k[rs] * e_tail).astype(BF16)
        for grp in range(n_grp):
            gs = lanes_of(grp)
            bk[c, grp] = _dot_nt(
                eye4_b, jnp.concatenate([bd(b_w[:, gs]), bd(k_w[:, gs])], axis=0)).astype(BF16)

    for c in range(n_chunks):
        rs = rows_of(c)
        w_last = jnp.exp(cum_last[c])
        bd_z = [bd(states[grp].astype(BF16)) for grp in range(n_grp)]
        bd_v = [bd(v_b[rs, lanes_of(grp)]) for grp in range(n_grp)]
        x = [_dot(jnp.concatenate([a_t[rs, lanes_of(grp)], a_ak[c, grp]], axis=1),
                  jnp.concatenate([bd_z[grp], bd_v[grp]], axis=0)) for grp in range(n_grp)]
        u = [_dot(inv[c, grp], bd(x[grp].astype(BF16))) for grp in range(n_grp)]
        for grp in range(n_grp):
            gs = lanes_of(grp)
            lhs = jnp.concatenate(
                [jnp.concatenate([r_t[rs, gs], a_rb[c, grp], a_rk[c, grp]], axis=1),
                 jnp.concatenate([(eye4 * w_last[:, gs]).astype(BF16), bk[c, grp]], axis=1)],
                axis=0)
            yz = _dot(lhs, jnp.concatenate([bd_z[grp], bd(u[grp].astype(BF16)), bd_v[grp]],
                                           axis=0))
            y_ref[rs, gs] = yz[:L]
            states[grp] = yz[L:]
    for grp in range(n_grp):
        state_ref[grp] = states[grp]

    y = y_ref[...]
    mean = _head_sum(y, ones_bd) * (1.0 / N)
    yc = y - mean
    var = _head_sum(yc * yc, ones_bd) * (1.0 / N)
    yn = yc * lax.rsqrt(var + LN_X_EPS) * lng_ref[...] + lnb_ref[...]
    bonus = _head_sum(r * k * rk_ref[...], ones_bd) * v
    o_ref[...] = ((yn + bonus) * g).astype(o_ref.dtype)


def _rwkv_time_mix(p, mu, w0, wd, a0, wa, wg, k_k, k_a, r_k, lnx_g, lnx_b):
    s = p.shape[0]
    L = RWKV_CHUNK
    W = RWKV_WIDTH
    row = lambda v: v.reshape(1, -1)
    consts = [row(mu), row(w0), wd, row(a0), wa, wg, row(k_k), row(k_a), row(r_k), row(lnx_g),
              row(lnx_b)]
    tb = RWKV_BLOCK
    return pl.pallas_call(
        _rwkv_kernel,
        grid=(s // tb,),
        in_specs=[pl.BlockSpec((tb, RWKV_COLS_PAD), lambda i: (i, 0)),
                  pl.BlockSpec((SUBLANES, RWKV_COLS_PAD),
                               lambda i: (jnp.maximum(i * (tb // SUBLANES) - 1, 0), 0))]
                 + [_const_spec(c.shape) for c in consts],
        out_specs=pl.BlockSpec((tb, W), lambda i: (i, 0)),
        out_shape=jax.ShapeDtypeStruct((s, W), BF16),
        scratch_shapes=[pltpu.VMEM((RWKV_HEADS // RWKV_GROUP, RWKV_HEAD_DIM, RWKV_GROUP_LANES),
                                   F32),
                        pltpu.VMEM((tb, W), F32)],
        compiler_params=pltpu.CompilerParams(dimension_semantics=("arbitrary",)),
        name="rwkv7",
    )(p, p, *consts)


def _diffattn_kernel(q_ref, k_ref, vt_ref, lq1_ref, lk1_ref, lq2_ref, lk2_ref, g_ref, o_ref,
                     sa_ref, sb_ref, m_ref, acc_ref, *, lambda_init):
    T = ATTN_BLOCK
    DV = DIFF_V_DIM
    qi = pl.program_id(1)
    q = q_ref[...]
    lane = lax.broadcasted_iota(jnp.int32, q.shape, 1)
    zero = jnp.zeros_like(q)
    q_halves = (jnp.where(lane < DIFF_HEAD_DIM, q, zero), jnp.where(lane >= DIFF_HEAD_DIM, q, zero))
    ones = jnp.ones((BF16_SUBLANES, T), BF16)
    causal = (lax.broadcasted_iota(jnp.int32, (T, T), 0)
              <= lax.broadcasted_iota(jnp.int32, (T, T), 1))

    def scores(j, s_ref):
        kb = k_ref[pl.ds(pl.multiple_of(j * T, T), T), :]
        for idx in range(2):
            s_ref[idx] = _dot_nt(kb, q_halves[idx])

    def absorb(j, s_ref, masked):
        vt = vt_ref[:, pl.ds(pl.multiple_of(j * T, T), T)]
        vext = jnp.concatenate([vt, ones], axis=0)
        for idx in range(2):
            sc = jnp.where(causal, s_ref[idx], NEG_INF) if masked else s_ref[idx]
            m_old = m_ref[idx]
            m_new = jnp.maximum(m_old, jnp.max(sc, axis=0, keepdims=True))
            sc = jnp.where(causal, s_ref[idx], NEG_INF) if masked else s_ref[idx]
            p = jnp.exp2(sc - m_new).astype(BF16)
            acc_ref[idx] = jnp.exp2(m_old - m_new) * acc_ref[idx] + _dot(vext, p)
            m_ref[idx] = m_new

    m_ref[...] = jnp.full_like(m_ref, NEG_INF)
    acc_ref[...] = jnp.zeros_like(acc_ref)
    scores(0, sa_ref)

    @pl.loop(0, qi // 2)
    def _(i):
        scores(2 * i + 1, sb_ref)
        absorb(2 * i, sa_ref, False)
        scores(2 * i + 2, sa_ref)
        absorb(2 * i + 1, sb_ref, False)

    @pl.when(qi % 2 == 1)
    def _():
        scores(qi, sb_ref)
        absorb(qi - 1, sa_ref, False)
        absorb(qi, sb_ref, True)

    @pl.when(qi % 2 == 0)
    def _():
        absorb(qi, sa_ref, True)

    lam = (jnp.exp(jnp.sum(lq1_ref[...] * lk1_ref[...], axis=-1, keepdims=True))
           - jnp.exp(jnp.sum(lq2_ref[...] * lk2_ref[...], axis=-1, keepdims=True))
           + lambda_init)
    o = (acc_ref[0, :DV, :] / acc_ref[0, DV:DV + 1, :]
         - lam * (acc_ref[1, :DV, :] / acc_ref[1, DV:DV + 1, :]))
    o = o * lax.rsqrt(jnp.mean(o * o, axis=0, keepdims=True) + SUBLN_EPS) * g_ref[...]
    o_ref[...] = (o * (1.0 - lambda_init)).T.astype(o_ref.dtype)


def _diff_attention(qk, vt, lq1, lk1, lq2, lk2, subln_g, lambda_init):
    s = qk.shape[0]
    H = DIFF_HEADS
    T = ATTN_BLOCK
    DV = DIFF_V_DIM
    row = lambda v: v.reshape(1, -1)
    lam_specs = [_const_spec((1, DIFF_HEAD_DIM))] * 4
    return pl.pallas_call(
        functools.partial(_diffattn_kernel, lambda_init=lambda_init),
        grid=(H, s // T),
        in_specs=[pl.BlockSpec((T, DV), lambda h, qi: (qi, h)),
                  pl.BlockSpec((s, DV), lambda h, qi: (0, H + h)),
                  pl.BlockSpec((DV, s), lambda h, qi: (h, 0))]
                 + lam_specs + [_const_spec((DV, 1))],
        out_specs=pl.BlockSpec((T, DV), lambda h, qi: (qi, h)),
        out_shape=jax.ShapeDtypeStruct((s, DIFF_WIDTH), BF16),
        scratch_shapes=[pltpu.VMEM((2, T, T), F32), pltpu.VMEM((2, T, T), F32),
                        pltpu.VMEM((2, 1, T), F32), pltpu.VMEM((2, DV + BF16_SUBLANES, T), F32)],
        compiler_params=pltpu.CompilerParams(dimension_semantics=("parallel", "arbitrary"),
                                             vmem_limit_bytes=40 * 1024 * 1024),
        name="diff_attention",
    )(qk, qk, vt, row(lq1), row(lk1), row(lq2), row(lk2), subln_g.reshape(DV, 1))


def _merge_kernel(ya_ref, yb_ref, gate_ref, x_ref, wa_ref, wb_ref, wo_ref, g2_ref,
                  h_ref, hh_ref, hl_ref, ht_ref):
    pa = _dot(ya_ref[...], wa_ref[...])
    pb = _dot(yb_ref[...], wb_ref[...])
    ga = jax.nn.sigmoid(gate_ref[:, :D_MODEL].astype(F32))
    gb = jax.nn.sigmoid(gate_ref[:, D_MODEL:].astype(F32))
    merged = ga * pa + gb * pb
    h = x_ref[...] + _dot(merged.astype(BF16), wo_ref[...])
    h_ref[...] = h
    hn = h * lax.rsqrt(jnp.mean(h * h, axis=-1, keepdims=True) + NORM_EPS) * g2_ref[...]
    hi = hn.astype(BF16)
    hh_ref[...] = hi
    hl_ref[...] = (hn - hi.astype(F32)).astype(BF16)
    ht_ref[...] = hn.T.astype(BF16)


def _merge(ya, yb, gate, x, wa, wb, wo, g2, tm):
    s = x.shape[0]
    D = D_MODEL
    rows = lambda w: pl.BlockSpec((tm, w), lambda i: (i, 0))
    single = lambda shape: pl.BlockSpec(shape, lambda i: (0, 0), pipeline_mode=pl.Buffered(1))
    return pl.pallas_call(
        _merge_kernel,
        grid=(s // tm,),
        in_specs=[rows(RWKV_WIDTH), rows(DIFF_WIDTH), rows(GATE_COLS), rows(D),
                  single((RWKV_WIDTH, D)), single((DIFF_WIDTH, D)), single((D, D)),
                  _const_spec((1, D))],
        out_specs=[rows(D), rows(D), rows(D), pl.BlockSpec((D, tm), lambda i: (0, i))],
        out_shape=[jax.ShapeDtypeStruct((s, D), F32), jax.ShapeDtypeStruct((s, D), BF16),
                   jax.ShapeDtypeStruct((s, D), BF16), jax.ShapeDtypeStruct((D, s), BF16)],
        compiler_params=pltpu.CompilerParams(dimension_semantics=("parallel",),
                                             vmem_limit_bytes=56 * 1024 * 1024),
        name="merge_out_proj",
    )(ya, yb, gate, x, wa, wb, wo, g2.reshape(1, D))


def _cmp_exchange(xs, i, l, descending):
    hi = jnp.maximum(xs[i], xs[l])
    lo = jnp.minimum(xs[i], xs[l])
    xs[i], xs[l] = (hi, lo) if descending else (lo, hi)


def _bitonic_merge_desc(xs):
    xs = list(xs)
    n = len(xs)
    j = n // 2
    while j >= 1:
        for i in range(n):
            l = i ^ j
            if l > i:
                _cmp_exchange(xs, i, l, True)
        j //= 2
    return xs


def _bitonic_sort_desc(xs):
    xs = list(xs)
    n = len(xs)
    k = 2
    while k <= n:
        j = k // 2
        while j >= 1:
            for i in range(n):
                l = i ^ j
                if l > i:
                    _cmp_exchange(xs, i, l, (i & k) == 0)
            j //= 2
        k *= 2
    return xs


def _merge_top(a, b):
    n = len(a)
    return _bitonic_merge_desc([jnp.maximum(a[i], b[n - 1 - i]) for i in range(n)])


def _top16_over_rows(s):
    groups = [s[g * SUBLANES:(g + 1) * SUBLANES, :] for g in range(s.shape[0] // SUBLANES)]
    top = _bitonic_sort_desc(groups)
    for shift in (4, 2, 1):
        top = _merge_top(top, [pltpu.roll(x, shift, axis=0) for x in top])
    return top


def _peer_score_kernel(q_ref, keys_ref, r2_ref, e2_ref, n_ref, d_ref):
    K = PEER_TOPK
    T = q_ref.shape[0]
    H = PEER_HEADS
    scores = []
    tops = []
    for hp in range(2 * H):
        s = _dot_nt(keys_ref[hp], q_ref[:, hp * PEER_HALF:(hp + 1) * PEER_HALF], HIGHEST)
        scores.append(s)
        tops.append(_top16_over_rows(s))
    sub = lax.broadcasted_iota(jnp.int32, (SUBLANES, T), 0)

    def by_head(p, i):
        out = tops[p][i]
        for h in range(1, H):
            out = jnp.where(sub == h, tops[2 * h + p][i], out)
        return out

    aa = [by_head(0, i) for i in range(K)]
    bb = [by_head(1, i) for i in range(K)]
    cands = [aa[i] + bb[j] for i in range(K) for j in range(K) if (i + 1) * (j + 1) <= K]
    cands += [jnp.full_like(cands[0], -jnp.inf)] * (-len(cands) % K)
    best = _bitonic_sort_desc(cands[:K])
    for c in range(K, len(cands), K):
        best = _merge_top(best, _bitonic_sort_desc(cands[c:c + K]))
    thr = best[K - 1]
    zsum = jnp.zeros_like(thr)
    for t in best:
        zsum = zsum + jnp.exp(t - best[0])
    inv_z = 1.0 / zsum
    for h in range(H):
        hs = slice(h, h + 1)
        s1, s2 = scores[2 * h], scores[2 * h + 1]
        thr_h = thr[hs]
        cnt = jnp.zeros_like(s1)
        rank = jnp.zeros_like(s2)
        for m in range(K):
            cnt = cnt + jnp.where(s1 + bb[m][hs] >= thr_h, 1.0, 0.0)
            rank = rank + jnp.where(bb[m][hs] > s2, 1.0, 0.0)
        n_ref[h] = cnt
        r2_ref[h] = rank.astype(BF16)
        d_ref[h] = jnp.exp(s1 - aa[0][hs]) * inv_z[hs]
        e2_ref[h] = jnp.exp(s2 - bb[0][hs]).astype(BF16)


def _peer_scores(q, keys, tt):
    s = q.shape[0]
    H = PEER_HEADS
    out = lambda dtype: jax.ShapeDtypeStruct((H, N_KEYS, s), dtype)
    ospec = pl.BlockSpec((H, N_KEYS, tt), lambda i: (0, 0, i))
    return pl.pallas_call(
        _peer_score_kernel,
        grid=(s // tt,),
        in_specs=[pl.BlockSpec((tt, 2 * H * PEER_HALF), lambda i: (i, 0)),
                  _const_spec((2 * H, N_KEYS, PEER_HALF))],
        out_specs=[ospec] * 4,
        out_shape=[out(BF16), out(BF16), out(F32), out(F32)],
        compiler_params=pltpu.CompilerParams(dimension_semantics=("parallel",)),
        name="peer_scores",
    )(q, keys)


def _peer_expert_kernel(hnt_ref, h1_ref, r2_ref, e2_ref, n_ref, d_ref, u_ref, vt_ref, fg_ref,
                        o_ref, acc_ref):
    e = pl.program_id(1)
    eb = u_ref.shape[0]
    tt = hnt_ref.shape[1]
    ni = eb // N_KEYS
    slab = BF16_SUBLANES
    strip = 2 * LANES

    @pl.when(e == 0)
    def _():
        acc_ref[...] = jnp.zeros_like(acc_ref)

    groups = []
    for ii in range(ni):
        i = e * ni + ii
        rows = slice(ii * N_KEYS, (ii + 1) * N_KEYS)
        pre = _dot(u_ref[rows, :], hnt_ref[...])
        act = (0.5 * pre * (1.0 + lax.erf(pre * math.sqrt(0.5)))).astype(BF16)
        strips = []
        for t0 in range(0, tt, strip):
            ts = slice(t0, t0 + strip)
            nb = [jnp.broadcast_to(n_ref[h, pl.ds(i, 1), ts], (slab, strip)).astype(BF16)
                  for h in range(PEER_HEADS)]
            db = [jnp.broadcast_to(d_ref[h, pl.ds(i, 1), ts], (slab, strip)).astype(BF16)
                  for h in range(PEER_HEADS)]
            slabs = []
            for j0 in range(0, N_KEYS, slab):
                js = slice(j0, j0 + slab)
                gate = None
                for h in range(PEER_HEADS):
                    term = jnp.where(r2_ref[h, js, ts] < nb[h], e2_ref[h, js, ts],
                                     jnp.zeros((), BF16)) * db[h]
                    gate = term if gate is None else gate + term
                slabs.append(gate * act[js, ts])
            strips.append(jnp.concatenate(slabs, axis=0))
        groups.append(jnp.concatenate(strips, axis=1))
    acc_ref[...] += _dot(vt_ref[...], jnp.concatenate(groups, axis=0))

    @pl.when(e == pl.num_programs(1) - 1)
    def _():
        h = h1_ref[...] + acc_ref[...].T
        o_ref[...] = h * lax.rsqrt(jnp.mean(h * h, axis=-1, keepdims=True) + NORM_EPS) * fg_ref[...]


def _peer_experts(hnt, h1, r2, e2, n, d, u, vt, final_g, tt, eb):
    s = hnt.shape[1]
    D = D_MODEL
    H = PEER_HEADS
    sel = pl.BlockSpec((H, N_KEYS, tt), lambda i, e: (0, 0, i), pipeline_mode=pl.Buffered(1))
    return pl.pallas_call(
        _peer_expert_kernel,
        grid=(s // tt, N_EXPERTS // eb),
        in_specs=[pl.BlockSpec((D, tt), lambda i, e: (0, i)),
                  pl.BlockSpec((tt, D), lambda i, e: (i, 0), pipeline_mode=pl.Buffered(1)),
                  sel, sel, sel, sel,
                  pl.BlockSpec((eb, D), lambda i, e: (e, 0)),
                  pl.BlockSpec((D, eb), lambda i, e: (0, e)),
                  pl.BlockSpec((1, D), lambda i, e: (0, 0))],
        out_specs=pl.BlockSpec((tt, D), lambda i, e: (i, 0)),
        out_shape=jax.ShapeDtypeStruct((s, D), F32),
        scratch_shapes=[pltpu.VMEM((D, tt), F32)],
        compiler_params=pltpu.CompilerParams(dimension_semantics=("parallel", "arbitrary"),
                                             vmem_limit_bytes=56 * 1024 * 1024),
        name="peer_experts",
    )(hnt, h1, r2, e2, n, d, u, vt, final_g.reshape(1, D))


def _pad_rows(w, rows):
    return jnp.pad(w, ((0, rows - w.shape[0]), (0, 0)))


def _split_bf16(w):
    hi = w.astype(BF16)
    return hi, (w - hi.astype(F32)).astype(BF16)


def _layer(h, norm1_g, w_in, shift_mu, rwkv_w0, w_decay_up, rwkv_a0, w_iclr_up, w_gate_up,
           k_k, k_a, r_k, lnx_g, lnx_b, lam_q1, lam_k1, lam_q2, lam_k2, subln_g, w_proj_a,
           w_proj_b, w_out, norm2_g, peer_wq, peer_sub_keys, peer_u, peer_v, out_g, lambda_init):
    s = h.shape[0]
    W = RWKV_WIDTH
    tm = min(s, 512)
    tmm = min(s, 1024)

    c0, c1, c2 = 3 * W, 3 * W + DECAY_LORA, 3 * W + DECAY_LORA + ICLR_LORA
    pad_cols = lambda m, n: jnp.pad(m, ((0, 0), (0, n - m.shape[1])))
    w_rwkv = jnp.concatenate([w_in[:, :c0], pad_cols(w_in[:, c0:c1], LORA_PAD),
                              pad_cols(w_in[:, c1:c2], LORA_PAD), w_in[:, c2:RWKV_COLS]], axis=1)
    mu2 = shift_mu.reshape(1, -1)
    mu = jnp.concatenate([mu2[:, :c0], pad_cols(mu2[:, c0:c1], LORA_PAD),
                          pad_cols(mu2[:, c1:c2], LORA_PAD), mu2[:, c2:]], axis=1)
    d0 = RWKV_COLS
    q_scale = DIFF_HEAD_DIM ** -0.5 * math.log2(math.e)
    w_qk = jnp.concatenate([w_in[:, d0:d0 + DIFF_WIDTH] * q_scale,
                            w_in[:, d0 + DIFF_WIDTH:d0 + 2 * DIFF_WIDTH]], axis=1)
    w_vt = w_in[:, d0 + 2 * DIFF_WIDTH:d0 + DIFF_COLS].T
    w_gate = w_in[:, d0 + DIFF_COLS:]

    xn = _rmsnorm(h, norm1_g, NORM_EPS, BF16, tm)
    p_rwkv = _matmul(xn, w_rwkv.astype(BF16), F32, tmm, 512, "in_proj_rwkv")
    p_qk = _matmul(xn, w_qk.astype(BF16), BF16, tmm, 512, "in_proj_qk")
    p_vt = _matmul_nt(w_vt.astype(BF16), xn, BF16, 256, tmm, "in_proj_vt")
    p_gate = _matmul(xn, w_gate.astype(BF16), BF16, tmm, 512, "in_proj_gate")

    y_a = _rwkv_time_mix(p_rwkv, mu, rwkv_w0, _pad_rows(w_decay_up, LORA_PAD), rwkv_a0,
                         _pad_rows(w_iclr_up, LORA_PAD), w_gate_up, k_k, k_a, r_k, lnx_g, lnx_b)
    y_b = _diff_attention(p_qk, p_vt, lam_q1, lam_k1, lam_q2, lam_k2, subln_g, lambda_init)
    h1, hn_hi, hn_lo, hn_t = _merge(y_a, y_b, p_gate, h, w_proj_a.astype(BF16),
                                    w_proj_b.astype(BF16), w_out.astype(BF16), norm2_g,
                                    min(s, 256))

    wq_hi, wq_lo = _split_bf16(peer_wq)
    q = _matmul_split(hn_hi, hn_lo, wq_hi, wq_lo, tmm, 512, "peer_query")
    keys = peer_sub_keys.reshape(2 * PEER_HEADS, N_KEYS, PEER_HALF)
    r2, e2, n, d = _peer_scores(q, keys, min(s, 256))
    return _peer_experts(hn_t, h1, r2, e2, n, d, peer_u.astype(BF16), peer_v.T.astype(BF16),
                         out_g, min(s, 512), 1024)


def kernel(x, norm1_g, w_in, shift_mu, rwkv_w0, w_decay_up, rwkv_a0, w_iclr_up, w_gate_up, k_k, k_a, r_k, lnx_g, lnx_b, lam_q1, lam_k1, lam_q2, lam_k2, subln_g, w_proj_a, w_proj_b, w_out, norm2_g, peer_wq, peer_sub_keys, peer_u, peer_v, final_g):
    B, S, D = x.shape
    assert B == 1 and D == D_MODEL and norm1_g.shape[0] == 1
    lambda_init = 0.8 - 0.6 * math.exp(-0.3 * 0)
    out = _layer(x[0], norm1_g[0], w_in[0], shift_mu[0], rwkv_w0[0], w_decay_up[0], rwkv_a0[0],
                 w_iclr_up[0], w_gate_up[0], k_k[0], k_a[0], r_k[0].reshape(-1), lnx_g[0],
                 lnx_b[0], lam_q1[0], lam_k1[0], lam_q2[0], lam_k2[0], subln_g[0], w_proj_a[0],
                 w_proj_b[0], w_out[0], norm2_g[0], peer_wq[0], peer_sub_keys[0], peer_u[0],
                 peer_v[0], final_g, lambda_init)
    return out[None]
```

```python
import functools
import math

import jax
import jax.numpy as jnp
from jax import lax
from jax.experimental import pallas as pl
from jax.experimental.pallas import tpu as pltpu

F32 = jnp.float32
BF16 = jnp.bfloat16
HIGHEST = lax.Precision.HIGHEST

LANES = 128
SUBLANES = 8
BF16_SUBLANES = 16

D_MODEL = 2048
RWKV_HEADS = 16
RWKV_HEAD_DIM = 64
RWKV_WIDTH = RWKV_HEADS * RWKV_HEAD_DIM
DECAY_LORA = 96
ICLR_LORA = 96
GATE_LORA = 256
LORA_PAD = 128
RWKV_COLS = 3 * RWKV_WIDTH + DECAY_LORA + ICLR_LORA + GATE_LORA
RWKV_COLS_PAD = 3 * RWKV_WIDTH + 2 * LORA_PAD + GATE_LORA
RWKV_CHUNK = 64
RWKV_BLOCK = 128
RWKV_GROUP = 4
RWKV_GROUP_LANES = RWKV_GROUP * RWKV_HEAD_DIM
DIFF_HEADS = 8
DIFF_HEAD_DIM = 64
DIFF_V_DIM = 2 * DIFF_HEAD_DIM
DIFF_WIDTH = DIFF_HEADS * DIFF_V_DIM
DIFF_COLS = 3 * DIFF_WIDTH
ATTN_K_BLOCK = 512
ATTN_Q_BLOCK = 1024
GATE_COLS = 2 * D_MODEL
PEER_HEADS = 8
PEER_HALF = 128
N_KEYS = 128
N_EXPERTS = N_KEYS * N_KEYS
PEER_TOPK = 16
NORM_EPS = 1e-6
LN_X_EPS = 64e-5
SUBLN_EPS = 1e-5
NEG_INF = -1e30


def _dot(a, b, precision=None):
    return jnp.dot(a, b, preferred_element_type=F32, precision=precision)


def _dot_nt(a, b, precision=None):
    return lax.dot_general(a, b, (((1,), (1,)), ((), ())), preferred_element_type=F32,
                           precision=precision)


def _const_spec(shape):
    nd = len(shape)
    return pl.BlockSpec(shape, lambda *_: (0,) * nd)


def _split(x):
    hi = x.astype(BF16)
    return hi, (x - hi.astype(F32)).astype(BF16)


def _dot3(a, b, nt=False):
    f = _dot_nt if nt else _dot
    ah, al = _split(a)
    bh, bl = _split(b)
    return f(ah, bh) + (f(ah, bl) + f(al, bh))


def _rmsnorm_kernel(x_ref, g_ref, o_ref, *, eps):
    x = x_ref[...]
    y = x * lax.rsqrt(jnp.mean(x * x, axis=-1, keepdims=True) + eps) * g_ref[...]
    o_ref[...] = y.astype(o_ref.dtype)


def _rmsnorm(x, g, eps, out_dtype, tm):
    s, d = x.shape
    return pl.pallas_call(
        functools.partial(_rmsnorm_kernel, eps=eps),
        grid=(s // tm,),
        in_specs=[pl.BlockSpec((tm, d), lambda i: (i, 0)), _const_spec((1, d))],
        out_specs=pl.BlockSpec((tm, d), lambda i: (i, 0)),
        out_shape=jax.ShapeDtypeStruct((s, d), out_dtype),
        compiler_params=pltpu.CompilerParams(dimension_semantics=("parallel",)),
        name="rmsnorm",
    )(x, g.reshape(1, d))


def _mm_kernel(x_ref, w_ref, o_ref):
    o_ref[...] = _dot(x_ref[...], w_ref[...]).astype(o_ref.dtype)


def _matmul(x, w, out_dtype, tm, tn, name):
    s, k = x.shape
    n = w.shape[1]
    return pl.pallas_call(
        _mm_kernel,
        grid=(s // tm, n // tn),
        in_specs=[pl.BlockSpec((tm, k), lambda i, j: (i, 0)),
                  pl.BlockSpec((k, tn), lambda i, j: (0, j))],
        out_specs=pl.BlockSpec((tm, tn), lambda i, j: (i, j)),
        out_shape=jax.ShapeDtypeStruct((s, n), out_dtype),
        compiler_params=pltpu.CompilerParams(dimension_semantics=("parallel", "parallel")),
        name=name,
    )(x, w)


def _mm_nt_kernel(w_ref, x_ref, o_ref):
    o_ref[...] = _dot_nt(w_ref[...], x_ref[...]).astype(o_ref.dtype)


def _matmul_nt(w, x, out_dtype, tn, tm, name):
    n, k = w.shape
    s = x.shape[0]
    return pl.pallas_call(
        _mm_nt_kernel,
        grid=(s // tm, n // tn),
        in_specs=[pl.BlockSpec((tn, k), lambda i, j: (j, 0)),
                  pl.BlockSpec((tm, k), lambda i, j: (i, 0))],
        out_specs=pl.BlockSpec((tn, tm), lambda i, j: (j, i)),
        out_shape=jax.ShapeDtypeStruct((n, s), out_dtype),
        compiler_params=pltpu.CompilerParams(dimension_semantics=("parallel", "parallel")),
        name=name,
    )(w, x)


def _mm_split_kernel(xh_ref, xl_ref, wh_ref, wl_ref, o_ref):
    xh = xh_ref[...]
    o_ref[...] = _dot(xh, wh_ref[...]) + (_dot(xh, wl_ref[...]) + _dot(xl_ref[...], wh_ref[...]))


def _matmul_split(xh, xl, wh, wl, tm, tn, name):
    s, k = xh.shape
    n = wh.shape[1]
    xspec = pl.BlockSpec((tm, k), lambda i, j: (i, 0))
    wspec = pl.BlockSpec((k, tn), lambda i, j: (0, j))
    return pl.pallas_call(
        _mm_split_kernel,
        grid=(s // tm, n // tn),
        in_specs=[xspec, xspec, wspec, wspec],
        out_specs=pl.BlockSpec((tm, tn), lambda i, j: (i, j)),
        out_shape=jax.ShapeDtypeStruct((s, n), F32),
        compiler_params=pltpu.CompilerParams(dimension_semantics=("parallel", "parallel")),
        name=name,
    )(xh, xl, wh, wl)


def _head_sum(x, ones_bd):
    xh, xl = _split(x)
    tiles = []
    for c in range(x.shape[1] // LANES):
        cs = slice(c * LANES, (c + 1) * LANES)
        tiles.append(_dot(xh[:, cs], ones_bd) + _dot(xl[:, cs], ones_bd))
    return jnp.concatenate(tiles, axis=1)


def _softplus(x):
    return jnp.maximum(x, 0.0) + jnp.log1p(jnp.exp(-jnp.abs(x)))


def _rwkv_kernel(p_ref, pprev_ref, mu_ref, w0_ref, wd_ref, a0_ref, wa_ref, wg_ref, kk_ref,
                 ka_ref, rk_ref, lng_ref, lnb_ref, o_ref, state_ref, y_ref):
    L = RWKV_CHUNK
    N = RWKV_HEAD_DIM
    W = RWKV_WIDTH
    G = RWKV_GROUP
    GL = RWKV_GROUP_LANES
    step = pl.program_id(0)

    @pl.when(step == 0)
    def _():
        state_ref[...] = jnp.zeros_like(state_ref)

    TB = p_ref.shape[0]
    row = lax.broadcasted_iota(jnp.int32, (TB, 1), 0)
    carry_on = jnp.where(step == 0, 0.0, 1.0)

    def shifted(c0, c1):
        p = p_ref[:, c0:c1]
        last = pprev_ref[SUBLANES - 1:SUBLANES, c0:c1] * carry_on
        prev = jnp.where(row == 0, last, pltpu.roll(p, 1, axis=0))
        return p + (prev - p) * mu_ref[:, c0:c1]

    r = shifted(0, W)
    k = shifted(W, 2 * W)
    v = shifted(2 * W, 3 * W)
    xw = shifted(3 * W, 3 * W + LORA_PAD)
    xa = shifted(3 * W + LORA_PAD, 3 * W + 2 * LORA_PAD)
    xg = shifted(3 * W + 2 * LORA_PAD, 3 * W + 2 * LORA_PAD + GATE_LORA)

    z = w0_ref[...] + _dot3(jnp.tanh(xw), wd_ref[...])
    w_log = -_softplus(-z) - 0.5
    lw = -jnp.exp(w_log)
    a = jax.nn.sigmoid(a0_ref[...] + _dot3(xa, wa_ref[...]))
    g = _dot3(jax.nn.sigmoid(xg), wg_ref[...])

    lane_i = lax.broadcasted_iota(jnp.int32, (LANES, LANES), 0) // N
    lane_j = lax.broadcasted_iota(jnp.int32, (LANES, LANES), 1) // N
    ones_bd = jnp.where(lane_i == lane_j, 1.0, 0.0).astype(BF16)

    kk = k * kk_ref[...]
    kk = kk / jnp.maximum(jnp.sqrt(_head_sum(kk * kk, ones_bd)), 1e-12)
    k = k * (1.0 + (a - 1.0) * ka_ref[...])

    ti = lax.broadcasted_iota(jnp.int32, (TB, TB), 0)
    tj = lax.broadcasted_iota(jnp.int32, (TB, TB), 1)
    tril = jnp.where((tj <= ti) & (tj // L == ti // L), 1.0, 0.0).astype(F32)
    gi = lax.broadcasted_iota(jnp.int32, (L, GL), 0)
    gj = lax.broadcasted_iota(jnp.int32, (L, GL), 1) % L
    incl = gj <= gi
    strict = gj < gi
    eye4 = jnp.where(gj == gi, 1.0, 0.0).astype(F32)
    bi = lax.broadcasted_iota(jnp.int32, (GL, GL), 0) // N
    bj = lax.broadcasted_iota(jnp.int32, (GL, GL), 1) // N
    bd_mask = bi == bj

    def bd(x):
        return jnp.where(bd_mask, jnp.concatenate([x] * G, axis=0), jnp.zeros((), x.dtype))

    cum = _dot(tril, lw, HIGHEST)
    e_inv = jnp.exp(-cum)
    r_t = (r * jnp.exp(cum)).astype(BF16)
    a_t = (-kk * jnp.exp(cum - lw)).astype(BF16)
    b = kk * a
    b_t = (b * e_inv).astype(BF16)
    k_t = (k * e_inv).astype(BF16)
    v_b = v.astype(BF16)
    eye4_b = eye4.astype(BF16)
    n_grp = RWKV_HEADS // G
    states = [state_ref[grp] for grp in range(n_grp)]

    n_chunks = TB // L
    pairs = [(c, grp) for c in range(n_chunks) for grp in range(n_grp)]
    rows_of = lambda c: slice(c * L, (c + 1) * L)
    lanes_of = lambda grp: slice(grp * GL, (grp + 1) * GL)
    cum_last = [cum[(c + 1) * L - 1:(c + 1) * L, :] for c in range(n_chunks)]

    a_ab, a_ak, a_rb, a_rk = {}, {}, {}, {}
    for c, grp in pairs:
        rs, gs = rows_of(c), lanes_of(grp)
        m = _dot_nt(jnp.concatenate([a_t[rs, gs], r_t[rs, gs]], axis=0),
                    jnp.concatenate([bd(b_t[rs, gs]), bd(k_t[rs, gs])], axis=0))
        a_ab[c, grp] = jnp.where(strict, m[:L, :GL], 0.0)
        a_ak[c, grp] = jnp.where(strict, m[:L, GL:], 0.0).astype(BF16)
        a_rb[c, grp] = jnp.where(incl, m[L:, :GL], 0.0).astype(BF16)
        a_rk[c, grp] = jnp.where(incl, m[L:, GL:], 0.0).astype(BF16)

    inv = {p: eye4 + a_ab[p] for p in pairs}
    pw = {}
    for p in pairs:
        pw_b = a_ab[p].astype(BF16)
        pw[p] = _dot(pw_b, bd(pw_b))
    for _ in range(4):
        for p in pairs:
            pw_b = pw[p].astype(BF16)
            both = _dot(jnp.concatenate([pw_b, inv[p].astype(BF16)], axis=0), bd(pw_b))
            pw[p], inv[p] = both[:L], inv[p] + both[L:]
    for p in pairs:
        inv[p] = (inv[p] + _dot(inv[p].astype(BF16), bd(pw[p].astype(BF16)))).astype(BF16)

    bk = {}
    for c in range(n_chunks):
        rs = rows_of(c)
        e_tail = jnp.exp(cum_last[c] - cum[rs])
        b_w = (b[rs] * e_tail).astype(BF16)
        k_w = (k[rs] * e_tail).astype(BF16)
        for grp in range(n_grp):
            gs = lanes_of(grp)
            bk[c, grp] = _dot_nt(
                eye4_b, jnp.concatenate([bd(b_w[:, gs]), bd(k_w[:, gs])], axis=0)).astype(BF16)

    for c in range(n_chunks):
        rs = rows_of(c)
        w_last = jnp.exp(cum_last[c])
        bd_z = [bd(states[grp].astype(BF16)) for grp in range(n_grp)]
        bd_v = [bd(v_b[rs, lanes_of(grp)]) for grp in range(n_grp)]
        x = [_dot(jnp.concatenate([a_t[rs, lanes_of(grp)], a_ak[c, grp]], axis=1),
                  jnp.concatenate([bd_z[grp], bd_v[grp]], axis=0)) for grp in range(n_grp)]
        u = [_dot(inv[c, grp], bd(x[grp].astype(BF16))) for grp in range(n_grp)]
        for grp in range(n_grp):
            gs = lanes_of(grp)
            lhs = jnp.concatenate(
                [jnp.concatenate([r_t[rs, gs], a_rb[c, grp], a_rk[c, grp]], axis=1),
                 jnp.concatenate([(eye4 * w_last[:, gs]).astype(BF16), bk[c, grp]], axis=1)],
                axis=0)
            yz = _dot(lhs, jnp.concatenate([bd_z[grp], bd(u[grp].astype(BF16)), bd_v[grp]],
                                           axis=0))
            y_ref[rs, gs] = yz[:L]
            states[grp] = yz[L:]
    for grp in range(n_grp):
        state_ref[grp] = states[grp]

    y = y_ref[...]
    mean = _head_sum(y, ones_bd) * (1.0 / N)
    yc = y - mean
    var = _head_sum(yc * yc, ones_bd) * (1.0 / N)
    yn = yc * lax.rsqrt(var + LN_X_EPS) * lng_ref[...] + lnb_ref[...]
    bonus = _head_sum(r * k * rk_ref[...], ones_bd) * v
    o_ref[...] = ((yn + bonus) * g).astype(o_ref.dtype)


def _rwkv_time_mix(p, mu, w0, wd, a0, wa, wg, k_k, k_a, r_k, lnx_g, lnx_b):
    s = p.shape[0]
    L = RWKV_CHUNK
    W = RWKV_WIDTH
    row = lambda v: v.reshape(1, -1)
    consts = [row(mu), row(w0), wd, row(a0), wa, wg, row(k_k), row(k_a), row(r_k), row(lnx_g),
              row(lnx_b)]
    tb = RWKV_BLOCK
    return pl.pallas_call(
        _rwkv_kernel,
        grid=(s // tb,),
        in_specs=[pl.BlockSpec((tb, RWKV_COLS_PAD), lambda i: (i, 0)),
                  pl.BlockSpec((SUBLANES, RWKV_COLS_PAD),
                               lambda i: (jnp.maximum(i * (tb // SUBLANES) - 1, 0), 0))]
                 + [_const_spec(c.shape) for c in consts],
        out_specs=pl.BlockSpec((tb, W), lambda i: (i, 0)),
        out_shape=jax.ShapeDtypeStruct((s, W), BF16),
        scratch_shapes=[pltpu.VMEM((RWKV_HEADS // RWKV_GROUP, RWKV_HEAD_DIM, RWKV_GROUP_LANES),
                                   F32),
                        pltpu.VMEM((tb, W), F32)],
        compiler_params=pltpu.CompilerParams(dimension_semantics=("arbitrary",)),
        name="rwkv7",
    )(p, p, *consts)


def _diffattn_kernel(q_ref, k_ref, vt_ref, lq1_ref, lk1_ref, lq2_ref, lk2_ref, g_ref, o_ref,
                     sa_ref, sb_ref, mx_ref, m_ref, acc_ref, *, lambda_init):
    TQ = ATTN_Q_BLOCK
    TK = ATTN_K_BLOCK
    DV = DIFF_V_DIM
    qi = pl.program_id(1)
    q = q_ref[...]
    lane = lax.broadcasted_iota(jnp.int32, q.shape, 1)
    zero = jnp.zeros_like(q)
    q_halves = (jnp.where(lane < DIFF_HEAD_DIM, q, zero), jnp.where(lane >= DIFF_HEAD_DIM, q, zero))
    ones = jnp.ones((BF16_SUBLANES, TK), BF16)
    key_i = lax.broadcasted_iota(jnp.int32, (TK, TQ), 0)
    qry_i = lax.broadcasted_iota(jnp.int32, (TK, TQ), 1)

    s_bufs = (sa_ref, sb_ref)

    def scores(j, buf, key_offset=None):
        kb = k_ref[pl.ds(pl.multiple_of(j * TK, TK), TK), :]
        for idx in range(2):
            s = _dot_nt(kb, q_halves[idx])
            if key_offset is not None:
                s = jnp.where(key_i + key_offset <= qry_i, s, NEG_INF)
            s_bufs[buf][idx] = s
            mx_ref[2 * buf + idx] = jnp.max(s, axis=0, keepdims=True)

    def probs(buf):
        out = []
        for idx in range(2):
            m_old = m_ref[idx]
            m_new = jnp.maximum(m_old, mx_ref[2 * buf + idx])
            m_ref[idx] = m_new
            out.append((jnp.exp2(s_bufs[buf][idx] - m_new).astype(BF16),
                        jnp.exp2(m_old - m_new)))
        return out

    def accumulate(j, pa):
        vt = vt_ref[:, pl.ds(pl.multiple_of(j * TK, TK), TK)]
        vext = jnp.concatenate([vt, ones], axis=0)
        for idx, (p, alpha) in enumerate(pa):
            acc_ref[idx] = alpha * acc_ref[idx] + _dot(vext, p)

    def step(j, buf, next_scores):
        pa = probs(buf)
        next_scores()
        accumulate(j, pa)

    m_ref[...] = jnp.full_like(m_ref, NEG_INF)
    acc_ref[...] = jnp.zeros_like(acc_ref)

    @pl.when(qi > 0)
    def _():
        scores(0, 0)

    @pl.loop(0, qi - 1)
    def _(i):
        step(2 * i, 0, lambda: scores(2 * i + 1, 1))
        step(2 * i + 1, 1, lambda: scores(2 * i + 2, 0))

    @pl.when(qi > 0)
    def _():
        step(2 * qi - 2, 0, lambda: scores(2 * qi - 1, 1))
        step(2 * qi - 1, 1, lambda: scores(2 * qi, 0, key_offset=0))

    @pl.when(qi == 0)
    def _():
        scores(0, 0, key_offset=0)

    step(2 * qi, 0, lambda: scores(2 * qi + 1, 1, key_offset=TK))
    step(2 * qi + 1, 1, lambda: None)

    lam = (jnp.exp(jnp.sum(lq1_ref[...] * lk1_ref[...], axis=-1, keepdims=True))
           - jnp.exp(jnp.sum(lq2_ref[...] * lk2_ref[...], axis=-1, keepdims=True))
           + lambda_init)
    o = (acc_ref[0, :DV, :] / acc_ref[0, DV:DV + 1, :]
         - lam * (acc_ref[1, :DV, :] / acc_ref[1, DV:DV + 1, :]))
    o = o * lax.rsqrt(jnp.mean(o * o, axis=0, keepdims=True) + SUBLN_EPS) * g_ref[...]
    o_ref[...] = (o * (1.0 - lambda_init)).T.astype(o_ref.dtype)


def _diff_attention(qk, vt, lq1, lk1, lq2, lk2, subln_g, lambda_init):
    s = qk.shape[0]
    H = DIFF_HEADS
    TQ = ATTN_Q_BLOCK
    TK = ATTN_K_BLOCK
    DV = DIFF_V_DIM
    assert TQ == 2 * TK and s % TQ == 0
    row = lambda v: v.reshape(1, -1)
    lam_specs = [_const_spec((1, DIFF_HEAD_DIM))] * 4
    return pl.pallas_call(
        functools.partial(_diffattn_kernel, lambda_init=lambda_init),
        grid=(H, s // TQ),
        in_specs=[pl.BlockSpec((TQ, DV), lambda h, qi: (qi, h)),
                  pl.BlockSpec((s, DV), lambda h, qi: (0, H + h)),
                  pl.BlockSpec((DV, s), lambda h, qi: (h, 0))]
                 + lam_specs + [_const_spec((DV, 1))],
        out_specs=pl.BlockSpec((TQ, DV), lambda h, qi: (qi, h)),
        out_shape=jax.ShapeDtypeStruct((s, DIFF_WIDTH), BF16),
        scratch_shapes=[pltpu.VMEM((2, TK, TQ), F32), pltpu.VMEM((2, TK, TQ), F32),
                        pltpu.VMEM((4, 1, TQ), F32), pltpu.VMEM((2, 1, TQ), F32),
                        pltpu.VMEM((2, DV + BF16_SUBLANES, TQ), F32)],
        compiler_params=pltpu.CompilerParams(dimension_semantics=("parallel", "arbitrary"),
                                             vmem_limit_bytes=40 * 1024 * 1024),
        name="diff_attention",
    )(qk, qk, vt, row(lq1), row(lk1), row(lq2), row(lk2), subln_g.reshape(DV, 1))


def _merge_kernel(ya_ref, yb_ref, gate_ref, x_ref, wa_ref, wb_ref, wo_ref, g2_ref,
                  h_ref, hh_ref, hl_ref, ht_ref):
    pa = _dot(ya_ref[...], wa_ref[...])
    pb = _dot(yb_ref[...], wb_ref[...])
    ga = jax.nn.sigmoid(gate_ref[:, :D_MODEL].astype(F32))
    gb = jax.nn.sigmoid(gate_ref[:, D_MODEL:].astype(F32))
    merged = ga * pa + gb * pb
    h = x_ref[...] + _dot(merged.astype(BF16), wo_ref[...])
    h_ref[...] = h
    hn = h * lax.rsqrt(jnp.mean(h * h, axis=-1, keepdims=True) + NORM_EPS) * g2_ref[...]
    hi = hn.astype(BF16)
    hh_ref[...] = hi
    hl_ref[...] = (hn - hi.astype(F32)).astype(BF16)
    ht_ref[...] = hn.T.astype(BF16)


def _merge(ya, yb, gate, x, wa, wb, wo, g2, tm):
    s = x.shape[0]
    D = D_MODEL
    rows = lambda w: pl.BlockSpec((tm, w), lambda i: (i, 0))
    single = lambda shape: pl.BlockSpec(shape, lambda i: (0, 0), pipeline_mode=pl.Buffered(1))
    return pl.pallas_call(
        _merge_kernel,
        grid=(s // tm,),
        in_specs=[rows(RWKV_WIDTH), rows(DIFF_WIDTH), rows(GATE_COLS), rows(D),
                  single((RWKV_WIDTH, D)), single((DIFF_WIDTH, D)), single((D, D)),
                  _const_spec((1, D))],
        out_specs=[rows(D), rows(D), rows(D), pl.BlockSpec((D, tm), lambda i: (0, i))],
        out_shape=[jax.ShapeDtypeStruct((s, D), F32), jax.ShapeDtypeStruct((s, D), BF16),
                   jax.ShapeDtypeStruct((s, D), BF16), jax.ShapeDtypeStruct((D, s), BF16)],
        compiler_params=pltpu.CompilerParams(dimension_semantics=("parallel",),
                                             vmem_limit_bytes=56 * 1024 * 1024),
        name="merge_out_proj",
    )(ya, yb, gate, x, wa, wb, wo, g2.reshape(1, D))


def _cmp_exchange(xs, i, l, descending):
    hi = jnp.maximum(xs[i], xs[l])
    lo = jnp.minimum(xs[i], xs[l])
    xs[i], xs[l] = (hi, lo) if descending else (lo, hi)


def _bitonic_merge_desc(xs):
    xs = list(xs)
    n = len(xs)
    j = n // 2
    while j >= 1:
        for i in range(n):
            l = i ^ j
            if l > i:
                _cmp_exchange(xs, i, l, True)
        j //= 2
    return xs


def _bitonic_sort_desc(xs):
    xs = list(xs)
    n = len(xs)
    k = 2
    while k <= n:
        j = k // 2
        while j >= 1:
            for i in range(n):
                l = i ^ j
                if l > i:
                    _cmp_exchange(xs, i, l, (i & k) == 0)
            j //= 2
        k *= 2
    return xs


def _merge_top(a, b):
    n = len(a)
    return _bitonic_merge_desc([jnp.maximum(a[i], b[n - 1 - i]) for i in range(n)])


def _top16_over_rows(s):
    groups = [s[g * SUBLANES:(g + 1) * SUBLANES, :] for g in range(s.shape[0] // SUBLANES)]
    top = _bitonic_sort_desc(groups)
    for shift in (4, 2, 1):
        top = _merge_top(top, [pltpu.roll(x, shift, axis=0) for x in top])
    return top


def _peer_score_kernel(q_ref, keys_ref, r2_ref, e2_ref, n_ref, d_ref):
    K = PEER_TOPK
    T = q_ref.shape[0]
    H = PEER_HEADS
    scores = []
    tops = []
    for hp in range(2 * H):
        s = _dot_nt(keys_ref[hp], q_ref[:, hp * PEER_HALF:(hp + 1) * PEER_HALF], HIGHEST)
        scores.append(s)
        tops.append(_top16_over_rows(s))
    sub = lax.broadcasted_iota(jnp.int32, (SUBLANES, T), 0)

    def by_head(p, i):
        out = tops[p][i]
        for h in range(1, H):
            out = jnp.where(sub == h, tops[2 * h + p][i], out)
        return out

    aa = [by_head(0, i) for i in range(K)]
    bb = [by_head(1, i) for i in range(K)]
    cands = [aa[i] + bb[j] for i in range(K) for j in range(K) if (i + 1) * (j + 1) <= K]
    cands += [jnp.full_like(cands[0], -jnp.inf)] * (-len(cands) % K)
    best = _bitonic_sort_desc(cands[:K])
    for c in range(K, len(cands), K):
        best = _merge_top(best, _bitonic_sort_desc(cands[c:c + K]))
    thr = best[K - 1]
    zsum = jnp.zeros_like(thr)
    for t in best:
        zsum = zsum + jnp.exp(t - best[0])
    inv_z = 1.0 / zsum
    for h in range(H):
        hs = slice(h, h + 1)
        s1, s2 = scores[2 * h], scores[2 * h + 1]
        thr_h = thr[hs]
        cnt = jnp.zeros_like(s1)
        rank = jnp.zeros_like(s2)
        for m in range(K):
            cnt = cnt + jnp.where(s1 + bb[m][hs] >= thr_h, 1.0, 0.0)
            rank = rank + jnp.where(bb[m][hs] > s2, 1.0, 0.0)
        n_ref[h] = cnt
        r2_ref[h] = rank.astype(BF16)
        d_ref[h] = jnp.exp(s1 - aa[0][hs]) * inv_z[hs]
        e2_ref[h] = jnp.exp(s2 - bb[0][hs]).astype(BF16)


def _peer_scores(q, keys, tt):
    s = q.shape[0]
    H = PEER_HEADS
    out = lambda dtype: jax.ShapeDtypeStruct((H, N_KEYS, s), dtype)
    ospec = pl.BlockSpec((H, N_KEYS, tt), lambda i: (0, 0, i))
    return pl.pallas_call(
        _peer_score_kernel,
        grid=(s // tt,),
        in_specs=[pl.BlockSpec((tt, 2 * H * PEER_HALF), lambda i: (i, 0)),
                  _const_spec((2 * H, N_KEYS, PEER_HALF))],
        out_specs=[ospec] * 4,
        out_shape=[out(BF16), out(BF16), out(F32), out(F32)],
        compiler_params=pltpu.CompilerParams(dimension_semantics=("parallel",)),
        name="peer_scores",
    )(q, keys)


def _peer_expert_kernel(hnt_ref, h1_ref, r2_ref, e2_ref, n_ref, d_ref, u_ref, vt_ref, fg_ref,
                        o_ref, acc_ref):
    e = pl.program_id(1)
    eb = u_ref.shape[0]
    tt = hnt_ref.shape[1]
    ni = eb // N_KEYS
    slab = BF16_SUBLANES
    strip = 2 * LANES

    @pl.when(e == 0)
    def _():
        acc_ref[...] = jnp.zeros_like(acc_ref)

    groups = []
    for ii in range(ni):
        i = e * ni + ii
        rows = slice(ii * N_KEYS, (ii + 1) * N_KEYS)
        pre = _dot(u_ref[rows, :], hnt_ref[...])
        act = (0.5 * pre * (1.0 + lax.erf(pre * math.sqrt(0.5)))).astype(BF16)
        strips = []
        for t0 in range(0, tt, strip):
            ts = slice(t0, t0 + strip)
            nb = [jnp.broadcast_to(n_ref[h, pl.ds(i, 1), ts], (slab, strip)).astype(BF16)
                  for h in range(PEER_HEADS)]
            db = [jnp.broadcast_to(d_ref[h, pl.ds(i, 1), ts], (slab, strip)).astype(BF16)
                  for h in range(PEER_HEADS)]
            slabs = []
            for j0 in range(0, N_KEYS, slab):
                js = slice(j0, j0 + slab)
                gate = None
                for h in range(PEER_HEADS):
                    term = jnp.where(r2_ref[h, js, ts] < nb[h], e2_ref[h, js, ts],
                                     jnp.zeros((), BF16)) * db[h]
                    gate = term if gate is None else gate + term
                slabs.append(gate * act[js, ts])
            strips.append(jnp.concatenate(slabs, axis=0))
        groups.append(jnp.concatenate(strips, axis=1))
    acc_ref[...] += _dot(vt_ref[...], jnp.concatenate(groups, axis=0))

    @pl.when(e == pl.num_programs(1) - 1)
    def _():
        h = h1_ref[...] + acc_ref[...].T
        o_ref[...] = h * lax.rsqrt(jnp.mean(h * h, axis=-1, keepdims=True) + NORM_EPS) * fg_ref[...]


def _peer_experts(hnt, h1, r2, e2, n, d, u, vt, final_g, tt, eb):
    s = hnt.shape[1]
    D = D_MODEL
    H = PEER_HEADS
    sel = pl.BlockSpec((H, N_KEYS, tt), lambda i, e: (0, 0, i), pipeline_mode=pl.Buffered(1))
    return pl.pallas_call(
        _peer_expert_kernel,
        grid=(s // tt, N_EXPERTS // eb),
        in_specs=[pl.BlockSpec((D, tt), lambda i, e: (0, i)),
                  pl.BlockSpec((tt, D), lambda i, e: (i, 0), pipeline_mode=pl.Buffered(1)),
                  sel, sel, sel, sel,
                  pl.BlockSpec((eb, D), lambda i, e: (e, 0)),
                  pl.BlockSpec((D, eb), lambda i, e: (0, e)),
                  pl.BlockSpec((1, D), lambda i, e: (0, 0))],
        out_specs=pl.BlockSpec((tt, D), lambda i, e: (i, 0)),
        out_shape=jax.ShapeDtypeStruct((s, D), F32),
        scratch_shapes=[pltpu.VMEM((D, tt), F32)],
        compiler_params=pltpu.CompilerParams(dimension_semantics=("parallel", "arbitrary"),
                                             vmem_limit_bytes=56 * 1024 * 1024),
        name="peer_experts",
    )(hnt, h1, r2, e2, n, d, u, vt, final_g.reshape(1, D))


def _pad_rows(w, rows):
    return jnp.pad(w, ((0, rows - w.shape[0]), (0, 0)))


def _split_bf16(w):
    hi = w.astype(BF16)
    return hi, (w - hi.astype(F32)).astype(BF16)


def _layer(h, norm1_g, w_in, shift_mu, rwkv_w0, w_decay_up, rwkv_a0, w_iclr_up, w_gate_up,
           k_k, k_a, r_k, lnx_g, lnx_b, lam_q1, lam_k1, lam_q2, lam_k2, subln_g, w_proj_a,
           w_proj_b, w_out, norm2_g, peer_wq, peer_sub_keys, peer_u, peer_v, out_g, lambda_init):
    s = h.shape[0]
    W = RWKV_WIDTH
    tm = min(s, 512)
    tmm = min(s, 1024)

    c0, c1, c2 = 3 * W, 3 * W + DECAY_LORA, 3 * W + DECAY_LORA + ICLR_LORA
    pad_cols = lambda m, n: jnp.pad(m, ((0, 0), (0, n - m.shape[1])))
    w_rwkv = jnp.concatenate([w_in[:, :c0], pad_cols(w_in[:, c0:c1], LORA_PAD),
                              pad_cols(w_in[:, c1:c2], LORA_PAD), w_in[:, c2:RWKV_COLS]], axis=1)
    mu2 = shift_mu.reshape(1, -1)
    mu = jnp.concatenate([mu2[:, :c0], pad_cols(mu2[:, c0:c1], LORA_PAD),
                          pad_cols(mu2[:, c1:c2], LORA_PAD), mu2[:, c2:]], axis=1)
    d0 = RWKV_COLS
    q_scale = DIFF_HEAD_DIM ** -0.5 * math.log2(math.e)
    w_qk = jnp.concatenate([w_in[:, d0:d0 + DIFF_WIDTH] * q_scale,
                            w_in[:, d0 + DIFF_WIDTH:d0 + 2 * DIFF_WIDTH]], axis=1)
    w_vt = w_in[:, d0 + 2 * DIFF_WIDTH:d0 + DIFF_COLS].T
    w_gate = w_in[:, d0 + DIFF_COLS:]

    xn = _rmsnorm(h, norm1_g, NORM_EPS, BF16, tm)
    p_rwkv = _matmul(xn, w_rwkv.astype(BF16), F32, tmm, 512, "in_proj_rwkv")
    p_qk = _matmul(xn, w_qk.astype(BF16), BF16, tmm, 512, "in_proj_qk")
    p_vt = _matmul_nt(w_vt.astype(BF16), xn, BF16, 256, tmm, "in_proj_vt")
    p_gate = _matmul(xn, w_gate.astype(BF16), BF16, tmm, 512, "in_proj_gate")

    y_a = _rwkv_time_mix(p_rwkv, mu, rwkv_w0, _pad_rows(w_decay_up, LORA_PAD), rwkv_a0,
                         _pad_rows(w_iclr_up, LORA_PAD), w_gate_up, k_k, k_a, r_k, lnx_g, lnx_b)
    y_b = _diff_attention(p_qk, p_vt, lam_q1, lam_k1, lam_q2, lam_k2, subln_g, lambda_init)
    h1, hn_hi, hn_lo, hn_t = _merge(y_a, y_b, p_gate, h, w_proj_a.astype(BF16),
                                    w_proj_b.astype(BF16), w_out.astype(BF16), norm2_g,
                                    min(s, 256))

    wq_hi, wq_lo = _split_bf16(peer_wq)
    q = _matmul_split(hn_hi, hn_lo, wq_hi, wq_lo, tmm, 512, "peer_query")
    keys = peer_sub_keys.reshape(2 * PEER_HEADS, N_KEYS, PEER_HALF)
    r2, e2, n, d = _peer_scores(q, keys, min(s, 256))
    return _peer_experts(hn_t, h1, r2, e2, n, d, peer_u.astype(BF16), peer_v.T.astype(BF16),
                         out_g, min(s, 512), 1024)


def kernel(x, norm1_g, w_in, shift_mu, rwkv_w0, w_decay_up, rwkv_a0, w_iclr_up, w_gate_up, k_k, k_a, r_k, lnx_g, lnx_b, lam_q1, lam_k1, lam_q2, lam_k2, subln_g, w_proj_a, w_proj_b, w_out, norm2_g, peer_wq, peer_sub_keys, peer_u, peer_v, final_g):
    B, S, D = x.shape
    assert B == 1 and D == D_MODEL and norm1_g.shape[0] == 1
    lambda_init = 0.8 - 0.6 * math.exp(-0.3 * 0)
    out = _layer(x[0], norm1_g[0], w_in[0], shift_mu[0], rwkv_w0[0], w_decay_up[0], rwkv_a0[0],
                 w_iclr_up[0], w_gate_up[0], k_k[0], k_a[0], r_k[0].reshape(-1), lnx_g[0],
                 lnx_b[0], lam_q1[0], lam_k1[0], lam_q2[0], lam_k2[0], subln_g[0], w_proj_a[0],
                 w_proj_b[0], w_out[0], norm2_g[0], peer_wq[0], peer_sub_keys[0], peer_u[0],
                 peer_v[0], final_g, lambda_init)
    return out[None]
```

```python
import functools
import math

import jax
import jax.numpy as jnp
from jax import lax
from jax.experimental import pallas as pl
from jax.experimental.pallas import tpu as pltpu

F32 = jnp.float32
BF16 = jnp.bfloat16
HIGHEST = lax.Precision.HIGHEST

LANES = 128
SUBLANES = 8
BF16_SUBLANES = 16

D_MODEL = 2048
RWKV_HEADS = 16
RWKV_HEAD_DIM = 64
RWKV_WIDTH = RWKV_HEADS * RWKV_HEAD_DIM
DECAY_LORA = 96
ICLR_LORA = 96
GATE_LORA = 256
LORA_PAD = 128
RWKV_COLS = 3 * RWKV_WIDTH + DECAY_LORA + ICLR_LORA + GATE_LORA
RWKV_COLS_PAD = 3 * RWKV_WIDTH + 2 * LORA_PAD + GATE_LORA
RWKV_CHUNK = 64
RWKV_BLOCK = 128
RWKV_GROUP = 4
RWKV_GROUP_LANES = RWKV_GROUP * RWKV_HEAD_DIM
DIFF_HEADS = 8
DIFF_HEAD_DIM = 64
DIFF_V_DIM = 2 * DIFF_HEAD_DIM
DIFF_WIDTH = DIFF_HEADS * DIFF_V_DIM
DIFF_COLS = 3 * DIFF_WIDTH
ATTN_K_BLOCK = 512
ATTN_Q_BLOCK = 1024
GATE_COLS = 2 * D_MODEL
PEER_HEADS = 8
PEER_HALF = 128
N_KEYS = 128
N_EXPERTS = N_KEYS * N_KEYS
PEER_TOPK = 16
NORM_EPS = 1e-6
LN_X_EPS = 64e-5
SUBLN_EPS = 1e-5
NEG_INF = -1e30


def _dot(a, b, precision=None):
    return jnp.dot(a, b, preferred_element_type=F32, precision=precision)


def _dot_nt(a, b, precision=None):
    return lax.dot_general(a, b, (((1,), (1,)), ((), ())), preferred_element_type=F32,
                           precision=precision)


def _const_spec(shape):
    nd = len(shape)
    return pl.BlockSpec(shape, lambda *_: (0,) * nd)


def _split(x):
    hi = x.astype(BF16)
    return hi, (x - hi.astype(F32)).astype(BF16)


def _dot3(a, b, nt=False):
    f = _dot_nt if nt else _dot
    ah, al = _split(a)
    bh, bl = _split(b)
    return f(ah, bh) + (f(ah, bl) + f(al, bh))


def _rmsnorm_kernel(x_ref, g_ref, o_ref, *, eps):
    x = x_ref[...]
    y = x * lax.rsqrt(jnp.mean(x * x, axis=-1, keepdims=True) + eps) * g_ref[...]
    o_ref[...] = y.astype(o_ref.dtype)


def _rmsnorm(x, g, eps, out_dtype, tm):
    s, d = x.shape
    return pl.pallas_call(
        functools.partial(_rmsnorm_kernel, eps=eps),
        grid=(s // tm,),
        in_specs=[pl.BlockSpec((tm, d), lambda i: (i, 0)), _const_spec((1, d))],
        out_specs=pl.BlockSpec((tm, d), lambda i: (i, 0)),
        out_shape=jax.ShapeDtypeStruct((s, d), out_dtype),
        compiler_params=pltpu.CompilerParams(dimension_semantics=("parallel",)),
        name="rmsnorm",
    )(x, g.reshape(1, d))


def _mm_kernel(x_ref, w_ref, o_ref):
    o_ref[...] = _dot(x_ref[...], w_ref[...]).astype(o_ref.dtype)


def _matmul(x, w, out_dtype, tm, tn, name):
    s, k = x.shape
    n = w.shape[1]
    return pl.pallas_call(
        _mm_kernel,
        grid=(s // tm, n // tn),
        in_specs=[pl.BlockSpec((tm, k), lambda i, j: (i, 0)),
                  pl.BlockSpec((k, tn), lambda i, j: (0, j))],
        out_specs=pl.BlockSpec((tm, tn), lambda i, j: (i, j)),
        out_shape=jax.ShapeDtypeStruct((s, n), out_dtype),
        compiler_params=pltpu.CompilerParams(dimension_semantics=("parallel", "parallel")),
        name=name,
    )(x, w)


def _mm_nt_kernel(w_ref, x_ref, o_ref):
    o_ref[...] = _dot_nt(w_ref[...], x_ref[...]).astype(o_ref.dtype)


def _matmul_nt(w, x, out_dtype, tn, tm, name):
    n, k = w.shape
    s = x.shape[0]
    return pl.pallas_call(
        _mm_nt_kernel,
        grid=(s // tm, n // tn),
        in_specs=[pl.BlockSpec((tn, k), lambda i, j: (j, 0)),
                  pl.BlockSpec((tm, k), lambda i, j: (i, 0))],
        out_specs=pl.BlockSpec((tn, tm), lambda i, j: (j, i)),
        out_shape=jax.ShapeDtypeStruct((n, s), out_dtype),
        compiler_params=pltpu.CompilerParams(dimension_semantics=("parallel", "parallel")),
        name=name,
    )(w, x)


def _mm_split_kernel(xh_ref, xl_ref, wh_ref, wl_ref, o_ref):
    xh = xh_ref[...]
    o_ref[...] = _dot(xh, wh_ref[...]) + (_dot(xh, wl_ref[...]) + _dot(xl_ref[...], wh_ref[...]))


def _matmul_split(xh, xl, wh, wl, tm, tn, name):
    s, k = xh.shape
    n = wh.shape[1]
    xspec = pl.BlockSpec((tm, k), lambda i, j: (i, 0))
    wspec = pl.BlockSpec((k, tn), lambda i, j: (0, j))
    return pl.pallas_call(
        _mm_split_kernel,
        grid=(s // tm, n // tn),
        in_specs=[xspec, xspec, wspec, wspec],
        out_specs=pl.BlockSpec((tm, tn), lambda i, j: (i, j)),
        out_shape=jax.ShapeDtypeStruct((s, n), F32),
        compiler_params=pltpu.CompilerParams(dimension_semantics=("parallel", "parallel")),
        name=name,
    )(xh, xl, wh, wl)


def _head_sum(x, ones_bd):
    xh, xl = _split(x)
    tiles = []
    for c in range(x.shape[1] // LANES):
        cs = slice(c * LANES, (c + 1) * LANES)
        tiles.append(_dot(xh[:, cs], ones_bd) + _dot(xl[:, cs], ones_bd))
    return jnp.concatenate(tiles, axis=1)


def _softplus(x):
    return jnp.maximum(x, 0.0) + jnp.log1p(jnp.exp(-jnp.abs(x)))


def _rwkv_kernel(p_ref, pprev_ref, mu_ref, w0_ref, wd_ref, a0_ref, wa_ref, wg_ref, kk_ref,
                 ka_ref, rk_ref, lng_ref, lnb_ref, o_ref, state_ref, y_ref):
    L = RWKV_CHUNK
    N = RWKV_HEAD_DIM
    W = RWKV_WIDTH
    G = RWKV_GROUP
    GL = RWKV_GROUP_LANES
    step = pl.program_id(0)

    @pl.when(step == 0)
    def _():
        state_ref[...] = jnp.zeros_like(state_ref)

    TB = p_ref.shape[0]
    row = lax.broadcasted_iota(jnp.int32, (TB, 1), 0)
    carry_on = jnp.where(step == 0, 0.0, 1.0)

    def shifted(c0, c1):
        p = p_ref[:, c0:c1]
        last = pprev_ref[SUBLANES - 1:SUBLANES, c0:c1] * carry_on
        prev = jnp.where(row == 0, last, pltpu.roll(p, 1, axis=0))
        return p + (prev - p) * mu_ref[:, c0:c1]

    r = shifted(0, W)
    k = shifted(W, 2 * W)
    v = shifted(2 * W, 3 * W)
    xw = shifted(3 * W, 3 * W + LORA_PAD)
    xa = shifted(3 * W + LORA_PAD, 3 * W + 2 * LORA_PAD)
    xg = shifted(3 * W + 2 * LORA_PAD, 3 * W + 2 * LORA_PAD + GATE_LORA)

    z = w0_ref[...] + _dot3(jnp.tanh(xw), wd_ref[...])
    w_log = -_softplus(-z) - 0.5
    lw = -jnp.exp(w_log)
    a = jax.nn.sigmoid(a0_ref[...] + _dot3(xa, wa_ref[...]))
    g = _dot3(jax.nn.sigmoid(xg), wg_ref[...])

    lane_i = lax.broadcasted_iota(jnp.int32, (LANES, LANES), 0) // N
    lane_j = lax.broadcasted_iota(jnp.int32, (LANES, LANES), 1) // N
    ones_bd = jnp.where(lane_i == lane_j, 1.0, 0.0).astype(BF16)

    kk = k * kk_ref[...]
    kk = kk / jnp.maximum(jnp.sqrt(_head_sum(kk * kk, ones_bd)), 1e-12)
    k = k * (1.0 + (a - 1.0) * ka_ref[...])

    ti = lax.broadcasted_iota(jnp.int32, (TB, TB), 0)
    tj = lax.broadcasted_iota(jnp.int32, (TB, TB), 1)
    tril = jnp.where((tj <= ti) & (tj // L == ti // L), 1.0, 0.0).astype(F32)
    gi = lax.broadcasted_iota(jnp.int32, (L, GL), 0)
    gj = lax.broadcasted_iota(jnp.int32, (L, GL), 1) % L
    incl = gj <= gi
    strict = gj < gi
    eye4 = jnp.where(gj == gi, 1.0, 0.0).astype(F32)
    bi = lax.broadcasted_iota(jnp.int32, (GL, GL), 0) // N
    bj = lax.broadcasted_iota(jnp.int32, (GL, GL), 1) // N
    bd_mask = bi == bj

    def bd(x):
        return jnp.where(bd_mask, jnp.concatenate([x] * G, axis=0), jnp.zeros((), x.dtype))

    cum = _dot(tril, lw, HIGHEST)
    e_inv = jnp.exp(-cum)
    r_t = (r * jnp.exp(cum)).astype(BF16)
    a_t = (-kk * jnp.exp(cum - lw)).astype(BF16)
    b = kk * a
    b_t = (b * e_inv).astype(BF16)
    k_t = (k * e_inv).astype(BF16)
    v_b = v.astype(BF16)
    eye4_b = eye4.astype(BF16)
    n_grp = RWKV_HEADS // G
    states = [state_ref[grp] for grp in range(n_grp)]

    n_chunks = TB // L
    pairs = [(c, grp) for c in range(n_chunks) for grp in range(n_grp)]
    rows_of = lambda c: slice(c * L, (c + 1) * L)
    lanes_of = lambda grp: slice(grp * GL, (grp + 1) * GL)
    cum_last = [cum[(c + 1) * L - 1:(c + 1) * L, :] for c in range(n_chunks)]

    a_ab, a_ak, a_rb, a_rk = {}, {}, {}, {}
    for c, grp in pairs:
        rs, gs = rows_of(c), lanes_of(grp)
        m = _dot_nt(jnp.concatenate([a_t[rs, gs], r_t[rs, gs]], axis=0),
                    jnp.concatenate([bd(b_t[rs, gs]), bd(k_t[rs, gs])], axis=0))
        a_ab[c, grp] = jnp.where(strict, m[:L, :GL], 0.0)
        a_ak[c, grp] = jnp.where(strict, m[:L, GL:], 0.0).astype(BF16)
        a_rb[c, grp] = jnp.where(incl, m[L:, :GL], 0.0).astype(BF16)
        a_rk[c, grp] = jnp.where(incl, m[L:, GL:], 0.0).astype(BF16)

    inv = {p: eye4 + a_ab[p] for p in pairs}
    pw = {}
    for p in pairs:
        pw_b = a_ab[p].astype(BF16)
        pw[p] = _dot(pw_b, bd(pw_b))
    for _ in range(4):
        for p in pairs:
            pw_b = pw[p].astype(BF16)
            both = _dot(jnp.concatenate([pw_b, inv[p].astype(BF16)], axis=0), bd(pw_b))
            pw[p], inv[p] = both[:L], inv[p] + both[L:]
    for p in pairs:
        inv[p] = (inv[p] + _dot(inv[p].astype(BF16), bd(pw[p].astype(BF16)))).astype(BF16)

    bk = {}
    for c in range(n_chunks):
        rs = rows_of(c)
        e_tail = jnp.exp(cum_last[c] - cum[rs])
        b_w = (b[rs] * e_tail).astype(BF16)
        k_w = (k[rs] * e_tail).astype(BF16)
        for grp in range(n_grp):
            gs = lanes_of(grp)
            bk[c, grp] = _dot_nt(
                eye4_b, jnp.concatenate([bd(b_w[:, gs]), bd(k_w[:, gs])], axis=0)).astype(BF16)

    for c in range(n_chunks):
        rs = rows_of(c)
        w_last = jnp.exp(cum_last[c])
        bd_z = [bd(states[grp].astype(BF16)) for grp in range(n_grp)]
        bd_v = [bd(v_b[rs, lanes_of(grp)]) for grp in range(n_grp)]
        x = [_dot(jnp.concatenate([a_t[rs, lanes_of(grp)], a_ak[c, grp]], axis=1),
                  jnp.concatenate([bd_z[grp], bd_v[grp]], axis=0)) for grp in range(n_grp)]
        u = [_dot(inv[c, grp], bd(x[grp].astype(BF16))) for grp in range(n_grp)]
        for grp in range(n_grp):
            gs = lanes_of(grp)
            lhs = jnp.concatenate(
                [jnp.concatenate([r_t[rs, gs], a_rb[c, grp], a_rk[c, grp]], axis=1),
                 jnp.concatenate([(eye4 * w_last[:, gs]).astype(BF16), bk[c, grp]], axis=1)],
                axis=0)
            yz = _dot(lhs, jnp.concatenate([bd_z[grp], bd(u[grp].astype(BF16)), bd_v[grp]],
                                           axis=0))
            y_ref[rs, gs] = yz[:L]
            states[grp] = yz[L:]
    for grp in range(n_grp):
        state_ref[grp] = states[grp]

    y = y_ref[...]
    mean = _head_sum(y, ones_bd) * (1.0 / N)
    yc = y - mean
    var = _head_sum(yc * yc, ones_bd) * (1.0 / N)
    yn = yc * lax.rsqrt(var + LN_X_EPS) * lng_ref[...] + lnb_ref[...]
    bonus = _head_sum(r * k * rk_ref[...], ones_bd) * v
    o_ref[...] = ((yn + bonus) * g).astype(o_ref.dtype)


def _rwkv_time_mix(p, mu, w0, wd, a0, wa, wg, k_k, k_a, r_k, lnx_g, lnx_b):
    s = p.shape[0]
    L = RWKV_CHUNK
    W = RWKV_WIDTH
    row = lambda v: v.reshape(1, -1)
    consts = [row(mu), row(w0), wd, row(a0), wa, wg, row(k_k), row(k_a), row(r_k), row(lnx_g),
              row(lnx_b)]
    tb = RWKV_BLOCK
    return pl.pallas_call(
        _rwkv_kernel,
        grid=(s // tb,),
        in_specs=[pl.BlockSpec((tb, RWKV_COLS_PAD), lambda i: (i, 0)),
                  pl.BlockSpec((SUBLANES, RWKV_COLS_PAD),
                               lambda i: (jnp.maximum(i * (tb // SUBLANES) - 1, 0), 0))]
                 + [_const_spec(c.shape) for c in consts],
        out_specs=pl.BlockSpec((tb, W), lambda i: (i, 0)),
        out_shape=jax.ShapeDtypeStruct((s, W), BF16),
        scratch_shapes=[pltpu.VMEM((RWKV_HEADS // RWKV_GROUP, RWKV_HEAD_DIM, RWKV_GROUP_LANES),
                                   F32),
                        pltpu.VMEM((tb, W), F32)],
        compiler_params=pltpu.CompilerParams(dimension_semantics=("arbitrary",)),
        name="rwkv7",
    )(p, p, *consts)


def _diffattn_kernel(q_ref, k_ref, vt_ref, lq1_ref, lk1_ref, lq2_ref, lk2_ref, g_ref, o_ref,
                     sa_ref, sb_ref, mx_ref, m_ref, acc_ref, *, lambda_init):
    TQ = ATTN_Q_BLOCK
    TK = ATTN_K_BLOCK
    DV = DIFF_V_DIM
    qi = pl.program_id(1)
    q = q_ref[...]
    lane = lax.broadcasted_iota(jnp.int32, q.shape, 1)
    zero = jnp.zeros_like(q)
    q_halves = (jnp.where(lane < DIFF_HEAD_DIM, q, zero), jnp.where(lane >= DIFF_HEAD_DIM, q, zero))
    ones = jnp.ones((BF16_SUBLANES, TK), BF16)
    key_i = lax.broadcasted_iota(jnp.int32, (TK, TQ), 0)
    qry_i = lax.broadcasted_iota(jnp.int32, (TK, TQ), 1)

    s_bufs = (sa_ref, sb_ref)

    def scores(j, buf, key_offset=None):
        kb = k_ref[pl.ds(pl.multiple_of(j * TK, TK), TK), :]
        for idx in range(2):
            s = _dot_nt(kb, q_halves[idx])
            if key_offset is not None:
                s = jnp.where(key_i + key_offset <= qry_i, s, NEG_INF)
            s_bufs[buf][idx] = s
            mx_ref[2 * buf + idx] = jnp.max(s, axis=0, keepdims=True)

    def probs(buf):
        out = []
        for idx in range(2):
            m_old = m_ref[idx]
            m_new = jnp.maximum(m_old, mx_ref[2 * buf + idx])
            m_ref[idx] = m_new
            out.append((jnp.exp2(s_bufs[buf][idx] - m_new).astype(BF16),
                        jnp.exp2(m_old - m_new)))
        return out

    def accumulate(j, pa):
        vt = vt_ref[:, pl.ds(pl.multiple_of(j * TK, TK), TK)]
        vext = jnp.concatenate([vt, ones], axis=0)
        for idx, (p, alpha) in enumerate(pa):
            acc_ref[idx] = alpha * acc_ref[idx] + _dot(vext, p)

    def step(j, buf, next_scores):
        pa = probs(buf)
        next_scores()
        accumulate(j, pa)

    m_ref[...] = jnp.full_like(m_ref, NEG_INF)
    acc_ref[...] = jnp.zeros_like(acc_ref)

    @pl.when(qi > 0)
    def _():
        scores(0, 0)

    @pl.loop(0, qi - 1)
    def _(i):
        step(2 * i, 0, lambda: scores(2 * i + 1, 1))
        step(2 * i + 1, 1, lambda: scores(2 * i + 2, 0))

    @pl.when(qi > 0)
    def _():
        step(2 * qi - 2, 0, lambda: scores(2 * qi - 1, 1))
        step(2 * qi - 1, 1, lambda: scores(2 * qi, 0, key_offset=0))

    @pl.when(qi == 0)
    def _():
        scores(0, 0, key_offset=0)

    step(2 * qi, 0, lambda: scores(2 * qi + 1, 1, key_offset=TK))
    step(2 * qi + 1, 1, lambda: None)

    lam = (jnp.exp(jnp.sum(lq1_ref[...] * lk1_ref[...], axis=-1, keepdims=True))
           - jnp.exp(jnp.sum(lq2_ref[...] * lk2_ref[...], axis=-1, keepdims=True))
           + lambda_init)
    o = (acc_ref[0, :DV, :] / acc_ref[0, DV:DV + 1, :]
         - lam * (acc_ref[1, :DV, :] / acc_ref[1, DV:DV + 1, :]))
    o = o * lax.rsqrt(jnp.mean(o * o, axis=0, keepdims=True) + SUBLN_EPS) * g_ref[...]
    o_ref[...] = (o * (1.0 - lambda_init)).T.astype(o_ref.dtype)


def _diff_attention(qk, vt, lq1, lk1, lq2, lk2, subln_g, lambda_init):
    s = qk.shape[0]
    H = DIFF_HEADS
    TQ = ATTN_Q_BLOCK
    TK = ATTN_K_BLOCK
    DV = DIFF_V_DIM
    assert TQ == 2 * TK and s % TQ == 0
    row = lambda v: v.reshape(1, -1)
    lam_specs = [_const_spec((1, DIFF_HEAD_DIM))] * 4
    return pl.pallas_call(
        functools.partial(_diffattn_kernel, lambda_init=lambda_init),
        grid=(H, s // TQ),
        in_specs=[pl.BlockSpec((TQ, DV), lambda h, qi: (qi, h)),
                  pl.BlockSpec((s, DV), lambda h, qi: (0, H + h)),
                  pl.BlockSpec((DV, s), lambda h, qi: (h, 0))]
                 + lam_specs + [_const_spec((DV, 1))],
        out_specs=pl.BlockSpec((TQ, DV), lambda h, qi: (qi, h)),
        out_shape=jax.ShapeDtypeStruct((s, DIFF_WIDTH), BF16),
        scratch_shapes=[pltpu.VMEM((2, TK, TQ), F32), pltpu.VMEM((2, TK, TQ), F32),
                        pltpu.VMEM((4, 1, TQ), F32), pltpu.VMEM((2, 1, TQ), F32),
                        pltpu.VMEM((2, DV + BF16_SUBLANES, TQ), F32)],
        compiler_params=pltpu.CompilerParams(dimension_semantics=("parallel", "arbitrary"),
                                             vmem_limit_bytes=40 * 1024 * 1024),
        name="diff_attention",
    )(qk, qk, vt, row(lq1), row(lk1), row(lq2), row(lk2), subln_g.reshape(DV, 1))


def _merge_kernel(ya_ref, yb_ref, gate_ref, x_ref, wa_ref, wb_ref, wo_ref, g2_ref,
                  h_ref, hh_ref, hl_ref, ht_ref):
    pa = _dot(ya_ref[...], wa_ref[...])
    pb = _dot(yb_ref[...], wb_ref[...])
    ga = jax.nn.sigmoid(gate_ref[:, :D_MODEL].astype(F32))
    gb = jax.nn.sigmoid(gate_ref[:, D_MODEL:].astype(F32))
    merged = ga * pa + gb * pb
    h = x_ref[...] + _dot(merged.astype(BF16), wo_ref[...])
    h_ref[...] = h
    hn = h * lax.rsqrt(jnp.mean(h * h, axis=-1, keepdims=True) + NORM_EPS) * g2_ref[...]
    hi = hn.astype(BF16)
    hh_ref[...] = hi
    hl_ref[...] = (hn - hi.astype(F32)).astype(BF16)
    ht_ref[...] = hn.T.astype(BF16)


def _merge(ya, yb, gate, x, wa, wb, wo, g2, tm):
    s = x.shape[0]
    D = D_MODEL
    rows = lambda w: pl.BlockSpec((tm, w), lambda i: (i, 0))
    single = lambda shape: pl.BlockSpec(shape, lambda i: (0, 0), pipeline_mode=pl.Buffered(1))
    return pl.pallas_call(
        _merge_kernel,
        grid=(s // tm,),
        in_specs=[rows(RWKV_WIDTH), rows(DIFF_WIDTH), rows(GATE_COLS), rows(D),
                  single((RWKV_WIDTH, D)), single((DIFF_WIDTH, D)), single((D, D)),
                  _const_spec((1, D))],
        out_specs=[rows(D), rows(D), rows(D), pl.BlockSpec((D, tm), lambda i: (0, i))],
        out_shape=[jax.ShapeDtypeStruct((s, D), F32), jax.ShapeDtypeStruct((s, D), BF16),
                   jax.ShapeDtypeStruct((s, D), BF16), jax.ShapeDtypeStruct((D, s), BF16)],
        compiler_params=pltpu.CompilerParams(dimension_semantics=("parallel",),
                                             vmem_limit_bytes=56 * 1024 * 1024),
        name="merge_out_proj",
    )(ya, yb, gate, x, wa, wb, wo, g2.reshape(1, D))


def _cmp_exchange(xs, i, l, descending):
    hi = jnp.maximum(xs[i], xs[l])
    lo = jnp.minimum(xs[i], xs[l])
    xs[i], xs[l] = (hi, lo) if descending else (lo, hi)


def _bitonic_merge_desc(xs):
    xs = list(xs)
    n = len(xs)
    j = n // 2
    while j >= 1:
        for i in range(n):
            l = i ^ j
            if l > i:
                _cmp_exchange(xs, i, l, True)
        j //= 2
    return xs


def _bitonic_sort_desc(xs):
    xs = list(xs)
    n = len(xs)
    k = 2
    while k <= n:
        j = k // 2
        while j >= 1:
            for i in range(n):
                l = i ^ j
                if l > i:
                    _cmp_exchange(xs, i, l, (i & k) == 0)
            j //= 2
        k *= 2
    return xs


def _merge_top(a, b):
    n = len(a)
    return _bitonic_merge_desc([jnp.maximum(a[i], b[n - 1 - i]) for i in range(n)])


def _top16_over_rows(s):
    groups = [s[g * SUBLANES:(g + 1) * SUBLANES, :] for g in range(s.shape[0] // SUBLANES)]
    top = _bitonic_sort_desc(groups)
    for shift in (4, 2, 1):
        top = _merge_top(top, [pltpu.roll(x, shift, axis=0) for x in top])
    return top


def _prefix_count(rows, pred):
    def pick(conds, cands):
        if not conds:
            return cands[0]
        half = len(cands) // 2
        return jnp.where(conds[0], pick(conds[1:], cands[half:]), pick(conds[1:], cands[:half]))

    n = len(rows)
    conds = []
    count = None
    step = n // 2
    while step >= 1:
        cands = [rows[lo + step - 1] for lo in range(0, n, 2 * step)]
        c = pred(pick(conds, cands))
        inc = jnp.where(c, float(step), 0.0)
        count = inc if count is None else count + inc
        conds.append(c)
        step //= 2
    return jnp.where(pred(rows[n - 1]), float(n), count)


def _peer_score_kernel(q_ref, keys_ref, r2_ref, e2_ref, n_ref, d_ref):
    K = PEER_TOPK
    T = q_ref.shape[0]
    H = PEER_HEADS
    scores = []
    tops = []
    for hp in range(2 * H):
        s = _dot_nt(keys_ref[hp], q_ref[:, hp * PEER_HALF:(hp + 1) * PEER_HALF], HIGHEST)
        scores.append(s)
        tops.append(_top16_over_rows(s))
    sub = lax.broadcasted_iota(jnp.int32, (SUBLANES, T), 0)

    def by_head(p, i):
        out = tops[p][i]
        for h in range(1, H):
            out = jnp.where(sub == h, tops[2 * h + p][i], out)
        return out

    aa = [by_head(0, i) for i in range(K)]
    bb = [by_head(1, i) for i in range(K)]
    cands = [aa[i] + bb[j] for i in range(K) for j in range(K) if (i + 1) * (j + 1) <= K]
    cands += [jnp.full_like(cands[0], -jnp.inf)] * (-len(cands) % K)
    best = _bitonic_sort_desc(cands[:K])
    for c in range(K, len(cands), K):
        best = _merge_top(best, _bitonic_sort_desc(cands[c:c + K]))
    thr = best[K - 1]
    zsum = jnp.zeros_like(thr)
    for t in best:
        zsum = zsum + jnp.exp(t - best[0])
    inv_z = 1.0 / zsum
    for h in range(H):
        hs = slice(h, h + 1)
        s1, s2 = scores[2 * h], scores[2 * h + 1]
        thr_h = thr[hs]
        b_rows = [bb[m][hs] for m in range(K)]
        cnt = _prefix_count(b_rows, lambda b: s1 + b >= thr_h)
        rank = _prefix_count(b_rows, lambda b: b > s2)
        n_ref[h] = cnt
        r2_ref[h] = rank.astype(BF16)
        d_ref[h] = jnp.exp(s1 - aa[0][hs]) * inv_z[hs]
        e2_ref[h] = jnp.exp(s2 - bb[0][hs]).astype(BF16)


def _peer_scores(q, keys, tt):
    s = q.shape[0]
    H = PEER_HEADS
    out = lambda dtype: jax.ShapeDtypeStruct((H, N_KEYS, s), dtype)
    ospec = pl.BlockSpec((H, N_KEYS, tt), lambda i: (0, 0, i))
    return pl.pallas_call(
        _peer_score_kernel,
        grid=(s // tt,),
        in_specs=[pl.BlockSpec((tt, 2 * H * PEER_HALF), lambda i: (i, 0)),
                  _const_spec((2 * H, N_KEYS, PEER_HALF))],
        out_specs=[ospec] * 4,
        out_shape=[out(BF16), out(BF16), out(F32), out(F32)],
        compiler_params=pltpu.CompilerParams(dimension_semantics=("parallel",)),
        name="peer_scores",
    )(q, keys)


def _peer_expert_kernel(hnt_ref, h1_ref, r2_ref, e2_ref, n_ref, d_ref, u_ref, v_ref, fg_ref,
                        o_ref, acc_ref):
    e = pl.program_id(1)
    eb = u_ref.shape[0]
    tt = hnt_ref.shape[1]
    ni = eb // N_KEYS
    slab = BF16_SUBLANES
    strip = 2 * LANES

    @pl.when(e == 0)
    def _():
        acc_ref[...] = jnp.zeros_like(acc_ref)

    groups = []
    for ii in range(ni):
        i = e * ni + ii
        rows = slice(ii * N_KEYS, (ii + 1) * N_KEYS)
        pre = _dot(u_ref[rows, :], hnt_ref[...])
        act = (0.5 * pre * (1.0 + lax.erf(pre * math.sqrt(0.5)))).astype(BF16)
        strips = []
        for t0 in range(0, tt, strip):
            ts = slice(t0, t0 + strip)
            nb = [jnp.broadcast_to(n_ref[h, pl.ds(i, 1), ts], (slab, strip)).astype(BF16)
                  for h in range(PEER_HEADS)]
            db = [jnp.broadcast_to(d_ref[h, pl.ds(i, 1), ts], (slab, strip)).astype(BF16)
                  for h in range(PEER_HEADS)]
            slabs = []
            for j0 in range(0, N_KEYS, slab):
                js = slice(j0, j0 + slab)
                gate = None
                for h in range(PEER_HEADS):
                    term = jnp.where(r2_ref[h, js, ts] < nb[h], e2_ref[h, js, ts],
                                     jnp.zeros((), BF16)) * db[h]
                    gate = term if gate is None else gate + term
                slabs.append(gate * act[js, ts])
            strips.append(jnp.concatenate(slabs, axis=0))
        groups.append(jnp.concatenate(strips, axis=1))
    gated = jnp.concatenate(groups, axis=0)
    acc_ref[...] += lax.dot_general(gated, v_ref[...], (((0,), (0,)), ((), ())),
                                    preferred_element_type=F32)

    @pl.when(e == pl.num_programs(1) - 1)
    def _():
        h = h1_ref[...] + acc_ref[...]
        o_ref[...] = h * lax.rsqrt(jnp.mean(h * h, axis=-1, keepdims=True) + NORM_EPS) * fg_ref[...]


def _peer_experts(hnt, h1, r2, e2, n, d, u, v, final_g, tt, eb):
    s = hnt.shape[1]
    D = D_MODEL
    H = PEER_HEADS
    sel = pl.BlockSpec((H, N_KEYS, tt), lambda i, e: (0, 0, i), pipeline_mode=pl.Buffered(1))
    return pl.pallas_call(
        _peer_expert_kernel,
        grid=(s // tt, N_EXPERTS // eb),
        in_specs=[pl.BlockSpec((D, tt), lambda i, e: (0, i)),
                  pl.BlockSpec((tt, D), lambda i, e: (i, 0), pipeline_mode=pl.Buffered(1)),
                  sel, sel, sel, sel,
                  pl.BlockSpec((eb, D), lambda i, e: (e, 0)),
                  pl.BlockSpec((eb, D), lambda i, e: (e, 0)),
                  pl.BlockSpec((1, D), lambda i, e: (0, 0))],
        out_specs=pl.BlockSpec((tt, D), lambda i, e: (i, 0)),
        out_shape=jax.ShapeDtypeStruct((s, D), F32),
        scratch_shapes=[pltpu.VMEM((tt, D), F32)],
        compiler_params=pltpu.CompilerParams(dimension_semantics=("parallel", "arbitrary"),
                                             vmem_limit_bytes=56 * 1024 * 1024),
        name="peer_experts",
    )(hnt, h1, r2, e2, n, d, u, v, final_g.reshape(1, D))


def _pad_rows(w, rows):
    return jnp.pad(w, ((0, rows - w.shape[0]), (0, 0)))


def _split_bf16(w):
    hi = w.astype(BF16)
    return hi, (w - hi.astype(F32)).astype(BF16)


def _layer(h, norm1_g, w_in, shift_mu, rwkv_w0, w_decay_up, rwkv_a0, w_iclr_up, w_gate_up,
           k_k, k_a, r_k, lnx_g, lnx_b, lam_q1, lam_k1, lam_q2, lam_k2, subln_g, w_proj_a,
           w_proj_b, w_out, norm2_g, peer_wq, peer_sub_keys, peer_u, peer_v, out_g, lambda_init):
    s = h.shape[0]
    W = RWKV_WIDTH
    tm = min(s, 512)
    tmm = min(s, 1024)

    c0, c1, c2 = 3 * W, 3 * W + DECAY_LORA, 3 * W + DECAY_LORA + ICLR_LORA
    pad_cols = lambda m, n: jnp.pad(m, ((0, 0), (0, n - m.shape[1])))
    w_rwkv = jnp.concatenate([w_in[:, :c0], pad_cols(w_in[:, c0:c1], LORA_PAD),
                              pad_cols(w_in[:, c1:c2], LORA_PAD), w_in[:, c2:RWKV_COLS]], axis=1)
    mu2 = shift_mu.reshape(1, -1)
    mu = jnp.concatenate([mu2[:, :c0], pad_cols(mu2[:, c0:c1], LORA_PAD),
                          pad_cols(mu2[:, c1:c2], LORA_PAD), mu2[:, c2:]], axis=1)
    d0 = RWKV_COLS
    q_scale = DIFF_HEAD_DIM ** -0.5 * math.log2(math.e)
    w_qk = jnp.concatenate([w_in[:, d0:d0 + DIFF_WIDTH] * q_scale,
                            w_in[:, d0 + DIFF_WIDTH:d0 + 2 * DIFF_WIDTH]], axis=1)
    w_vt = w_in[:, d0 + 2 * DIFF_WIDTH:d0 + DIFF_COLS].T
    w_gate = w_in[:, d0 + DIFF_COLS:]

    xn = _rmsnorm(h, norm1_g, NORM_EPS, BF16, tm)
    p_rwkv = _matmul(xn, w_rwkv.astype(BF16), F32, tmm, 512, "in_proj_rwkv")
    p_qk = _matmul(xn, w_qk.astype(BF16), BF16, tmm, 512, "in_proj_qk")
    p_vt = _matmul_nt(w_vt.astype(BF16), xn, BF16, 256, tmm, "in_proj_vt")
    p_gate = _matmul(xn, w_gate.astype(BF16), BF16, tmm, 512, "in_proj_gate")

    y_a = _rwkv_time_mix(p_rwkv, mu, rwkv_w0, _pad_rows(w_decay_up, LORA_PAD), rwkv_a0,
                         _pad_rows(w_iclr_up, LORA_PAD), w_gate_up, k_k, k_a, r_k, lnx_g, lnx_b)
    y_b = _diff_attention(p_qk, p_vt, lam_q1, lam_k1, lam_q2, lam_k2, subln_g, lambda_init)
    h1, hn_hi, hn_lo, hn_t = _merge(y_a, y_b, p_gate, h, w_proj_a.astype(BF16),
                                    w_proj_b.astype(BF16), w_out.astype(BF16), norm2_g,
                                    min(s, 256))

    wq_hi, wq_lo = _split_bf16(peer_wq)
    q = _matmul_split(hn_hi, hn_lo, wq_hi, wq_lo, tmm, 512, "peer_query")
    keys = peer_sub_keys.reshape(2 * PEER_HEADS, N_KEYS, PEER_HALF)
    r2, e2, n, d = _peer_scores(q, keys, min(s, 256))
    return _peer_experts(hn_t, h1, r2, e2, n, d, peer_u.astype(BF16), peer_v.astype(BF16),
                         out_g, min(s, 512), 1024)


def kernel(x, norm1_g, w_in, shift_mu, rwkv_w0, w_decay_up, rwkv_a0, w_iclr_up, w_gate_up, k_k, k_a, r_k, lnx_g, lnx_b, lam_q1, lam_k1, lam_q2, lam_k2, subln_g, w_proj_a, w_proj_b, w_out, norm2_g, peer_wq, peer_sub_keys, peer_u, peer_v, final_g):
    B, S, D = x.shape
    assert B == 1 and D == D_MODEL and norm1_g.shape[0] == 1
    lambda_init = 0.8 - 0.6 * math.exp(-0.3 * 0)
    out = _layer(x[0], norm1_g[0], w_in[0], shift_mu[0], rwkv_w0[0], w_decay_up[0], rwkv_a0[0],
                 w_iclr_up[0], w_gate_up[0], k_k[0], k_a[0], r_k[0].reshape(-1), lnx_g[0],
                 lnx_b[0], lam_q1[0], lam_k1[0], lam_q2[0], lam_k2[0], subln_g[0], w_proj_a[0],
                 w_proj_b[0], w_out[0], norm2_g[0], peer_wq[0], peer_sub_keys[0], peer_u[0],
                 peer_v[0], final_g, lambda_init)
    return out[None]
```

```python
import functools
import math

import jax
import jax.numpy as jnp
from jax import lax
from jax.experimental import pallas as pl
from jax.experimental.pallas import tpu as pltpu

F32 = jnp.float32
BF16 = jnp.bfloat16
HIGHEST = lax.Precision.HIGHEST

LANES = 128
SUBLANES = 8
BF16_SUBLANES = 16

D_MODEL = 2048
RWKV_HEADS = 16
RWKV_HEAD_DIM = 64
RWKV_WIDTH = RWKV_HEADS * RWKV_HEAD_DIM
DECAY_LORA = 96
ICLR_LORA = 96
GATE_LORA = 256
LORA_PAD = 128
RWKV_COLS = 3 * RWKV_WIDTH + DECAY_LORA + ICLR_LORA + GATE_LORA
RWKV_COLS_PAD = 3 * RWKV_WIDTH + 2 * LORA_PAD + GATE_LORA
RWKV_CHUNK = 64
RWKV_BLOCK = 128
RWKV_GROUP = 4
RWKV_GROUP_LANES = RWKV_GROUP * RWKV_HEAD_DIM
DIFF_HEADS = 8
DIFF_HEAD_DIM = 64
DIFF_V_DIM = 2 * DIFF_HEAD_DIM
DIFF_WIDTH = DIFF_HEADS * DIFF_V_DIM
DIFF_COLS = 3 * DIFF_WIDTH
ATTN_K_BLOCK = 512
ATTN_Q_BLOCK = 1024
GATE_COLS = 2 * D_MODEL
PEER_HEADS = 8
PEER_HALF = 128
N_KEYS = 128
N_EXPERTS = N_KEYS * N_KEYS
PEER_TOPK = 16
NORM_EPS = 1e-6
LN_X_EPS = 64e-5
SUBLN_EPS = 1e-5
NEG_INF = -1e30


def _dot(a, b, precision=None):
    return jnp.dot(a, b, preferred_element_type=F32, precision=precision)


def _dot_nt(a, b, precision=None):
    return lax.dot_general(a, b, (((1,), (1,)), ((), ())), preferred_element_type=F32,
                           precision=precision)


def _const_spec(shape):
    nd = len(shape)
    return pl.BlockSpec(shape, lambda *_: (0,) * nd)


def _split(x):
    hi = x.astype(BF16)
    return hi, (x - hi.astype(F32)).astype(BF16)


def _dot3(a, b, nt=False):
    f = _dot_nt if nt else _dot
    ah, al = _split(a)
    bh, bl = _split(b)
    return f(ah, bh) + (f(ah, bl) + f(al, bh))


def _rmsnorm_kernel(x_ref, g_ref, o_ref, *, eps):
    x = x_ref[...]
    y = x * lax.rsqrt(jnp.mean(x * x, axis=-1, keepdims=True) + eps) * g_ref[...]
    o_ref[...] = y.astype(o_ref.dtype)


def _rmsnorm(x, g, eps, out_dtype, tm):
    s, d = x.shape
    return pl.pallas_call(
        functools.partial(_rmsnorm_kernel, eps=eps),
        grid=(s // tm,),
        in_specs=[pl.BlockSpec((tm, d), lambda i: (i, 0)), _const_spec((1, d))],
        out_specs=pl.BlockSpec((tm, d), lambda i: (i, 0)),
        out_shape=jax.ShapeDtypeStruct((s, d), out_dtype),
        compiler_params=pltpu.CompilerParams(dimension_semantics=("parallel",)),
        name="rmsnorm",
    )(x, g.reshape(1, d))


def _mm_kernel(x_ref, w_ref, o_ref):
    o_ref[...] = _dot(x_ref[...], w_ref[...]).astype(o_ref.dtype)


def _matmul(x, w, out_dtype, tm, tn, name):
    s, k = x.shape
    n = w.shape[1]
    return pl.pallas_call(
        _mm_kernel,
        grid=(s // tm, n // tn),
        in_specs=[pl.BlockSpec((tm, k), lambda i, j: (i, 0)),
                  pl.BlockSpec((k, tn), lambda i, j: (0, j))],
        out_specs=pl.BlockSpec((tm, tn), lambda i, j: (i, j)),
        out_shape=jax.ShapeDtypeStruct((s, n), out_dtype),
        compiler_params=pltpu.CompilerParams(dimension_semantics=("parallel", "parallel")),
        name=name,
    )(x, w)


def _mm_nt_kernel(w_ref, x_ref, o_ref):
    o_ref[...] = _dot_nt(w_ref[...], x_ref[...]).astype(o_ref.dtype)


def _matmul_nt(w, x, out_dtype, tn, tm, name):
    n, k = w.shape
    s = x.shape[0]
    return pl.pallas_call(
        _mm_nt_kernel,
        grid=(s // tm, n // tn),
        in_specs=[pl.BlockSpec((tn, k), lambda i, j: (j, 0)),
                  pl.BlockSpec((tm, k), lambda i, j: (i, 0))],
        out_specs=pl.BlockSpec((tn, tm), lambda i, j: (j, i)),
        out_shape=jax.ShapeDtypeStruct((n, s), out_dtype),
        compiler_params=pltpu.CompilerParams(dimension_semantics=("parallel", "parallel")),
        name=name,
    )(w, x)


def _mm_split_kernel(xh_ref, xl_ref, wh_ref, wl_ref, o_ref):
    xh = xh_ref[...]
    o_ref[...] = _dot(xh, wh_ref[...]) + (_dot(xh, wl_ref[...]) + _dot(xl_ref[...], wh_ref[...]))


def _matmul_split(xh, xl, wh, wl, tm, tn, name):
    s, k = xh.shape
    n = wh.shape[1]
    xspec = pl.BlockSpec((tm, k), lambda i, j: (i, 0))
    wspec = pl.BlockSpec((k, tn), lambda i, j: (0, j))
    return pl.pallas_call(
        _mm_split_kernel,
        grid=(s // tm, n // tn),
        in_specs=[xspec, xspec, wspec, wspec],
        out_specs=pl.BlockSpec((tm, tn), lambda i, j: (i, j)),
        out_shape=jax.ShapeDtypeStruct((s, n), F32),
        compiler_params=pltpu.CompilerParams(dimension_semantics=("parallel", "parallel")),
        name=name,
    )(xh, xl, wh, wl)


def _head_sum(x, ones_bd):
    xh, xl = _split(x)
    tiles = []
    for c in range(x.shape[1] // LANES):
        cs = slice(c * LANES, (c + 1) * LANES)
        tiles.append(_dot(xh[:, cs], ones_bd) + _dot(xl[:, cs], ones_bd))
    return jnp.concatenate(tiles, axis=1)


def _softplus(x):
    return jnp.maximum(x, 0.0) + jnp.log1p(jnp.exp(-jnp.abs(x)))


def _rwkv_kernel(p_ref, pprev_ref, mu_ref, w0_ref, wd_ref, a0_ref, wa_ref, wg_ref, kk_ref,
                 ka_ref, rk_ref, lng_ref, lnb_ref, o_ref, state_ref, y_ref):
    L = RWKV_CHUNK
    N = RWKV_HEAD_DIM
    W = RWKV_WIDTH
    G = RWKV_GROUP
    GL = RWKV_GROUP_LANES
    step = pl.program_id(0)

    @pl.when(step == 0)
    def _():
        state_ref[...] = jnp.zeros_like(state_ref)

    TB = p_ref.shape[0]
    row = lax.broadcasted_iota(jnp.int32, (TB, 1), 0)
    carry_on = jnp.where(step == 0, 0.0, 1.0)

    def shifted(c0, c1):
        p = p_ref[:, c0:c1]
        last = pprev_ref[SUBLANES - 1:SUBLANES, c0:c1] * carry_on
        prev = jnp.where(row == 0, last, pltpu.roll(p, 1, axis=0))
        return p + (prev - p) * mu_ref[:, c0:c1]

    r = shifted(0, W)
    k = shifted(W, 2 * W)
    v = shifted(2 * W, 3 * W)
    xw = shifted(3 * W, 3 * W + LORA_PAD)
    xa = shifted(3 * W + LORA_PAD, 3 * W + 2 * LORA_PAD)
    xg = shifted(3 * W + 2 * LORA_PAD, 3 * W + 2 * LORA_PAD + GATE_LORA)

    z = w0_ref[...] + _dot3(jnp.tanh(xw), wd_ref[...])
    w_log = -_softplus(-z) - 0.5
    lw = -jnp.exp(w_log)
    a = jax.nn.sigmoid(a0_ref[...] + _dot3(xa, wa_ref[...]))
    g = _dot3(jax.nn.sigmoid(xg), wg_ref[...])

    lane_i = lax.broadcasted_iota(jnp.int32, (LANES, LANES), 0) // N
    lane_j = lax.broadcasted_iota(jnp.int32, (LANES, LANES), 1) // N
    ones_bd = jnp.where(lane_i == lane_j, 1.0, 0.0).astype(BF16)

    kk = k * kk_ref[...]
    kk = kk / jnp.maximum(jnp.sqrt(_head_sum(kk * kk, ones_bd)), 1e-12)
    k = k * (1.0 + (a - 1.0) * ka_ref[...])

    ti = lax.broadcasted_iota(jnp.int32, (TB, TB), 0)
    tj = lax.broadcasted_iota(jnp.int32, (TB, TB), 1)
    tril = jnp.where((tj <= ti) & (tj // L == ti // L), 1.0, 0.0).astype(F32)
    gi = lax.broadcasted_iota(jnp.int32, (L, GL), 0)
    gj = lax.broadcasted_iota(jnp.int32, (L, GL), 1) % L
    incl = gj <= gi
    strict = gj < gi
    eye4 = jnp.where(gj == gi, 1.0, 0.0).astype(F32)
    bi = lax.broadcasted_iota(jnp.int32, (GL, GL), 0) // N
    bj = lax.broadcasted_iota(jnp.int32, (GL, GL), 1) // N
    bd_mask = bi == bj

    def bd(x):
        return jnp.where(bd_mask, jnp.concatenate([x] * G, axis=0), jnp.zeros((), x.dtype))

    cum = _dot(tril, lw, HIGHEST)
    e_inv = jnp.exp(-cum)
    r_t = (r * jnp.exp(cum)).astype(BF16)
    a_t = (-kk * jnp.exp(cum - lw)).astype(BF16)
    b = kk * a
    b_t = (b * e_inv).astype(BF16)
    k_t = (k * e_inv).astype(BF16)
    v_b = v.astype(BF16)
    eye4_b = eye4.astype(BF16)
    n_grp = RWKV_HEADS // G
    states = [state_ref[grp] for grp in range(n_grp)]

    n_chunks = TB // L
    pairs = [(c, grp) for c in range(n_chunks) for grp in range(n_grp)]
    rows_of = lambda c: slice(c * L, (c + 1) * L)
    lanes_of = lambda grp: slice(grp * GL, (grp + 1) * GL)
    cum_last = [cum[(c + 1) * L - 1:(c + 1) * L, :] for c in range(n_chunks)]

    a_ab, a_ak, a_rb, a_rk = {}, {}, {}, {}
    for c, grp in pairs:
        rs, gs = rows_of(c), lanes_of(grp)
        m = _dot_nt(jnp.concatenate([a_t[rs, gs], r_t[rs, gs]], axis=0),
                    jnp.concatenate([bd(b_t[rs, gs]), bd(k_t[rs, gs])], axis=0))
        a_ab[c, grp] = jnp.where(strict, m[:L, :GL], 0.0)
        a_ak[c, grp] = jnp.where(strict, m[:L, GL:], 0.0).astype(BF16)
        a_rb[c, grp] = jnp.where(incl, m[L:, :GL], 0.0).astype(BF16)
        a_rk[c, grp] = jnp.where(incl, m[L:, GL:], 0.0).astype(BF16)

    inv = {p: eye4 + a_ab[p] for p in pairs}
    pw = {}
    for p in pairs:
        pw_b = a_ab[p].astype(BF16)
        pw[p] = _dot(pw_b, bd(pw_b))
    for _ in range(4):
        for p in pairs:
            pw_b = pw[p].astype(BF16)
            both = _dot(jnp.concatenate([pw_b, inv[p].astype(BF16)], axis=0), bd(pw_b))
            pw[p], inv[p] = both[:L], inv[p] + both[L:]
    for p in pairs:
        inv[p] = (inv[p] + _dot(inv[p].astype(BF16), bd(pw[p].astype(BF16)))).astype(BF16)

    bk = {}
    for c in range(n_chunks):
        rs = rows_of(c)
        e_tail = jnp.exp(cum_last[c] - cum[rs])
        b_w = (b[rs] * e_tail).astype(BF16)
        k_w = (k[rs] * e_tail).astype(BF16)
        for grp in range(n_grp):
            gs = lanes_of(grp)
            bk[c, grp] = _dot_nt(
                eye4_b, jnp.concatenate([bd(b_w[:, gs]), bd(k_w[:, gs])], axis=0)).astype(BF16)

    for c in range(n_chunks):
        rs = rows_of(c)
        w_last = jnp.exp(cum_last[c])
        bd_z = [bd(states[grp].astype(BF16)) for grp in range(n_grp)]
        bd_v = [bd(v_b[rs, lanes_of(grp)]) for grp in range(n_grp)]
        x = [_dot(jnp.concatenate([a_t[rs, lanes_of(grp)], a_ak[c, grp]], axis=1),
                  jnp.concatenate([bd_z[grp], bd_v[grp]], axis=0)) for grp in range(n_grp)]
        u = [_dot(inv[c, grp], bd(x[grp].astype(BF16))) for grp in range(n_grp)]
        for grp in range(n_grp):
            gs = lanes_of(grp)
            lhs = jnp.concatenate(
                [jnp.concatenate([r_t[rs, gs], a_rb[c, grp], a_rk[c, grp]], axis=1),
                 jnp.concatenate([(eye4 * w_last[:, gs]).astype(BF16), bk[c, grp]], axis=1)],
                axis=0)
            yz = _dot(lhs, jnp.concatenate([bd_z[grp], bd(u[grp].astype(BF16)), bd_v[grp]],
                                           axis=0))
            y_ref[rs, gs] = yz[:L]
            states[grp] = yz[L:]
    for grp in range(n_grp):
        state_ref[grp] = states[grp]

    y = y_ref[...]
    mean = _head_sum(y, ones_bd) * (1.0 / N)
    yc = y - mean
    var = _head_sum(yc * yc, ones_bd) * (1.0 / N)
    yn = yc * lax.rsqrt(var + LN_X_EPS) * lng_ref[...] + lnb_ref[...]
    bonus = _head_sum(r * k * rk_ref[...], ones_bd) * v
    o_ref[...] = ((yn + bonus) * g).astype(o_ref.dtype)


def _rwkv_time_mix(p, mu, w0, wd, a0, wa, wg, k_k, k_a, r_k, lnx_g, lnx_b):
    s = p.shape[0]
    L = RWKV_CHUNK
    W = RWKV_WIDTH
    row = lambda v: v.reshape(1, -1)
    consts = [row(mu), row(w0), wd, row(a0), wa, wg, row(k_k), row(k_a), row(r_k), row(lnx_g),
              row(lnx_b)]
    tb = RWKV_BLOCK
    return pl.pallas_call(
        _rwkv_kernel,
        grid=(s // tb,),
        in_specs=[pl.BlockSpec((tb, RWKV_COLS_PAD), lambda i: (i, 0)),
                  pl.BlockSpec((SUBLANES, RWKV_COLS_PAD),
                               lambda i: (jnp.maximum(i * (tb // SUBLANES) - 1, 0), 0))]
                 + [_const_spec(c.shape) for c in consts],
        out_specs=pl.BlockSpec((tb, W), lambda i: (i, 0)),
        out_shape=jax.ShapeDtypeStruct((s, W), BF16),
        scratch_shapes=[pltpu.VMEM((RWKV_HEADS // RWKV_GROUP, RWKV_HEAD_DIM, RWKV_GROUP_LANES),
                                   F32),
                        pltpu.VMEM((tb, W), F32)],
        compiler_params=pltpu.CompilerParams(dimension_semantics=("arbitrary",)),
        name="rwkv7",
    )(p, p, *consts)


def _diffattn_kernel(q_ref, k_ref, vt_ref, lq1_ref, lk1_ref, lq2_ref, lk2_ref, g_ref, o_ref,
                     sa_ref, sb_ref, mx_ref, m_ref, acc_ref, *, lambda_init):
    TQ = ATTN_Q_BLOCK
    TK = ATTN_K_BLOCK
    DV = DIFF_V_DIM
    qi = pl.program_id(1)
    q = q_ref[...]
    lane = lax.broadcasted_iota(jnp.int32, q.shape, 1)
    zero = jnp.zeros_like(q)
    q_halves = (jnp.where(lane < DIFF_HEAD_DIM, q, zero), jnp.where(lane >= DIFF_HEAD_DIM, q, zero))
    ones = jnp.ones((BF16_SUBLANES, TK), BF16)
    key_i = lax.broadcasted_iota(jnp.int32, (TK, TQ), 0)
    qry_i = lax.broadcasted_iota(jnp.int32, (TK, TQ), 1)

    s_bufs = (sa_ref, sb_ref)

    def scores(j, buf, key_offset=None):
        kb = k_ref[pl.ds(pl.multiple_of(j * TK, TK), TK), :]
        for idx in range(2):
            s = _dot_nt(kb, q_halves[idx])
            if key_offset is not None:
                s = jnp.where(key_i + key_offset <= qry_i, s, NEG_INF)
            s_bufs[buf][idx] = s
            mx_ref[2 * buf + idx] = jnp.max(s, axis=0, keepdims=True)

    def probs(buf):
        out = []
        for idx in range(2):
            m_old = m_ref[idx]
            m_new = jnp.maximum(m_old, mx_ref[2 * buf + idx])
            m_ref[idx] = m_new
            out.append((jnp.exp2(s_bufs[buf][idx] - m_new).astype(BF16),
                        jnp.exp2(m_old - m_new)))
        return out

    def accumulate(j, pa):
        vt = vt_ref[:, pl.ds(pl.multiple_of(j * TK, TK), TK)]
        vext = jnp.concatenate([vt, ones], axis=0)
        for idx, (p, alpha) in enumerate(pa):
            acc_ref[idx] = alpha * acc_ref[idx] + _dot(vext, p)

    def step(j, buf, next_scores):
        pa = probs(buf)
        next_scores()
        accumulate(j, pa)

    m_ref[...] = jnp.full_like(m_ref, NEG_INF)
    acc_ref[...] = jnp.zeros_like(acc_ref)

    @pl.when(qi > 0)
    def _():
        scores(0, 0)

    @pl.loop(0, qi - 1)
    def _(i):
        step(2 * i, 0, lambda: scores(2 * i + 1, 1))
        step(2 * i + 1, 1, lambda: scores(2 * i + 2, 0))

    @pl.when(qi > 0)
    def _():
        step(2 * qi - 2, 0, lambda: scores(2 * qi - 1, 1))
        step(2 * qi - 1, 1, lambda: scores(2 * qi, 0, key_offset=0))

    @pl.when(qi == 0)
    def _():
        scores(0, 0, key_offset=0)

    step(2 * qi, 0, lambda: scores(2 * qi + 1, 1, key_offset=TK))
    step(2 * qi + 1, 1, lambda: None)

    lam = (jnp.exp(jnp.sum(lq1_ref[...] * lk1_ref[...], axis=-1, keepdims=True))
           - jnp.exp(jnp.sum(lq2_ref[...] * lk2_ref[...], axis=-1, keepdims=True))
           + lambda_init)
    o = (acc_ref[0, :DV, :] / acc_ref[0, DV:DV + 1, :]
         - lam * (acc_ref[1, :DV, :] / acc_ref[1, DV:DV + 1, :]))
    o = o * lax.rsqrt(jnp.mean(o * o, axis=0, keepdims=True) + SUBLN_EPS) * g_ref[...]
    o_ref[...] = (o * (1.0 - lambda_init)).T.astype(o_ref.dtype)


def _diff_attention(qk, vt, lq1, lk1, lq2, lk2, subln_g, lambda_init):
    s = qk.shape[0]
    H = DIFF_HEADS
    TQ = ATTN_Q_BLOCK
    TK = ATTN_K_BLOCK
    DV = DIFF_V_DIM
    assert TQ == 2 * TK and s % TQ == 0
    row = lambda v: v.reshape(1, -1)
    lam_specs = [_const_spec((1, DIFF_HEAD_DIM))] * 4
    return pl.pallas_call(
        functools.partial(_diffattn_kernel, lambda_init=lambda_init),
        grid=(H, s // TQ),
        in_specs=[pl.BlockSpec((TQ, DV), lambda h, qi: (qi, h)),
                  pl.BlockSpec((s, DV), lambda h, qi: (0, H + h)),
                  pl.BlockSpec((DV, s), lambda h, qi: (h, 0))]
                 + lam_specs + [_const_spec((DV, 1))],
        out_specs=pl.BlockSpec((TQ, DV), lambda h, qi: (qi, h)),
        out_shape=jax.ShapeDtypeStruct((s, DIFF_WIDTH), BF16),
        scratch_shapes=[pltpu.VMEM((2, TK, TQ), F32), pltpu.VMEM((2, TK, TQ), F32),
                        pltpu.VMEM((4, 1, TQ), F32), pltpu.VMEM((2, 1, TQ), F32),
                        pltpu.VMEM((2, DV + BF16_SUBLANES, TQ), F32)],
        compiler_params=pltpu.CompilerParams(dimension_semantics=("parallel", "arbitrary"),
                                             vmem_limit_bytes=40 * 1024 * 1024),
        name="diff_attention",
    )(qk, qk, vt, row(lq1), row(lk1), row(lq2), row(lk2), subln_g.reshape(DV, 1))


def _merge_kernel(ya_ref, yb_ref, gate_ref, x_ref, wa_ref, wb_ref, wo_ref, g2_ref,
                  h_ref, hh_ref, hl_ref, ht_ref):
    pa = _dot(ya_ref[...], wa_ref[...])
    pb = _dot(yb_ref[...], wb_ref[...])
    ga = jax.nn.sigmoid(gate_ref[:, :D_MODEL].astype(F32))
    gb = jax.nn.sigmoid(gate_ref[:, D_MODEL:].astype(F32))
    merged = ga * pa + gb * pb
    h = x_ref[...] + _dot(merged.astype(BF16), wo_ref[...])
    h_ref[...] = h
    hn = h * lax.rsqrt(jnp.mean(h * h, axis=-1, keepdims=True) + NORM_EPS) * g2_ref[...]
    hi = hn.astype(BF16)
    hh_ref[...] = hi
    hl_ref[...] = (hn - hi.astype(F32)).astype(BF16)
    ht_ref[...] = hn.T.astype(BF16)


def _merge(ya, yb, gate, x, wa, wb, wo, g2, tm):
    s = x.shape[0]
    D = D_MODEL
    rows = lambda w: pl.BlockSpec((tm, w), lambda i: (i, 0))
    single = lambda shape: pl.BlockSpec(shape, lambda i: (0, 0), pipeline_mode=pl.Buffered(1))
    return pl.pallas_call(
        _merge_kernel,
        grid=(s // tm,),
        in_specs=[rows(RWKV_WIDTH), rows(DIFF_WIDTH), rows(GATE_COLS), rows(D),
                  single((RWKV_WIDTH, D)), single((DIFF_WIDTH, D)), single((D, D)),
                  _const_spec((1, D))],
        out_specs=[rows(D), rows(D), rows(D), pl.BlockSpec((D, tm), lambda i: (0, i))],
        out_shape=[jax.ShapeDtypeStruct((s, D), F32), jax.ShapeDtypeStruct((s, D), BF16),
                   jax.ShapeDtypeStruct((s, D), BF16), jax.ShapeDtypeStruct((D, s), BF16)],
        compiler_params=pltpu.CompilerParams(dimension_semantics=("parallel",),
                                             vmem_limit_bytes=56 * 1024 * 1024),
        name="merge_out_proj",
    )(ya, yb, gate, x, wa, wb, wo, g2.reshape(1, D))


def _cmp_exchange(xs, i, l, descending):
    hi = jnp.maximum(xs[i], xs[l])
    lo = jnp.minimum(xs[i], xs[l])
    xs[i], xs[l] = (hi, lo) if descending else (lo, hi)


def _bitonic_merge_desc(xs):
    xs = list(xs)
    n = len(xs)
    j = n // 2
    while j >= 1:
        for i in range(n):
            l = i ^ j
            if l > i:
                _cmp_exchange(xs, i, l, True)
        j //= 2
    return xs


def _bitonic_sort_desc(xs):
    xs = list(xs)
    n = len(xs)
    k = 2
    while k <= n:
        j = k // 2
        while j >= 1:
            for i in range(n):
                l = i ^ j
                if l > i:
                    _cmp_exchange(xs, i, l, (i & k) == 0)
            j //= 2
        k *= 2
    return xs


def _merge_top(a, b):
    n = len(a)
    return _bitonic_merge_desc([jnp.maximum(a[i], b[n - 1 - i]) for i in range(n)])


def _top16_over_rows(s):
    groups = [s[g * SUBLANES:(g + 1) * SUBLANES, :] for g in range(s.shape[0] // SUBLANES)]
    top = _bitonic_sort_desc(groups)
    for shift in (4, 2, 1):
        top = _merge_top(top, [pltpu.roll(x, shift, axis=0) for x in top])
    return top


def _prefix_count(rows, pred):
    def pick(conds, cands):
        if not conds:
            return cands[0]
        half = len(cands) // 2
        return jnp.where(conds[0], pick(conds[1:], cands[half:]), pick(conds[1:], cands[:half]))

    n = len(rows)
    conds = []
    count = None
    step = n // 2
    while step >= 1:
        cands = [rows[lo + step - 1] for lo in range(0, n, 2 * step)]
        c = pred(pick(conds, cands))
        inc = jnp.where(c, float(step), 0.0)
        count = inc if count is None else count + inc
        conds.append(c)
        step //= 2
    return jnp.where(pred(rows[n - 1]), float(n), count)


def _peer_score_kernel(q_ref, keys_ref, r2_ref, e2_ref, n_ref, d_ref):
    K = PEER_TOPK
    T = q_ref.shape[0]
    H = PEER_HEADS
    scores = []
    tops = []
    for hp in range(2 * H):
        s = _dot_nt(keys_ref[hp], q_ref[:, hp * PEER_HALF:(hp + 1) * PEER_HALF], HIGHEST)
        scores.append(s)
        tops.append(_top16_over_rows(s))
    sub = lax.broadcasted_iota(jnp.int32, (SUBLANES, T), 0)

    def by_head(p, i):
        out = tops[p][i]
        for h in range(1, H):
            out = jnp.where(sub == h, tops[2 * h + p][i], out)
        return out

    aa = [by_head(0, i) for i in range(K)]
    bb = [by_head(1, i) for i in range(K)]
    cands = [aa[i] + bb[j] for i in range(K) for j in range(K) if (i + 1) * (j + 1) <= K]
    cands += [jnp.full_like(cands[0], -jnp.inf)] * (-len(cands) % K)
    best = _bitonic_sort_desc(cands[:K])
    for c in range(K, len(cands), K):
        best = _merge_top(best, _bitonic_sort_desc(cands[c:c + K]))
    thr = best[K - 1]
    zsum = jnp.zeros_like(thr)
    for t in best:
        zsum = zsum + jnp.exp(t - best[0])
    inv_z = 1.0 / zsum
    for h in range(H):
        hs = slice(h, h + 1)
        s1, s2 = scores[2 * h], scores[2 * h + 1]
        thr_h = thr[hs]
        b_rows = [bb[m][hs] for m in range(K)]
        cnt = _prefix_count(b_rows, lambda b: s1 + b >= thr_h)
        rank = _prefix_count(b_rows, lambda b: b > s2)
        n_ref[h] = cnt
        r2_ref[h] = rank.astype(BF16)
        d_ref[h] = jnp.exp(s1 - aa[0][hs]) * inv_z[hs]
        e2_ref[h] = jnp.exp(s2 - bb[0][hs]).astype(BF16)


def _peer_scores(q, keys, tt):
    s = q.shape[0]
    H = PEER_HEADS
    out = lambda dtype: jax.ShapeDtypeStruct((H, N_KEYS, s), dtype)
    ospec = pl.BlockSpec((H, N_KEYS, tt), lambda i: (0, 0, i))
    return pl.pallas_call(
        _peer_score_kernel,
        grid=(s // tt,),
        in_specs=[pl.BlockSpec((tt, 2 * H * PEER_HALF), lambda i: (i, 0)),
                  _const_spec((2 * H, N_KEYS, PEER_HALF))],
        out_specs=[ospec] * 4,
        out_shape=[out(BF16), out(BF16), out(F32), out(F32)],
        compiler_params=pltpu.CompilerParams(dimension_semantics=("parallel",)),
        name="peer_scores",
    )(q, keys)


def _peer_expert_kernel(hnt_ref, h1_ref, r2_ref, e2_ref, n_ref, d_ref, u_ref, vt_ref, fg_ref,
                        o_ref, acc_ref):
    e = pl.program_id(1)
    eb = u_ref.shape[0]
    tt = hnt_ref.shape[1]
    ni = eb // N_KEYS
    slab = BF16_SUBLANES
    strip = 2 * LANES

    @pl.when(e == 0)
    def _():
        acc_ref[...] = jnp.zeros_like(acc_ref)

    groups = []
    for ii in range(ni):
        i = e * ni + ii
        rows = slice(ii * N_KEYS, (ii + 1) * N_KEYS)
        pre = _dot(u_ref[rows, :], hnt_ref[...])
        act = (0.5 * pre * (1.0 + lax.erf(pre * math.sqrt(0.5)))).astype(BF16)
        strips = []
        for t0 in range(0, tt, strip):
            ts = slice(t0, t0 + strip)
            nb = [jnp.broadcast_to(n_ref[h, pl.ds(i, 1), ts], (slab, strip)).astype(BF16)
                  for h in range(PEER_HEADS)]
            db = [jnp.broadcast_to(d_ref[h, pl.ds(i, 1), ts], (slab, strip)).astype(BF16)
                  for h in range(PEER_HEADS)]
            slabs = []
            for j0 in range(0, N_KEYS, slab):
                js = slice(j0, j0 + slab)
                gate = None
                for h in range(PEER_HEADS):
                    term = jnp.where(r2_ref[h, js, ts] < nb[h], e2_ref[h, js, ts],
                                     jnp.zeros((), BF16)) * db[h]
                    gate = term if gate is None else gate + term
                slabs.append(gate * act[js, ts])
            strips.append(jnp.concatenate(slabs, axis=0))
        groups.append(jnp.concatenate(strips, axis=1))
    acc_ref[...] += _dot(vt_ref[...], jnp.concatenate(groups, axis=0))

    @pl.when(e == pl.num_programs(1) - 1)
    def _():
        h = h1_ref[...] + acc_ref[...].T
        o_ref[...] = h * lax.rsqrt(jnp.mean(h * h, axis=-1, keepdims=True) + NORM_EPS) * fg_ref[...]


def _peer_experts(hnt, h1, r2, e2, n, d, u, vt, final_g, tt, eb):
    s = hnt.shape[1]
    D = D_MODEL
    H = PEER_HEADS
    sel = pl.BlockSpec((H, N_KEYS, tt), lambda i, e: (0, 0, i), pipeline_mode=pl.Buffered(1))
    return pl.pallas_call(
        _peer_expert_kernel,
        grid=(s // tt, N_EXPERTS // eb),
        in_specs=[pl.BlockSpec((D, tt), lambda i, e: (0, i)),
                  pl.BlockSpec((tt, D), lambda i, e: (i, 0), pipeline_mode=pl.Buffered(1)),
                  sel, sel, sel, sel,
                  pl.BlockSpec((eb, D), lambda i, e: (e, 0)),
                  pl.BlockSpec((D, eb), lambda i, e: (0, e)),
                  pl.BlockSpec((1, D), lambda i, e: (0, 0))],
        out_specs=pl.BlockSpec((tt, D), lambda i, e: (i, 0)),
        out_shape=jax.ShapeDtypeStruct((s, D), F32),
        scratch_shapes=[pltpu.VMEM((D, tt), F32)],
        compiler_params=pltpu.CompilerParams(dimension_semantics=("parallel", "arbitrary"),
                                             vmem_limit_bytes=56 * 1024 * 1024),
        name="peer_experts",
    )(hnt, h1, r2, e2, n, d, u, vt, final_g.reshape(1, D))


def _pad_rows(w, rows):
    return jnp.pad(w, ((0, rows - w.shape[0]), (0, 0)))


def _split_bf16(w):
    hi = w.astype(BF16)
    return hi, (w - hi.astype(F32)).astype(BF16)


def _layer(h, norm1_g, w_in, shift_mu, rwkv_w0, w_decay_up, rwkv_a0, w_iclr_up, w_gate_up,
           k_k, k_a, r_k, lnx_g, lnx_b, lam_q1, lam_k1, lam_q2, lam_k2, subln_g, w_proj_a,
           w_proj_b, w_out, norm2_g, peer_wq, peer_sub_keys, peer_u, peer_v, out_g, lambda_init):
    s = h.shape[0]
    W = RWKV_WIDTH
    tm = min(s, 512)
    tmm = min(s, 1024)

    c0, c1, c2 = 3 * W, 3 * W + DECAY_LORA, 3 * W + DECAY_LORA + ICLR_LORA
    pad_cols = lambda m, n: jnp.pad(m, ((0, 0), (0, n - m.shape[1])))
    w_rwkv = jnp.concatenate([w_in[:, :c0], pad_cols(w_in[:, c0:c1], LORA_PAD),
                              pad_cols(w_in[:, c1:c2], LORA_PAD), w_in[:, c2:RWKV_COLS]], axis=1)
    mu2 = shift_mu.reshape(1, -1)
    mu = jnp.concatenate([mu2[:, :c0], pad_cols(mu2[:, c0:c1], LORA_PAD),
                          pad_cols(mu2[:, c1:c2], LORA_PAD), mu2[:, c2:]], axis=1)
    d0 = RWKV_COLS
    q_scale = DIFF_HEAD_DIM ** -0.5 * math.log2(math.e)
    w_qk = jnp.concatenate([w_in[:, d0:d0 + DIFF_WIDTH] * q_scale,
                            w_in[:, d0 + DIFF_WIDTH:d0 + 2 * DIFF_WIDTH]], axis=1)
    w_vt = w_in[:, d0 + 2 * DIFF_WIDTH:d0 + DIFF_COLS].T
    w_gate = w_in[:, d0 + DIFF_COLS:]

    xn = _rmsnorm(h, norm1_g, NORM_EPS, BF16, tm)
    p_rwkv = _matmul(xn, w_rwkv.astype(BF16), F32, tmm, 512, "in_proj_rwkv")
    p_qk = _matmul(xn, w_qk.astype(BF16), BF16, tmm, 512, "in_proj_qk")
    p_vt = _matmul_nt(w_vt.astype(BF16), xn, BF16, 256, tmm, "in_proj_vt")
    p_gate = _matmul(xn, w_gate.astype(BF16), BF16, tmm, 512, "in_proj_gate")

    y_a = _rwkv_time_mix(p_rwkv, mu, rwkv_w0, _pad_rows(w_decay_up, LORA_PAD), rwkv_a0,
                         _pad_rows(w_iclr_up, LORA_PAD), w_gate_up, k_k, k_a, r_k, lnx_g, lnx_b)
    y_b = _diff_attention(p_qk, p_vt, lam_q1, lam_k1, lam_q2, lam_k2, subln_g, lambda_init)
    h1, hn_hi, hn_lo, hn_t = _merge(y_a, y_b, p_gate, h, w_proj_a.astype(BF16),
                                    w_proj_b.astype(BF16), w_out.astype(BF16), norm2_g,
                                    min(s, 256))

    wq_hi, wq_lo = _split_bf16(peer_wq)
    q = _matmul_split(hn_hi, hn_lo, wq_hi, wq_lo, tmm, 512, "peer_query")
    keys = peer_sub_keys.reshape(2 * PEER_HEADS, N_KEYS, PEER_HALF)
    r2, e2, n, d = _peer_scores(q, keys, min(s, 256))
    return _peer_experts(hn_t, h1, r2, e2, n, d, peer_u.astype(BF16), peer_v.T.astype(BF16),
                         out_g, min(s, 512), 1024)


def kernel(x, norm1_g, w_in, shift_mu, rwkv_w0, w_decay_up, rwkv_a0, w_iclr_up, w_gate_up, k_k, k_a, r_k, lnx_g, lnx_b, lam_q1, lam_k1, lam_q2, lam_k2, subln_g, w_proj_a, w_proj_b, w_out, norm2_g, peer_wq, peer_sub_keys, peer_u, peer_v, final_g):
    B, S, D = x.shape
    assert B == 1 and D == D_MODEL and norm1_g.shape[0] == 1
    lambda_init = 0.8 - 0.6 * math.exp(-0.3 * 0)
    out = _layer(x[0], norm1_g[0], w_in[0], shift_mu[0], rwkv_w0[0], w_decay_up[0], rwkv_a0[0],
                 w_iclr_up[0], w_gate_up[0], k_k[0], k_a[0], r_k[0].reshape(-1), lnx_g[0],
                 lnx_b[0], lam_q1[0], lam_k1[0], lam_q2[0], lam_k2[0], subln_g[0], w_proj_a[0],
                 w_proj_b[0], w_out[0], norm2_g[0], peer_wq[0], peer_sub_keys[0], peer_u[0],
                 peer_v[0], final_g, lambda_init)
    return out[None]
```

```python
import functools
import math

import jax
import jax.numpy as jnp
from jax import lax
from jax.experimental import pallas as pl
from jax.experimental.pallas import tpu as pltpu

F32 = jnp.float32
BF16 = jnp.bfloat16
HIGHEST = lax.Precision.HIGHEST

LANES = 128
SUBLANES = 8
BF16_SUBLANES = 16

D_MODEL = 2048
RWKV_HEADS = 16
RWKV_HEAD_DIM = 64
RWKV_WIDTH = RWKV_HEADS * RWKV_HEAD_DIM
DECAY_LORA = 96
ICLR_LORA = 96
GATE_LORA = 256
LORA_PAD = 128
RWKV_COLS = 3 * RWKV_WIDTH + DECAY_LORA + ICLR_LORA + GATE_LORA
RWKV_COLS_PAD = 3 * RWKV_WIDTH + 2 * LORA_PAD + GATE_LORA
RWKV_CHUNK = 64
RWKV_BLOCK = 128
RWKV_GROUP = 4
RWKV_GROUP_LANES = RWKV_GROUP * RWKV_HEAD_DIM
DIFF_HEADS = 8
DIFF_HEAD_DIM = 64
DIFF_V_DIM = 2 * DIFF_HEAD_DIM
DIFF_WIDTH = DIFF_HEADS * DIFF_V_DIM
DIFF_COLS = 3 * DIFF_WIDTH
ATTN_K_BLOCK = 512
ATTN_Q_BLOCK = 1024
GATE_COLS = 2 * D_MODEL
PEER_HEADS = 8
PEER_HALF = 128
N_KEYS = 128
N_EXPERTS = N_KEYS * N_KEYS
PEER_TOPK = 16
NORM_EPS = 1e-6
LN_X_EPS = 64e-5
SUBLN_EPS = 1e-5
NEG_INF = -1e30


def _dot(a, b, precision=None):
    return jnp.dot(a, b, preferred_element_type=F32, precision=precision)


def _dot_nt(a, b, precision=None):
    return lax.dot_general(a, b, (((1,), (1,)), ((), ())), preferred_element_type=F32,
                           precision=precision)


def _const_spec(shape):
    nd = len(shape)
    return pl.BlockSpec(shape, lambda *_: (0,) * nd)


def _split(x):
    hi = x.astype(BF16)
    return hi, (x - hi.astype(F32)).astype(BF16)


def _dot3(a, b, nt=False):
    f = _dot_nt if nt else _dot
    ah, al = _split(a)
    bh, bl = _split(b)
    return f(ah, bh) + (f(ah, bl) + f(al, bh))


def _mm_kernel(x_ref, w_ref, o_ref):
    o_ref[...] = _dot(x_ref[...], w_ref[...]).astype(o_ref.dtype)


def _matmul(x, w, out_dtype, tm, tn, name):
    s, k = x.shape
    n = w.shape[1]
    return pl.pallas_call(
        _mm_kernel,
        grid=(s // tm, n // tn),
        in_specs=[pl.BlockSpec((tm, k), lambda i, j: (i, 0)),
                  pl.BlockSpec((k, tn), lambda i, j: (0, j))],
        out_specs=pl.BlockSpec((tm, tn), lambda i, j: (i, j)),
        out_shape=jax.ShapeDtypeStruct((s, n), out_dtype),
        compiler_params=pltpu.CompilerParams(dimension_semantics=("parallel", "parallel")),
        name=name,
    )(x, w)


def _norm_mm_kernel(x_ref, g_ref, w_ref, o_ref, xn_ref, *, eps):
    @pl.when(pl.program_id(1) == 0)
    def _():
        x = x_ref[...]
        y = x * lax.rsqrt(jnp.mean(x * x, axis=-1, keepdims=True) + eps) * g_ref[...]
        xn_ref[...] = y.astype(xn_ref.dtype)

    o_ref[...] = _dot(xn_ref[...], w_ref[...]).astype(o_ref.dtype)


def _norm_matmul(x, g, w, eps, out_dtype, tm, tn, name):
    s, k = x.shape
    n = w.shape[1]
    return pl.pallas_call(
        functools.partial(_norm_mm_kernel, eps=eps),
        grid=(s // tm, n // tn),
        in_specs=[pl.BlockSpec((tm, k), lambda i, j: (i, 0)), _const_spec((1, k)),
                  pl.BlockSpec((k, tn), lambda i, j: (0, j))],
        out_specs=[pl.BlockSpec((tm, tn), lambda i, j: (i, j)),
                   pl.BlockSpec((tm, k), lambda i, j: (i, 0))],
        out_shape=[jax.ShapeDtypeStruct((s, n), out_dtype), jax.ShapeDtypeStruct((s, k), w.dtype)],
        compiler_params=pltpu.CompilerParams(dimension_semantics=("parallel", "arbitrary")),
        name=name,
    )(x, g.reshape(1, k), w)


def _mm_nt_kernel(w_ref, x_ref, o_ref):
    o_ref[...] = _dot_nt(w_ref[...], x_ref[...]).astype(o_ref.dtype)


def _matmul_nt(w, x, out_dtype, tn, tm, name):
    n, k = w.shape
    s = x.shape[0]
    return pl.pallas_call(
        _mm_nt_kernel,
        grid=(s // tm, n // tn),
        in_specs=[pl.BlockSpec((tn, k), lambda i, j: (j, 0)),
                  pl.BlockSpec((tm, k), lambda i, j: (i, 0))],
        out_specs=pl.BlockSpec((tn, tm), lambda i, j: (j, i)),
        out_shape=jax.ShapeDtypeStruct((n, s), out_dtype),
        compiler_params=pltpu.CompilerParams(dimension_semantics=("parallel", "parallel")),
        name=name,
    )(w, x)


def _transpose_cast_kernel(x_ref, o_ref):
    o_ref[...] = x_ref[...].T.astype(o_ref.dtype)


def _transpose_cast(x, out_dtype, tb):
    r, c = x.shape
    return pl.pallas_call(
        _transpose_cast_kernel,
        grid=(r // tb, c // tb),
        in_specs=[pl.BlockSpec((tb, tb), lambda i, j: (i, j))],
        out_specs=pl.BlockSpec((tb, tb), lambda i, j: (j, i)),
        out_shape=jax.ShapeDtypeStruct((c, r), out_dtype),
        compiler_params=pltpu.CompilerParams(dimension_semantics=("parallel", "parallel")),
        name="transpose_cast",
    )(x)


def _mm_split_kernel(xh_ref, xl_ref, wh_ref, wl_ref, o_ref):
    xh = xh_ref[...]
    o_ref[...] = _dot(xh, wh_ref[...]) + (_dot(xh, wl_ref[...]) + _dot(xl_ref[...], wh_ref[...]))


def _matmul_split(xh, xl, wh, wl, tm, tn, name):
    s, k = xh.shape
    n = wh.shape[1]
    xspec = pl.BlockSpec((tm, k), lambda i, j: (i, 0))
    wspec = pl.BlockSpec((k, tn), lambda i, j: (0, j))
    return pl.pallas_call(
        _mm_split_kernel,
        grid=(s // tm, n // tn),
        in_specs=[xspec, xspec, wspec, wspec],
        out_specs=pl.BlockSpec((tm, tn), lambda i, j: (i, j)),
        out_shape=jax.ShapeDtypeStruct((s, n), F32),
        compiler_params=pltpu.CompilerParams(dimension_semantics=("parallel", "parallel")),
        name=name,
    )(xh, xl, wh, wl)


def _head_sum(x, ones_bd):
    xh, xl = _split(x)
    tiles = []
    for c in range(x.shape[1] // LANES):
        cs = slice(c * LANES, (c + 1) * LANES)
        tiles.append(_dot(xh[:, cs], ones_bd) + _dot(xl[:, cs], ones_bd))
    return jnp.concatenate(tiles, axis=1)


def _softplus(x):
    return jnp.maximum(x, 0.0) + jnp.log1p(jnp.exp(-jnp.abs(x)))


def _rwkv_kernel(p_ref, pprev_ref, mu_ref, w0_ref, wd_ref, a0_ref, wa_ref, wg_ref, kk_ref,
                 ka_ref, rk_ref, lng_ref, lnb_ref, o_ref, state_ref, y_ref):
    L = RWKV_CHUNK
    N = RWKV_HEAD_DIM
    W = RWKV_WIDTH
    G = RWKV_GROUP
    GL = RWKV_GROUP_LANES
    step = pl.program_id(0)

    @pl.when(step == 0)
    def _():
        state_ref[...] = jnp.zeros_like(state_ref)

    TB = p_ref.shape[0]
    row = lax.broadcasted_iota(jnp.int32, (TB, 1), 0)
    carry_on = jnp.where(step == 0, 0.0, 1.0)

    def shifted(c0, c1):
        p = p_ref[:, c0:c1]
        last = pprev_ref[SUBLANES - 1:SUBLANES, c0:c1] * carry_on
        prev = jnp.where(row == 0, last, pltpu.roll(p, 1, axis=0))
        return p + (prev - p) * mu_ref[:, c0:c1]

    r = shifted(0, W)
    k = shifted(W, 2 * W)
    v = shifted(2 * W, 3 * W)
    xw = shifted(3 * W, 3 * W + LORA_PAD)
    xa = shifted(3 * W + LORA_PAD, 3 * W + 2 * LORA_PAD)
    xg = shifted(3 * W + 2 * LORA_PAD, 3 * W + 2 * LORA_PAD + GATE_LORA)

    z = w0_ref[...] + _dot3(jnp.tanh(xw), wd_ref[...])
    w_log = -_softplus(-z) - 0.5
    lw = -jnp.exp(w_log)
    a = jax.nn.sigmoid(a0_ref[...] + _dot3(xa, wa_ref[...]))
    g = _dot3(jax.nn.sigmoid(xg), wg_ref[...])

    lane_i = lax.broadcasted_iota(jnp.int32, (LANES, LANES), 0) // N
    lane_j = lax.broadcasted_iota(jnp.int32, (LANES, LANES), 1) // N
    ones_bd = jnp.where(lane_i == lane_j, 1.0, 0.0).astype(BF16)

    kk = k * kk_ref[...]
    kk = kk / jnp.maximum(jnp.sqrt(_head_sum(kk * kk, ones_bd)), 1e-12)
    k = k * (1.0 + (a - 1.0) * ka_ref[...])

    ti = lax.broadcasted_iota(jnp.int32, (TB, TB), 0)
    tj = lax.broadcasted_iota(jnp.int32, (TB, TB), 1)
    tril = jnp.where((tj <= ti) & (tj // L == ti // L), 1.0, 0.0).astype(F32)
    gi = lax.broadcasted_iota(jnp.int32, (L, GL), 0)
    gj = lax.broadcasted_iota(jnp.int32, (L, GL), 1) % L
    incl = gj <= gi
    strict = gj < gi
    eye4 = jnp.where(gj == gi, 1.0, 0.0).astype(F32)
    bi = lax.broadcasted_iota(jnp.int32, (GL, GL), 0) // N
    bj = lax.broadcasted_iota(jnp.int32, (GL, GL), 1) // N
    bd_mask = bi == bj

    def bd(x):
        return jnp.where(bd_mask, jnp.concatenate([x] * G, axis=0), jnp.zeros((), x.dtype))

    cum = _dot(tril, lw, HIGHEST)
    e_inv = jnp.exp(-cum)
    r_t = (r * jnp.exp(cum)).astype(BF16)
    a_t = (-kk * jnp.exp(cum - lw)).astype(BF16)
    b = kk * a
    b_t = (b * e_inv).astype(BF16)
    k_t = (k * e_inv).astype(BF16)
    v_b = v.astype(BF16)
    eye4_b = eye4.astype(BF16)
    n_grp = RWKV_HEADS // G
    states = [state_ref[grp] for grp in range(n_grp)]

    n_chunks = TB // L
    pairs = [(c, grp) for c in range(n_chunks) for grp in range(n_grp)]
    rows_of = lambda c: slice(c * L, (c + 1) * L)
    lanes_of = lambda grp: slice(grp * GL, (grp + 1) * GL)
    cum_last = [cum[(c + 1) * L - 1:(c + 1) * L, :] for c in range(n_chunks)]

    a_ab, a_ak, a_rb, a_rk = {}, {}, {}, {}
    for c, grp in pairs:
        rs, gs = rows_of(c), lanes_of(grp)
        m = _dot_nt(jnp.concatenate([a_t[rs, gs], r_t[rs, gs]], axis=0),
                    jnp.concatenate([bd(b_t[rs, gs]), bd(k_t[rs, gs])], axis=0))
        a_ab[c, grp] = jnp.where(strict, m[:L, :GL], 0.0)
        a_ak[c, grp] = jnp.where(strict, m[:L, GL:], 0.0).astype(BF16)
        a_rb[c, grp] = jnp.where(incl, m[L:, :GL], 0.0).astype(BF16)
        a_rk[c, grp] = jnp.where(incl, m[L:, GL:], 0.0).astype(BF16)

    inv = {p: eye4 + a_ab[p] for p in pairs}
    pw = {}
    for p in pairs:
        pw_b = a_ab[p].astype(BF16)
        pw[p] = _dot(pw_b, bd(pw_b))
    for _ in range(4):
        for p in pairs:
            pw_b = pw[p].astype(BF16)
            both = _dot(jnp.concatenate([pw_b, inv[p].astype(BF16)], axis=0), bd(pw_b))
            pw[p], inv[p] = both[:L], inv[p] + both[L:]
    for p in pairs:
        inv[p] = (inv[p] + _dot(inv[p].astype(BF16), bd(pw[p].astype(BF16)))).astype(BF16)

    bk = {}
    for c in range(n_chunks):
        rs = rows_of(c)
        e_tail = jnp.exp(cum_last[c] - cum[rs])
        b_w = (b[rs] * e_tail).astype(BF16)
        k_w = (k[rs] * e_tail).astype(BF16)
        for grp in range(n_grp):
            gs = lanes_of(grp)
            bk[c, grp] = _dot_nt(
                eye4_b, jnp.concatenate([bd(b_w[:, gs]), bd(k_w[:, gs])], axis=0)).astype(BF16)

    for c in range(n_chunks):
        rs = rows_of(c)
        w_last = jnp.exp(cum_last[c])
        bd_z = [bd(states[grp].astype(BF16)) for grp in range(n_grp)]
        bd_v = [bd(v_b[rs, lanes_of(grp)]) for grp in range(n_grp)]
        x = [_dot(jnp.concatenate([a_t[rs, lanes_of(grp)], a_ak[c, grp]], axis=1),
                  jnp.concatenate([bd_z[grp], bd_v[grp]], axis=0)) for grp in range(n_grp)]
        u = [_dot(inv[c, grp], bd(x[grp].astype(BF16))) for grp in range(n_grp)]
        for grp in range(n_grp):
            gs = lanes_of(grp)
            lhs = jnp.concatenate(
                [jnp.concatenate([r_t[rs, gs], a_rb[c, grp], a_rk[c, grp]], axis=1),
                 jnp.concatenate([(eye4 * w_last[:, gs]).astype(BF16), bk[c, grp]], axis=1)],
                axis=0)
            yz = _dot(lhs, jnp.concatenate([bd_z[grp], bd(u[grp].astype(BF16)), bd_v[grp]],
                                           axis=0))
            y_ref[rs, gs] = yz[:L]
            states[grp] = yz[L:]
    for grp in range(n_grp):
        state_ref[grp] = states[grp]

    y = y_ref[...]
    mean = _head_sum(y, ones_bd) * (1.0 / N)
    yc = y - mean
    var = _head_sum(yc * yc, ones_bd) * (1.0 / N)
    yn = yc * lax.rsqrt(var + LN_X_EPS) * lng_ref[...] + lnb_ref[...]
    bonus = _head_sum(r * k * rk_ref[...], ones_bd) * v
    o_ref[...] = ((yn + bonus) * g).astype(o_ref.dtype)


def _rwkv_time_mix(p, mu, w0, wd, a0, wa, wg, k_k, k_a, r_k, lnx_g, lnx_b):
    s = p.shape[0]
    L = RWKV_CHUNK
    W = RWKV_WIDTH
    row = lambda v: v.reshape(1, -1)
    consts = [row(mu), row(w0), wd, row(a0), wa, wg, row(k_k), row(k_a), row(r_k), row(lnx_g),
              row(lnx_b)]
    tb = RWKV_BLOCK
    return pl.pallas_call(
        _rwkv_kernel,
        grid=(s // tb,),
        in_specs=[pl.BlockSpec((tb, RWKV_COLS_PAD), lambda i: (i, 0)),
                  pl.BlockSpec((SUBLANES, RWKV_COLS_PAD),
                               lambda i: (jnp.maximum(i * (tb // SUBLANES) - 1, 0), 0))]
                 + [_const_spec(c.shape) for c in consts],
        out_specs=pl.BlockSpec((tb, W), lambda i: (i, 0)),
        out_shape=jax.ShapeDtypeStruct((s, W), BF16),
        scratch_shapes=[pltpu.VMEM((RWKV_HEADS // RWKV_GROUP, RWKV_HEAD_DIM, RWKV_GROUP_LANES),
                                   F32),
                        pltpu.VMEM((tb, W), F32)],
        compiler_params=pltpu.CompilerParams(dimension_semantics=("arbitrary",)),
        name="rwkv7",
    )(p, p, *consts)


def _diffattn_kernel(q_ref, k_ref, vt_ref, lq1_ref, lk1_ref, lq2_ref, lk2_ref, g_ref, o_ref,
                     sa_ref, sb_ref, mx_ref, m_ref, acc_ref, *, lambda_init):
    TQ = ATTN_Q_BLOCK
    TK = ATTN_K_BLOCK
    DV = DIFF_V_DIM
    qi = pl.program_id(1)
    q = q_ref[...]
    lane = lax.broadcasted_iota(jnp.int32, q.shape, 1)
    zero = jnp.zeros_like(q)
    q_halves = (jnp.where(lane < DIFF_HEAD_DIM, q, zero), jnp.where(lane >= DIFF_HEAD_DIM, q, zero))
    ones = jnp.ones((BF16_SUBLANES, TK), BF16)
    key_i = lax.broadcasted_iota(jnp.int32, (TK, TQ), 0)
    qry_i = lax.broadcasted_iota(jnp.int32, (TK, TQ), 1)

    s_bufs = (sa_ref, sb_ref)

    def scores(j, buf, key_offset=None):
        kb = k_ref[pl.ds(pl.multiple_of(j * TK, TK), TK), :]
        for idx in range(2):
            s = _dot_nt(kb, q_halves[idx])
            if key_offset is not None:
                s = jnp.where(key_i + key_offset <= qry_i, s, NEG_INF)
            s_bufs[buf][idx] = s
            mx_ref[2 * buf + idx] = jnp.max(s, axis=0, keepdims=True)

    def probs(buf):
        out = []
        for idx in range(2):
            m_old = m_ref[idx]
            m_new = jnp.maximum(m_old, mx_ref[2 * buf + idx])
            m_ref[idx] = m_new
            out.append((jnp.exp2(s_bufs[buf][idx] - m_new).astype(BF16),
                        jnp.exp2(m_old - m_new)))
        return out

    def accumulate(j, pa):
        vt = vt_ref[:, pl.ds(pl.multiple_of(j * TK, TK), TK)]
        vext = jnp.concatenate([vt, ones], axis=0)
        for idx, (p, alpha) in enumerate(pa):
            acc_ref[idx] = alpha * acc_ref[idx] + _dot(vext, p)

    def step(j, buf, next_scores):
        pa = probs(buf)
        next_scores()
        accumulate(j, pa)

    m_ref[...] = jnp.full_like(m_ref, NEG_INF)
    acc_ref[...] = jnp.zeros_like(acc_ref)

    @pl.when(qi > 0)
    def _():
        scores(0, 0)

    @pl.loop(0, qi - 1)
    def _(i):
        step(2 * i, 0, lambda: scores(2 * i + 1, 1))
        step(2 * i + 1, 1, lambda: scores(2 * i + 2, 0))

    @pl.when(qi > 0)
    def _():
        step(2 * qi - 2, 0, lambda: scores(2 * qi - 1, 1))
        step(2 * qi - 1, 1, lambda: scores(2 * qi, 0, key_offset=0))

    @pl.when(qi == 0)
    def _():
        scores(0, 0, key_offset=0)

    step(2 * qi, 0, lambda: scores(2 * qi + 1, 1, key_offset=TK))
    step(2 * qi + 1, 1, lambda: None)

    lam = (jnp.exp(jnp.sum(lq1_ref[...] * lk1_ref[...], axis=-1, keepdims=True))
           - jnp.exp(jnp.sum(lq2_ref[...] * lk2_ref[...], axis=-1, keepdims=True))
           + lambda_init)
    o = (acc_ref[0, :DV, :] / acc_ref[0, DV:DV + 1, :]
         - lam * (acc_ref[1, :DV, :] / acc_ref[1, DV:DV + 1, :]))
    o = o * lax.rsqrt(jnp.mean(o * o, axis=0, keepdims=True) + SUBLN_EPS) * g_ref[...]
    o_ref[...] = (o * (1.0 - lambda_init)).T.astype(o_ref.dtype)


def _diff_attention(qk, vt, lq1, lk1, lq2, lk2, subln_g, lambda_init):
    s = qk.shape[0]
    H = DIFF_HEADS
    TQ = ATTN_Q_BLOCK
    TK = ATTN_K_BLOCK
    DV = DIFF_V_DIM
    assert TQ == 2 * TK and s % TQ == 0
    row = lambda v: v.reshape(1, -1)
    lam_specs = [_const_spec((1, DIFF_HEAD_DIM))] * 4
    return pl.pallas_call(
        functools.partial(_diffattn_kernel, lambda_init=lambda_init),
        grid=(H, s // TQ),
        in_specs=[pl.BlockSpec((TQ, DV), lambda h, qi: (qi, h)),
                  pl.BlockSpec((s, DV), lambda h, qi: (0, H + h)),
                  pl.BlockSpec((DV, s), lambda h, qi: (h, 0))]
                 + lam_specs + [_const_spec((DV, 1))],
        out_specs=pl.BlockSpec((TQ, DV), lambda h, qi: (qi, h)),
        out_shape=jax.ShapeDtypeStruct((s, DIFF_WIDTH), BF16),
        scratch_shapes=[pltpu.VMEM((2, TK, TQ), F32), pltpu.VMEM((2, TK, TQ), F32),
                        pltpu.VMEM((4, 1, TQ), F32), pltpu.VMEM((2, 1, TQ), F32),
                        pltpu.VMEM((2, DV + BF16_SUBLANES, TQ), F32)],
        compiler_params=pltpu.CompilerParams(dimension_semantics=("parallel", "arbitrary"),
                                             vmem_limit_bytes=40 * 1024 * 1024),
        name="diff_attention",
    )(qk, qk, vt, row(lq1), row(lk1), row(lq2), row(lk2), subln_g.reshape(DV, 1))


def _merge_kernel(ya_ref, yb_ref, gate_ref, x_ref, wa_ref, wb_ref, wo_ref, g2_ref,
                  h_ref, hh_ref, hl_ref, ht_ref):
    pa = _dot(ya_ref[...], wa_ref[...])
    pb = _dot(yb_ref[...], wb_ref[...])
    ga = jax.nn.sigmoid(gate_ref[:, :D_MODEL].astype(F32))
    gb = jax.nn.sigmoid(gate_ref[:, D_MODEL:].astype(F32))
    merged = ga * pa + gb * pb
    h = x_ref[...] + _dot(merged.astype(BF16), wo_ref[...])
    h_ref[...] = h
    hn = h * lax.rsqrt(jnp.mean(h * h, axis=-1, keepdims=True) + NORM_EPS) * g2_ref[...]
    hi = hn.astype(BF16)
    hh_ref[...] = hi
    hl_ref[...] = (hn - hi.astype(F32)).astype(BF16)
    ht_ref[...] = hn.T.astype(BF16)


def _merge(ya, yb, gate, x, wa, wb, wo, g2, tm):
    s = x.shape[0]
    D = D_MODEL
    rows = lambda w: pl.BlockSpec((tm, w), lambda i: (i, 0))
    single = lambda shape: pl.BlockSpec(shape, lambda i: (0, 0), pipeline_mode=pl.Buffered(1))
    return pl.pallas_call(
        _merge_kernel,
        grid=(s // tm,),
        in_specs=[rows(RWKV_WIDTH), rows(DIFF_WIDTH), rows(GATE_COLS), rows(D),
                  single((RWKV_WIDTH, D)), single((DIFF_WIDTH, D)), single((D, D)),
                  _const_spec((1, D))],
        out_specs=[rows(D), rows(D), rows(D), pl.BlockSpec((D, tm), lambda i: (0, i))],
        out_shape=[jax.ShapeDtypeStruct((s, D), F32), jax.ShapeDtypeStruct((s, D), BF16),
                   jax.ShapeDtypeStruct((s, D), BF16), jax.ShapeDtypeStruct((D, s), BF16)],
        compiler_params=pltpu.CompilerParams(dimension_semantics=("parallel",),
                                             vmem_limit_bytes=56 * 1024 * 1024),
        name="merge_out_proj",
    )(ya, yb, gate, x, wa, wb, wo, g2.reshape(1, D))


def _cmp_exchange(xs, i, l, descending):
    hi = jnp.maximum(xs[i], xs[l])
    lo = jnp.minimum(xs[i], xs[l])
    xs[i], xs[l] = (hi, lo) if descending else (lo, hi)


def _bitonic_merge_desc(xs):
    xs = list(xs)
    n = len(xs)
    j = n // 2
    while j >= 1:
        for i in range(n):
            l = i ^ j
            if l > i:
                _cmp_exchange(xs, i, l, True)
        j //= 2
    return xs


def _bitonic_sort_desc(xs):
    xs = list(xs)
    n = len(xs)
    k = 2
    while k <= n:
        j = k // 2
        while j >= 1:
            for i in range(n):
                l = i ^ j
                if l > i:
                    _cmp_exchange(xs, i, l, (i & k) == 0)
            j //= 2
        k *= 2
    return xs


def _merge_top(a, b):
    n = len(a)
    return _bitonic_merge_desc([jnp.maximum(a[i], b[n - 1 - i]) for i in range(n)])


def _top16_over_rows(s):
    groups = [s[g * SUBLANES:(g + 1) * SUBLANES, :] for g in range(s.shape[0] // SUBLANES)]
    top = _bitonic_sort_desc(groups)
    for shift in (4, 2, 1):
        top = _merge_top(top, [pltpu.roll(x, shift, axis=0) for x in top])
    return top


def _prefix_count(rows, pred):
    def pick(conds, cands):
        if not conds:
            return cands[0]
        half = len(cands) // 2
        return jnp.where(conds[0], pick(conds[1:], cands[half:]), pick(conds[1:], cands[:half]))

    n = len(rows)
    conds = []
    count = None
    step = n // 2
    while step >= 1:
        cands = [rows[lo + step - 1] for lo in range(0, n, 2 * step)]
        c = pred(pick(conds, cands))
        inc = jnp.where(c, float(step), 0.0)
        count = inc if count is None else count + inc
        conds.append(c)
        step //= 2
    return jnp.where(pred(rows[n - 1]), float(n), count)


def _peer_score_kernel(q_ref, keys_ref, r2_ref, e2_ref, n_ref, d_ref):
    K = PEER_TOPK
    T = q_ref.shape[0]
    H = PEER_HEADS
    scores = []
    tops = []
    for hp in range(2 * H):
        s = _dot_nt(keys_ref[hp], q_ref[:, hp * PEER_HALF:(hp + 1) * PEER_HALF], HIGHEST)
        scores.append(s)
        tops.append(_top16_over_rows(s))
    sub = lax.broadcasted_iota(jnp.int32, (SUBLANES, T), 0)

    def by_head(p, i):
        out = tops[p][i]
        for h in range(1, H):
            out = jnp.where(sub == h, tops[2 * h + p][i], out)
        return out

    aa = [by_head(0, i) for i in range(K)]
    bb = [by_head(1, i) for i in range(K)]
    cands = [aa[i] + bb[j] for i in range(K) for j in range(K) if (i + 1) * (j + 1) <= K]
    cands += [jnp.full_like(cands[0], -jnp.inf)] * (-len(cands) % K)
    best = _bitonic_sort_desc(cands[:K])
    for c in range(K, len(cands), K):
        best = _merge_top(best, _bitonic_sort_desc(cands[c:c + K]))
    thr = best[K - 1]
    zsum = jnp.zeros_like(thr)
    for t in best:
        zsum = zsum + jnp.exp(t - best[0])
    inv_z = 1.0 / zsum
    for h in range(H):
        hs = slice(h, h + 1)
        s1, s2 = scores[2 * h], scores[2 * h + 1]
        thr_h = thr[hs]
        b_rows = [bb[m][hs] for m in range(K)]
        cnt = _prefix_count(b_rows, lambda b: s1 + b >= thr_h)
        rank = _prefix_count(b_rows, lambda b: b > s2)
        n_ref[h] = cnt
        r2_ref[h] = rank.astype(BF16)
        d_ref[h] = jnp.exp(s1 - aa[0][hs]) * inv_z[hs]
        e2_ref[h] = jnp.exp(s2 - bb[0][hs]).astype(BF16)


def _peer_scores(q, keys, tt):
    s = q.shape[0]
    H = PEER_HEADS
    out = lambda dtype: jax.ShapeDtypeStruct((H, N_KEYS, s), dtype)
    ospec = pl.BlockSpec((H, N_KEYS, tt), lambda i: (0, 0, i))
    return pl.pallas_call(
        _peer_score_kernel,
        grid=(s // tt,),
        in_specs=[pl.BlockSpec((tt, 2 * H * PEER_HALF), lambda i: (i, 0)),
                  _const_spec((2 * H, N_KEYS, PEER_HALF))],
        out_specs=[ospec] * 4,
        out_shape=[out(BF16), out(BF16), out(F32), out(F32)],
        compiler_params=pltpu.CompilerParams(dimension_semantics=("parallel",)),
        name="peer_scores",
    )(q, keys)


def _peer_expert_kernel(hnt_ref, h1_ref, r2_ref, e2_ref, n_ref, d_ref, u_ref, vt_ref, fg_ref,
                        o_ref, acc_ref):
    e = pl.program_id(1)
    eb = u_ref.shape[0]
    tt = hnt_ref.shape[1]
    ni = eb // N_KEYS
    slab = BF16_SUBLANES
    strip = 2 * LANES

    groups = []
    for ii in range(ni):
        i = e * ni + ii
        rows = slice(ii * N_KEYS, (ii + 1) * N_KEYS)
        pre = _dot(u_ref[rows, :], hnt_ref[...])
        act = (0.5 * pre * (1.0 + lax.erf(pre * math.sqrt(0.5)))).astype(BF16)
        strips = []
        for t0 in range(0, tt, strip):
            ts = slice(t0, t0 + strip)
            nb = [jnp.broadcast_to(n_ref[h, pl.ds(i, 1), ts], (slab, strip)).astype(BF16)
                  for h in range(PEER_HEADS)]
            db = [jnp.broadcast_to(d_ref[h, pl.ds(i, 1), ts], (slab, strip)).astype(BF16)
                  for h in range(PEER_HEADS)]
            slabs = []
            for j0 in range(0, N_KEYS, slab):
                js = slice(j0, j0 + slab)
                gate = None
                for h in range(PEER_HEADS):
                    term = jnp.where(r2_ref[h, js, ts] < nb[h], e2_ref[h, js, ts],
                                     jnp.zeros((), BF16)) * db[h]
                    gate = term if gate is None else gate + term
                slabs.append(gate * act[js, ts])
            strips.append(jnp.concatenate(slabs, axis=0))
        groups.append(jnp.concatenate(strips, axis=1))
    prev = jnp.where(e == 0, 0.0, acc_ref[...])
    acc_ref[...] = prev + _dot(vt_ref[...], jnp.concatenate(groups, axis=0))

    @pl.when(e == pl.num_programs(1) - 1)
    def _():
        h = h1_ref[...] + acc_ref[...].T
        o_ref[...] = h * lax.rsqrt(jnp.mean(h * h, axis=-1, keepdims=True) + NORM_EPS) * fg_ref[...]


def _peer_experts(hnt, h1, r2, e2, n, d, u, vt, final_g, tt, eb):
    s = hnt.shape[1]
    D = D_MODEL
    H = PEER_HEADS
    sel = pl.BlockSpec((H, N_KEYS, tt), lambda i, e: (0, 0, i), pipeline_mode=pl.Buffered(1))
    return pl.pallas_call(
        _peer_expert_kernel,
        grid=(s // tt, N_EXPERTS // eb),
        in_specs=[pl.BlockSpec((D, tt), lambda i, e: (0, i)),
                  pl.BlockSpec((tt, D), lambda i, e: (i, 0), pipeline_mode=pl.Buffered(1)),
                  sel, sel, sel, sel,
                  pl.BlockSpec((eb, D), lambda i, e: (e, 0)),
                  pl.BlockSpec((D, eb), lambda i, e: (0, e)),
                  pl.BlockSpec((1, D), lambda i, e: (0, 0))],
        out_specs=pl.BlockSpec((tt, D), lambda i, e: (i, 0)),
        out_shape=jax.ShapeDtypeStruct((s, D), F32),
        scratch_shapes=[pltpu.VMEM((D, tt), F32)],
        compiler_params=pltpu.CompilerParams(dimension_semantics=("parallel", "arbitrary"),
                                             vmem_limit_bytes=56 * 1024 * 1024),
        name="peer_experts",
    )(hnt, h1, r2, e2, n, d, u, vt, final_g.reshape(1, D))


def _pad_rows(w, rows):
    return jnp.pad(w, ((0, rows - w.shape[0]), (0, 0)))


def _split_bf16(w):
    hi = w.astype(BF16)
    return hi, (w - hi.astype(F32)).astype(BF16)


def _layer(h, norm1_g, w_in, shift_mu, rwkv_w0, w_decay_up, rwkv_a0, w_iclr_up, w_gate_up,
           k_k, k_a, r_k, lnx_g, lnx_b, lam_q1, lam_k1, lam_q2, lam_k2, subln_g, w_proj_a,
           w_proj_b, w_out, norm2_g, peer_wq, peer_sub_keys, peer_u, peer_v, out_g, lambda_init):
    s = h.shape[0]
    W = RWKV_WIDTH
    tmm = min(s, 1024)

    c0, c1, c2 = 3 * W, 3 * W + DECAY_LORA, 3 * W + DECAY_LORA + ICLR_LORA
    pad_cols = lambda m, n: jnp.pad(m, ((0, 0), (0, n - m.shape[1])))
    w_rwkv = jnp.concatenate([w_in[:, :c0], pad_cols(w_in[:, c0:c1], LORA_PAD),
                              pad_cols(w_in[:, c1:c2], LORA_PAD), w_in[:, c2:RWKV_COLS]], axis=1)
    mu2 = shift_mu.reshape(1, -1)
    mu = jnp.concatenate([mu2[:, :c0], pad_cols(mu2[:, c0:c1], LORA_PAD),
                          pad_cols(mu2[:, c1:c2], LORA_PAD), mu2[:, c2:]], axis=1)
    d0 = RWKV_COLS
    q_scale = DIFF_HEAD_DIM ** -0.5 * math.log2(math.e)
    w_qk = jnp.concatenate([w_in[:, d0:d0 + DIFF_WIDTH] * q_scale,
                            w_in[:, d0 + DIFF_WIDTH:d0 + 2 * DIFF_WIDTH]], axis=1)
    w_vt = w_in[:, d0 + 2 * DIFF_WIDTH:d0 + DIFF_COLS].T
    w_gate = w_in[:, d0 + DIFF_COLS:]

    p_rwkv, xn = _norm_matmul(h, norm1_g, w_rwkv.astype(BF16), NORM_EPS, F32, tmm, 512,
                              "in_proj_rwkv")
    p_qk = _matmul(xn, w_qk.astype(BF16), BF16, tmm, 512, "in_proj_qk")
    p_vt = _matmul_nt(w_vt.astype(BF16), xn, BF16, 256, tmm, "in_proj_vt")
    p_gate = _matmul(xn, w_gate.astype(BF16), BF16, tmm, 512, "in_proj_gate")

    y_a = _rwkv_time_mix(p_rwkv, mu, rwkv_w0, _pad_rows(w_decay_up, LORA_PAD), rwkv_a0,
                         _pad_rows(w_iclr_up, LORA_PAD), w_gate_up, k_k, k_a, r_k, lnx_g, lnx_b)
    y_b = _diff_attention(p_qk, p_vt, lam_q1, lam_k1, lam_q2, lam_k2, subln_g, lambda_init)
    h1, hn_hi, hn_lo, hn_t = _merge(y_a, y_b, p_gate, h, w_proj_a.astype(BF16),
                                    w_proj_b.astype(BF16), w_out.astype(BF16), norm2_g,
                                    min(s, 256))

    wq_hi, wq_lo = _split_bf16(peer_wq)
    q = _matmul_split(hn_hi, hn_lo, wq_hi, wq_lo, tmm, 512, "peer_query")
    keys = peer_sub_keys.reshape(2 * PEER_HEADS, N_KEYS, PEER_HALF)
    r2, e2, n, d = _peer_scores(q, keys, min(s, 256))
    return _peer_experts(hn_t, h1, r2, e2, n, d, peer_u.astype(BF16), _transpose_cast(peer_v, BF16, 1024),
                         out_g, min(s, 512), 1024)


def kernel(x, norm1_g, w_in, shift_mu, rwkv_w0, w_decay_up, rwkv_a0, w_iclr_up, w_gate_up, k_k, k_a, r_k, lnx_g, lnx_b, lam_q1, lam_k1, lam_q2, lam_k2, subln_g, w_proj_a, w_proj_b, w_out, norm2_g, peer_wq, peer_sub_keys, peer_u, peer_v, final_g):
    B, S, D = x.shape
    assert B == 1 and D == D_MODEL and norm1_g.shape[0] == 1
    lambda_init = 0.8 - 0.6 * math.exp(-0.3 * 0)
    out = _layer(x[0], norm1_g[0], w_in[0], shift_mu[0], rwkv_w0[0], w_decay_up[0], rwkv_a0[0],
                 w_iclr_up[0], w_gate_up[0], k_k[0], k_a[0], r_k[0].reshape(-1), lnx_g[0],
                 lnx_b[0], lam_q1[0], lam_k1[0], lam_q2[0], lam_k2[0], subln_g[0], w_proj_a[0],
                 w_proj_b[0], w_out[0], norm2_g[0], peer_wq[0], peer_sub_keys[0], peer_u[0],
                 peer_v[0], final_g, lambda_init)
    return out[None]
```

```python
import functools
import math

import jax
import jax.numpy as jnp
from jax import lax
from jax.experimental import pallas as pl
from jax.experimental.pallas import tpu as pltpu

F32 = jnp.float32
BF16 = jnp.bfloat16
HIGHEST = lax.Precision.HIGHEST

LANES = 128
SUBLANES = 8
BF16_SUBLANES = 16

D_MODEL = 2048
RWKV_HEADS = 16
RWKV_HEAD_DIM = 64
RWKV_WIDTH = RWKV_HEADS * RWKV_HEAD_DIM
DECAY_LORA = 96
ICLR_LORA = 96
GATE_LORA = 256
LORA_PAD = 128
RWKV_COLS = 3 * RWKV_WIDTH + DECAY_LORA + ICLR_LORA + GATE_LORA
RWKV_COLS_PAD = 3 * RWKV_WIDTH + 2 * LORA_PAD + GATE_LORA
RWKV_CHUNK = 64
RWKV_BLOCK = 128
RWKV_GROUP = 4
RWKV_GROUP_LANES = RWKV_GROUP * RWKV_HEAD_DIM
DIFF_HEADS = 8
DIFF_HEAD_DIM = 64
DIFF_V_DIM = 2 * DIFF_HEAD_DIM
DIFF_WIDTH = DIFF_HEADS * DIFF_V_DIM
DIFF_COLS = 3 * DIFF_WIDTH
ATTN_K_BLOCK = 512
ATTN_Q_BLOCK = 1024
GATE_COLS = 2 * D_MODEL
PEER_HEADS = 8
PEER_HALF = 128
N_KEYS = 128
N_EXPERTS = N_KEYS * N_KEYS
PEER_TOPK = 16
NORM_EPS = 1e-6
LN_X_EPS = 64e-5
SUBLN_EPS = 1e-5
NEG_INF = -1e30


def _dot(a, b, precision=None):
    return jnp.dot(a, b, preferred_element_type=F32, precision=precision)


def _dot_nt(a, b, precision=None):
    return lax.dot_general(a, b, (((1,), (1,)), ((), ())), preferred_element_type=F32,
                           precision=precision)


def _const_spec(shape):
    nd = len(shape)
    return pl.BlockSpec(shape, lambda *_: (0,) * nd)


def _split(x):
    hi = x.astype(BF16)
    return hi, (x - hi.astype(F32)).astype(BF16)


def _dot3(a, b, nt=False):
    f = _dot_nt if nt else _dot
    ah, al = _split(a)
    bh, bl = _split(b)
    return f(ah, bh) + (f(ah, bl) + f(al, bh))


def _column_blocks(w, tn):
    k, n = w.shape
    return w.reshape(k, n // tn, tn).transpose(1, 0, 2)


def _mm_kernel(x_ref, w_ref, o_ref):
    o_ref[...] = _dot(x_ref[...], w_ref[...]).astype(o_ref.dtype)


def _matmul(x, w, out_dtype, tm, tn, name):
    s, k = x.shape
    n = w.shape[1]
    return pl.pallas_call(
        _mm_kernel,
        grid=(s // tm, n // tn),
        in_specs=[pl.BlockSpec((tm, k), lambda i, j: (i, 0)),
                  pl.BlockSpec((None, k, tn), lambda i, j: (j, 0, 0))],
        out_specs=pl.BlockSpec((tm, tn), lambda i, j: (i, j)),
        out_shape=jax.ShapeDtypeStruct((s, n), out_dtype),
        compiler_params=pltpu.CompilerParams(dimension_semantics=("parallel", "parallel")),
        name=name,
    )(x, _column_blocks(w, tn))


def _norm_mm_kernel(x_ref, g_ref, w_ref, o_ref, xn_ref, *, eps):
    @pl.when(pl.program_id(1) == 0)
    def _():
        x = x_ref[...]
        y = x * lax.rsqrt(jnp.mean(x * x, axis=-1, keepdims=True) + eps) * g_ref[...]
        xn_ref[...] = y.astype(xn_ref.dtype)

    o_ref[...] = _dot(xn_ref[...], w_ref[...]).astype(o_ref.dtype)


def _norm_matmul(x, g, w, eps, out_dtype, tm, tn, name):
    s, k = x.shape
    n = w.shape[1]
    return pl.pallas_call(
        functools.partial(_norm_mm_kernel, eps=eps),
        grid=(s // tm, n // tn),
        in_specs=[pl.BlockSpec((tm, k), lambda i, j: (i, 0)), _const_spec((1, k)),
                  pl.BlockSpec((None, k, tn), lambda i, j: (j, 0, 0))],
        out_specs=[pl.BlockSpec((tm, tn), lambda i, j: (i, j)),
                   pl.BlockSpec((tm, k), lambda i, j: (i, 0))],
        out_shape=[jax.ShapeDtypeStruct((s, n), out_dtype), jax.ShapeDtypeStruct((s, k), w.dtype)],
        compiler_params=pltpu.CompilerParams(dimension_semantics=("parallel", "arbitrary")),
        name=name,
    )(x, g.reshape(1, k), _column_blocks(w, tn))


def _mm_nt_kernel(w_ref, x_ref, o_ref):
    o_ref[...] = _dot_nt(w_ref[...], x_ref[...]).astype(o_ref.dtype)


def _matmul_nt(w, x, out_dtype, tn, tm, name):
    n, k = w.shape
    s = x.shape[0]
    return pl.pallas_call(
        _mm_nt_kernel,
        grid=(s // tm, n // tn),
        in_specs=[pl.BlockSpec((tn, k), lambda i, j: (j, 0)),
                  pl.BlockSpec((tm, k), lambda i, j: (i, 0))],
        out_specs=pl.BlockSpec((tn, tm), lambda i, j: (j, i)),
        out_shape=jax.ShapeDtypeStruct((n, s), out_dtype),
        compiler_params=pltpu.CompilerParams(dimension_semantics=("parallel", "parallel")),
        name=name,
    )(w, x)


def _transpose_cast_kernel(x_ref, o_ref):
    o_ref[0] = x_ref[...].T.astype(o_ref.dtype)


def _transpose_cast_blocks(x, out_dtype, tb):
    r, c = x.shape
    return pl.pallas_call(
        _transpose_cast_kernel,
        grid=(r // tb, c // tb),
        in_specs=[pl.BlockSpec((tb, tb), lambda i, j: (i, j))],
        out_specs=pl.BlockSpec((1, tb, tb), lambda i, j: (i, j, 0)),
        out_shape=jax.ShapeDtypeStruct((r // tb, c, tb), out_dtype),
        compiler_params=pltpu.CompilerParams(dimension_semantics=("parallel", "parallel")),
        name="transpose_cast",
    )(x)


def _mm_split_kernel(xh_ref, xl_ref, wh_ref, wl_ref, o_ref):
    xh = xh_ref[...]
    o_ref[...] = _dot(xh, wh_ref[...]) + (_dot(xh, wl_ref[...]) + _dot(xl_ref[...], wh_ref[...]))


def _matmul_split(xh, xl, wh, wl, tm, tn, name):
    s, k = xh.shape
    n = wh.shape[1]
    xspec = pl.BlockSpec((tm, k), lambda i, j: (i, 0))
    wspec = pl.BlockSpec((None, k, tn), lambda i, j: (j, 0, 0))
    return pl.pallas_call(
        _mm_split_kernel,
        grid=(s // tm, n // tn),
        in_specs=[xspec, xspec, wspec, wspec],
        out_specs=pl.BlockSpec((tm, tn), lambda i, j: (i, j)),
        out_shape=jax.ShapeDtypeStruct((s, n), F32),
        compiler_params=pltpu.CompilerParams(dimension_semantics=("parallel", "parallel")),
        name=name,
    )(xh, xl, _column_blocks(wh, tn), _column_blocks(wl, tn))


def _head_sum(x, ones_bd):
    xh, xl = _split(x)
    tiles = []
    for c in range(x.shape[1] // LANES):
        cs = slice(c * LANES, (c + 1) * LANES)
        tiles.append(_dot(xh[:, cs], ones_bd) + _dot(xl[:, cs], ones_bd))
    return jnp.concatenate(tiles, axis=1)


def _softplus(x):
    return jnp.maximum(x, 0.0) + jnp.log1p(jnp.exp(-jnp.abs(x)))


def _rwkv_kernel(p_ref, pprev_ref, mu_ref, w0_ref, wd_ref, a0_ref, wa_ref, wg_ref, kk_ref,
                 ka_ref, rk_ref, lng_ref, lnb_ref, o_ref, state_ref, y_ref):
    L = RWKV_CHUNK
    N = RWKV_HEAD_DIM
    W = RWKV_WIDTH
    G = RWKV_GROUP
    GL = RWKV_GROUP_LANES
    step = pl.program_id(0)

    @pl.when(step == 0)
    def _():
        state_ref[...] = jnp.zeros_like(state_ref)

    TB = p_ref.shape[0]
    row = lax.broadcasted_iota(jnp.int32, (TB, 1), 0)
    carry_on = jnp.where(step == 0, 0.0, 1.0)

    def shifted(c0, c1):
        p = p_ref[:, c0:c1]
        last = pprev_ref[SUBLANES - 1:SUBLANES, c0:c1] * carry_on
        prev = jnp.where(row == 0, last, pltpu.roll(p, 1, axis=0))
        return p + (prev - p) * mu_ref[:, c0:c1]

    r = shifted(0, W)
    k = shifted(W, 2 * W)
    v = shifted(2 * W, 3 * W)
    xw = shifted(3 * W, 3 * W + LORA_PAD)
    xa = shifted(3 * W + LORA_PAD, 3 * W + 2 * LORA_PAD)
    xg = shifted(3 * W + 2 * LORA_PAD, 3 * W + 2 * LORA_PAD + GATE_LORA)

    z = w0_ref[...] + _dot3(jnp.tanh(xw), wd_ref[...])
    w_log = -_softplus(-z) - 0.5
    lw = -jnp.exp(w_log)
    a = jax.nn.sigmoid(a0_ref[...] + _dot3(xa, wa_ref[...]))
    g = _dot3(jax.nn.sigmoid(xg), wg_ref[...])

    lane_i = lax.broadcasted_iota(jnp.int32, (LANES, LANES), 0) // N
    lane_j = lax.broadcasted_iota(jnp.int32, (LANES, LANES), 1) // N
    ones_bd = jnp.where(lane_i == lane_j, 1.0, 0.0).astype(BF16)

    kk = k * kk_ref[...]
    kk = kk / jnp.maximum(jnp.sqrt(_head_sum(kk * kk, ones_bd)), 1e-12)
    k = k * (1.0 + (a - 1.0) * ka_ref[...])

    ti = lax.broadcasted_iota(jnp.int32, (TB, TB), 0)
    tj = lax.broadcasted_iota(jnp.int32, (TB, TB), 1)
    tril = jnp.where((tj <= ti) & (tj // L == ti // L), 1.0, 0.0).astype(F32)
    gi = lax.broadcasted_iota(jnp.int32, (L, GL), 0)
    gj = lax.broadcasted_iota(jnp.int32, (L, GL), 1) % L
    incl = gj <= gi
    strict = gj < gi
    eye4 = jnp.where(gj == gi, 1.0, 0.0).astype(F32)
    bi = lax.broadcasted_iota(jnp.int32, (GL, GL), 0) // N
    bj = lax.broadcasted_iota(jnp.int32, (GL, GL), 1) // N
    bd_mask = bi == bj

    def bd(x):
        return jnp.where(bd_mask, jnp.concatenate([x] * G, axis=0), jnp.zeros((), x.dtype))

    cum = _dot(tril, lw, HIGHEST)
    e_inv = jnp.exp(-cum)
    r_t = (r * jnp.exp(cum)).astype(BF16)
    a_t = (-kk * jnp.exp(cum - lw)).astype(BF16)
    b = kk * a
    b_t = (b * e_inv).astype(BF16)
    k_t = (k * e_inv).astype(BF16)
    v_b = v.astype(BF16)
    eye4_b = eye4.astype(BF16)
    n_grp = RWKV_HEADS // G
    states = [state_ref[grp] for grp in range(n_grp)]

    n_chunks = TB // L
    pairs = [(c, grp) for c in range(n_chunks) for grp in range(n_grp)]
    rows_of = lambda c: slice(c * L, (c + 1) * L)
    lanes_of = lambda grp: slice(grp * GL, (grp + 1) * GL)
    cum_last = [cum[(c + 1) * L - 1:(c + 1) * L, :] for c in range(n_chunks)]

    a_ab, a_ak, a_rb, a_rk = {}, {}, {}, {}
    for c, grp in pairs:
        rs, gs = rows_of(c), lanes_of(grp)
        m = _dot_nt(jnp.concatenate([a_t[rs, gs], r_t[rs, gs]], axis=0),
                    jnp.concatenate([bd(b_t[rs, gs]), bd(k_t[rs, gs])], axis=0))
        a_ab[c, grp] = jnp.where(strict, m[:L, :GL], 0.0)
        a_ak[c, grp] = jnp.where(strict, m[:L, GL:], 0.0).astype(BF16)
        a_rb[c, grp] = jnp.where(incl, m[L:, :GL], 0.0).astype(BF16)
        a_rk[c, grp] = jnp.where(incl, m[L:, GL:], 0.0).astype(BF16)

    inv = {p: eye4 + a_ab[p] for p in pairs}
    pw = {}
    for p in pairs:
        pw_b = a_ab[p].astype(BF16)
        pw[p] = _dot(pw_b, bd(pw_b))
    for _ in range(4):
        for p in pairs:
            pw_b = pw[p].astype(BF16)
            both = _dot(jnp.concatenate([pw_b, inv[p].astype(BF16)], axis=0), bd(pw_b))
            pw[p], inv[p] = both[:L], inv[p] + both[L:]
    for p in pairs:
        inv[p] = (inv[p] + _dot(inv[p].astype(BF16), bd(pw[p].astype(BF16)))).astype(BF16)

    bk = {}
    for c in range(n_chunks):
        rs = rows_of(c)
        e_tail = jnp.exp(cum_last[c] - cum[rs])
        b_w = (b[rs] * e_tail).astype(BF16)
        k_w = (k[rs] * e_tail).astype(BF16)
        for grp in range(n_grp):
            gs = lanes_of(grp)
            bk[c, grp] = _dot_nt(
                eye4_b, jnp.concatenate([bd(b_w[:, gs]), bd(k_w[:, gs])], axis=0)).astype(BF16)

    for c in range(n_chunks):
        rs = rows_of(c)
        w_last = jnp.exp(cum_last[c])
        bd_z = [bd(states[grp].astype(BF16)) for grp in range(n_grp)]
        bd_v = [bd(v_b[rs, lanes_of(grp)]) for grp in range(n_grp)]
        x = [_dot(jnp.concatenate([a_t[rs, lanes_of(grp)], a_ak[c, grp]], axis=1),
                  jnp.concatenate([bd_z[grp], bd_v[grp]], axis=0)) for grp in range(n_grp)]
        u = [_dot(inv[c, grp], bd(x[grp].astype(BF16))) for grp in range(n_grp)]
        for grp in range(n_grp):
            gs = lanes_of(grp)
            lhs = jnp.concatenate(
                [jnp.concatenate([r_t[rs, gs], a_rb[c, grp], a_rk[c, grp]], axis=1),
                 jnp.concatenate([(eye4 * w_last[:, gs]).astype(BF16), bk[c, grp]], axis=1)],
                axis=0)
            yz = _dot(lhs, jnp.concatenate([bd_z[grp], bd(u[grp].astype(BF16)), bd_v[grp]],
                                           axis=0))
            y_ref[rs, gs] = yz[:L]
            states[grp] = yz[L:]
    for grp in range(n_grp):
        state_ref[grp] = states[grp]

    y = y_ref[...]
    mean = _head_sum(y, ones_bd) * (1.0 / N)
    yc = y - mean
    var = _head_sum(yc * yc, ones_bd) * (1.0 / N)
    yn = yc * lax.rsqrt(var + LN_X_EPS) * lng_ref[...] + lnb_ref[...]
    bonus = _head_sum(r * k * rk_ref[...], ones_bd) * v
    o_ref[...] = ((yn + bonus) * g).astype(o_ref.dtype)


def _rwkv_time_mix(p, mu, w0, wd, a0, wa, wg, k_k, k_a, r_k, lnx_g, lnx_b):
    s = p.shape[0]
    L = RWKV_CHUNK
    W = RWKV_WIDTH
    row = lambda v: v.reshape(1, -1)
    consts = [row(mu), row(w0), wd, row(a0), wa, wg, row(k_k), row(k_a), row(r_k), row(lnx_g),
              row(lnx_b)]
    tb = RWKV_BLOCK
    return pl.pallas_call(
        _rwkv_kernel,
        grid=(s // tb,),
        in_specs=[pl.BlockSpec((tb, RWKV_COLS_PAD), lambda i: (i, 0)),
                  pl.BlockSpec((SUBLANES, RWKV_COLS_PAD),
                               lambda i: (jnp.maximum(i * (tb // SUBLANES) - 1, 0), 0))]
                 + [_const_spec(c.shape) for c in consts],
        out_specs=pl.BlockSpec((tb, W), lambda i: (i, 0)),
        out_shape=jax.ShapeDtypeStruct((s, W), BF16),
        scratch_shapes=[pltpu.VMEM((RWKV_HEADS // RWKV_GROUP, RWKV_HEAD_DIM, RWKV_GROUP_LANES),
                                   F32),
                        pltpu.VMEM((tb, W), F32)],
        compiler_params=pltpu.CompilerParams(dimension_semantics=("arbitrary",)),
        name="rwkv7",
    )(p, p, *consts)


def _diffattn_kernel(q_ref, k_ref, vt_ref, lq1_ref, lk1_ref, lq2_ref, lk2_ref, g_ref, o_ref,
                     sa_ref, sb_ref, mx_ref, m_ref, acc_ref, *, lambda_init):
    TQ = ATTN_Q_BLOCK
    TK = ATTN_K_BLOCK
    DV = DIFF_V_DIM
    qi = pl.program_id(1)
    q = q_ref[...]
    lane = lax.broadcasted_iota(jnp.int32, q.shape, 1)
    zero = jnp.zeros_like(q)
    q_halves = (jnp.where(lane < DIFF_HEAD_DIM, q, zero), jnp.where(lane >= DIFF_HEAD_DIM, q, zero))
    ones = jnp.ones((BF16_SUBLANES, TK), BF16)
    key_i = lax.broadcasted_iota(jnp.int32, (TK, TQ), 0)
    qry_i = lax.broadcasted_iota(jnp.int32, (TK, TQ), 1)

    s_bufs = (sa_ref, sb_ref)

    def scores(j, buf, key_offset=None):
        kb = k_ref[pl.ds(pl.multiple_of(j * TK, TK), TK), :]
        for idx in range(2):
            s = _dot_nt(kb, q_halves[idx])
            if key_offset is not None:
                s = jnp.where(key_i + key_offset <= qry_i, s, NEG_INF)
            s_bufs[buf][idx] = s
            mx_ref[2 * buf + idx] = jnp.max(s, axis=0, keepdims=True)

    def probs(buf):
        out = []
        for idx in range(2):
            m_old = m_ref[idx]
            m_new = jnp.maximum(m_old, mx_ref[2 * buf + idx])
            m_ref[idx] = m_new
            out.append((jnp.exp2(s_bufs[buf][idx] - m_new).astype(BF16),
                        jnp.exp2(m_old - m_new)))
        return out

    def accumulate(j, pa):
        vt = vt_ref[:, pl.ds(pl.multiple_of(j * TK, TK), TK)]
        vext = jnp.concatenate([vt, ones], axis=0)
        for idx, (p, alpha) in enumerate(pa):
            acc_ref[idx] = alpha * acc_ref[idx] + _dot(vext, p)

    def step(j, buf, next_scores):
        pa = probs(buf)
        next_scores()
        accumulate(j, pa)

    m_ref[...] = jnp.full_like(m_ref, NEG_INF)
    acc_ref[...] = jnp.zeros_like(acc_ref)

    @pl.when(qi > 0)
    def _():
        scores(0, 0)

    @pl.loop(0, qi - 1)
    def _(i):
        step(2 * i, 0, lambda: scores(2 * i + 1, 1))
        step(2 * i + 1, 1, lambda: scores(2 * i + 2, 0))

    @pl.when(qi > 0)
    def _():
        step(2 * qi - 2, 0, lambda: scores(2 * qi - 1, 1))
        step(2 * qi - 1, 1, lambda: scores(2 * qi, 0, key_offset=0))

    @pl.when(qi == 0)
    def _():
        scores(0, 0, key_offset=0)

    step(2 * qi, 0, lambda: scores(2 * qi + 1, 1, key_offset=TK))
    step(2 * qi + 1, 1, lambda: None)

    lam = (jnp.exp(jnp.sum(lq1_ref[...] * lk1_ref[...], axis=-1, keepdims=True))
           - jnp.exp(jnp.sum(lq2_ref[...] * lk2_ref[...], axis=-1, keepdims=True))
           + lambda_init)
    o = (acc_ref[0, :DV, :] / acc_ref[0, DV:DV + 1, :]
         - lam * (acc_ref[1, :DV, :] / acc_ref[1, DV:DV + 1, :]))
    o = o * lax.rsqrt(jnp.mean(o * o, axis=0, keepdims=True) + SUBLN_EPS) * g_ref[...]
    o_ref[...] = (o * (1.0 - lambda_init)).T.astype(o_ref.dtype)


def _diff_attention(qk, vt, lq1, lk1, lq2, lk2, subln_g, lambda_init):
    s = qk.shape[0]
    H = DIFF_HEADS
    TQ = ATTN_Q_BLOCK
    TK = ATTN_K_BLOCK
    DV = DIFF_V_DIM
    assert TQ == 2 * TK and s % TQ == 0
    row = lambda v: v.reshape(1, -1)
    lam_specs = [_const_spec((1, DIFF_HEAD_DIM))] * 4
    return pl.pallas_call(
        functools.partial(_diffattn_kernel, lambda_init=lambda_init),
        grid=(H, s // TQ),
        in_specs=[pl.BlockSpec((TQ, DV), lambda h, qi: (qi, h)),
                  pl.BlockSpec((s, DV), lambda h, qi: (0, H + h)),
                  pl.BlockSpec((DV, s), lambda h, qi: (h, 0))]
                 + lam_specs + [_const_spec((DV, 1))],
        out_specs=pl.BlockSpec((TQ, DV), lambda h, qi: (qi, h)),
        out_shape=jax.ShapeDtypeStruct((s, DIFF_WIDTH), BF16),
        scratch_shapes=[pltpu.VMEM((2, TK, TQ), F32), pltpu.VMEM((2, TK, TQ), F32),
                        pltpu.VMEM((4, 1, TQ), F32), pltpu.VMEM((2, 1, TQ), F32),
                        pltpu.VMEM((2, DV + BF16_SUBLANES, TQ), F32)],
        compiler_params=pltpu.CompilerParams(dimension_semantics=("parallel", "arbitrary"),
                                             vmem_limit_bytes=40 * 1024 * 1024),
        name="diff_attention",
    )(qk, qk, vt, row(lq1), row(lk1), row(lq2), row(lk2), subln_g.reshape(DV, 1))


def _merge_kernel(ya_ref, yb_ref, gate_ref, x_ref, wa_ref, wb_ref, wo_ref, g2_ref,
                  h_ref, hh_ref, hl_ref, ht_ref):
    pa = _dot(ya_ref[...], wa_ref[...])
    pb = _dot(yb_ref[...], wb_ref[...])
    ga = jax.nn.sigmoid(gate_ref[:, :D_MODEL].astype(F32))
    gb = jax.nn.sigmoid(gate_ref[:, D_MODEL:].astype(F32))
    merged = ga * pa + gb * pb
    h = x_ref[...] + _dot(merged.astype(BF16), wo_ref[...])
    h_ref[...] = h
    hn = h * lax.rsqrt(jnp.mean(h * h, axis=-1, keepdims=True) + NORM_EPS) * g2_ref[...]
    hi = hn.astype(BF16)
    hh_ref[...] = hi
    hl_ref[...] = (hn - hi.astype(F32)).astype(BF16)
    ht_ref[...] = hn.T.astype(BF16)


def _merge(ya, yb, gate, x, wa, wb, wo, g2, tm):
    s = x.shape[0]
    D = D_MODEL
    rows = lambda w: pl.BlockSpec((tm, w), lambda i: (i, 0))
    single = lambda shape: pl.BlockSpec(shape, lambda i: (0, 0), pipeline_mode=pl.Buffered(1))
    return pl.pallas_call(
        _merge_kernel,
        grid=(s // tm,),
        in_specs=[rows(RWKV_WIDTH), rows(DIFF_WIDTH), rows(GATE_COLS), rows(D),
                  single((RWKV_WIDTH, D)), single((DIFF_WIDTH, D)), single((D, D)),
                  _const_spec((1, D))],
        out_specs=[rows(D), rows(D), rows(D), pl.BlockSpec((D, tm), lambda i: (0, i))],
        out_shape=[jax.ShapeDtypeStruct((s, D), F32), jax.ShapeDtypeStruct((s, D), BF16),
                   jax.ShapeDtypeStruct((s, D), BF16), jax.ShapeDtypeStruct((D, s), BF16)],
        compiler_params=pltpu.CompilerParams(dimension_semantics=("parallel",),
                                             vmem_limit_bytes=56 * 1024 * 1024),
        name="merge_out_proj",
    )(ya, yb, gate, x, wa, wb, wo, g2.reshape(1, D))


def _cmp_exchange(xs, i, l, descending):
    hi = jnp.maximum(xs[i], xs[l])
    lo = jnp.minimum(xs[i], xs[l])
    xs[i], xs[l] = (hi, lo) if descending else (lo, hi)


def _bitonic_merge_desc(xs):
    xs = list(xs)
    n = len(xs)
    j = n // 2
    while j >= 1:
        for i in range(n):
            l = i ^ j
            if l > i:
                _cmp_exchange(xs, i, l, True)
        j //= 2
    return xs


def _bitonic_sort_desc(xs):
    xs = list(xs)
    n = len(xs)
    k = 2
    while k <= n:
        j = k // 2
        while j >= 1:
            for i in range(n):
                l = i ^ j
                if l > i:
                    _cmp_exchange(xs, i, l, (i & k) == 0)
            j //= 2
        k *= 2
    return xs


def _merge_top(a, b):
    n = len(a)
    return _bitonic_merge_desc([jnp.maximum(a[i], b[n - 1 - i]) for i in range(n)])


def _top16_over_rows(s):
    groups = [s[g * SUBLANES:(g + 1) * SUBLANES, :] for g in range(s.shape[0] // SUBLANES)]
    top = _bitonic_sort_desc(groups)
    for shift in (4, 2, 1):
        top = _merge_top(top, [pltpu.roll(x, shift, axis=0) for x in top])
    return top


def _prefix_count(rows, pred):
    def pick(conds, cands):
        if not conds:
            return cands[0]
        half = len(cands) // 2
        return jnp.where(conds[0], pick(conds[1:], cands[half:]), pick(conds[1:], cands[:half]))

    n = len(rows)
    conds = []
    count = None
    step = n // 2
    while step >= 1:
        cands = [rows[lo + step - 1] for lo in range(0, n, 2 * step)]
        c = pred(pick(conds, cands))
        inc = jnp.where(c, float(step), 0.0)
        count = inc if count is None else count + inc
        conds.append(c)
        step //= 2
    return jnp.where(pred(rows[n - 1]), float(n), count)


def _peer_score_kernel(q_ref, keys_ref, r2_ref, e2_ref, n_ref, d_ref):
    K = PEER_TOPK
    T = q_ref.shape[0]
    H = PEER_HEADS
    scores = []
    tops = []
    for hp in range(2 * H):
        s = _dot_nt(keys_ref[hp], q_ref[:, hp * PEER_HALF:(hp + 1) * PEER_HALF], HIGHEST)
        scores.append(s)
        tops.append(_top16_over_rows(s))
    sub = lax.broadcasted_iota(jnp.int32, (SUBLANES, T), 0)

    def by_head(p, i):
        out = tops[p][i]
        for h in range(1, H):
            out = jnp.where(sub == h, tops[2 * h + p][i], out)
        return out

    aa = [by_head(0, i) for i in range(K)]
    bb = [by_head(1, i) for i in range(K)]
    cands = [aa[i] + bb[j] for i in range(K) for j in range(K) if (i + 1) * (j + 1) <= K]
    cands += [jnp.full_like(cands[0], -jnp.inf)] * (-len(cands) % K)
    best = _bitonic_sort_desc(cands[:K])
    for c in range(K, len(cands), K):
        best = _merge_top(best, _bitonic_sort_desc(cands[c:c + K]))
    thr = best[K - 1]
    zsum = jnp.zeros_like(thr)
    for t in best:
        zsum = zsum + jnp.exp(t - best[0])
    inv_z = 1.0 / zsum
    for h in range(H):
        hs = slice(h, h + 1)
        s1, s2 = scores[2 * h], scores[2 * h + 1]
        thr_h = thr[hs]
        b_rows = [bb[m][hs] for m in range(K)]
        cnt = _prefix_count(b_rows, lambda b: s1 + b >= thr_h)
        rank = _prefix_count(b_rows, lambda b: b > s2)
        n_ref[h] = cnt
        r2_ref[h] = rank.astype(BF16)
        d_ref[h] = jnp.exp(s1 - aa[0][hs]) * inv_z[hs]
        e2_ref[h] = jnp.exp(s2 - bb[0][hs]).astype(BF16)


def _peer_scores(q, keys, tt):
    s = q.shape[0]
    H = PEER_HEADS
    out = lambda dtype: jax.ShapeDtypeStruct((H, N_KEYS, s), dtype)
    ospec = pl.BlockSpec((H, N_KEYS, tt), lambda i: (0, 0, i))
    return pl.pallas_call(
        _peer_score_kernel,
        grid=(s // tt,),
        in_specs=[pl.BlockSpec((tt, 2 * H * PEER_HALF), lambda i: (i, 0)),
                  _const_spec((2 * H, N_KEYS, PEER_HALF))],
        out_specs=[ospec] * 4,
        out_shape=[out(BF16), out(BF16), out(F32), out(F32)],
        compiler_params=pltpu.CompilerParams(dimension_semantics=("parallel",)),
        name="peer_scores",
    )(q, keys)


def _peer_expert_kernel(hnt_ref, h1_ref, r2_ref, e2_ref, n_ref, d_ref, u_ref, vt_ref, fg_ref,
                        o_ref, acc_ref):
    e = pl.program_id(1)
    eb = u_ref.shape[0]
    tt = hnt_ref.shape[1]
    ni = eb // N_KEYS
    slab = BF16_SUBLANES
    strip = 2 * LANES

    groups = []
    for ii in range(ni):
        i = e * ni + ii
        rows = slice(ii * N_KEYS, (ii + 1) * N_KEYS)
        pre = _dot(u_ref[rows, :], hnt_ref[...])
        act = (0.5 * pre * (1.0 + lax.erf(pre * math.sqrt(0.5)))).astype(BF16)
        strips = []
        for t0 in range(0, tt, strip):
            ts = slice(t0, t0 + strip)
            nb = [jnp.broadcast_to(n_ref[h, pl.ds(i, 1), ts], (slab, strip)).astype(BF16)
                  for h in range(PEER_HEADS)]
            db = [jnp.broadcast_to(d_ref[h, pl.ds(i, 1), ts], (slab, strip)).astype(BF16)
                  for h in range(PEER_HEADS)]
            slabs = []
            for j0 in range(0, N_KEYS, slab):
                js = slice(j0, j0 + slab)
                gate = None
                for h in range(PEER_HEADS):
                    term = jnp.where(r2_ref[h, js, ts] < nb[h], e2_ref[h, js, ts],
                                     jnp.zeros((), BF16)) * db[h]
                    gate = term if gate is None else gate + term
                slabs.append(gate * act[js, ts])
            strips.append(jnp.concatenate(slabs, axis=0))
        groups.append(jnp.concatenate(strips, axis=1))
    prev = jnp.where(e == 0, 0.0, acc_ref[...])
    acc_ref[...] = prev + _dot(vt_ref[0], jnp.concatenate(groups, axis=0))

    @pl.when(e == pl.num_programs(1) - 1)
    def _():
        h = h1_ref[...] + acc_ref[...].T
        o_ref[...] = h * lax.rsqrt(jnp.mean(h * h, axis=-1, keepdims=True) + NORM_EPS) * fg_ref[...]


def _peer_experts(hnt, h1, r2, e2, n, d, u, vt, final_g, tt, eb):
    s = hnt.shape[1]
    D = D_MODEL
    H = PEER_HEADS
    sel = pl.BlockSpec((H, N_KEYS, tt), lambda i, e: (0, 0, i), pipeline_mode=pl.Buffered(1))
    return pl.pallas_call(
        _peer_expert_kernel,
        grid=(s // tt, N_EXPERTS // eb),
        in_specs=[pl.BlockSpec((D, tt), lambda i, e: (0, i)),
                  pl.BlockSpec((tt, D), lambda i, e: (i, 0), pipeline_mode=pl.Buffered(1)),
                  sel, sel, sel, sel,
                  pl.BlockSpec((eb, D), lambda i, e: (e, 0)),
                  pl.BlockSpec((1, D, eb), lambda i, e: (e, 0, 0)),
                  pl.BlockSpec((1, D), lambda i, e: (0, 0))],
        out_specs=pl.BlockSpec((tt, D), lambda i, e: (i, 0)),
        out_shape=jax.ShapeDtypeStruct((s, D), F32),
        scratch_shapes=[pltpu.VMEM((D, tt), F32)],
        compiler_params=pltpu.CompilerParams(dimension_semantics=("parallel", "arbitrary"),
                                             vmem_limit_bytes=56 * 1024 * 1024),
        name="peer_experts",
    )(hnt, h1, r2, e2, n, d, u, vt, final_g.reshape(1, D))


def _pad_rows(w, rows):
    return jnp.pad(w, ((0, rows - w.shape[0]), (0, 0)))


def _split_bf16(w):
    hi = w.astype(BF16)
    return hi, (w - hi.astype(F32)).astype(BF16)


def _layer(h, norm1_g, w_in, shift_mu, rwkv_w0, w_decay_up, rwkv_a0, w_iclr_up, w_gate_up,
           k_k, k_a, r_k, lnx_g, lnx_b, lam_q1, lam_k1, lam_q2, lam_k2, subln_g, w_proj_a,
           w_proj_b, w_out, norm2_g, peer_wq, peer_sub_keys, peer_u, peer_v, out_g, lambda_init):
    s = h.shape[0]
    W = RWKV_WIDTH
    tmm = min(s, 1024)

    c0, c1, c2 = 3 * W, 3 * W + DECAY_LORA, 3 * W + DECAY_LORA + ICLR_LORA
    pad_cols = lambda m, n: jnp.pad(m, ((0, 0), (0, n - m.shape[1])))
    w_rwkv = jnp.concatenate([w_in[:, :c0], pad_cols(w_in[:, c0:c1], LORA_PAD),
                              pad_cols(w_in[:, c1:c2], LORA_PAD), w_in[:, c2:RWKV_COLS]], axis=1)
    mu2 = shift_mu.reshape(1, -1)
    mu = jnp.concatenate([mu2[:, :c0], pad_cols(mu2[:, c0:c1], LORA_PAD),
                          pad_cols(mu2[:, c1:c2], LORA_PAD), mu2[:, c2:]], axis=1)
    d0 = RWKV_COLS
    q_scale = DIFF_HEAD_DIM ** -0.5 * math.log2(math.e)
    w_qk = jnp.concatenate([w_in[:, d0:d0 + DIFF_WIDTH] * q_scale,
                            w_in[:, d0 + DIFF_WIDTH:d0 + 2 * DIFF_WIDTH]], axis=1)
    w_vt = w_in[:, d0 + 2 * DIFF_WIDTH:d0 + DIFF_COLS].T
    w_gate = w_in[:, d0 + DIFF_COLS:]

    p_rwkv, xn = _norm_matmul(h, norm1_g, w_rwkv.astype(BF16), NORM_EPS, F32, tmm, 512,
                              "in_proj_rwkv")
    p_qk = _matmul(xn, w_qk.astype(BF16), BF16, tmm, 512, "in_proj_qk")
    p_vt = _matmul_nt(w_vt.astype(BF16), xn, BF16, 256, tmm, "in_proj_vt")
    p_gate = _matmul(xn, w_gate.astype(BF16), BF16, tmm, 512, "in_proj_gate")

    y_a = _rwkv_time_mix(p_rwkv, mu, rwkv_w0, _pad_rows(w_decay_up, LORA_PAD), rwkv_a0,
                         _pad_rows(w_iclr_up, LORA_PAD), w_gate_up, k_k, k_a, r_k, lnx_g, lnx_b)
    y_b = _diff_attention(p_qk, p_vt, lam_q1, lam_k1, lam_q2, lam_k2, subln_g, lambda_init)
    h1, hn_hi, hn_lo, hn_t = _merge(y_a, y_b, p_gate, h, w_proj_a.astype(BF16),
                                    w_proj_b.astype(BF16), w_out.astype(BF16), norm2_g,
                                    min(s, 256))

    wq_hi, wq_lo = _split_bf16(peer_wq)
    q = _matmul_split(hn_hi, hn_lo, wq_hi, wq_lo, tmm, 512, "peer_query")
    keys = peer_sub_keys.reshape(2 * PEER_HEADS, N_KEYS, PEER_HALF)
    r2, e2, n, d = _peer_scores(q, keys, min(s, 256))
    eb = 1024
    return _peer_experts(hn_t, h1, r2, e2, n, d, peer_u.astype(BF16),
                         _transpose_cast_blocks(peer_v, BF16, eb), out_g, min(s, 512), eb)


def kernel(x, norm1_g, w_in, shift_mu, rwkv_w0, w_decay_up, rwkv_a0, w_iclr_up, w_gate_up, k_k, k_a, r_k, lnx_g, lnx_b, lam_q1, lam_k1, lam_q2, lam_k2, subln_g, w_proj_a, w_proj_b, w_out, norm2_g, peer_wq, peer_sub_keys, peer_u, peer_v, final_g):
    B, S, D = x.shape
    assert B == 1 and D == D_MODEL and norm1_g.shape[0] == 1
    lambda_init = 0.8 - 0.6 * math.exp(-0.3 * 0)
    out = _layer(x[0], norm1_g[0], w_in[0], shift_mu[0], rwkv_w0[0], w_decay_up[0], rwkv_a0[0],
                 w_iclr_up[0], w_gate_up[0], k_k[0], k_a[0], r_k[0].reshape(-1), lnx_g[0],
                 lnx_b[0], lam_q1[0], lam_k1[0], lam_q2[0], lam_k2[0], subln_g[0], w_proj_a[0],
                 w_proj_b[0], w_out[0], norm2_g[0], peer_wq[0], peer_sub_keys[0], peer_u[0],
                 peer_v[0], final_g, lambda_init)
    return out[None]
```

```python
import functools
import math

import jax
import jax.numpy as jnp
from jax import lax
from jax.experimental import pallas as pl
from jax.experimental.pallas import tpu as pltpu

F32 = jnp.float32
BF16 = jnp.bfloat16
HIGHEST = lax.Precision.HIGHEST

LANES = 128
SUBLANES = 8
BF16_SUBLANES = 16

D_MODEL = 2048
RWKV_HEADS = 16
RWKV_HEAD_DIM = 64
RWKV_WIDTH = RWKV_HEADS * RWKV_HEAD_DIM
DECAY_LORA = 96
ICLR_LORA = 96
GATE_LORA = 256
LORA_PAD = 128
RWKV_COLS = 3 * RWKV_WIDTH + DECAY_LORA + ICLR_LORA + GATE_LORA
RWKV_COLS_PAD = 3 * RWKV_WIDTH + 2 * LORA_PAD + GATE_LORA
RWKV_CHUNK = 64
RWKV_BLOCK = 128
RWKV_GROUP = 4
RWKV_GROUP_LANES = RWKV_GROUP * RWKV_HEAD_DIM
DIFF_HEADS = 8
DIFF_HEAD_DIM = 64
DIFF_V_DIM = 2 * DIFF_HEAD_DIM
DIFF_WIDTH = DIFF_HEADS * DIFF_V_DIM
DIFF_COLS = 3 * DIFF_WIDTH
ATTN_K_BLOCK = 512
ATTN_Q_BLOCK = 1024
GATE_COLS = 2 * D_MODEL
PEER_HEADS = 8
PEER_HALF = 128
N_KEYS = 128
N_EXPERTS = N_KEYS * N_KEYS
PEER_TOPK = 16
NORM_EPS = 1e-6
LN_X_EPS = 64e-5
SUBLN_EPS = 1e-5
NEG_INF = -1e30

MM_ROW_BLOCK = 1024
MM_COL_BLOCK = 512
VT_ROW_BLOCK = 256
MERGE_ROW_BLOCK = 256
PEER_SCORE_BLOCK = 256
PEER_TOKEN_BLOCK = 512
PEER_EXPERT_BLOCK = 1024
MIB = 1024 * 1024
VMEM_LIMIT_RESIDENT = 56 * MIB
VMEM_LIMIT_ATTENTION = 40 * MIB


def _dot(a, b, precision=None):
    return jnp.dot(a, b, preferred_element_type=F32, precision=precision)


def _dot_nt(a, b, precision=None):
    return lax.dot_general(a, b, (((1,), (1,)), ((), ())), preferred_element_type=F32,
                           precision=precision)


def _const_spec(shape):
    nd = len(shape)
    return pl.BlockSpec(shape, lambda *_: (0,) * nd)


def _split(x):
    hi = x.astype(BF16)
    return hi, (x - hi.astype(F32)).astype(BF16)


def _dot3(a, b, nt=False):
    f = _dot_nt if nt else _dot
    ah, al = _split(a)
    bh, bl = _split(b)
    return f(ah, bh) + (f(ah, bl) + f(al, bh))


def _column_blocks(w, tn):
    k, n = w.shape
    return w.reshape(k, n // tn, tn).transpose(1, 0, 2)


def _mm_kernel(x_ref, w_ref, o_ref):
    o_ref[...] = _dot(x_ref[...], w_ref[...]).astype(o_ref.dtype)


def _matmul(x, w, out_dtype, tm, tn, name):
    s, k = x.shape
    n = w.shape[1]
    return pl.pallas_call(
        _mm_kernel,
        grid=(s // tm, n // tn),
        in_specs=[pl.BlockSpec((tm, k), lambda i, j: (i, 0)),
                  pl.BlockSpec((None, k, tn), lambda i, j: (j, 0, 0))],
        out_specs=pl.BlockSpec((tm, tn), lambda i, j: (i, j)),
        out_shape=jax.ShapeDtypeStruct((s, n), out_dtype),
        compiler_params=pltpu.CompilerParams(dimension_semantics=("parallel", "parallel")),
        name=name,
    )(x, _column_blocks(w, tn))


def _norm_mm_kernel(x_ref, g_ref, w_ref, o_ref, xn_ref, *, eps):
    @pl.when(pl.program_id(1) == 0)
    def _():
        x = x_ref[...]
        y = x * lax.rsqrt(jnp.mean(x * x, axis=-1, keepdims=True) + eps) * g_ref[...]
        xn_ref[...] = y.astype(xn_ref.dtype)

    o_ref[...] = _dot(xn_ref[...], w_ref[...]).astype(o_ref.dtype)


def _norm_matmul(x, g, w, eps, out_dtype, tm, tn, name):
    s, k = x.shape
    n = w.shape[1]
    return pl.pallas_call(
        functools.partial(_norm_mm_kernel, eps=eps),
        grid=(s // tm, n // tn),
        in_specs=[pl.BlockSpec((tm, k), lambda i, j: (i, 0)), _const_spec((1, k)),
                  pl.BlockSpec((None, k, tn), lambda i, j: (j, 0, 0))],
        out_specs=[pl.BlockSpec((tm, tn), lambda i, j: (i, j)),
                   pl.BlockSpec((tm, k), lambda i, j: (i, 0))],
        out_shape=[jax.ShapeDtypeStruct((s, n), out_dtype), jax.ShapeDtypeStruct((s, k), w.dtype)],
        compiler_params=pltpu.CompilerParams(dimension_semantics=("parallel", "arbitrary")),
        name=name,
    )(x, g.reshape(1, k), _column_blocks(w, tn))


def _mm_nt_kernel(w_ref, x_ref, o_ref):
    o_ref[...] = _dot_nt(w_ref[...], x_ref[...]).astype(o_ref.dtype)


def _matmul_nt(w, x, out_dtype, tn, tm, name):
    n, k = w.shape
    s = x.shape[0]
    return pl.pallas_call(
        _mm_nt_kernel,
        grid=(s // tm, n // tn),
        in_specs=[pl.BlockSpec((tn, k), lambda i, j: (j, 0)),
                  pl.BlockSpec((tm, k), lambda i, j: (i, 0))],
        out_specs=pl.BlockSpec((tn, tm), lambda i, j: (j, i)),
        out_shape=jax.ShapeDtypeStruct((n, s), out_dtype),
        compiler_params=pltpu.CompilerParams(dimension_semantics=("parallel", "parallel")),
        name=name,
    )(w, x)


def _transpose_cast_kernel(x_ref, o_ref):
    o_ref[0] = x_ref[...].T.astype(o_ref.dtype)


def _transpose_cast_blocks(x, out_dtype, tb):
    r, c = x.shape
    return pl.pallas_call(
        _transpose_cast_kernel,
        grid=(r // tb, c // tb),
        in_specs=[pl.BlockSpec((tb, tb), lambda i, j: (i, j))],
        out_specs=pl.BlockSpec((1, tb, tb), lambda i, j: (i, j, 0)),
        out_shape=jax.ShapeDtypeStruct((r // tb, c, tb), out_dtype),
        compiler_params=pltpu.CompilerParams(dimension_semantics=("parallel", "parallel")),
        name="transpose_cast",
    )(x)


def _mm_split_kernel(xh_ref, xl_ref, wh_ref, wl_ref, o_ref):
    xh = xh_ref[...]
    o_ref[...] = _dot(xh, wh_ref[...]) + (_dot(xh, wl_ref[...]) + _dot(xl_ref[...], wh_ref[...]))


def _matmul_split(xh, xl, wh, wl, tm, tn, name):
    s, k = xh.shape
    n = wh.shape[1]
    xspec = pl.BlockSpec((tm, k), lambda i, j: (i, 0))
    wspec = pl.BlockSpec((None, k, tn), lambda i, j: (j, 0, 0))
    return pl.pallas_call(
        _mm_split_kernel,
        grid=(s // tm, n // tn),
        in_specs=[xspec, xspec, wspec, wspec],
        out_specs=pl.BlockSpec((tm, tn), lambda i, j: (i, j)),
        out_shape=jax.ShapeDtypeStruct((s, n), F32),
        compiler_params=pltpu.CompilerParams(dimension_semantics=("parallel", "parallel")),
        name=name,
    )(xh, xl, _column_blocks(wh, tn), _column_blocks(wl, tn))


def _head_sum(x, ones_bd):
    xh, xl = _split(x)
    width = ones_bd.shape[0]
    tiles = []
    for c in range(x.shape[1] // width):
        cs = slice(c * width, (c + 1) * width)
        tiles.append(_dot(xh[:, cs], ones_bd) + _dot(xl[:, cs], ones_bd))
    return jnp.concatenate(tiles, axis=1)


def _softplus(x):
    return jnp.maximum(x, 0.0) + jnp.log1p(jnp.exp(-jnp.abs(x)))


def _rwkv_kernel(p_ref, pprev_ref, mu_ref, w0_ref, wd_ref, a0_ref, wa_ref, wg_ref, kk_ref,
                 ka_ref, rk_ref, lng_ref, lnb_ref, o_ref, state_ref, y_ref):
    L = RWKV_CHUNK
    N = RWKV_HEAD_DIM
    W = RWKV_WIDTH
    G = RWKV_GROUP
    GL = RWKV_GROUP_LANES
    step = pl.program_id(0)

    @pl.when(step == 0)
    def _():
        state_ref[...] = jnp.zeros_like(state_ref)

    TB = p_ref.shape[0]
    row = lax.broadcasted_iota(jnp.int32, (TB, 1), 0)
    carry_on = jnp.where(step == 0, 0.0, 1.0)

    def shifted(c0, c1):
        p = p_ref[:, c0:c1]
        last = pprev_ref[SUBLANES - 1:SUBLANES, c0:c1] * carry_on
        prev = jnp.where(row == 0, last, pltpu.roll(p, 1, axis=0))
        return p + (prev - p) * mu_ref[:, c0:c1]

    r = shifted(0, W)
    k = shifted(W, 2 * W)
    v = shifted(2 * W, 3 * W)
    xw = shifted(3 * W, 3 * W + LORA_PAD)
    xa = shifted(3 * W + LORA_PAD, 3 * W + 2 * LORA_PAD)
    xg = shifted(3 * W + 2 * LORA_PAD, 3 * W + 2 * LORA_PAD + GATE_LORA)

    z = w0_ref[...] + _dot3(jnp.tanh(xw), wd_ref[...])
    w_log = -_softplus(-z) - 0.5
    lw = -jnp.exp(w_log)
    a = jax.nn.sigmoid(a0_ref[...] + _dot3(xa, wa_ref[...]))
    g = _dot3(jax.nn.sigmoid(xg), wg_ref[...])

    bi = lax.broadcasted_iota(jnp.int32, (GL, GL), 0) // N
    bj = lax.broadcasted_iota(jnp.int32, (GL, GL), 1) // N
    bd_mask = bi == bj
    ones_bd = jnp.where(bd_mask, 1.0, 0.0).astype(BF16)

    kk = k * kk_ref[...]
    kk = kk / jnp.maximum(jnp.sqrt(_head_sum(kk * kk, ones_bd)), 1e-12)
    k = k * (1.0 + (a - 1.0) * ka_ref[...])

    ti = lax.broadcasted_iota(jnp.int32, (TB, TB), 0)
    tj = lax.broadcasted_iota(jnp.int32, (TB, TB), 1)
    tril = jnp.where((tj <= ti) & (tj // L == ti // L), 1.0, 0.0).astype(F32)
    gi = lax.broadcasted_iota(jnp.int32, (L, GL), 0)
    gj = lax.broadcasted_iota(jnp.int32, (L, GL), 1) % L
    incl = gj <= gi
    strict = gj < gi
    eye4 = jnp.where(gj == gi, 1.0, 0.0).astype(F32)
    def bd(x):
        return jnp.where(bd_mask, jnp.concatenate([x] * G, axis=0), jnp.zeros((), x.dtype))

    cum = _dot(tril, lw, HIGHEST)
    e_inv = jnp.exp(-cum)
    r_t = (r * jnp.exp(cum)).astype(BF16)
    a_t = (-kk * jnp.exp(cum - lw)).astype(BF16)
    b = kk * a
    b_t = (b * e_inv).astype(BF16)
    k_t = (k * e_inv).astype(BF16)
    v_b = v.astype(BF16)
    eye4_b = eye4.astype(BF16)
    n_grp = RWKV_HEADS // G
    states = [state_ref[grp] for grp in range(n_grp)]

    n_chunks = TB // L
    pairs = [(c, grp) for c in range(n_chunks) for grp in range(n_grp)]
    rows_of = lambda c: slice(c * L, (c + 1) * L)
    lanes_of = lambda grp: slice(grp * GL, (grp + 1) * GL)
    cum_last = [cum[(c + 1) * L - 1:(c + 1) * L, :] for c in range(n_chunks)]

    a_ab, a_ak, a_rb, a_rk = {}, {}, {}, {}
    for c, grp in pairs:
        rs, gs = rows_of(c), lanes_of(grp)
        m = _dot_nt(jnp.concatenate([a_t[rs, gs], r_t[rs, gs]], axis=0),
                    jnp.concatenate([bd(b_t[rs, gs]), bd(k_t[rs, gs])], axis=0))
        a_ab[c, grp] = jnp.where(strict, m[:L, :GL], 0.0)
        a_ak[c, grp] = jnp.where(strict, m[:L, GL:], 0.0).astype(BF16)
        a_rb[c, grp] = jnp.where(incl, m[L:, :GL], 0.0).astype(BF16)
        a_rk[c, grp] = jnp.where(incl, m[L:, GL:], 0.0).astype(BF16)

    inv = {p: eye4 + a_ab[p] for p in pairs}
    pw = {}
    for p in pairs:
        pw_b = a_ab[p].astype(BF16)
        pw[p] = _dot(pw_b, bd(pw_b))
    for _ in range(4):
        for p in pairs:
            pw_b = pw[p].astype(BF16)
            both = _dot(jnp.concatenate([pw_b, inv[p].astype(BF16)], axis=0), bd(pw_b))
            pw[p], inv[p] = both[:L], inv[p] + both[L:]
    for p in pairs:
        inv[p] = (inv[p] + _dot(inv[p].astype(BF16), bd(pw[p].astype(BF16)))).astype(BF16)

    bk = {}
    for c in range(n_chunks):
        rs = rows_of(c)
        e_tail = jnp.exp(cum_last[c] - cum[rs])
        b_w = (b[rs] * e_tail).astype(BF16)
        k_w = (k[rs] * e_tail).astype(BF16)
        for grp in range(n_grp):
            gs = lanes_of(grp)
            bk[c, grp] = _dot_nt(
                eye4_b, jnp.concatenate([bd(b_w[:, gs]), bd(k_w[:, gs])], axis=0)).astype(BF16)

    for c in range(n_chunks):
        rs = rows_of(c)
        w_last = jnp.exp(cum_last[c])
        bd_z = [bd(states[grp].astype(BF16)) for grp in range(n_grp)]
        bd_v = [bd(v_b[rs, lanes_of(grp)]) for grp in range(n_grp)]
        x = [_dot(jnp.concatenate([a_t[rs, lanes_of(grp)], a_ak[c, grp]], axis=1),
                  jnp.concatenate([bd_z[grp], bd_v[grp]], axis=0)) for grp in range(n_grp)]
        u = [_dot(inv[c, grp], bd(x[grp].astype(BF16))) for grp in range(n_grp)]
        for grp in range(n_grp):
            gs = lanes_of(grp)
            lhs = jnp.concatenate(
                [jnp.concatenate([r_t[rs, gs], a_rb[c, grp], a_rk[c, grp]], axis=1),
                 jnp.concatenate([(eye4 * w_last[:, gs]).astype(BF16), bk[c, grp]], axis=1)],
                axis=0)
            yz = _dot(lhs, jnp.concatenate([bd_z[grp], bd(u[grp].astype(BF16)), bd_v[grp]],
                                           axis=0))
            y_ref[rs, gs] = yz[:L]
            states[grp] = yz[L:]
    for grp in range(n_grp):
        state_ref[grp] = states[grp]

    y = y_ref[...]
    mean = _head_sum(y, ones_bd) * (1.0 / N)
    yc = y - mean
    var = _head_sum(yc * yc, ones_bd) * (1.0 / N)
    yn = yc * lax.rsqrt(var + LN_X_EPS) * lng_ref[...] + lnb_ref[...]
    bonus = _head_sum(r * k * rk_ref[...], ones_bd) * v
    o_ref[...] = ((yn + bonus) * g).astype(o_ref.dtype)


def _rwkv_time_mix(p, mu, w0, wd, a0, wa, wg, k_k, k_a, r_k, lnx_g, lnx_b):
    s = p.shape[0]
    L = RWKV_CHUNK
    W = RWKV_WIDTH
    row = lambda v: v.reshape(1, -1)
    consts = [row(mu), row(w0), wd, row(a0), wa, wg, row(k_k), row(k_a), row(r_k), row(lnx_g),
              row(lnx_b)]
    tb = RWKV_BLOCK
    return pl.pallas_call(
        _rwkv_kernel,
        grid=(s // tb,),
        in_specs=[pl.BlockSpec((tb, RWKV_COLS_PAD), lambda i: (i, 0)),
                  pl.BlockSpec((SUBLANES, RWKV_COLS_PAD),
                               lambda i: (jnp.maximum(i * (tb // SUBLANES) - 1, 0), 0))]
                 + [_const_spec(c.shape) for c in consts],
        out_specs=pl.BlockSpec((tb, W), lambda i: (i, 0)),
        out_shape=jax.ShapeDtypeStruct((s, W), BF16),
        scratch_shapes=[pltpu.VMEM((RWKV_HEADS // RWKV_GROUP, RWKV_HEAD_DIM, RWKV_GROUP_LANES),
                                   F32),
                        pltpu.VMEM((tb, W), F32)],
        compiler_params=pltpu.CompilerParams(dimension_semantics=("arbitrary",)),
        name="rwkv7",
    )(p, p, *consts)


def _diffattn_kernel(q_ref, k_ref, vt_ref, lq1_ref, lk1_ref, lq2_ref, lk2_ref, g_ref, o_ref,
                     sa_ref, sb_ref, mx_ref, m_ref, acc_ref, *, lambda_init):
    TQ = ATTN_Q_BLOCK
    TK = ATTN_K_BLOCK
    DV = DIFF_V_DIM
    qi = pl.program_id(1)
    q = q_ref[...]
    lane = lax.broadcasted_iota(jnp.int32, q.shape, 1)
    zero = jnp.zeros_like(q)
    q_halves = (jnp.where(lane < DIFF_HEAD_DIM, q, zero), jnp.where(lane >= DIFF_HEAD_DIM, q, zero))
    ones = jnp.ones((BF16_SUBLANES, TK), BF16)
    key_i = lax.broadcasted_iota(jnp.int32, (TK, TQ), 0)
    qry_i = lax.broadcasted_iota(jnp.int32, (TK, TQ), 1)

    s_bufs = (sa_ref, sb_ref)

    def scores(j, buf, key_offset=None):
        kb = k_ref[pl.ds(pl.multiple_of(j * TK, TK), TK), :]
        for idx in range(2):
            s = _dot_nt(kb, q_halves[idx])
            if key_offset is not None:
                s = jnp.where(key_i + key_offset <= qry_i, s, NEG_INF)
            s_bufs[buf][idx] = s
            mx_ref[2 * buf + idx] = jnp.max(s, axis=0, keepdims=True)

    def probs(buf):
        out = []
        for idx in range(2):
            m_old = m_ref[idx]
            m_new = jnp.maximum(m_old, mx_ref[2 * buf + idx])
            m_ref[idx] = m_new
            out.append((jnp.exp2(s_bufs[buf][idx] - m_new).astype(BF16),
                        jnp.exp2(m_old - m_new)))
        return out

    def accumulate(j, pa):
        vt = vt_ref[:, pl.ds(pl.multiple_of(j * TK, TK), TK)]
        vext = jnp.concatenate([vt, ones], axis=0)
        for idx, (p, alpha) in enumerate(pa):
            acc_ref[idx] = alpha * acc_ref[idx] + _dot(vext, p)

    def step(j, buf, next_scores):
        pa = probs(buf)
        next_scores()
        accumulate(j, pa)

    m_ref[...] = jnp.full_like(m_ref, NEG_INF)
    acc_ref[...] = jnp.zeros_like(acc_ref)

    @pl.when(qi > 0)
    def _():
        scores(0, 0)

    @pl.loop(0, qi - 1)
    def _(i):
        step(2 * i, 0, lambda: scores(2 * i + 1, 1))
        step(2 * i + 1, 1, lambda: scores(2 * i + 2, 0))

    @pl.when(qi > 0)
    def _():
        step(2 * qi - 2, 0, lambda: scores(2 * qi - 1, 1))
        step(2 * qi - 1, 1, lambda: scores(2 * qi, 0, key_offset=0))

    @pl.when(qi == 0)
    def _():
        scores(0, 0, key_offset=0)

    step(2 * qi, 0, lambda: scores(2 * qi + 1, 1, key_offset=TK))
    step(2 * qi + 1, 1, lambda: None)

    lam = (jnp.exp(jnp.sum(lq1_ref[...] * lk1_ref[...], axis=-1, keepdims=True))
           - jnp.exp(jnp.sum(lq2_ref[...] * lk2_ref[...], axis=-1, keepdims=True))
           + lambda_init)
    o = (acc_ref[0, :DV, :] / acc_ref[0, DV:DV + 1, :]
         - lam * (acc_ref[1, :DV, :] / acc_ref[1, DV:DV + 1, :]))
    o = o * lax.rsqrt(jnp.mean(o * o, axis=0, keepdims=True) + SUBLN_EPS) * g_ref[...]
    o_ref[...] = (o * (1.0 - lambda_init)).T.astype(o_ref.dtype)


def _diff_attention(qk, vt, lq1, lk1, lq2, lk2, subln_g, lambda_init):
    s = qk.shape[0]
    H = DIFF_HEADS
    TQ = ATTN_Q_BLOCK
    TK = ATTN_K_BLOCK
    DV = DIFF_V_DIM
    assert TQ == 2 * TK and s % TQ == 0
    row = lambda v: v.reshape(1, -1)
    lam_specs = [_const_spec((1, DIFF_HEAD_DIM))] * 4
    return pl.pallas_call(
        functools.partial(_diffattn_kernel, lambda_init=lambda_init),
        grid=(H, s // TQ),
        in_specs=[pl.BlockSpec((TQ, DV), lambda h, qi: (qi, h)),
                  pl.BlockSpec((s, DV), lambda h, qi: (0, H + h)),
                  pl.BlockSpec((DV, s), lambda h, qi: (h, 0))]
                 + lam_specs + [_const_spec((DV, 1))],
        out_specs=pl.BlockSpec((TQ, DV), lambda h, qi: (qi, h)),
        out_shape=jax.ShapeDtypeStruct((s, DIFF_WIDTH), BF16),
        scratch_shapes=[pltpu.VMEM((2, TK, TQ), F32), pltpu.VMEM((2, TK, TQ), F32),
                        pltpu.VMEM((4, 1, TQ), F32), pltpu.VMEM((2, 1, TQ), F32),
                        pltpu.VMEM((2, DV + BF16_SUBLANES, TQ), F32)],
        compiler_params=pltpu.CompilerParams(dimension_semantics=("parallel", "arbitrary"),
                                             vmem_limit_bytes=VMEM_LIMIT_ATTENTION),
        name="diff_attention",
    )(qk, qk, vt, row(lq1), row(lk1), row(lq2), row(lk2), subln_g.reshape(DV, 1))


def _merge_kernel(ya_ref, yb_ref, gate_ref, x_ref, wa_ref, wb_ref, wo_ref, g2_ref,
                  h_ref, hh_ref, hl_ref, ht_ref):
    pa = _dot(ya_ref[...], wa_ref[...])
    pb = _dot(yb_ref[...], wb_ref[...])
    ga = jax.nn.sigmoid(gate_ref[:, :D_MODEL].astype(F32))
    gb = jax.nn.sigmoid(gate_ref[:, D_MODEL:].astype(F32))
    merged = ga * pa + gb * pb
    h = x_ref[...] + _dot(merged.astype(BF16), wo_ref[...])
    h_ref[...] = h
    hn = h * lax.rsqrt(jnp.mean(h * h, axis=-1, keepdims=True) + NORM_EPS) * g2_ref[...]
    hi = hn.astype(BF16)
    hh_ref[...] = hi
    hl_ref[...] = (hn - hi.astype(F32)).astype(BF16)
    ht_ref[...] = hn.T.astype(BF16)


def _merge(ya, yb, gate, x, wa, wb, wo, g2, tm):
    s = x.shape[0]
    D = D_MODEL
    rows = lambda w: pl.BlockSpec((tm, w), lambda i: (i, 0))
    single = lambda shape: pl.BlockSpec(shape, lambda i: (0, 0), pipeline_mode=pl.Buffered(1))
    return pl.pallas_call(
        _merge_kernel,
        grid=(s // tm,),
        in_specs=[rows(RWKV_WIDTH), rows(DIFF_WIDTH), rows(GATE_COLS), rows(D),
                  single((RWKV_WIDTH, D)), single((DIFF_WIDTH, D)), single((D, D)),
                  _const_spec((1, D))],
        out_specs=[rows(D), rows(D), rows(D), pl.BlockSpec((D, tm), lambda i: (0, i))],
        out_shape=[jax.ShapeDtypeStruct((s, D), F32), jax.ShapeDtypeStruct((s, D), BF16),
                   jax.ShapeDtypeStruct((s, D), BF16), jax.ShapeDtypeStruct((D, s), BF16)],
        compiler_params=pltpu.CompilerParams(dimension_semantics=("parallel",),
                                             vmem_limit_bytes=VMEM_LIMIT_RESIDENT),
        name="merge_out_proj",
    )(ya, yb, gate, x, wa, wb, wo, g2.reshape(1, D))


def _cmp_exchange(xs, i, l, descending):
    hi = jnp.maximum(xs[i], xs[l])
    lo = jnp.minimum(xs[i], xs[l])
    xs[i], xs[l] = (hi, lo) if descending else (lo, hi)


def _bitonic_merge_desc(xs):
    xs = list(xs)
    n = len(xs)
    j = n // 2
    while j >= 1:
        for i in range(n):
            l = i ^ j
            if l > i:
                _cmp_exchange(xs, i, l, True)
        j //= 2
    return xs


def _bitonic_sort_desc(xs):
    xs = list(xs)
    n = len(xs)
    k = 2
    while k <= n:
        j = k // 2
        while j >= 1:
            for i in range(n):
                l = i ^ j
                if l > i:
                    _cmp_exchange(xs, i, l, (i & k) == 0)
            j //= 2
        k *= 2
    return xs


def _merge_top(a, b):
    n = len(a)
    return _bitonic_merge_desc([jnp.maximum(a[i], b[n - 1 - i]) for i in range(n)])


def _top16_over_rows(s):
    groups = [s[g * SUBLANES:(g + 1) * SUBLANES, :] for g in range(s.shape[0] // SUBLANES)]
    top = _bitonic_sort_desc(groups)
    for shift in (4, 2, 1):
        top = _merge_top(top, [pltpu.roll(x, shift, axis=0) for x in top])
    return top


def _prefix_count(rows, pred):
    def pick(conds, cands):
        if not conds:
            return cands[0]
        half = len(cands) // 2
        return jnp.where(conds[0], pick(conds[1:], cands[half:]), pick(conds[1:], cands[:half]))

    n = len(rows)
    conds = []
    count = None
    step = n // 2
    while step >= 1:
        cands = [rows[lo + step - 1] for lo in range(0, n, 2 * step)]
        c = pred(pick(conds, cands))
        inc = jnp.where(c, float(step), 0.0)
        count = inc if count is None else count + inc
        conds.append(c)
        step //= 2
    return jnp.where(pred(rows[n - 1]), float(n), count)


def _peer_score_kernel(q_ref, keys_ref, r2_ref, e2_ref, n_ref, d_ref):
    K = PEER_TOPK
    T = q_ref.shape[0]
    H = PEER_HEADS
    scores = []
    tops = []
    for hp in range(2 * H):
        s = _dot_nt(keys_ref[hp], q_ref[:, hp * PEER_HALF:(hp + 1) * PEER_HALF], HIGHEST)
        scores.append(s)
        tops.append(_top16_over_rows(s))
    sub = lax.broadcasted_iota(jnp.int32, (SUBLANES, T), 0)

    def by_head(p, i):
        out = tops[p][i]
        for h in range(1, H):
            out = jnp.where(sub == h, tops[2 * h + p][i], out)
        return out

    aa = [by_head(0, i) for i in range(K)]
    bb = [by_head(1, i) for i in range(K)]
    cands = [aa[i] + bb[j] for i in range(K) for j in range(K) if (i + 1) * (j + 1) <= K]
    cands += [jnp.full_like(cands[0], -jnp.inf)] * (-len(cands) % K)
    best = _bitonic_sort_desc(cands[:K])
    for c in range(K, len(cands), K):
        best = _merge_top(best, _bitonic_sort_desc(cands[c:c + K]))
    thr = best[K - 1]
    zsum = jnp.zeros_like(thr)
    for t in best:
        zsum = zsum + jnp.exp(t - best[0])
    inv_z = 1.0 / zsum
    for h in range(H):
        hs = slice(h, h + 1)
        s1, s2 = scores[2 * h], scores[2 * h + 1]
        thr_h = thr[hs]
        b_rows = [bb[m][hs] for m in range(K)]
        cnt = _prefix_count(b_rows, lambda b: s1 + b >= thr_h)
        rank = _prefix_count(b_rows, lambda b: b > s2)
        n_ref[h] = cnt
        r2_ref[h] = rank.astype(BF16)
        d_ref[h] = jnp.exp(s1 - aa[0][hs]) * inv_z[hs]
        e2_ref[h] = jnp.exp(s2 - bb[0][hs]).astype(BF16)


def _peer_scores(q, keys, tt):
    s = q.shape[0]
    H = PEER_HEADS
    out = lambda dtype: jax.ShapeDtypeStruct((H, N_KEYS, s), dtype)
    ospec = pl.BlockSpec((H, N_KEYS, tt), lambda i: (0, 0, i))
    return pl.pallas_call(
        _peer_score_kernel,
        grid=(s // tt,),
        in_specs=[pl.BlockSpec((tt, 2 * H * PEER_HALF), lambda i: (i, 0)),
                  _const_spec((2 * H, N_KEYS, PEER_HALF))],
        out_specs=[ospec] * 4,
        out_shape=[out(BF16), out(BF16), out(F32), out(F32)],
        compiler_params=pltpu.CompilerParams(dimension_semantics=("parallel",)),
        name="peer_scores",
    )(q, keys)


def _peer_expert_kernel(hnt_ref, h1_ref, r2_ref, e2_ref, n_ref, d_ref, u_ref, vt_ref, fg_ref,
                        o_ref, acc_ref):
    e = pl.program_id(1)
    eb = u_ref.shape[0]
    tt = hnt_ref.shape[1]
    ni = eb // N_KEYS
    slab = BF16_SUBLANES
    strip = 2 * LANES

    groups = []
    for ii in range(ni):
        i = e * ni + ii
        rows = slice(ii * N_KEYS, (ii + 1) * N_KEYS)
        pre = _dot(u_ref[rows, :], hnt_ref[...])
        act = (0.5 * pre * (1.0 + lax.erf(pre * math.sqrt(0.5)))).astype(BF16)
        strips = []
        for t0 in range(0, tt, strip):
            ts = slice(t0, t0 + strip)
            nb = [jnp.broadcast_to(n_ref[h, pl.ds(i, 1), ts], (slab, strip)).astype(BF16)
                  for h in range(PEER_HEADS)]
            db = [jnp.broadcast_to(d_ref[h, pl.ds(i, 1), ts], (slab, strip)).astype(BF16)
                  for h in range(PEER_HEADS)]
            slabs = []
            for j0 in range(0, N_KEYS, slab):
                js = slice(j0, j0 + slab)
                gate = None
                for h in range(PEER_HEADS):
                    term = jnp.where(r2_ref[h, js, ts] < nb[h], e2_ref[h, js, ts],
                                     jnp.zeros((), BF16)) * db[h]
                    gate = term if gate is None else gate + term
                slabs.append(gate * act[js, ts])
            strips.append(jnp.concatenate(slabs, axis=0))
        groups.append(jnp.concatenate(strips, axis=1))
    prev = jnp.where(e == 0, 0.0, acc_ref[...])
    acc_ref[...] = prev + _dot(vt_ref[0], jnp.concatenate(groups, axis=0))

    @pl.when(e == pl.num_programs(1) - 1)
    def _():
        h = h1_ref[...] + acc_ref[...].T
        o_ref[...] = h * lax.rsqrt(jnp.mean(h * h, axis=-1, keepdims=True) + NORM_EPS) * fg_ref[...]


def _peer_experts(hnt, h1, r2, e2, n, d, u, vt, final_g, tt, eb):
    s = hnt.shape[1]
    D = D_MODEL
    H = PEER_HEADS
    sel = pl.BlockSpec((H, N_KEYS, tt), lambda i, e: (0, 0, i), pipeline_mode=pl.Buffered(1))
    return pl.pallas_call(
        _peer_expert_kernel,
        grid=(s // tt, N_EXPERTS // eb),
        in_specs=[pl.BlockSpec((D, tt), lambda i, e: (0, i)),
                  pl.BlockSpec((tt, D), lambda i, e: (i, 0), pipeline_mode=pl.Buffered(1)),
                  sel, sel, sel, sel,
                  pl.BlockSpec((eb, D), lambda i, e: (e, 0)),
                  pl.BlockSpec((1, D, eb), lambda i, e: (e, 0, 0)),
                  pl.BlockSpec((1, D), lambda i, e: (0, 0))],
        out_specs=pl.BlockSpec((tt, D), lambda i, e: (i, 0)),
        out_shape=jax.ShapeDtypeStruct((s, D), F32),
        scratch_shapes=[pltpu.VMEM((D, tt), F32)],
        compiler_params=pltpu.CompilerParams(dimension_semantics=("parallel", "arbitrary"),
                                             vmem_limit_bytes=VMEM_LIMIT_RESIDENT),
        name="peer_experts",
    )(hnt, h1, r2, e2, n, d, u, vt, final_g.reshape(1, D))


def _pad_rows(w, rows):
    return jnp.pad(w, ((0, rows - w.shape[0]), (0, 0)))


def _split_bf16(w):
    hi = w.astype(BF16)
    return hi, (w - hi.astype(F32)).astype(BF16)


def _layer(h, norm1_g, w_in, shift_mu, rwkv_w0, w_decay_up, rwkv_a0, w_iclr_up, w_gate_up,
           k_k, k_a, r_k, lnx_g, lnx_b, lam_q1, lam_k1, lam_q2, lam_k2, subln_g, w_proj_a,
           w_proj_b, w_out, norm2_g, peer_wq, peer_sub_keys, peer_u, peer_v, out_g, lambda_init):
    s = h.shape[0]
    W = RWKV_WIDTH
    tmm = min(s, MM_ROW_BLOCK)
    tn = MM_COL_BLOCK

    c0, c1, c2 = 3 * W, 3 * W + DECAY_LORA, 3 * W + DECAY_LORA + ICLR_LORA
    pad_cols = lambda m, n: jnp.pad(m, ((0, 0), (0, n - m.shape[1])))
    w_rwkv = jnp.concatenate([w_in[:, :c0], pad_cols(w_in[:, c0:c1], LORA_PAD),
                              pad_cols(w_in[:, c1:c2], LORA_PAD), w_in[:, c2:RWKV_COLS]], axis=1)
    mu2 = shift_mu.reshape(1, -1)
    mu = jnp.concatenate([mu2[:, :c0], pad_cols(mu2[:, c0:c1], LORA_PAD),
                          pad_cols(mu2[:, c1:c2], LORA_PAD), mu2[:, c2:]], axis=1)
    d0 = RWKV_COLS
    q_scale = DIFF_HEAD_DIM ** -0.5 * math.log2(math.e)
    w_qk = jnp.concatenate([w_in[:, d0:d0 + DIFF_WIDTH] * q_scale,
                            w_in[:, d0 + DIFF_WIDTH:d0 + 2 * DIFF_WIDTH]], axis=1)
    w_vt = w_in[:, d0 + 2 * DIFF_WIDTH:d0 + DIFF_COLS].T
    w_gate = w_in[:, d0 + DIFF_COLS:]

    p_rwkv, xn = _norm_matmul(h, norm1_g, w_rwkv.astype(BF16), NORM_EPS, F32, tmm, tn,
                              "in_proj_rwkv")
    p_qk = _matmul(xn, w_qk.astype(BF16), BF16, tmm, tn, "in_proj_qk")
    p_vt = _matmul_nt(w_vt.astype(BF16), xn, BF16, VT_ROW_BLOCK, tmm, "in_proj_vt")
    p_gate = _matmul(xn, w_gate.astype(BF16), BF16, tmm, tn, "in_proj_gate")

    y_a = _rwkv_time_mix(p_rwkv, mu, rwkv_w0, _pad_rows(w_decay_up, LORA_PAD), rwkv_a0,
                         _pad_rows(w_iclr_up, LORA_PAD), w_gate_up, k_k, k_a, r_k, lnx_g, lnx_b)
    y_b = _diff_attention(p_qk, p_vt, lam_q1, lam_k1, lam_q2, lam_k2, subln_g, lambda_init)
    h1, hn_hi, hn_lo, hn_t = _merge(y_a, y_b, p_gate, h, w_proj_a.astype(BF16),
                                    w_proj_b.astype(BF16), w_out.astype(BF16), norm2_g,
                                    min(s, MERGE_ROW_BLOCK))

    wq_hi, wq_lo = _split_bf16(peer_wq)
    q = _matmul_split(hn_hi, hn_lo, wq_hi, wq_lo, tmm, tn, "peer_query")
    keys = peer_sub_keys.reshape(2 * PEER_HEADS, N_KEYS, PEER_HALF)
    r2, e2, n, d = _peer_scores(q, keys, min(s, PEER_SCORE_BLOCK))
    eb = PEER_EXPERT_BLOCK
    return _peer_experts(hn_t, h1, r2, e2, n, d, peer_u.astype(BF16),
                         _transpose_cast_blocks(peer_v, BF16, eb), out_g,
                         min(s, PEER_TOKEN_BLOCK), eb)


def kernel(x, norm1_g, w_in, shift_mu, rwkv_w0, w_decay_up, rwkv_a0, w_iclr_up, w_gate_up, k_k, k_a, r_k, lnx_g, lnx_b, lam_q1, lam_k1, lam_q2, lam_k2, subln_g, w_proj_a, w_proj_b, w_out, norm2_g, peer_wq, peer_sub_keys, peer_u, peer_v, final_g):
    B, S, D = x.shape
    assert B == 1 and D == D_MODEL and norm1_g.shape[0] == 1
    lambda_init = 0.8 - 0.6 * math.exp(-0.3 * 0)
    out = _layer(x[0], norm1_g[0], w_in[0], shift_mu[0], rwkv_w0[0], w_decay_up[0], rwkv_a0[0],
                 w_iclr_up[0], w_gate_up[0], k_k[0], k_a[0], r_k[0].reshape(-1), lnx_g[0],
                 lnx_b[0], lam_q1[0], lam_k1[0], lam_q2[0], lam_k2[0], subln_g[0], w_proj_a[0],
                 w_proj_b[0], w_out[0], norm2_g[0], peer_wq[0], peer_sub_keys[0], peer_u[0],
                 peer_v[0], final_g, lambda_init)
    return out[None]
```

```python
import functools
import math

import jax
import jax.numpy as jnp
from jax import lax
from jax.experimental import pallas as pl
from jax.experimental.pallas import tpu as pltpu

F32 = jnp.float32
BF16 = jnp.bfloat16
HIGHEST = lax.Precision.HIGHEST

LANES = 128
SUBLANES = 8
BF16_SUBLANES = 16

D_MODEL = 2048
RWKV_HEADS = 16
RWKV_HEAD_DIM = 64
RWKV_WIDTH = RWKV_HEADS * RWKV_HEAD_DIM
DECAY_LORA = 96
ICLR_LORA = 96
GATE_LORA = 256
LORA_PAD = 128
RWKV_COLS = 3 * RWKV_WIDTH + DECAY_LORA + ICLR_LORA + GATE_LORA
RWKV_COLS_PAD = 3 * RWKV_WIDTH + 2 * LORA_PAD + GATE_LORA
RWKV_CHUNK = 64
RWKV_BLOCK = 128
RWKV_GROUP = 4
RWKV_GROUP_LANES = RWKV_GROUP * RWKV_HEAD_DIM
DIFF_HEADS = 8
DIFF_HEAD_DIM = 64
DIFF_V_DIM = 2 * DIFF_HEAD_DIM
DIFF_WIDTH = DIFF_HEADS * DIFF_V_DIM
DIFF_COLS = 3 * DIFF_WIDTH
ATTN_K_BLOCK = 512
ATTN_Q_BLOCK = 1024
GATE_COLS = 2 * D_MODEL
PEER_HEADS = 8
PEER_HALF = 128
N_KEYS = 128
N_EXPERTS = N_KEYS * N_KEYS
PEER_TOPK = 16
NORM_EPS = 1e-6
LN_X_EPS = 64e-5
SUBLN_EPS = 1e-5
NEG_INF = -1e30

MM_ROW_BLOCK = 1024
MM_COL_BLOCK = 512
VT_ROW_BLOCK = 256
MERGE_ROW_BLOCK = 256
PEER_SCORE_BLOCK = 256
PEER_TOKEN_BLOCK = 512
PEER_EXPERT_BLOCK = 1024
MIB = 1024 * 1024
VMEM_LIMIT_RESIDENT = 56 * MIB
VMEM_LIMIT_ATTENTION = 40 * MIB


def _dot(a, b, precision=None):
    return jnp.dot(a, b, preferred_element_type=F32, precision=precision)


def _dot_nt(a, b, precision=None):
    return lax.dot_general(a, b, (((1,), (1,)), ((), ())), preferred_element_type=F32,
                           precision=precision)


def _const_spec(shape):
    nd = len(shape)
    return pl.BlockSpec(shape, lambda *_: (0,) * nd)


def _split(x):
    hi = x.astype(BF16)
    return hi, (x - hi.astype(F32)).astype(BF16)


def _dot3(a, b, nt=False):
    f = _dot_nt if nt else _dot
    ah, al = _split(a)
    bh, bl = _split(b)
    return f(ah, bh) + (f(ah, bl) + f(al, bh))


def _mm_kernel(x_ref, w_ref, o_ref):
    o_ref[...] = _dot(x_ref[...], w_ref[...]).astype(o_ref.dtype)


def _matmul(x, w, out_dtype, tm, tn, name):
    s, k = x.shape
    n = w.shape[1]
    return pl.pallas_call(
        _mm_kernel,
        grid=(s // tm, n // tn),
        in_specs=[pl.BlockSpec((tm, k), lambda i, j: (i, 0)),
                  pl.BlockSpec((k, tn), lambda i, j: (0, j))],
        out_specs=pl.BlockSpec((tm, tn), lambda i, j: (i, j)),
        out_shape=jax.ShapeDtypeStruct((s, n), out_dtype),
        compiler_params=pltpu.CompilerParams(dimension_semantics=("parallel", "parallel")),
        name=name,
    )(x, w)


def _norm_mm_kernel(x_ref, g_ref, w_ref, o_ref, xn_ref, *, eps):
    @pl.when(pl.program_id(1) == 0)
    def _():
        x = x_ref[...]
        y = x * lax.rsqrt(jnp.mean(x * x, axis=-1, keepdims=True) + eps) * g_ref[...]
        xn_ref[...] = y.astype(xn_ref.dtype)

    o_ref[...] = _dot(xn_ref[...], w_ref[...]).astype(o_ref.dtype)


def _norm_matmul(x, g, w, eps, out_dtype, tm, tn, name):
    s, k = x.shape
    n = w.shape[1]
    return pl.pallas_call(
        functools.partial(_norm_mm_kernel, eps=eps),
        grid=(s // tm, n // tn),
        in_specs=[pl.BlockSpec((tm, k), lambda i, j: (i, 0)), _const_spec((1, k)),
                  pl.BlockSpec((k, tn), lambda i, j: (0, j))],
        out_specs=[pl.BlockSpec((tm, tn), lambda i, j: (i, j)),
                   pl.BlockSpec((tm, k), lambda i, j: (i, 0))],
        out_shape=[jax.ShapeDtypeStruct((s, n), out_dtype), jax.ShapeDtypeStruct((s, k), w.dtype)],
        compiler_params=pltpu.CompilerParams(dimension_semantics=("parallel", "arbitrary")),
        name=name,
    )(x, g.reshape(1, k), w)


def _mm_nt_kernel(w_ref, x_ref, o_ref):
    o_ref[...] = _dot_nt(w_ref[...], x_ref[...]).astype(o_ref.dtype)


def _matmul_nt(w, x, out_dtype, tn, tm, name):
    n, k = w.shape
    s = x.shape[0]
    return pl.pallas_call(
        _mm_nt_kernel,
        grid=(s // tm, n // tn),
        in_specs=[pl.BlockSpec((tn, k), lambda i, j: (j, 0)),
                  pl.BlockSpec((tm, k), lambda i, j: (i, 0))],
        out_specs=pl.BlockSpec((tn, tm), lambda i, j: (j, i)),
        out_shape=jax.ShapeDtypeStruct((n, s), out_dtype),
        compiler_params=pltpu.CompilerParams(dimension_semantics=("parallel", "parallel")),
        name=name,
    )(w, x)


def _transpose_cast_kernel(x_ref, o_ref):
    o_ref[0] = x_ref[...].T.astype(o_ref.dtype)


def _transpose_cast_blocks(x, out_dtype, tb):
    r, c = x.shape
    return pl.pallas_call(
        _transpose_cast_kernel,
        grid=(r // tb, c // tb),
        in_specs=[pl.BlockSpec((tb, tb), lambda i, j: (i, j))],
        out_specs=pl.BlockSpec((1, tb, tb), lambda i, j: (i, j, 0)),
        out_shape=jax.ShapeDtypeStruct((r // tb, c, tb), out_dtype),
        compiler_params=pltpu.CompilerParams(dimension_semantics=("parallel", "parallel")),
        name="transpose_cast",
    )(x)


def _head_sum(x, ones_bd):
    xh, xl = _split(x)
    width = ones_bd.shape[0]
    tiles = []
    for c in range(x.shape[1] // width):
        cs = slice(c * width, (c + 1) * width)
        tiles.append(_dot(xh[:, cs], ones_bd) + _dot(xl[:, cs], ones_bd))
    return jnp.concatenate(tiles, axis=1)


def _softplus(x):
    return jnp.maximum(x, 0.0) + jnp.log1p(jnp.exp(-jnp.abs(x)))


def _rwkv_kernel(p_ref, pprev_ref, mu_ref, w0_ref, wd_ref, a0_ref, wa_ref, wg_ref, kk_ref,
                 ka_ref, rk_ref, lng_ref, lnb_ref, o_ref, state_ref, y_ref):
    L = RWKV_CHUNK
    N = RWKV_HEAD_DIM
    W = RWKV_WIDTH
    G = RWKV_GROUP
    GL = RWKV_GROUP_LANES
    step = pl.program_id(0)

    @pl.when(step == 0)
    def _():
        state_ref[...] = jnp.zeros_like(state_ref)

    TB = p_ref.shape[0]
    row = lax.broadcasted_iota(jnp.int32, (TB, 1), 0)
    carry_on = jnp.where(step == 0, 0.0, 1.0)

    def shifted(c0, c1):
        p = p_ref[:, c0:c1]
        last = pprev_ref[SUBLANES - 1:SUBLANES, c0:c1] * carry_on
        prev = jnp.where(row == 0, last, pltpu.roll(p, 1, axis=0))
        return p + (prev - p) * mu_ref[:, c0:c1]

    r = shifted(0, W)
    k = shifted(W, 2 * W)
    v = shifted(2 * W, 3 * W)
    xw = shifted(3 * W, 3 * W + LORA_PAD)
    xa = shifted(3 * W + LORA_PAD, 3 * W + 2 * LORA_PAD)
    xg = shifted(3 * W + 2 * LORA_PAD, 3 * W + 2 * LORA_PAD + GATE_LORA)

    z = w0_ref[...] + _dot3(jnp.tanh(xw), wd_ref[...])
    w_log = -_softplus(-z) - 0.5
    lw = -jnp.exp(w_log)
    a = jax.nn.sigmoid(a0_ref[...] + _dot3(xa, wa_ref[...]))
    g = _dot3(jax.nn.sigmoid(xg), wg_ref[...])

    bi = lax.broadcasted_iota(jnp.int32, (GL, GL), 0) // N
    bj = lax.broadcasted_iota(jnp.int32, (GL, GL), 1) // N
    bd_mask = bi == bj
    ones_bd = jnp.where(bd_mask, 1.0, 0.0).astype(BF16)

    kk = k * kk_ref[...]
    kk = kk / jnp.maximum(jnp.sqrt(_head_sum(kk * kk, ones_bd)), 1e-12)
    k = k * (1.0 + (a - 1.0) * ka_ref[...])

    ti = lax.broadcasted_iota(jnp.int32, (TB, TB), 0)
    tj = lax.broadcasted_iota(jnp.int32, (TB, TB), 1)
    tril = jnp.where((tj <= ti) & (tj // L == ti // L), 1.0, 0.0).astype(F32)
    gi = lax.broadcasted_iota(jnp.int32, (L, GL), 0)
    gj = lax.broadcasted_iota(jnp.int32, (L, GL), 1) % L
    incl = gj <= gi
    strict = gj < gi
    eye4 = jnp.where(gj == gi, 1.0, 0.0).astype(F32)
    def bd(x):
        return jnp.where(bd_mask, jnp.concatenate([x] * G, axis=0), jnp.zeros((), x.dtype))

    cum = _dot(tril, lw, HIGHEST)
    e_inv = jnp.exp(-cum)
    r_t = (r * jnp.exp(cum)).astype(BF16)
    a_t = (-kk * jnp.exp(cum - lw)).astype(BF16)
    b = kk * a
    b_t = (b * e_inv).astype(BF16)
    k_t = (k * e_inv).astype(BF16)
    v_b = v.astype(BF16)
    eye4_b = eye4.astype(BF16)
    n_grp = RWKV_HEADS // G
    states = [state_ref[grp] for grp in range(n_grp)]

    n_chunks = TB // L
    pairs = [(c, grp) for c in range(n_chunks) for grp in range(n_grp)]
    rows_of = lambda c: slice(c * L, (c + 1) * L)
    lanes_of = lambda grp: slice(grp * GL, (grp + 1) * GL)
    cum_last = [cum[(c + 1) * L - 1:(c + 1) * L, :] for c in range(n_chunks)]

    a_ab, a_ak, a_rb, a_rk = {}, {}, {}, {}
    for c, grp in pairs:
        rs, gs = rows_of(c), lanes_of(grp)
        m = _dot_nt(jnp.concatenate([a_t[rs, gs], r_t[rs, gs]], axis=0),
                    jnp.concatenate([bd(b_t[rs, gs]), bd(k_t[rs, gs])], axis=0))
        a_ab[c, grp] = jnp.where(strict, m[:L, :GL], 0.0)
        a_ak[c, grp] = jnp.where(strict, m[:L, GL:], 0.0).astype(BF16)
        a_rb[c, grp] = jnp.where(incl, m[L:, :GL], 0.0).astype(BF16)
        a_rk[c, grp] = jnp.where(incl, m[L:, GL:], 0.0).astype(BF16)

    inv = {p: eye4 + a_ab[p] for p in pairs}
    pw = {}
    for p in pairs:
        pw_b = a_ab[p].astype(BF16)
        pw[p] = _dot(pw_b, bd(pw_b))
    for _ in range(4):
        for p in pairs:
            pw_b = pw[p].astype(BF16)
            both = _dot(jnp.concatenate([pw_b, inv[p].astype(BF16)], axis=0), bd(pw_b))
            pw[p], inv[p] = both[:L], inv[p] + both[L:]
    for p in pairs:
        inv[p] = (inv[p] + _dot(inv[p].astype(BF16), bd(pw[p].astype(BF16)))).astype(BF16)

    bk = {}
    for c in range(n_chunks):
        rs = rows_of(c)
        e_tail = jnp.exp(cum_last[c] - cum[rs])
        b_w = (b[rs] * e_tail).astype(BF16)
        k_w = (k[rs] * e_tail).astype(BF16)
        for grp in range(n_grp):
            gs = lanes_of(grp)
            bk[c, grp] = _dot_nt(
                eye4_b, jnp.concatenate([bd(b_w[:, gs]), bd(k_w[:, gs])], axis=0)).astype(BF16)

    for c in range(n_chunks):
        rs = rows_of(c)
        w_last = jnp.exp(cum_last[c])
        bd_z = [bd(states[grp].astype(BF16)) for grp in range(n_grp)]
        bd_v = [bd(v_b[rs, lanes_of(grp)]) for grp in range(n_grp)]
        x = [_dot(jnp.concatenate([a_t[rs, lanes_of(grp)], a_ak[c, grp]], axis=1),
                  jnp.concatenate([bd_z[grp], bd_v[grp]], axis=0)) for grp in range(n_grp)]
        u = [_dot(inv[c, grp], bd(x[grp].astype(BF16))) for grp in range(n_grp)]
        for grp in range(n_grp):
            gs = lanes_of(grp)
            lhs = jnp.concatenate(
                [jnp.concatenate([r_t[rs, gs], a_rb[c, grp], a_rk[c, grp]], axis=1),
                 jnp.concatenate([(eye4 * w_last[:, gs]).astype(BF16), bk[c, grp]], axis=1)],
                axis=0)
            yz = _dot(lhs, jnp.concatenate([bd_z[grp], bd(u[grp].astype(BF16)), bd_v[grp]],
                                           axis=0))
            y_ref[rs, gs] = yz[:L]
            states[grp] = yz[L:]
    for grp in range(n_grp):
        state_ref[grp] = states[grp]

    y = y_ref[...]
    mean = _head_sum(y, ones_bd) * (1.0 / N)
    yc = y - mean
    var = _head_sum(yc * yc, ones_bd) * (1.0 / N)
    yn = yc * lax.rsqrt(var + LN_X_EPS) * lng_ref[...] + lnb_ref[...]
    bonus = _head_sum(r * k * rk_ref[...], ones_bd) * v
    o_ref[...] = ((yn + bonus) * g).astype(o_ref.dtype)


def _rwkv_time_mix(p, mu, w0, wd, a0, wa, wg, k_k, k_a, r_k, lnx_g, lnx_b):
    s = p.shape[0]
    L = RWKV_CHUNK
    W = RWKV_WIDTH
    row = lambda v: v.reshape(1, -1)
    consts = [row(mu), row(w0), wd, row(a0), wa, wg, row(k_k), row(k_a), row(r_k), row(lnx_g),
              row(lnx_b)]
    tb = RWKV_BLOCK
    return pl.pallas_call(
        _rwkv_kernel,
        grid=(s // tb,),
        in_specs=[pl.BlockSpec((tb, RWKV_COLS_PAD), lambda i: (i, 0)),
                  pl.BlockSpec((SUBLANES, RWKV_COLS_PAD),
                               lambda i: (jnp.maximum(i * (tb // SUBLANES) - 1, 0), 0))]
                 + [_const_spec(c.shape) for c in consts],
        out_specs=pl.BlockSpec((tb, W), lambda i: (i, 0)),
        out_shape=jax.ShapeDtypeStruct((s, W), BF16),
        scratch_shapes=[pltpu.VMEM((RWKV_HEADS // RWKV_GROUP, RWKV_HEAD_DIM, RWKV_GROUP_LANES),
                                   F32),
                        pltpu.VMEM((tb, W), F32)],
        compiler_params=pltpu.CompilerParams(dimension_semantics=("arbitrary",)),
        name="rwkv7",
    )(p, p, *consts)


def _diffattn_kernel(q_ref, k_ref, vt_ref, lq1_ref, lk1_ref, lq2_ref, lk2_ref, g_ref, o_ref,
                     sa_ref, sb_ref, mx_ref, m_ref, acc_ref, *, lambda_init):
    TQ = ATTN_Q_BLOCK
    TK = ATTN_K_BLOCK
    DV = DIFF_V_DIM
    qi = pl.program_id(1)
    q = q_ref[...]
    lane = lax.broadcasted_iota(jnp.int32, q.shape, 1)
    zero = jnp.zeros_like(q)
    q_halves = (jnp.where(lane < DIFF_HEAD_DIM, q, zero), jnp.where(lane >= DIFF_HEAD_DIM, q, zero))
    ones = jnp.ones((BF16_SUBLANES, TK), BF16)
    key_i = lax.broadcasted_iota(jnp.int32, (TK, TQ), 0)
    qry_i = lax.broadcasted_iota(jnp.int32, (TK, TQ), 1)

    s_bufs = (sa_ref, sb_ref)

    def scores(j, buf, key_offset=None):
        kb = k_ref[pl.ds(pl.multiple_of(j * TK, TK), TK), :]
        for idx in range(2):
            s = _dot_nt(kb, q_halves[idx])
            if key_offset is not None:
                s = jnp.where(key_i + key_offset <= qry_i, s, NEG_INF)
            s_bufs[buf][idx] = s
            mx_ref[2 * buf + idx] = jnp.max(s, axis=0, keepdims=True)

    def probs(buf):
        out = []
        for idx in range(2):
            m_old = m_ref[idx]
            m_new = jnp.maximum(m_old, mx_ref[2 * buf + idx])
            m_ref[idx] = m_new
            out.append((jnp.exp2(s_bufs[buf][idx] - m_new).astype(BF16),
                        jnp.exp2(m_old - m_new)))
        return out

    def accumulate(j, pa):
        vt = vt_ref[:, pl.ds(pl.multiple_of(j * TK, TK), TK)]
        vext = jnp.concatenate([vt, ones], axis=0)
        for idx, (p, alpha) in enumerate(pa):
            acc_ref[idx] = alpha * acc_ref[idx] + _dot(vext, p)

    def step(j, buf, next_scores):
        pa = probs(buf)
        next_scores()
        accumulate(j, pa)

    m_ref[...] = jnp.full_like(m_ref, NEG_INF)
    acc_ref[...] = jnp.zeros_like(acc_ref)

    @pl.when(qi > 0)
    def _():
        scores(0, 0)

    @pl.loop(0, qi - 1)
    def _(i):
        step(2 * i, 0, lambda: scores(2 * i + 1, 1))
        step(2 * i + 1, 1, lambda: scores(2 * i + 2, 0))

    @pl.when(qi > 0)
    def _():
        step(2 * qi - 2, 0, lambda: scores(2 * qi - 1, 1))
        step(2 * qi - 1, 1, lambda: scores(2 * qi, 0, key_offset=0))

    @pl.when(qi == 0)
    def _():
        scores(0, 0, key_offset=0)

    step(2 * qi, 0, lambda: scores(2 * qi + 1, 1, key_offset=TK))
    step(2 * qi + 1, 1, lambda: None)

    lam = (jnp.exp(jnp.sum(lq1_ref[...] * lk1_ref[...], axis=-1, keepdims=True))
           - jnp.exp(jnp.sum(lq2_ref[...] * lk2_ref[...], axis=-1, keepdims=True))
           + lambda_init)
    o = (acc_ref[0, :DV, :] / acc_ref[0, DV:DV + 1, :]
         - lam * (acc_ref[1, :DV, :] / acc_ref[1, DV:DV + 1, :]))
    o = o * lax.rsqrt(jnp.mean(o * o, axis=0, keepdims=True) + SUBLN_EPS) * g_ref[...]
    o_ref[...] = (o * (1.0 - lambda_init)).T.astype(o_ref.dtype)


def _diff_attention(qk, vt, lq1, lk1, lq2, lk2, subln_g, lambda_init):
    s = qk.shape[0]
    H = DIFF_HEADS
    TQ = ATTN_Q_BLOCK
    TK = ATTN_K_BLOCK
    DV = DIFF_V_DIM
    assert TQ == 2 * TK and s % TQ == 0
    row = lambda v: v.reshape(1, -1)
    lam_specs = [_const_spec((1, DIFF_HEAD_DIM))] * 4
    return pl.pallas_call(
        functools.partial(_diffattn_kernel, lambda_init=lambda_init),
        grid=(H, s // TQ),
        in_specs=[pl.BlockSpec((TQ, DV), lambda h, qi: (qi, h)),
                  pl.BlockSpec((s, DV), lambda h, qi: (0, H + h)),
                  pl.BlockSpec((DV, s), lambda h, qi: (h, 0))]
                 + lam_specs + [_const_spec((DV, 1))],
        out_specs=pl.BlockSpec((TQ, DV), lambda h, qi: (qi, h)),
        out_shape=jax.ShapeDtypeStruct((s, DIFF_WIDTH), BF16),
        scratch_shapes=[pltpu.VMEM((2, TK, TQ), F32), pltpu.VMEM((2, TK, TQ), F32),
                        pltpu.VMEM((4, 1, TQ), F32), pltpu.VMEM((2, 1, TQ), F32),
                        pltpu.VMEM((2, DV + BF16_SUBLANES, TQ), F32)],
        compiler_params=pltpu.CompilerParams(dimension_semantics=("parallel", "arbitrary"),
                                             vmem_limit_bytes=VMEM_LIMIT_ATTENTION),
        name="diff_attention",
    )(qk, qk, vt, row(lq1), row(lk1), row(lq2), row(lk2), subln_g.reshape(DV, 1))


def _merge_kernel(ya_ref, yb_ref, gate_ref, x_ref, wa_ref, wb_ref, wo_ref, g2_ref,
                  h_ref, ht_ref):
    pa = _dot(ya_ref[...], wa_ref[...])
    pb = _dot(yb_ref[...], wb_ref[...])
    ga = jax.nn.sigmoid(gate_ref[:, :D_MODEL].astype(F32))
    gb = jax.nn.sigmoid(gate_ref[:, D_MODEL:].astype(F32))
    merged = ga * pa + gb * pb
    h = x_ref[...] + _dot(merged.astype(BF16), wo_ref[...])
    h_ref[...] = h
    hn = h * lax.rsqrt(jnp.mean(h * h, axis=-1, keepdims=True) + NORM_EPS) * g2_ref[...]
    ht_ref[...] = hn.T.astype(BF16)


def _merge(ya, yb, gate, x, wa, wb, wo, g2, tm):
    s = x.shape[0]
    D = D_MODEL
    rows = lambda w: pl.BlockSpec((tm, w), lambda i: (i, 0))
    single = lambda shape: pl.BlockSpec(shape, lambda i: (0, 0), pipeline_mode=pl.Buffered(1))
    return pl.pallas_call(
        _merge_kernel,
        grid=(s // tm,),
        in_specs=[rows(RWKV_WIDTH), rows(DIFF_WIDTH), rows(GATE_COLS), rows(D),
                  single((RWKV_WIDTH, D)), single((DIFF_WIDTH, D)), single((D, D)),
                  _const_spec((1, D))],
        out_specs=[rows(D), pl.BlockSpec((D, tm), lambda i: (0, i))],
        out_shape=[jax.ShapeDtypeStruct((s, D), F32), jax.ShapeDtypeStruct((D, s), BF16)],
        compiler_params=pltpu.CompilerParams(dimension_semantics=("parallel",),
                                             vmem_limit_bytes=VMEM_LIMIT_RESIDENT),
        name="merge_out_proj",
    )(ya, yb, gate, x, wa, wb, wo, g2.reshape(1, D))


def _cmp_exchange(xs, i, l, descending):
    hi = jnp.maximum(xs[i], xs[l])
    lo = jnp.minimum(xs[i], xs[l])
    xs[i], xs[l] = (hi, lo) if descending else (lo, hi)


def _bitonic_merge_desc(xs):
    xs = list(xs)
    n = len(xs)
    j = n // 2
    while j >= 1:
        for i in range(n):
            l = i ^ j
            if l > i:
                _cmp_exchange(xs, i, l, True)
        j //= 2
    return xs


def _bitonic_sort_desc(xs):
    xs = list(xs)
    n = len(xs)
    k = 2
    while k <= n:
        j = k // 2
        while j >= 1:
            for i in range(n):
                l = i ^ j
                if l > i:
                    _cmp_exchange(xs, i, l, (i & k) == 0)
            j //= 2
        k *= 2
    return xs


def _merge_top(a, b):
    n = len(a)
    return _bitonic_merge_desc([jnp.maximum(a[i], b[n - 1 - i]) for i in range(n)])


def _top16_over_rows(s):
    groups = [s[g * SUBLANES:(g + 1) * SUBLANES, :] for g in range(s.shape[0] // SUBLANES)]
    top = _bitonic_sort_desc(groups)
    for shift in (4, 2, 1):
        top = _merge_top(top, [pltpu.roll(x, shift, axis=0) for x in top])
    return top


def _prefix_count(rows, pred):
    def pick(conds, cands):
        if not conds:
            return cands[0]
        half = len(cands) // 2
        return jnp.where(conds[0], pick(conds[1:], cands[half:]), pick(conds[1:], cands[:half]))

    n = len(rows)
    conds = []
    count = None
    step = n // 2
    while step >= 1:
        cands = [rows[lo + step - 1] for lo in range(0, n, 2 * step)]
        c = pred(pick(conds, cands))
        inc = jnp.where(c, float(step), 0.0)
        count = inc if count is None else count + inc
        conds.append(c)
        step //= 2
    return jnp.where(pred(rows[n - 1]), float(n), count)


def _peer_score_kernel(qt_ref, keys_ref, r2_ref, e2_ref, n_ref, d_ref):
    K = PEER_TOPK
    T = qt_ref.shape[1]
    H = PEER_HEADS
    scores = []
    tops = []
    for hp in range(2 * H):
        s = _dot(keys_ref[hp], qt_ref[hp * PEER_HALF:(hp + 1) * PEER_HALF, :], HIGHEST)
        scores.append(s)
        tops.append(_top16_over_rows(s))
    sub = lax.broadcasted_iota(jnp.int32, (SUBLANES, T), 0)

    def by_head(p, i):
        out = tops[p][i]
        for h in range(1, H):
            out = jnp.where(sub == h, tops[2 * h + p][i], out)
        return out

    aa = [by_head(0, i) for i in range(K)]
    bb = [by_head(1, i) for i in range(K)]
    cands = [aa[i] + bb[j] for i in range(K) for j in range(K) if (i + 1) * (j + 1) <= K]
    cands += [jnp.full_like(cands[0], -jnp.inf)] * (-len(cands) % K)
    best = _bitonic_sort_desc(cands[:K])
    for c in range(K, len(cands), K):
        best = _merge_top(best, _bitonic_sort_desc(cands[c:c + K]))
    thr = best[K - 1]
    zsum = jnp.zeros_like(thr)
    for t in best:
        zsum = zsum + jnp.exp(t - best[0])
    inv_z = 1.0 / zsum
    for h in range(H):
        hs = slice(h, h + 1)
        s1, s2 = scores[2 * h], scores[2 * h + 1]
        thr_h = thr[hs]
        b_rows = [bb[m][hs] for m in range(K)]
        cnt = _prefix_count(b_rows, lambda b: s1 + b >= thr_h)
        rank = _prefix_count(b_rows, lambda b: b > s2)
        n_ref[h] = cnt
        r2_ref[h] = rank.astype(BF16)
        d_ref[h] = jnp.exp(s1 - aa[0][hs]) * inv_z[hs]
        e2_ref[h] = jnp.exp(s2 - bb[0][hs]).astype(BF16)


def _peer_scores(qt, keys, tt):
    nq, s = qt.shape
    H = PEER_HEADS
    out = lambda dtype: jax.ShapeDtypeStruct((H, N_KEYS, s), dtype)
    ospec = pl.BlockSpec((H, N_KEYS, tt), lambda i: (0, 0, i))
    return pl.pallas_call(
        _peer_score_kernel,
        grid=(s // tt,),
        in_specs=[pl.BlockSpec((nq, tt), lambda i: (0, i)),
                  _const_spec((2 * H, N_KEYS, PEER_HALF))],
        out_specs=[ospec] * 4,
        out_shape=[out(BF16), out(BF16), out(F32), out(F32)],
        compiler_params=pltpu.CompilerParams(dimension_semantics=("parallel",)),
        name="peer_scores",
    )(qt, keys)


def _peer_expert_kernel(hnt_ref, h1_ref, r2_ref, e2_ref, n_ref, d_ref, u_ref, vt_ref, fg_ref,
                        o_ref, acc_ref):
    e = pl.program_id(1)
    eb = u_ref.shape[0]
    tt = hnt_ref.shape[1]
    ni = eb // N_KEYS
    slab = BF16_SUBLANES
    strip = 2 * LANES

    groups = []
    for ii in range(ni):
        i = e * ni + ii
        rows = slice(ii * N_KEYS, (ii + 1) * N_KEYS)
        pre = _dot(u_ref[rows, :], hnt_ref[...])
        act = (0.5 * pre * (1.0 + lax.erf(pre * math.sqrt(0.5)))).astype(BF16)
        strips = []
        for t0 in range(0, tt, strip):
            ts = slice(t0, t0 + strip)
            nb = [jnp.broadcast_to(n_ref[h, pl.ds(i, 1), ts], (slab, strip)).astype(BF16)
                  for h in range(PEER_HEADS)]
            db = [jnp.broadcast_to(d_ref[h, pl.ds(i, 1), ts], (slab, strip)).astype(BF16)
                  for h in range(PEER_HEADS)]
            slabs = []
            for j0 in range(0, N_KEYS, slab):
                js = slice(j0, j0 + slab)
                gate = None
                for h in range(PEER_HEADS):
                    term = jnp.where(r2_ref[h, js, ts] < nb[h], e2_ref[h, js, ts],
                                     jnp.zeros((), BF16)) * db[h]
                    gate = term if gate is None else gate + term
                slabs.append(gate * act[js, ts])
            strips.append(jnp.concatenate(slabs, axis=0))
        groups.append(jnp.concatenate(strips, axis=1))
    prev = jnp.where(e == 0, 0.0, acc_ref[...])
    acc_ref[...] = prev + _dot(vt_ref[0], jnp.concatenate(groups, axis=0))

    @pl.when(e == pl.num_programs(1) - 1)
    def _():
        h = h1_ref[...] + acc_ref[...].T
        o_ref[...] = h * lax.rsqrt(jnp.mean(h * h, axis=-1, keepdims=True) + NORM_EPS) * fg_ref[...]


def _peer_experts(hnt, h1, r2, e2, n, d, u, vt, final_g, tt, eb):
    s = hnt.shape[1]
    D = D_MODEL
    H = PEER_HEADS
    sel = pl.BlockSpec((H, N_KEYS, tt), lambda i, e: (0, 0, i), pipeline_mode=pl.Buffered(1))
    return pl.pallas_call(
        _peer_expert_kernel,
        grid=(s // tt, N_EXPERTS // eb),
        in_specs=[pl.BlockSpec((D, tt), lambda i, e: (0, i)),
                  pl.BlockSpec((tt, D), lambda i, e: (i, 0), pipeline_mode=pl.Buffered(1)),
                  sel, sel, sel, sel,
                  pl.BlockSpec((eb, D), lambda i, e: (e, 0)),
                  pl.BlockSpec((1, D, eb), lambda i, e: (e, 0, 0)),
                  pl.BlockSpec((1, D), lambda i, e: (0, 0))],
        out_specs=pl.BlockSpec((tt, D), lambda i, e: (i, 0)),
        out_shape=jax.ShapeDtypeStruct((s, D), F32),
        scratch_shapes=[pltpu.VMEM((D, tt), F32)],
        compiler_params=pltpu.CompilerParams(dimension_semantics=("parallel", "arbitrary"),
                                             vmem_limit_bytes=VMEM_LIMIT_RESIDENT),
        name="peer_experts",
    )(hnt, h1, r2, e2, n, d, u, vt, final_g.reshape(1, D))


def _pad_rows(w, rows):
    return jnp.pad(w, ((0, rows - w.shape[0]), (0, 0)))


def _layer(h, norm1_g, w_in, shift_mu, rwkv_w0, w_decay_up, rwkv_a0, w_iclr_up, w_gate_up,
           k_k, k_a, r_k, lnx_g, lnx_b, lam_q1, lam_k1, lam_q2, lam_k2, subln_g, w_proj_a,
           w_proj_b, w_out, norm2_g, peer_wq, peer_sub_keys, peer_u, peer_v, out_g, lambda_init):
    s = h.shape[0]
    W = RWKV_WIDTH
    tmm = min(s, MM_ROW_BLOCK)
    tn = MM_COL_BLOCK

    c0, c1, c2 = 3 * W, 3 * W + DECAY_LORA, 3 * W + DECAY_LORA + ICLR_LORA
    pad_cols = lambda m, n: jnp.pad(m, ((0, 0), (0, n - m.shape[1])))
    w_rwkv = jnp.concatenate([w_in[:, :c0], pad_cols(w_in[:, c0:c1], LORA_PAD),
                              pad_cols(w_in[:, c1:c2], LORA_PAD), w_in[:, c2:RWKV_COLS]], axis=1)
    mu2 = shift_mu.reshape(1, -1)
    mu = jnp.concatenate([mu2[:, :c0], pad_cols(mu2[:, c0:c1], LORA_PAD),
                          pad_cols(mu2[:, c1:c2], LORA_PAD), mu2[:, c2:]], axis=1)
    d0 = RWKV_COLS
    q_scale = DIFF_HEAD_DIM ** -0.5 * math.log2(math.e)
    w_qk = jnp.concatenate([w_in[:, d0:d0 + DIFF_WIDTH] * q_scale,
                            w_in[:, d0 + DIFF_WIDTH:d0 + 2 * DIFF_WIDTH]], axis=1)
    w_vt = w_in[:, d0 + 2 * DIFF_WIDTH:d0 + DIFF_COLS].T
    w_gate = w_in[:, d0 + DIFF_COLS:]

    p_rwkv, xn = _norm_matmul(h, norm1_g, w_rwkv.astype(BF16), NORM_EPS, F32, tmm, tn,
                              "in_proj_rwkv")
    p_qk = _matmul(xn, w_qk.astype(BF16), BF16, tmm, tn, "in_proj_qk")
    p_vt = _matmul_nt(w_vt.astype(BF16), xn, BF16, VT_ROW_BLOCK, tmm, "in_proj_vt")
    p_gate = _matmul(xn, w_gate.astype(BF16), BF16, tmm, tn, "in_proj_gate")

    y_a = _rwkv_time_mix(p_rwkv, mu, rwkv_w0, _pad_rows(w_decay_up, LORA_PAD), rwkv_a0,
                         _pad_rows(w_iclr_up, LORA_PAD), w_gate_up, k_k, k_a, r_k, lnx_g, lnx_b)
    y_b = _diff_attention(p_qk, p_vt, lam_q1, lam_k1, lam_q2, lam_k2, subln_g, lambda_init)
    h1, hn_t = _merge(y_a, y_b, p_gate, h, w_proj_a.astype(BF16), w_proj_b.astype(BF16),
                      w_out.astype(BF16), norm2_g, min(s, MERGE_ROW_BLOCK))

    keys = peer_sub_keys.reshape(2 * PEER_HEADS, N_KEYS, PEER_HALF)
    q_t = _matmul(peer_wq.T.astype(BF16), hn_t, F32, MM_ROW_BLOCK, tn, "peer_query")
    r2, e2, n, d = _peer_scores(q_t, keys, min(s, PEER_SCORE_BLOCK))
    eb = PEER_EXPERT_BLOCK
    return _peer_experts(hn_t, h1, r2, e2, n, d, peer_u.astype(BF16),
                         _transpose_cast_blocks(peer_v, BF16, eb), out_g,
                         min(s, PEER_TOKEN_BLOCK), eb)


def kernel(x, norm1_g, w_in, shift_mu, rwkv_w0, w_decay_up, rwkv_a0, w_iclr_up, w_gate_up, k_k, k_a, r_k, lnx_g, lnx_b, lam_q1, lam_k1, lam_q2, lam_k2, subln_g, w_proj_a, w_proj_b, w_out, norm2_g, peer_wq, peer_sub_keys, peer_u, peer_v, final_g):
    B, S, D = x.shape
    assert B == 1 and D == D_MODEL and norm1_g.shape[0] == 1
    lambda_init = 0.8 - 0.6 * math.exp(-0.3 * 0)
    out = _layer(x[0], norm1_g[0], w_in[0], shift_mu[0], rwkv_w0[0], w_decay_up[0], rwkv_a0[0],
                 w_iclr_up[0], w_gate_up[0], k_k[0], k_a[0], r_k[0].reshape(-1), lnx_g[0],
                 lnx_b[0], lam_q1[0], lam_k1[0], lam_q2[0], lam_k2[0], subln_g[0], w_proj_a[0],
                 w_proj_b[0], w_out[0], norm2_g[0], peer_wq[0], peer_sub_keys[0], peer_u[0],
                 peer_v[0], final_g, lambda_init)
    return out[None]
```

```python
import functools
import math

import jax
import jax.numpy as jnp
from jax import lax
from jax.experimental import pallas as pl
from jax.experimental.pallas import tpu as pltpu

F32 = jnp.float32
BF16 = jnp.bfloat16
HIGHEST = lax.Precision.HIGHEST

LANES = 128
SUBLANES = 8
BF16_SUBLANES = 16

D_MODEL = 2048
RWKV_HEADS = 16
RWKV_HEAD_DIM = 64
RWKV_WIDTH = RWKV_HEADS * RWKV_HEAD_DIM
DECAY_LORA = 96
ICLR_LORA = 96
GATE_LORA = 256
LORA_PAD = 128
RWKV_COLS = 3 * RWKV_WIDTH + DECAY_LORA + ICLR_LORA + GATE_LORA
RWKV_COLS_PAD = 3 * RWKV_WIDTH + 2 * LORA_PAD + GATE_LORA
RWKV_CHUNK = 64
RWKV_BLOCK = 128
RWKV_GROUP = 4
RWKV_GROUP_LANES = RWKV_GROUP * RWKV_HEAD_DIM
DIFF_HEADS = 8
DIFF_HEAD_DIM = 64
DIFF_V_DIM = 2 * DIFF_HEAD_DIM
DIFF_WIDTH = DIFF_HEADS * DIFF_V_DIM
DIFF_COLS = 3 * DIFF_WIDTH
ATTN_K_BLOCK = 512
ATTN_Q_BLOCK = 1024
GATE_COLS = 2 * D_MODEL
PEER_HEADS = 8
PEER_HALF = 128
N_KEYS = 128
N_EXPERTS = N_KEYS * N_KEYS
PEER_TOPK = 16
NORM_EPS = 1e-6
LN_X_EPS = 64e-5
SUBLN_EPS = 1e-5
NEG_INF = -1e30

MM_ROW_BLOCK = 1024
MM_COL_BLOCK = 512
VT_ROW_BLOCK = 256
MERGE_ROW_BLOCK = 256
PEER_SCORE_BLOCK = 256
PEER_TOKEN_BLOCK = 512
PEER_EXPERT_BLOCK = 1024
MIB = 1024 * 1024
VMEM_LIMIT_RESIDENT = 56 * MIB
VMEM_LIMIT_ATTENTION = 40 * MIB


def _dot(a, b, precision=None):
    return jnp.dot(a, b, preferred_element_type=F32, precision=precision)


def _dot_nt(a, b, precision=None):
    return lax.dot_general(a, b, (((1,), (1,)), ((), ())), preferred_element_type=F32,
                           precision=precision)


def _const_spec(shape):
    nd = len(shape)
    return pl.BlockSpec(shape, lambda *_: (0,) * nd)


def _dot_bf16(a, b):
    return _dot(a.astype(BF16), b.astype(BF16))


def _mm_kernel(x_ref, w_ref, o_ref):
    o_ref[...] = _dot(x_ref[...], w_ref[...]).astype(o_ref.dtype)


def _matmul(x, w, out_dtype, tm, tn, name):
    s, k = x.shape
    n = w.shape[1]
    return pl.pallas_call(
        _mm_kernel,
        grid=(s // tm, n // tn),
        in_specs=[pl.BlockSpec((tm, k), lambda i, j: (i, 0)),
                  pl.BlockSpec((k, tn), lambda i, j: (0, j))],
        out_specs=pl.BlockSpec((tm, tn), lambda i, j: (i, j)),
        out_shape=jax.ShapeDtypeStruct((s, n), out_dtype),
        compiler_params=pltpu.CompilerParams(dimension_semantics=("parallel", "parallel")),
        name=name,
    )(x, w)


def _norm_mm_kernel(x_ref, g_ref, w_ref, o_ref, xn_ref, *, eps):
    @pl.when(pl.program_id(1) == 0)
    def _():
        x = x_ref[...]
        y = x * lax.rsqrt(jnp.mean(x * x, axis=-1, keepdims=True) + eps) * g_ref[...]
        xn_ref[...] = y.astype(xn_ref.dtype)

    o_ref[...] = _dot(xn_ref[...], w_ref[...]).astype(o_ref.dtype)


def _norm_matmul(x, g, w, eps, out_dtype, tm, tn, name):
    s, k = x.shape
    n = w.shape[1]
    return pl.pallas_call(
        functools.partial(_norm_mm_kernel, eps=eps),
        grid=(s // tm, n // tn),
        in_specs=[pl.BlockSpec((tm, k), lambda i, j: (i, 0)), _const_spec((1, k)),
                  pl.BlockSpec((k, tn), lambda i, j: (0, j))],
        out_specs=[pl.BlockSpec((tm, tn), lambda i, j: (i, j)),
                   pl.BlockSpec((tm, k), lambda i, j: (i, 0))],
        out_shape=[jax.ShapeDtypeStruct((s, n), out_dtype), jax.ShapeDtypeStruct((s, k), w.dtype)],
        compiler_params=pltpu.CompilerParams(dimension_semantics=("parallel", "arbitrary")),
        name=name,
    )(x, g.reshape(1, k), w)


def _mm_nt_kernel(w_ref, x_ref, o_ref):
    o_ref[...] = _dot_nt(w_ref[...], x_ref[...]).astype(o_ref.dtype)


def _matmul_nt(w, x, out_dtype, tn, tm, name):
    n, k = w.shape
    s = x.shape[0]
    return pl.pallas_call(
        _mm_nt_kernel,
        grid=(s // tm, n // tn),
        in_specs=[pl.BlockSpec((tn, k), lambda i, j: (j, 0)),
                  pl.BlockSpec((tm, k), lambda i, j: (i, 0))],
        out_specs=pl.BlockSpec((tn, tm), lambda i, j: (j, i)),
        out_shape=jax.ShapeDtypeStruct((n, s), out_dtype),
        compiler_params=pltpu.CompilerParams(dimension_semantics=("parallel", "parallel")),
        name=name,
    )(w, x)


def _transpose_cast_kernel(x_ref, o_ref):
    o_ref[0] = x_ref[...].T.astype(o_ref.dtype)


def _transpose_cast_blocks(x, out_dtype, tb):
    r, c = x.shape
    return pl.pallas_call(
        _transpose_cast_kernel,
        grid=(r // tb, c // tb),
        in_specs=[pl.BlockSpec((tb, tb), lambda i, j: (i, j))],
        out_specs=pl.BlockSpec((1, tb, tb), lambda i, j: (i, j, 0)),
        out_shape=jax.ShapeDtypeStruct((r // tb, c, tb), out_dtype),
        compiler_params=pltpu.CompilerParams(dimension_semantics=("parallel", "parallel")),
        name="transpose_cast",
    )(x)


def _head_sum(x, ones_bd):
    xb = x.astype(BF16)
    width = ones_bd.shape[0]
    tiles = [_dot(xb[:, c * width:(c + 1) * width], ones_bd)
             for c in range(x.shape[1] // width)]
    return jnp.concatenate(tiles, axis=1)


def _softplus(x):
    return jnp.maximum(x, 0.0) + jnp.log1p(jnp.exp(-jnp.abs(x)))


def _rwkv_kernel(p_ref, pprev_ref, mu_ref, w0_ref, wd_ref, a0_ref, wa_ref, wg_ref, kk_ref,
                 ka_ref, rk_ref, lng_ref, lnb_ref, o_ref, state_ref, y_ref):
    L = RWKV_CHUNK
    N = RWKV_HEAD_DIM
    W = RWKV_WIDTH
    G = RWKV_GROUP
    GL = RWKV_GROUP_LANES
    step = pl.program_id(0)

    @pl.when(step == 0)
    def _():
        state_ref[...] = jnp.zeros_like(state_ref)

    TB = p_ref.shape[0]
    row = lax.broadcasted_iota(jnp.int32, (TB, 1), 0)
    carry_on = jnp.where(step == 0, 0.0, 1.0)

    def shifted(c0, c1):
        p = p_ref[:, c0:c1]
        last = pprev_ref[SUBLANES - 1:SUBLANES, c0:c1] * carry_on
        prev = jnp.where(row == 0, last, pltpu.roll(p, 1, axis=0))
        return p + (prev - p) * mu_ref[:, c0:c1]

    r = shifted(0, W)
    k = shifted(W, 2 * W)
    v = shifted(2 * W, 3 * W)
    xw = shifted(3 * W, 3 * W + LORA_PAD)
    xa = shifted(3 * W + LORA_PAD, 3 * W + 2 * LORA_PAD)
    xg = shifted(3 * W + 2 * LORA_PAD, 3 * W + 2 * LORA_PAD + GATE_LORA)

    z = w0_ref[...] + _dot_bf16(jnp.tanh(xw), wd_ref[...])
    w_log = -_softplus(-z) - 0.5
    lw = -jnp.exp(w_log)
    a = jax.nn.sigmoid(a0_ref[...] + _dot_bf16(xa, wa_ref[...]))
    g = _dot_bf16(jax.nn.sigmoid(xg), wg_ref[...])

    bi = lax.broadcasted_iota(jnp.int32, (GL, GL), 0) // N
    bj = lax.broadcasted_iota(jnp.int32, (GL, GL), 1) // N
    bd_mask = bi == bj
    ones_bd = jnp.where(bd_mask, 1.0, 0.0).astype(BF16)

    kk = k * kk_ref[...]
    kk = kk / jnp.maximum(jnp.sqrt(_head_sum(kk * kk, ones_bd)), 1e-12)
    k = k * (1.0 + (a - 1.0) * ka_ref[...])

    ti = lax.broadcasted_iota(jnp.int32, (TB, TB), 0)
    tj = lax.broadcasted_iota(jnp.int32, (TB, TB), 1)
    tril = jnp.where((tj <= ti) & (tj // L == ti // L), 1.0, 0.0).astype(BF16)
    gi = lax.broadcasted_iota(jnp.int32, (L, GL), 0)
    gj = lax.broadcasted_iota(jnp.int32, (L, GL), 1) % L
    incl = gj <= gi
    strict = gj < gi
    eye4 = jnp.where(gj == gi, 1.0, 0.0).astype(F32)
    def bd(x):
        return jnp.where(bd_mask, jnp.concatenate([x] * G, axis=0), jnp.zeros((), x.dtype))

    lw_hi = lw.astype(BF16)
    lw_rest = lw - lw_hi.astype(F32)
    lw_mid = lw_rest.astype(BF16)
    lw_lo = (lw_rest - lw_mid.astype(F32)).astype(BF16)
    cum = _dot(tril, lw_hi) + (_dot(tril, lw_mid) + _dot(tril, lw_lo))
    e_inv = jnp.exp(-cum)
    r_t = (r * jnp.exp(cum)).astype(BF16)
    a_t = (-kk * jnp.exp(cum - lw)).astype(BF16)
    b = kk * a
    b_t = (b * e_inv).astype(BF16)
    k_t = (k * e_inv).astype(BF16)
    v_b = v.astype(BF16)
    eye4_b = eye4.astype(BF16)
    n_grp = RWKV_HEADS // G
    states = [state_ref[grp] for grp in range(n_grp)]

    n_chunks = TB // L
    pairs = [(c, grp) for c in range(n_chunks) for grp in range(n_grp)]
    rows_of = lambda c: slice(c * L, (c + 1) * L)
    lanes_of = lambda grp: slice(grp * GL, (grp + 1) * GL)
    cum_last = [cum[(c + 1) * L - 1:(c + 1) * L, :] for c in range(n_chunks)]

    a_ab, a_ak, a_rb, a_rk = {}, {}, {}, {}
    for c, grp in pairs:
        rs, gs = rows_of(c), lanes_of(grp)
        m = _dot_nt(jnp.concatenate([a_t[rs, gs], r_t[rs, gs]], axis=0),
                    jnp.concatenate([bd(b_t[rs, gs]), bd(k_t[rs, gs])], axis=0))
        a_ab[c, grp] = jnp.where(strict, m[:L, :GL], 0.0)
        a_ak[c, grp] = jnp.where(strict, m[:L, GL:], 0.0).astype(BF16)
        a_rb[c, grp] = jnp.where(incl, m[L:, :GL], 0.0).astype(BF16)
        a_rk[c, grp] = jnp.where(incl, m[L:, GL:], 0.0).astype(BF16)

    inv = {p: eye4 + a_ab[p] for p in pairs}
    pw = {}
    for p in pairs:
        pw_b = a_ab[p].astype(BF16)
        pw[p] = _dot(pw_b, bd(pw_b))
    for _ in range(4):
        for p in pairs:
            pw_b = pw[p].astype(BF16)
            both = _dot(jnp.concatenate([pw_b, inv[p].astype(BF16)], axis=0), bd(pw_b))
            pw[p], inv[p] = both[:L], inv[p] + both[L:]
    for p in pairs:
        inv[p] = (inv[p] + _dot(inv[p].astype(BF16), bd(pw[p].astype(BF16)))).astype(BF16)

    bk = {}
    for c in range(n_chunks):
        rs = rows_of(c)
        e_tail = jnp.exp(cum_last[c] - cum[rs])
        b_w = (b[rs] * e_tail).astype(BF16)
        k_w = (k[rs] * e_tail).astype(BF16)
        for grp in range(n_grp):
            gs = lanes_of(grp)
            bk[c, grp] = _dot_nt(
                eye4_b, jnp.concatenate([bd(b_w[:, gs]), bd(k_w[:, gs])], axis=0)).astype(BF16)

    for c in range(n_chunks):
        rs = rows_of(c)
        w_last = jnp.exp(cum_last[c])
        bd_z = [bd(states[grp].astype(BF16)) for grp in range(n_grp)]
        bd_v = [bd(v_b[rs, lanes_of(grp)]) for grp in range(n_grp)]
        x = [_dot(jnp.concatenate([a_t[rs, lanes_of(grp)], a_ak[c, grp]], axis=1),
                  jnp.concatenate([bd_z[grp], bd_v[grp]], axis=0)) for grp in range(n_grp)]
        u = [_dot(inv[c, grp], bd(x[grp].astype(BF16))) for grp in range(n_grp)]
        for grp in range(n_grp):
            gs = lanes_of(grp)
            lhs = jnp.concatenate(
                [jnp.concatenate([r_t[rs, gs], a_rb[c, grp], a_rk[c, grp]], axis=1),
                 jnp.concatenate([(eye4 * w_last[:, gs]).astype(BF16), bk[c, grp]], axis=1)],
                axis=0)
            yz = _dot(lhs, jnp.concatenate([bd_z[grp], bd(u[grp].astype(BF16)), bd_v[grp]],
                                           axis=0))
            y_ref[rs, gs] = yz[:L]
            states[grp] = yz[L:]
    for grp in range(n_grp):
        state_ref[grp] = states[grp]

    y = y_ref[...]
    mean = _head_sum(y, ones_bd) * (1.0 / N)
    yc = y - mean
    var = _head_sum(yc * yc, ones_bd) * (1.0 / N)
    yn = yc * lax.rsqrt(var + LN_X_EPS) * lng_ref[...] + lnb_ref[...]
    bonus = _head_sum(r * k * rk_ref[...], ones_bd) * v
    o_ref[...] = ((yn + bonus) * g).astype(o_ref.dtype)


def _rwkv_time_mix(p, mu, w0, wd, a0, wa, wg, k_k, k_a, r_k, lnx_g, lnx_b):
    s = p.shape[0]
    L = RWKV_CHUNK
    W = RWKV_WIDTH
    row = lambda v: v.reshape(1, -1)
    consts = [row(mu), row(w0), wd, row(a0), wa, wg, row(k_k), row(k_a), row(r_k), row(lnx_g),
              row(lnx_b)]
    tb = RWKV_BLOCK
    return pl.pallas_call(
        _rwkv_kernel,
        grid=(s // tb,),
        in_specs=[pl.BlockSpec((tb, RWKV_COLS_PAD), lambda i: (i, 0)),
                  pl.BlockSpec((SUBLANES, RWKV_COLS_PAD),
                               lambda i: (jnp.maximum(i * (tb // SUBLANES) - 1, 0), 0))]
                 + [_const_spec(c.shape) for c in consts],
        out_specs=pl.BlockSpec((tb, W), lambda i: (i, 0)),
        out_shape=jax.ShapeDtypeStruct((s, W), BF16),
        scratch_shapes=[pltpu.VMEM((RWKV_HEADS // RWKV_GROUP, RWKV_HEAD_DIM, RWKV_GROUP_LANES),
                                   F32),
                        pltpu.VMEM((tb, W), F32)],
        compiler_params=pltpu.CompilerParams(dimension_semantics=("arbitrary",)),
        name="rwkv7",
    )(p, p, *consts)


def _diffattn_kernel(q_ref, k_ref, vt_ref, lq1_ref, lk1_ref, lq2_ref, lk2_ref, g_ref, o_ref,
                     sa_ref, sb_ref, mx_ref, m_ref, acc_ref, *, lambda_init):
    TQ = ATTN_Q_BLOCK
    TK = ATTN_K_BLOCK
    DV = DIFF_V_DIM
    qi = pl.program_id(1)
    q = q_ref[...]
    lane = lax.broadcasted_iota(jnp.int32, q.shape, 1)
    zero = jnp.zeros_like(q)
    q_halves = (jnp.where(lane < DIFF_HEAD_DIM, q, zero), jnp.where(lane >= DIFF_HEAD_DIM, q, zero))
    ones = jnp.ones((BF16_SUBLANES, TK), BF16)
    key_i = lax.broadcasted_iota(jnp.int32, (TK, TQ), 0)
    qry_i = lax.broadcasted_iota(jnp.int32, (TK, TQ), 1)

    s_bufs = (sa_ref, sb_ref)

    def scores(j, buf, key_offset=None):
        kb = k_ref[pl.ds(pl.multiple_of(j * TK, TK), TK), :]
        for idx in range(2):
            s = _dot_nt(kb, q_halves[idx])
            if key_offset is not None:
                s = jnp.where(key_i + key_offset <= qry_i, s, NEG_INF)
            s_bufs[buf][idx] = s
            mx_ref[2 * buf + idx] = jnp.max(s, axis=0, keepdims=True)

    def probs(buf):
        out = []
        for idx in range(2):
            m_old = m_ref[idx]
            m_new = jnp.maximum(m_old, mx_ref[2 * buf + idx])
            m_ref[idx] = m_new
            out.append((jnp.exp2(s_bufs[buf][idx] - m_new).astype(BF16),
                        jnp.exp2(m_old - m_new)))
        return out

    def accumulate(j, pa):
        vt = vt_ref[:, pl.ds(pl.multiple_of(j * TK, TK), TK)]
        vext = jnp.concatenate([vt, ones], axis=0)
        for idx, (p, alpha) in enumerate(pa):
            acc_ref[idx] = alpha * acc_ref[idx] + _dot(vext, p)

    def step(j, buf, next_scores):
        pa = probs(buf)
        next_scores()
        accumulate(j, pa)

    m_ref[...] = jnp.full_like(m_ref, NEG_INF)
    acc_ref[...] = jnp.zeros_like(acc_ref)

    @pl.when(qi > 0)
    def _():
        scores(0, 0)

    @pl.loop(0, qi - 1)
    def _(i):
        step(2 * i, 0, lambda: scores(2 * i + 1, 1))
        step(2 * i + 1, 1, lambda: scores(2 * i + 2, 0))

    @pl.when(qi > 0)
    def _():
        step(2 * qi - 2, 0, lambda: scores(2 * qi - 1, 1))
        step(2 * qi - 1, 1, lambda: scores(2 * qi, 0, key_offset=0))

    @pl.when(qi == 0)
    def _():
        scores(0, 0, key_offset=0)

    step(2 * qi, 0, lambda: scores(2 * qi + 1, 1, key_offset=TK))
    step(2 * qi + 1, 1, lambda: None)

    lam = (jnp.exp(jnp.sum(lq1_ref[...] * lk1_ref[...], axis=-1, keepdims=True))
           - jnp.exp(jnp.sum(lq2_ref[...] * lk2_ref[...], axis=-1, keepdims=True))
           + lambda_init)
    o = (acc_ref[0, :DV, :] / acc_ref[0, DV:DV + 1, :]
         - lam * (acc_ref[1, :DV, :] / acc_ref[1, DV:DV + 1, :]))
    o = o * lax.rsqrt(jnp.mean(o * o, axis=0, keepdims=True) + SUBLN_EPS) * g_ref[...]
    o_ref[...] = (o * (1.0 - lambda_init)).T.astype(o_ref.dtype)


def _diff_attention(qk, vt, lq1, lk1, lq2, lk2, subln_g, lambda_init):
    s = qk.shape[0]
    H = DIFF_HEADS
    TQ = ATTN_Q_BLOCK
    TK = ATTN_K_BLOCK
    DV = DIFF_V_DIM
    assert TQ == 2 * TK and s % TQ == 0
    row = lambda v: v.reshape(1, -1)
    lam_specs = [_const_spec((1, DIFF_HEAD_DIM))] * 4
    return pl.pallas_call(
        functools.partial(_diffattn_kernel, lambda_init=lambda_init),
        grid=(H, s // TQ),
        in_specs=[pl.BlockSpec((TQ, DV), lambda h, qi: (qi, h)),
                  pl.BlockSpec((s, DV), lambda h, qi: (0, H + h)),
                  pl.BlockSpec((DV, s), lambda h, qi: (h, 0))]
                 + lam_specs + [_const_spec((DV, 1))],
        out_specs=pl.BlockSpec((TQ, DV), lambda h, qi: (qi, h)),
        out_shape=jax.ShapeDtypeStruct((s, DIFF_WIDTH), BF16),
        scratch_shapes=[pltpu.VMEM((2, TK, TQ), F32), pltpu.VMEM((2, TK, TQ), F32),
                        pltpu.VMEM((4, 1, TQ), F32), pltpu.VMEM((2, 1, TQ), F32),
                        pltpu.VMEM((2, DV + BF16_SUBLANES, TQ), F32)],
        compiler_params=pltpu.CompilerParams(dimension_semantics=("parallel", "arbitrary"),
                                             vmem_limit_bytes=VMEM_LIMIT_ATTENTION),
        name="diff_attention",
    )(qk, qk, vt, row(lq1), row(lk1), row(lq2), row(lk2), subln_g.reshape(DV, 1))


def _merge_kernel(ya_ref, yb_ref, gate_ref, x_ref, wa_ref, wb_ref, wo_ref, g2_ref,
                  h_ref, ht_ref):
    pa = _dot(ya_ref[...], wa_ref[...])
    pb = _dot(yb_ref[...], wb_ref[...])
    ga = jax.nn.sigmoid(gate_ref[:, :D_MODEL].astype(F32))
    gb = jax.nn.sigmoid(gate_ref[:, D_MODEL:].astype(F32))
    merged = ga * pa + gb * pb
    h = x_ref[...] + _dot(merged.astype(BF16), wo_ref[...])
    h_ref[...] = h
    hn = h * lax.rsqrt(jnp.mean(h * h, axis=-1, keepdims=True) + NORM_EPS) * g2_ref[...]
    ht_ref[...] = hn.T.astype(BF16)


def _merge(ya, yb, gate, x, wa, wb, wo, g2, tm):
    s = x.shape[0]
    D = D_MODEL
    rows = lambda w: pl.BlockSpec((tm, w), lambda i: (i, 0))
    single = lambda shape: pl.BlockSpec(shape, lambda i: (0, 0), pipeline_mode=pl.Buffered(1))
    return pl.pallas_call(
        _merge_kernel,
        grid=(s // tm,),
        in_specs=[rows(RWKV_WIDTH), rows(DIFF_WIDTH), rows(GATE_COLS), rows(D),
                  single((RWKV_WIDTH, D)), single((DIFF_WIDTH, D)), single((D, D)),
                  _const_spec((1, D))],
        out_specs=[rows(D), pl.BlockSpec((D, tm), lambda i: (0, i))],
        out_shape=[jax.ShapeDtypeStruct((s, D), F32), jax.ShapeDtypeStruct((D, s), BF16)],
        compiler_params=pltpu.CompilerParams(dimension_semantics=("parallel",),
                                             vmem_limit_bytes=VMEM_LIMIT_RESIDENT),
        name="merge_out_proj",
    )(ya, yb, gate, x, wa, wb, wo, g2.reshape(1, D))


def _cmp_exchange(xs, i, l, descending):
    hi = jnp.maximum(xs[i], xs[l])
    lo = jnp.minimum(xs[i], xs[l])
    xs[i], xs[l] = (hi, lo) if descending else (lo, hi)


def _bitonic_merge_desc(xs):
    xs = list(xs)
    n = len(xs)
    j = n // 2
    while j >= 1:
        for i in range(n):
            l = i ^ j
            if l > i:
                _cmp_exchange(xs, i, l, True)
        j //= 2
    return xs


def _bitonic_sort_desc(xs):
    xs = list(xs)
    n = len(xs)
    k = 2
    while k <= n:
        j = k // 2
        while j >= 1:
            for i in range(n):
                l = i ^ j
                if l > i:
                    _cmp_exchange(xs, i, l, (i & k) == 0)
            j //= 2
        k *= 2
    return xs


def _merge_top(a, b):
    n = len(a)
    return _bitonic_merge_desc([jnp.maximum(a[i], b[n - 1 - i]) for i in range(n)])


def _top16_over_rows(s):
    groups = [s[g * SUBLANES:(g + 1) * SUBLANES, :] for g in range(s.shape[0] // SUBLANES)]
    top = _bitonic_sort_desc(groups)
    for shift in (4, 2, 1):
        top = _merge_top(top, [pltpu.roll(x, shift, axis=0) for x in top])
    return top


def _prefix_count(rows, pred):
    def pick(conds, cands):
        if not conds:
            return cands[0]
        half = len(cands) // 2
        return jnp.where(conds[0], pick(conds[1:], cands[half:]), pick(conds[1:], cands[:half]))

    n = len(rows)
    conds = []
    count = None
    step = n // 2
    while step >= 1:
        cands = [rows[lo + step - 1] for lo in range(0, n, 2 * step)]
        c = pred(pick(conds, cands))
        inc = jnp.where(c, float(step), 0.0)
        count = inc if count is None else count + inc
        conds.append(c)
        step //= 2
    return jnp.where(pred(rows[n - 1]), float(n), count)


def _peer_score_kernel(qt_ref, keys_ref, r2_ref, e2_ref, n_ref, d_ref):
    K = PEER_TOPK
    T = qt_ref.shape[1]
    H = PEER_HEADS
    scores = []
    tops = []
    for hp in range(2 * H):
        s = _dot(keys_ref[hp], qt_ref[hp * PEER_HALF:(hp + 1) * PEER_HALF, :], HIGHEST)
        scores.append(s)
        tops.append(_top16_over_rows(s))
    sub = lax.broadcasted_iota(jnp.int32, (SUBLANES, T), 0)

    def by_head(p, i):
        out = tops[p][i]
        for h in range(1, H):
            out = jnp.where(sub == h, tops[2 * h + p][i], out)
        return out

    aa = [by_head(0, i) for i in range(K)]
    bb = [by_head(1, i) for i in range(K)]
    cands = [aa[i] + bb[j] for i in range(K) for j in range(K) if (i + 1) * (j + 1) <= K]
    cands += [jnp.full_like(cands[0], -jnp.inf)] * (-len(cands) % K)
    best = _bitonic_sort_desc(cands[:K])
    for c in range(K, len(cands), K):
        best = _merge_top(best, _bitonic_sort_desc(cands[c:c + K]))
    thr = best[K - 1]
    zsum = jnp.zeros_like(thr)
    for t in best:
        zsum = zsum + jnp.exp(t - best[0])
    inv_z = 1.0 / zsum
    for h in range(H):
        hs = slice(h, h + 1)
        s1, s2 = scores[2 * h], scores[2 * h + 1]
        thr_h = thr[hs]
        b_rows = [bb[m][hs] for m in range(K)]
        cnt = _prefix_count(b_rows, lambda b: s1 + b >= thr_h)
        rank = _prefix_count(b_rows, lambda b: b > s2)
        n_ref[h] = cnt
        r2_ref[h] = rank.astype(BF16)
        d_ref[h] = jnp.exp(s1 - aa[0][hs]) * inv_z[hs]
        e2_ref[h] = jnp.exp(s2 - bb[0][hs]).astype(BF16)


def _peer_scores(qt, keys, tt):
    nq, s = qt.shape
    H = PEER_HEADS
    out = lambda dtype: jax.ShapeDtypeStruct((H, N_KEYS, s), dtype)
    ospec = pl.BlockSpec((H, N_KEYS, tt), lambda i: (0, 0, i))
    return pl.pallas_call(
        _peer_score_kernel,
        grid=(s // tt,),
        in_specs=[pl.BlockSpec((nq, tt), lambda i: (0, i)),
                  _const_spec((2 * H, N_KEYS, PEER_HALF))],
        out_specs=[ospec] * 4,
        out_shape=[out(BF16), out(BF16), out(F32), out(F32)],
        compiler_params=pltpu.CompilerParams(dimension_semantics=("parallel",)),
        name="peer_scores",
    )(qt, keys)


def _peer_expert_kernel(hnt_ref, h1_ref, r2_ref, e2_ref, n_ref, d_ref, u_ref, vt_ref, fg_ref,
                        o_ref, acc_ref):
    e = pl.program_id(1)
    eb = u_ref.shape[0]
    tt = hnt_ref.shape[1]
    ni = eb // N_KEYS
    slab = BF16_SUBLANES
    strip = 2 * LANES

    groups = []
    for ii in range(ni):
        i = e * ni + ii
        rows = slice(ii * N_KEYS, (ii + 1) * N_KEYS)
        pre = _dot(u_ref[rows, :], hnt_ref[...])
        act = (0.5 * pre * (1.0 + lax.erf(pre * math.sqrt(0.5)))).astype(BF16)
        strips = []
        for t0 in range(0, tt, strip):
            ts = slice(t0, t0 + strip)
            nb = [jnp.broadcast_to(n_ref[h, pl.ds(i, 1), ts], (slab, strip)).astype(BF16)
                  for h in range(PEER_HEADS)]
            db = [jnp.broadcast_to(d_ref[h, pl.ds(i, 1), ts], (slab, strip)).astype(BF16)
                  for h in range(PEER_HEADS)]
            slabs = []
            for j0 in range(0, N_KEYS, slab):
                js = slice(j0, j0 + slab)
                gate = None
                for h in range(PEER_HEADS):
                    term = jnp.where(r2_ref[h, js, ts] < nb[h], e2_ref[h, js, ts],
                                     jnp.zeros((), BF16)) * db[h]
                    gate = term if gate is None else gate + term
                slabs.append(gate * act[js, ts])
            strips.append(jnp.concatenate(slabs, axis=0))
        groups.append(jnp.concatenate(strips, axis=1))
    prev = jnp.where(e == 0, 0.0, acc_ref[...])
    acc_ref[...] = prev + _dot(vt_ref[0], jnp.concatenate(groups, axis=0))

    @pl.when(e == pl.num_programs(1) - 1)
    def _():
        h = h1_ref[...] + acc_ref[...].T
        o_ref[...] = h * lax.rsqrt(jnp.mean(h * h, axis=-1, keepdims=True) + NORM_EPS) * fg_ref[...]


def _peer_experts(hnt, h1, r2, e2, n, d, u, vt, final_g, tt, eb):
    s = hnt.shape[1]
    D = D_MODEL
    H = PEER_HEADS
    sel = pl.BlockSpec((H, N_KEYS, tt), lambda i, e: (0, 0, i), pipeline_mode=pl.Buffered(1))
    return pl.pallas_call(
        _peer_expert_kernel,
        grid=(s // tt, N_EXPERTS // eb),
        in_specs=[pl.BlockSpec((D, tt), lambda i, e: (0, i)),
                  pl.BlockSpec((tt, D), lambda i, e: (i, 0), pipeline_mode=pl.Buffered(1)),
                  sel, sel, sel, sel,
                  pl.BlockSpec((eb, D), lambda i, e: (e, 0)),
                  pl.BlockSpec((1, D, eb), lambda i, e: (e, 0, 0)),
                  pl.BlockSpec((1, D), lambda i, e: (0, 0))],
        out_specs=pl.BlockSpec((tt, D), lambda i, e: (i, 0)),
        out_shape=jax.ShapeDtypeStruct((s, D), F32),
        scratch_shapes=[pltpu.VMEM((D, tt), F32)],
        compiler_params=pltpu.CompilerParams(dimension_semantics=("parallel", "arbitrary"),
                                             vmem_limit_bytes=VMEM_LIMIT_RESIDENT),
        name="peer_experts",
    )(hnt, h1, r2, e2, n, d, u, vt, final_g.reshape(1, D))


def _pad_rows(w, rows):
    return jnp.pad(w, ((0, rows - w.shape[0]), (0, 0)))


def _layer(h, norm1_g, w_in, shift_mu, rwkv_w0, w_decay_up, rwkv_a0, w_iclr_up, w_gate_up,
           k_k, k_a, r_k, lnx_g, lnx_b, lam_q1, lam_k1, lam_q2, lam_k2, subln_g, w_proj_a,
           w_proj_b, w_out, norm2_g, peer_wq, peer_sub_keys, peer_u, peer_v, out_g, lambda_init):
    s = h.shape[0]
    W = RWKV_WIDTH
    tmm = min(s, MM_ROW_BLOCK)
    tn = MM_COL_BLOCK

    c0, c1, c2 = 3 * W, 3 * W + DECAY_LORA, 3 * W + DECAY_LORA + ICLR_LORA
    pad_cols = lambda m, n: jnp.pad(m, ((0, 0), (0, n - m.shape[1])))
    w_rwkv = jnp.concatenate([w_in[:, :c0], pad_cols(w_in[:, c0:c1], LORA_PAD),
                              pad_cols(w_in[:, c1:c2], LORA_PAD), w_in[:, c2:RWKV_COLS]], axis=1)
    mu2 = shift_mu.reshape(1, -1)
    mu = jnp.concatenate([mu2[:, :c0], pad_cols(mu2[:, c0:c1], LORA_PAD),
                          pad_cols(mu2[:, c1:c2], LORA_PAD), mu2[:, c2:]], axis=1)
    d0 = RWKV_COLS
    q_scale = DIFF_HEAD_DIM ** -0.5 * math.log2(math.e)
    w_qk = jnp.concatenate([w_in[:, d0:d0 + DIFF_WIDTH] * q_scale,
                            w_in[:, d0 + DIFF_WIDTH:d0 + 2 * DIFF_WIDTH]], axis=1)
    w_vt = w_in[:, d0 + 2 * DIFF_WIDTH:d0 + DIFF_COLS].T
    w_gate = w_in[:, d0 + DIFF_COLS:]

    p_rwkv, xn = _norm_matmul(h, norm1_g, w_rwkv.astype(BF16), NORM_EPS, F32, tmm, tn,
                              "in_proj_rwkv")
    p_qk = _matmul(xn, w_qk.astype(BF16), BF16, tmm, tn, "in_proj_qk")
    p_vt = _matmul_nt(w_vt.astype(BF16), xn, BF16, VT_ROW_BLOCK, tmm, "in_proj_vt")
    p_gate = _matmul(xn, w_gate.astype(BF16), BF16, tmm, tn, "in_proj_gate")

    y_a = _rwkv_time_mix(p_rwkv, mu, rwkv_w0, _pad_rows(w_decay_up, LORA_PAD), rwkv_a0,
                         _pad_rows(w_iclr_up, LORA_PAD), w_gate_up, k_k, k_a, r_k, lnx_g, lnx_b)
    y_b = _diff_attention(p_qk, p_vt, lam_q1, lam_k1, lam_q2, lam_k2, subln_g, lambda_init)
    h1, hn_t = _merge(y_a, y_b, p_gate, h, w_proj_a.astype(BF16), w_proj_b.astype(BF16),
                      w_out.astype(BF16), norm2_g, min(s, MERGE_ROW_BLOCK))

    keys = peer_sub_keys.reshape(2 * PEER_HEADS, N_KEYS, PEER_HALF)
    q_t = _matmul(peer_wq.T.astype(BF16), hn_t, F32, MM_ROW_BLOCK, tn, "peer_query")
    r2, e2, n, d = _peer_scores(q_t, keys, min(s, PEER_SCORE_BLOCK))
    eb = PEER_EXPERT_BLOCK
    return _peer_experts(hn_t, h1, r2, e2, n, d, peer_u.astype(BF16),
                         _transpose_cast_blocks(peer_v, BF16, eb), out_g,
                         min(s, PEER_TOKEN_BLOCK), eb)


def kernel(x, norm1_g, w_in, shift_mu, rwkv_w0, w_decay_up, rwkv_a0, w_iclr_up, w_gate_up, k_k, k_a, r_k, lnx_g, lnx_b, lam_q1, lam_k1, lam_q2, lam_k2, subln_g, w_proj_a, w_proj_b, w_out, norm2_g, peer_wq, peer_sub_keys, peer_u, peer_v, final_g):
    B, S, D = x.shape
    assert B == 1 and D == D_MODEL and norm1_g.shape[0] == 1
    lambda_init = 0.8 - 0.6 * math.exp(-0.3 * 0)
    out = _layer(x[0], norm1_g[0], w_in[0], shift_mu[0], rwkv_w0[0], w_decay_up[0], rwkv_a0[0],
                 w_iclr_up[0], w_gate_up[0], k_k[0], k_a[0], r_k[0].reshape(-1), lnx_g[0],
                 lnx_b[0], lam_q1[0], lam_k1[0], lam_q2[0], lam_k2[0], subln_g[0], w_proj_a[0],
                 w_proj_b[0], w_out[0], norm2_g[0], peer_wq[0], peer_sub_keys[0], peer_u[0],
                 peer_v[0], final_g, lambda_init)
    return out[None]
```

```python
import functools
import math

import jax
import jax.numpy as jnp
from jax import lax
from jax.experimental import pallas as pl
from jax.experimental.pallas import tpu as pltpu

F32 = jnp.float32
BF16 = jnp.bfloat16
HIGHEST = lax.Precision.HIGHEST

LANES = 128
SUBLANES = 8
BF16_SUBLANES = 16

D_MODEL = 2048
RWKV_HEADS = 16
RWKV_HEAD_DIM = 64
RWKV_WIDTH = RWKV_HEADS * RWKV_HEAD_DIM
DECAY_LORA = 96
ICLR_LORA = 96
GATE_LORA = 256
LORA_PAD = 128
RWKV_COLS = 3 * RWKV_WIDTH + DECAY_LORA + ICLR_LORA + GATE_LORA
RWKV_COLS_PAD = 3 * RWKV_WIDTH + 2 * LORA_PAD + GATE_LORA
RWKV_CHUNK = 64
RWKV_BLOCK = 256
RWKV_GROUP = 4
RWKV_GROUP_LANES = RWKV_GROUP * RWKV_HEAD_DIM
DIFF_HEADS = 8
DIFF_HEAD_DIM = 64
DIFF_V_DIM = 2 * DIFF_HEAD_DIM
DIFF_WIDTH = DIFF_HEADS * DIFF_V_DIM
DIFF_COLS = 3 * DIFF_WIDTH
ATTN_K_BLOCK = 512
ATTN_Q_BLOCK = 1024
GATE_COLS = 2 * D_MODEL
PEER_HEADS = 8
PEER_HALF = 128
N_KEYS = 128
N_EXPERTS = N_KEYS * N_KEYS
PEER_TOPK = 16
NORM_EPS = 1e-6
LN_X_EPS = 64e-5
SUBLN_EPS = 1e-5
NEG_INF = -1e30

MM_ROW_BLOCK = 1024
MM_COL_BLOCK = 1024
RWKV_COL_BLOCK = RWKV_COLS_PAD // 4
VT_ROW_BLOCK = 256
MERGE_ROW_BLOCK = 256
PEER_SCORE_BLOCK = 256
PEER_TOKEN_BLOCK = 512
PEER_EXPERT_BLOCK = 1024
MIB = 1024 * 1024
VMEM_LIMIT_RESIDENT = 56 * MIB
VMEM_LIMIT_ATTENTION = 40 * MIB


def _dot(a, b, precision=None):
    return jnp.dot(a, b, preferred_element_type=F32, precision=precision)


def _dot_nt(a, b, precision=None):
    return lax.dot_general(a, b, (((1,), (1,)), ((), ())), preferred_element_type=F32,
                           precision=precision)


def _const_spec(shape):
    nd = len(shape)
    return pl.BlockSpec(shape, lambda *_: (0,) * nd)


def _dot_bf16(a, b):
    return _dot(a.astype(BF16), b.astype(BF16))


def _mm_kernel(x_ref, w_ref, o_ref):
    o_ref[...] = _dot(x_ref[...], w_ref[...]).astype(o_ref.dtype)


def _matmul(x, w, out_dtype, tm, tn, name):
    s, k = x.shape
    n = w.shape[1]
    return pl.pallas_call(
        _mm_kernel,
        grid=(s // tm, n // tn),
        in_specs=[pl.BlockSpec((tm, k), lambda i, j: (i, 0)),
                  pl.BlockSpec((k, tn), lambda i, j: (0, j))],
        out_specs=pl.BlockSpec((tm, tn), lambda i, j: (i, j)),
        out_shape=jax.ShapeDtypeStruct((s, n), out_dtype),
        compiler_params=pltpu.CompilerParams(dimension_semantics=("parallel", "parallel")),
        name=name,
    )(x, w)


def _norm_mm_kernel(x_ref, g_ref, w_ref, o_ref, xn_ref, *, eps):
    @pl.when(pl.program_id(1) == 0)
    def _():
        x = x_ref[...]
        y = x * lax.rsqrt(jnp.mean(x * x, axis=-1, keepdims=True) + eps) * g_ref[...]
        xn_ref[...] = y.astype(xn_ref.dtype)

    o_ref[...] = _dot(xn_ref[...], w_ref[...]).astype(o_ref.dtype)


def _norm_matmul(x, g, w, eps, out_dtype, tm, tn, name):
    s, k = x.shape
    n = w.shape[1]
    return pl.pallas_call(
        functools.partial(_norm_mm_kernel, eps=eps),
        grid=(s // tm, n // tn),
        in_specs=[pl.BlockSpec((tm, k), lambda i, j: (i, 0)), _const_spec((1, k)),
                  pl.BlockSpec((k, tn), lambda i, j: (0, j))],
        out_specs=[pl.BlockSpec((tm, tn), lambda i, j: (i, j)),
                   pl.BlockSpec((tm, k), lambda i, j: (i, 0))],
        out_shape=[jax.ShapeDtypeStruct((s, n), out_dtype), jax.ShapeDtypeStruct((s, k), w.dtype)],
        compiler_params=pltpu.CompilerParams(dimension_semantics=("parallel", "arbitrary")),
        name=name,
    )(x, g.reshape(1, k), w)


def _mm_nt_kernel(w_ref, x_ref, o_ref):
    o_ref[...] = _dot_nt(w_ref[...], x_ref[...]).astype(o_ref.dtype)


def _matmul_nt(w, x, out_dtype, tn, tm, name):
    n, k = w.shape
    s = x.shape[0]
    return pl.pallas_call(
        _mm_nt_kernel,
        grid=(s // tm, n // tn),
        in_specs=[pl.BlockSpec((tn, k), lambda i, j: (j, 0)),
                  pl.BlockSpec((tm, k), lambda i, j: (i, 0))],
        out_specs=pl.BlockSpec((tn, tm), lambda i, j: (j, i)),
        out_shape=jax.ShapeDtypeStruct((n, s), out_dtype),
        compiler_params=pltpu.CompilerParams(dimension_semantics=("parallel", "parallel")),
        name=name,
    )(w, x)


def _transpose_cast_kernel(x_ref, o_ref):
    o_ref[0] = x_ref[...].T.astype(o_ref.dtype)


def _transpose_cast_blocks(x, out_dtype, tb):
    r, c = x.shape
    return pl.pallas_call(
        _transpose_cast_kernel,
        grid=(r // tb, c // tb),
        in_specs=[pl.BlockSpec((tb, tb), lambda i, j: (i, j))],
        out_specs=pl.BlockSpec((1, tb, tb), lambda i, j: (i, j, 0)),
        out_shape=jax.ShapeDtypeStruct((r // tb, c, tb), out_dtype),
        compiler_params=pltpu.CompilerParams(dimension_semantics=("parallel", "parallel")),
        name="transpose_cast",
    )(x)


def _head_sum(x, ones_bd):
    xb = x.astype(BF16)
    width = ones_bd.shape[0]
    tiles = [_dot(xb[:, c * width:(c + 1) * width], ones_bd)
             for c in range(x.shape[1] // width)]
    return jnp.concatenate(tiles, axis=1)


def _softplus(x):
    return jnp.maximum(x, 0.0) + jnp.log1p(jnp.exp(-jnp.abs(x)))


def _rwkv_kernel(p_ref, pprev_ref, mu_ref, w0_ref, wd_ref, a0_ref, wa_ref, wg_ref, kk_ref,
                 ka_ref, rk_ref, lng_ref, lnb_ref, o_ref, state_ref, y_ref):
    L = RWKV_CHUNK
    N = RWKV_HEAD_DIM
    W = RWKV_WIDTH
    G = RWKV_GROUP
    GL = RWKV_GROUP_LANES
    step = pl.program_id(0)

    @pl.when(step == 0)
    def _():
        state_ref[...] = jnp.zeros_like(state_ref)

    TB = p_ref.shape[0]
    row = lax.broadcasted_iota(jnp.int32, (TB, 1), 0)
    carry_on = jnp.where(step == 0, 0.0, 1.0)

    def shifted(c0, c1):
        p = p_ref[:, c0:c1]
        last = pprev_ref[SUBLANES - 1:SUBLANES, c0:c1] * carry_on
        prev = jnp.where(row == 0, last, pltpu.roll(p, 1, axis=0))
        return p + (prev - p) * mu_ref[:, c0:c1]

    r = shifted(0, W)
    k = shifted(W, 2 * W)
    v = shifted(2 * W, 3 * W)
    xw = shifted(3 * W, 3 * W + LORA_PAD)
    xa = shifted(3 * W + LORA_PAD, 3 * W + 2 * LORA_PAD)
    xg = shifted(3 * W + 2 * LORA_PAD, 3 * W + 2 * LORA_PAD + GATE_LORA)

    z = w0_ref[...] + _dot_bf16(jnp.tanh(xw), wd_ref[...])
    w_log = -_softplus(-z) - 0.5
    lw = -jnp.exp(w_log)
    a = jax.nn.sigmoid(a0_ref[...] + _dot_bf16(xa, wa_ref[...]))
    g = _dot_bf16(jax.nn.sigmoid(xg), wg_ref[...])

    bi = lax.broadcasted_iota(jnp.int32, (GL, GL), 0) // N
    bj = lax.broadcasted_iota(jnp.int32, (GL, GL), 1) // N
    bd_mask = bi == bj
    ones_bd = jnp.where(bd_mask, 1.0, 0.0).astype(BF16)

    kk = k * kk_ref[...]
    kk = kk / jnp.maximum(jnp.sqrt(_head_sum(kk * kk, ones_bd)), 1e-12)
    k = k * (1.0 + (a - 1.0) * ka_ref[...])

    ti = lax.broadcasted_iota(jnp.int32, (TB, TB), 0)
    tj = lax.broadcasted_iota(jnp.int32, (TB, TB), 1)
    tril = jnp.where((tj <= ti) & (tj // L == ti // L), 1.0, 0.0).astype(BF16)
    gi = lax.broadcasted_iota(jnp.int32, (L, GL), 0)
    gj = lax.broadcasted_iota(jnp.int32, (L, GL), 1) % L
    incl = gj <= gi
    strict = gj < gi
    eye4 = jnp.where(gj == gi, 1.0, 0.0).astype(F32)
    def bd(x):
        return jnp.where(bd_mask, jnp.concatenate([x] * G, axis=0), jnp.zeros((), x.dtype))

    lw_hi = lw.astype(BF16)
    lw_rest = lw - lw_hi.astype(F32)
    lw_mid = lw_rest.astype(BF16)
    lw_lo = (lw_rest - lw_mid.astype(F32)).astype(BF16)
    cum = _dot(tril, lw_hi) + (_dot(tril, lw_mid) + _dot(tril, lw_lo))
    e_inv = jnp.exp(-cum)
    r_t = (r * jnp.exp(cum)).astype(BF16)
    a_t = (-kk * jnp.exp(cum - lw)).astype(BF16)
    b = kk * a
    b_t = (b * e_inv).astype(BF16)
    k_t = (k * e_inv).astype(BF16)
    v_b = v.astype(BF16)
    eye4_b = eye4.astype(BF16)
    n_grp = RWKV_HEADS // G
    states = [state_ref[grp] for grp in range(n_grp)]

    n_chunks = TB // L
    pairs = [(c, grp) for c in range(n_chunks) for grp in range(n_grp)]
    rows_of = lambda c: slice(c * L, (c + 1) * L)
    lanes_of = lambda grp: slice(grp * GL, (grp + 1) * GL)
    cum_last = [cum[(c + 1) * L - 1:(c + 1) * L, :] for c in range(n_chunks)]

    a_ab, a_ak, a_rb, a_rk = {}, {}, {}, {}
    for c, grp in pairs:
        rs, gs = rows_of(c), lanes_of(grp)
        m = _dot_nt(jnp.concatenate([a_t[rs, gs], r_t[rs, gs]], axis=0),
                    jnp.concatenate([bd(b_t[rs, gs]), bd(k_t[rs, gs])], axis=0))
        a_ab[c, grp] = jnp.where(strict, m[:L, :GL], 0.0)
        a_ak[c, grp] = jnp.where(strict, m[:L, GL:], 0.0).astype(BF16)
        a_rb[c, grp] = jnp.where(incl, m[L:, :GL], 0.0).astype(BF16)
        a_rk[c, grp] = jnp.where(incl, m[L:, GL:], 0.0).astype(BF16)

    inv = {p: eye4 + a_ab[p] for p in pairs}
    pw = {}
    for p in pairs:
        pw_b = a_ab[p].astype(BF16)
        pw[p] = _dot(pw_b, bd(pw_b))
    for _ in range(4):
        for p in pairs:
            pw_b = pw[p].astype(BF16)
            both = _dot(jnp.concatenate([pw_b, inv[p].astype(BF16)], axis=0), bd(pw_b))
            pw[p], inv[p] = both[:L], inv[p] + both[L:]
    for p in pairs:
        inv[p] = (inv[p] + _dot(inv[p].astype(BF16), bd(pw[p].astype(BF16)))).astype(BF16)

    bk = {}
    for c in range(n_chunks):
        rs = rows_of(c)
        e_tail = jnp.exp(cum_last[c] - cum[rs])
        b_w = (b[rs] * e_tail).astype(BF16)
        k_w = (k[rs] * e_tail).astype(BF16)
        for grp in range(n_grp):
            gs = lanes_of(grp)
            bk[c, grp] = _dot_nt(
                eye4_b, jnp.concatenate([bd(b_w[:, gs]), bd(k_w[:, gs])], axis=0)).astype(BF16)

    for c in range(n_chunks):
        rs = rows_of(c)
        w_last = jnp.exp(cum_last[c])
        bd_z = [bd(states[grp].astype(BF16)) for grp in range(n_grp)]
        bd_v = [bd(v_b[rs, lanes_of(grp)]) for grp in range(n_grp)]
        x = [_dot(jnp.concatenate([a_t[rs, lanes_of(grp)], a_ak[c, grp]], axis=1),
                  jnp.concatenate([bd_z[grp], bd_v[grp]], axis=0)) for grp in range(n_grp)]
        u = [_dot(inv[c, grp], bd(x[grp].astype(BF16))) for grp in range(n_grp)]
        for grp in range(n_grp):
            gs = lanes_of(grp)
            lhs = jnp.concatenate(
                [jnp.concatenate([r_t[rs, gs], a_rb[c, grp], a_rk[c, grp]], axis=1),
                 jnp.concatenate([(eye4 * w_last[:, gs]).astype(BF16), bk[c, grp]], axis=1)],
                axis=0)
            yz = _dot(lhs, jnp.concatenate([bd_z[grp], bd(u[grp].astype(BF16)), bd_v[grp]],
                                           axis=0))
            y_ref[rs, gs] = yz[:L]
            states[grp] = yz[L:]
    for grp in range(n_grp):
        state_ref[grp] = states[grp]

    y = y_ref[...]
    mean = _head_sum(y, ones_bd) * (1.0 / N)
    yc = y - mean
    var = _head_sum(yc * yc, ones_bd) * (1.0 / N)
    yn = yc * lax.rsqrt(var + LN_X_EPS) * lng_ref[...] + lnb_ref[...]
    bonus = _head_sum(r * k * rk_ref[...], ones_bd) * v
    o_ref[...] = ((yn + bonus) * g).astype(o_ref.dtype)


def _rwkv_time_mix(p, mu, w0, wd, a0, wa, wg, k_k, k_a, r_k, lnx_g, lnx_b):
    s = p.shape[0]
    L = RWKV_CHUNK
    W = RWKV_WIDTH
    row = lambda v: v.reshape(1, -1)
    consts = [row(mu), row(w0), wd, row(a0), wa, wg, row(k_k), row(k_a), row(r_k), row(lnx_g),
              row(lnx_b)]
    tb = RWKV_BLOCK
    return pl.pallas_call(
        _rwkv_kernel,
        grid=(s // tb,),
        in_specs=[pl.BlockSpec((tb, RWKV_COLS_PAD), lambda i: (i, 0)),
                  pl.BlockSpec((SUBLANES, RWKV_COLS_PAD),
                               lambda i: (jnp.maximum(i * (tb // SUBLANES) - 1, 0), 0))]
                 + [_const_spec(c.shape) for c in consts],
        out_specs=pl.BlockSpec((tb, W), lambda i: (i, 0)),
        out_shape=jax.ShapeDtypeStruct((s, W), BF16),
        scratch_shapes=[pltpu.VMEM((RWKV_HEADS // RWKV_GROUP, RWKV_HEAD_DIM, RWKV_GROUP_LANES),
                                   F32),
                        pltpu.VMEM((tb, W), F32)],
        compiler_params=pltpu.CompilerParams(dimension_semantics=("arbitrary",)),
        name="rwkv7",
    )(p, p, *consts)


def _diffattn_kernel(q_ref, k_ref, vt_ref, lq1_ref, lk1_ref, lq2_ref, lk2_ref, g_ref, o_ref,
                     sa_ref, sb_ref, mx_ref, m_ref, acc_ref, *, lambda_init):
    TQ = ATTN_Q_BLOCK
    TK = ATTN_K_BLOCK
    DV = DIFF_V_DIM
    qi = pl.program_id(1)
    q = q_ref[...]
    lane = lax.broadcasted_iota(jnp.int32, q.shape, 1)
    zero = jnp.zeros_like(q)
    q_halves = (jnp.where(lane < DIFF_HEAD_DIM, q, zero), jnp.where(lane >= DIFF_HEAD_DIM, q, zero))
    ones = jnp.ones((BF16_SUBLANES, TK), BF16)
    key_i = lax.broadcasted_iota(jnp.int32, (TK, TQ), 0)
    qry_i = lax.broadcasted_iota(jnp.int32, (TK, TQ), 1)

    s_bufs = (sa_ref, sb_ref)

    def scores(j, buf, key_offset=None):
        kb = k_ref[pl.ds(pl.multiple_of(j * TK, TK), TK), :]
        for idx in range(2):
            s = _dot_nt(kb, q_halves[idx])
            if key_offset is not None:
                s = jnp.where(key_i + key_offset <= qry_i, s, NEG_INF)
            s_bufs[buf][idx] = s
            mx_ref[2 * buf + idx] = jnp.max(s, axis=0, keepdims=True)

    def probs(buf):
        out = []
        for idx in range(2):
            m_old = m_ref[idx]
            m_new = jnp.maximum(m_old, mx_ref[2 * buf + idx])
            m_ref[idx] = m_new
            out.append((jnp.exp2(s_bufs[buf][idx] - m_new).astype(BF16),
                        jnp.exp2(m_old - m_new)))
        return out

    def accumulate(j, pa):
        vt = vt_ref[:, pl.ds(pl.multiple_of(j * TK, TK), TK)]
        vext = jnp.concatenate([vt, ones], axis=0)
        for idx, (p, alpha) in enumerate(pa):
            acc_ref[idx] = alpha * acc_ref[idx] + _dot(vext, p)

    def step(j, buf, next_scores):
        pa = probs(buf)
        next_scores()
        accumulate(j, pa)

    m_ref[...] = jnp.full_like(m_ref, NEG_INF)
    acc_ref[...] = jnp.zeros_like(acc_ref)

    @pl.when(qi > 0)
    def _():
        scores(0, 0)

    @pl.loop(0, qi - 1)
    def _(i):
        step(2 * i, 0, lambda: scores(2 * i + 1, 1))
        step(2 * i + 1, 1, lambda: scores(2 * i + 2, 0))

    @pl.when(qi > 0)
    def _():
        step(2 * qi - 2, 0, lambda: scores(2 * qi - 1, 1))
        step(2 * qi - 1, 1, lambda: scores(2 * qi, 0, key_offset=0))

    @pl.when(qi == 0)
    def _():
        scores(0, 0, key_offset=0)

    step(2 * qi, 0, lambda: scores(2 * qi + 1, 1, key_offset=TK))
    step(2 * qi + 1, 1, lambda: None)

    lam = (jnp.exp(jnp.sum(lq1_ref[...] * lk1_ref[...], axis=-1, keepdims=True))
           - jnp.exp(jnp.sum(lq2_ref[...] * lk2_ref[...], axis=-1, keepdims=True))
           + lambda_init)
    o = (acc_ref[0, :DV, :] / acc_ref[0, DV:DV + 1, :]
         - lam * (acc_ref[1, :DV, :] / acc_ref[1, DV:DV + 1, :]))
    o = o * lax.rsqrt(jnp.mean(o * o, axis=0, keepdims=True) + SUBLN_EPS) * g_ref[...]
    o_ref[...] = (o * (1.0 - lambda_init)).T.astype(o_ref.dtype)


def _diff_attention(qk, vt, lq1, lk1, lq2, lk2, subln_g, lambda_init):
    s = qk.shape[0]
    H = DIFF_HEADS
    TQ = ATTN_Q_BLOCK
    TK = ATTN_K_BLOCK
    DV = DIFF_V_DIM
    assert TQ == 2 * TK and s % TQ == 0
    row = lambda v: v.reshape(1, -1)
    lam_specs = [_const_spec((1, DIFF_HEAD_DIM))] * 4
    return pl.pallas_call(
        functools.partial(_diffattn_kernel, lambda_init=lambda_init),
        grid=(H, s // TQ),
        in_specs=[pl.BlockSpec((TQ, DV), lambda h, qi: (qi, h)),
                  pl.BlockSpec((s, DV), lambda h, qi: (0, H + h)),
                  pl.BlockSpec((DV, s), lambda h, qi: (h, 0))]
                 + lam_specs + [_const_spec((DV, 1))],
        out_specs=pl.BlockSpec((TQ, DV), lambda h, qi: (qi, h)),
        out_shape=jax.ShapeDtypeStruct((s, DIFF_WIDTH), BF16),
        scratch_shapes=[pltpu.VMEM((2, TK, TQ), F32), pltpu.VMEM((2, TK, TQ), F32),
                        pltpu.VMEM((4, 1, TQ), F32), pltpu.VMEM((2, 1, TQ), F32),
                        pltpu.VMEM((2, DV + BF16_SUBLANES, TQ), F32)],
        compiler_params=pltpu.CompilerParams(dimension_semantics=("parallel", "arbitrary"),
                                             vmem_limit_bytes=VMEM_LIMIT_ATTENTION),
        name="diff_attention",
    )(qk, qk, vt, row(lq1), row(lk1), row(lq2), row(lk2), subln_g.reshape(DV, 1))


def _merge_kernel(ya_ref, yb_ref, gate_ref, x_ref, wa_ref, wb_ref, wo_ref, g2_ref,
                  h_ref, ht_ref):
    pa = _dot(ya_ref[...], wa_ref[...])
    pb = _dot(yb_ref[...], wb_ref[...])
    ga = jax.nn.sigmoid(gate_ref[:, :D_MODEL].astype(F32))
    gb = jax.nn.sigmoid(gate_ref[:, D_MODEL:].astype(F32))
    merged = ga * pa + gb * pb
    h = x_ref[...] + _dot(merged.astype(BF16), wo_ref[...])
    h_ref[...] = h
    hn = h * lax.rsqrt(jnp.mean(h * h, axis=-1, keepdims=True) + NORM_EPS) * g2_ref[...]
    ht_ref[...] = hn.T.astype(BF16)


def _merge(ya, yb, gate, x, wa, wb, wo, g2, tm):
    s = x.shape[0]
    D = D_MODEL
    rows = lambda w: pl.BlockSpec((tm, w), lambda i: (i, 0))
    single = lambda shape: pl.BlockSpec(shape, lambda i: (0, 0), pipeline_mode=pl.Buffered(1))
    return pl.pallas_call(
        _merge_kernel,
        grid=(s // tm,),
        in_specs=[rows(RWKV_WIDTH), rows(DIFF_WIDTH), rows(GATE_COLS), rows(D),
                  single((RWKV_WIDTH, D)), single((DIFF_WIDTH, D)), single((D, D)),
                  _const_spec((1, D))],
        out_specs=[rows(D), pl.BlockSpec((D, tm), lambda i: (0, i))],
        out_shape=[jax.ShapeDtypeStruct((s, D), F32), jax.ShapeDtypeStruct((D, s), BF16)],
        compiler_params=pltpu.CompilerParams(dimension_semantics=("parallel",),
                                             vmem_limit_bytes=VMEM_LIMIT_RESIDENT),
        name="merge_out_proj",
    )(ya, yb, gate, x, wa, wb, wo, g2.reshape(1, D))


def _cmp_exchange(xs, i, l, descending):
    hi = jnp.maximum(xs[i], xs[l])
    lo = jnp.minimum(xs[i], xs[l])
    xs[i], xs[l] = (hi, lo) if descending else (lo, hi)


def _bitonic_merge_desc(xs):
    xs = list(xs)
    n = len(xs)
    j = n // 2
    while j >= 1:
        for i in range(n):
            l = i ^ j
            if l > i:
                _cmp_exchange(xs, i, l, True)
        j //= 2
    return xs


def _bitonic_sort_desc(xs):
    xs = list(xs)
    n = len(xs)
    k = 2
    while k <= n:
        j = k // 2
        while j >= 1:
            for i in range(n):
                l = i ^ j
                if l > i:
                    _cmp_exchange(xs, i, l, (i & k) == 0)
            j //= 2
        k *= 2
    return xs


def _merge_top(a, b):
    n = len(a)
    return _bitonic_merge_desc([jnp.maximum(a[i], b[n - 1 - i]) for i in range(n)])


def _top16_over_rows(s):
    groups = [s[g * SUBLANES:(g + 1) * SUBLANES, :] for g in range(s.shape[0] // SUBLANES)]
    top = _bitonic_sort_desc(groups)
    for shift in (4, 2, 1):
        top = _merge_top(top, [pltpu.roll(x, shift, axis=0) for x in top])
    return top


def _prefix_count(rows, pred):
    def pick(conds, cands):
        if not conds:
            return cands[0]
        half = len(cands) // 2
        return jnp.where(conds[0], pick(conds[1:], cands[half:]), pick(conds[1:], cands[:half]))

    n = len(rows)
    conds = []
    count = None
    step = n // 2
    while step >= 1:
        cands = [rows[lo + step - 1] for lo in range(0, n, 2 * step)]
        c = pred(pick(conds, cands))
        inc = jnp.where(c, float(step), 0.0)
        count = inc if count is None else count + inc
        conds.append(c)
        step //= 2
    return jnp.where(pred(rows[n - 1]), float(n), count)


def _peer_score_kernel(qt_ref, keys_ref, r2_ref, e2_ref, n_ref, d_ref):
    K = PEER_TOPK
    T = qt_ref.shape[1]
    H = PEER_HEADS
    scores = []
    tops = []
    for hp in range(2 * H):
        s = _dot(keys_ref[hp], qt_ref[hp * PEER_HALF:(hp + 1) * PEER_HALF, :], HIGHEST)
        scores.append(s)
        tops.append(_top16_over_rows(s))
    sub = lax.broadcasted_iota(jnp.int32, (SUBLANES, T), 0)

    def by_head(p, i):
        out = tops[p][i]
        for h in range(1, H):
            out = jnp.where(sub == h, tops[2 * h + p][i], out)
        return out

    aa = [by_head(0, i) for i in range(K)]
    bb = [by_head(1, i) for i in range(K)]
    cands = [aa[i] + bb[j] for i in range(K) for j in range(K) if (i + 1) * (j + 1) <= K]
    cands += [jnp.full_like(cands[0], -jnp.inf)] * (-len(cands) % K)
    best = _bitonic_sort_desc(cands[:K])
    for c in range(K, len(cands), K):
        best = _merge_top(best, _bitonic_sort_desc(cands[c:c + K]))
    thr = best[K - 1]
    zsum = jnp.zeros_like(thr)
    for t in best:
        zsum = zsum + jnp.exp(t - best[0])
    inv_z = 1.0 / zsum
    for h in range(H):
        hs = slice(h, h + 1)
        s1, s2 = scores[2 * h], scores[2 * h + 1]
        thr_h = thr[hs]
        b_rows = [bb[m][hs] for m in range(K)]
        cnt = _prefix_count(b_rows, lambda b: s1 + b >= thr_h)
        rank = _prefix_count(b_rows, lambda b: b > s2)
        n_ref[h] = cnt
        r2_ref[h] = rank.astype(BF16)
        d_ref[h] = jnp.exp(s1 - aa[0][hs]) * inv_z[hs]
        e2_ref[h] = jnp.exp(s2 - bb[0][hs]).astype(BF16)


def _peer_scores(qt, keys, tt):
    nq, s = qt.shape
    H = PEER_HEADS
    out = lambda dtype: jax.ShapeDtypeStruct((H, N_KEYS, s), dtype)
    ospec = pl.BlockSpec((H, N_KEYS, tt), lambda i: (0, 0, i))
    return pl.pallas_call(
        _peer_score_kernel,
        grid=(s // tt,),
        in_specs=[pl.BlockSpec((nq, tt), lambda i: (0, i)),
                  _const_spec((2 * H, N_KEYS, PEER_HALF))],
        out_specs=[ospec] * 4,
        out_shape=[out(BF16), out(BF16), out(F32), out(F32)],
        compiler_params=pltpu.CompilerParams(dimension_semantics=("parallel",)),
        name="peer_scores",
    )(qt, keys)


def _peer_expert_kernel(hnt_ref, h1_ref, r2_ref, e2_ref, n_ref, d_ref, u_ref, vt_ref, fg_ref,
                        o_ref, acc_ref):
    e = pl.program_id(1)
    eb = u_ref.shape[0]
    tt = hnt_ref.shape[1]
    ni = eb // N_KEYS
    slab = BF16_SUBLANES
    strip = 2 * LANES

    groups = []
    for ii in range(ni):
        i = e * ni + ii
        rows = slice(ii * N_KEYS, (ii + 1) * N_KEYS)
        pre = _dot(u_ref[rows, :], hnt_ref[...])
        act = (0.5 * pre * (1.0 + lax.erf(pre * math.sqrt(0.5)))).astype(BF16)
        strips = []
        for t0 in range(0, tt, strip):
            ts = slice(t0, t0 + strip)
            nb = [jnp.broadcast_to(n_ref[h, pl.ds(i, 1), ts], (slab, strip)).astype(BF16)
                  for h in range(PEER_HEADS)]
            db = [jnp.broadcast_to(d_ref[h, pl.ds(i, 1), ts], (slab, strip)).astype(BF16)
                  for h in range(PEER_HEADS)]
            slabs = []
            for j0 in range(0, N_KEYS, slab):
                js = slice(j0, j0 + slab)
                gate = None
                for h in range(PEER_HEADS):
                    term = jnp.where(r2_ref[h, js, ts] < nb[h], e2_ref[h, js, ts],
                                     jnp.zeros((), BF16)) * db[h]
                    gate = term if gate is None else gate + term
                slabs.append(gate * act[js, ts])
            strips.append(jnp.concatenate(slabs, axis=0))
        groups.append(jnp.concatenate(strips, axis=1))
    prev = jnp.where(e == 0, 0.0, acc_ref[...])
    acc_ref[...] = prev + _dot(vt_ref[0], jnp.concatenate(groups, axis=0))

    @pl.when(e == pl.num_programs(1) - 1)
    def _():
        h = h1_ref[...] + acc_ref[...].T
        o_ref[...] = h * lax.rsqrt(jnp.mean(h * h, axis=-1, keepdims=True) + NORM_EPS) * fg_ref[...]


def _peer_experts(hnt, h1, r2, e2, n, d, u, vt, final_g, tt, eb):
    s = hnt.shape[1]
    D = D_MODEL
    H = PEER_HEADS
    sel = pl.BlockSpec((H, N_KEYS, tt), lambda i, e: (0, 0, i), pipeline_mode=pl.Buffered(1))
    return pl.pallas_call(
        _peer_expert_kernel,
        grid=(s // tt, N_EXPERTS // eb),
        in_specs=[pl.BlockSpec((D, tt), lambda i, e: (0, i)),
                  pl.BlockSpec((tt, D), lambda i, e: (i, 0), pipeline_mode=pl.Buffered(1)),
                  sel, sel, sel, sel,
                  pl.BlockSpec((eb, D), lambda i, e: (e, 0)),
                  pl.BlockSpec((1, D, eb), lambda i, e: (e, 0, 0)),
                  pl.BlockSpec((1, D), lambda i, e: (0, 0))],
        out_specs=pl.BlockSpec((tt, D), lambda i, e: (i, 0)),
        out_shape=jax.ShapeDtypeStruct((s, D), F32),
        scratch_shapes=[pltpu.VMEM((D, tt), F32)],
        compiler_params=pltpu.CompilerParams(dimension_semantics=("parallel", "arbitrary"),
                                             vmem_limit_bytes=VMEM_LIMIT_RESIDENT),
        name="peer_experts",
    )(hnt, h1, r2, e2, n, d, u, vt, final_g.reshape(1, D))


def _pad_rows(w, rows):
    return jnp.pad(w, ((0, rows - w.shape[0]), (0, 0)))


def _layer(h, norm1_g, w_in, shift_mu, rwkv_w0, w_decay_up, rwkv_a0, w_iclr_up, w_gate_up,
           k_k, k_a, r_k, lnx_g, lnx_b, lam_q1, lam_k1, lam_q2, lam_k2, subln_g, w_proj_a,
           w_proj_b, w_out, norm2_g, peer_wq, peer_sub_keys, peer_u, peer_v, out_g, lambda_init):
    s = h.shape[0]
    W = RWKV_WIDTH
    tmm = min(s, MM_ROW_BLOCK)
    tn = MM_COL_BLOCK

    c0, c1, c2 = 3 * W, 3 * W + DECAY_LORA, 3 * W + DECAY_LORA + ICLR_LORA
    pad_cols = lambda m, n: jnp.pad(m, ((0, 0), (0, n - m.shape[1])))
    w_rwkv = jnp.concatenate([w_in[:, :c0], pad_cols(w_in[:, c0:c1], LORA_PAD),
                              pad_cols(w_in[:, c1:c2], LORA_PAD), w_in[:, c2:RWKV_COLS]], axis=1)
    mu2 = shift_mu.reshape(1, -1)
    mu = jnp.concatenate([mu2[:, :c0], pad_cols(mu2[:, c0:c1], LORA_PAD),
                          pad_cols(mu2[:, c1:c2], LORA_PAD), mu2[:, c2:]], axis=1)
    d0 = RWKV_COLS
    q_scale = DIFF_HEAD_DIM ** -0.5 * math.log2(math.e)
    w_qk = jnp.concatenate([w_in[:, d0:d0 + DIFF_WIDTH] * q_scale,
                            w_in[:, d0 + DIFF_WIDTH:d0 + 2 * DIFF_WIDTH]], axis=1)
    w_vt = w_in[:, d0 + 2 * DIFF_WIDTH:d0 + DIFF_COLS].T
    w_gate = w_in[:, d0 + DIFF_COLS:]

    p_rwkv, xn = _norm_matmul(h, norm1_g, w_rwkv.astype(BF16), NORM_EPS, F32, tmm,
                              RWKV_COL_BLOCK, "in_proj_rwkv")
    p_qk = _matmul(xn, w_qk.astype(BF16), BF16, tmm, tn, "in_proj_qk")
    p_vt = _matmul_nt(w_vt.astype(BF16), xn, BF16, VT_ROW_BLOCK, tmm, "in_proj_vt")
    p_gate = _matmul(xn, w_gate.astype(BF16), BF16, tmm, tn, "in_proj_gate")

    y_a = _rwkv_time_mix(p_rwkv, mu, rwkv_w0, _pad_rows(w_decay_up, LORA_PAD), rwkv_a0,
                         _pad_rows(w_iclr_up, LORA_PAD), w_gate_up, k_k, k_a, r_k, lnx_g, lnx_b)
    y_b = _diff_attention(p_qk, p_vt, lam_q1, lam_k1, lam_q2, lam_k2, subln_g, lambda_init)
    h1, hn_t = _merge(y_a, y_b, p_gate, h, w_proj_a.astype(BF16), w_proj_b.astype(BF16),
                      w_out.astype(BF16), norm2_g, min(s, MERGE_ROW_BLOCK))

    keys = peer_sub_keys.reshape(2 * PEER_HEADS, N_KEYS, PEER_HALF)
    q_t = _matmul(peer_wq.T.astype(BF16), hn_t, F32, MM_ROW_BLOCK, tn, "peer_query")
    r2, e2, n, d = _peer_scores(q_t, keys, min(s, PEER_SCORE_BLOCK))
    eb = PEER_EXPERT_BLOCK
    return _peer_experts(hn_t, h1, r2, e2, n, d, peer_u.astype(BF16),
                         _transpose_cast_blocks(peer_v, BF16, eb), out_g,
                         min(s, PEER_TOKEN_BLOCK), eb)


def kernel(x, norm1_g, w_in, shift_mu, rwkv_w0, w_decay_up, rwkv_a0, w_iclr_up, w_gate_up, k_k, k_a, r_k, lnx_g, lnx_b, lam_q1, lam_k1, lam_q2, lam_k2, subln_g, w_proj_a, w_proj_b, w_out, norm2_g, peer_wq, peer_sub_keys, peer_u, peer_v, final_g):
    B, S, D = x.shape
    assert B == 1 and D == D_MODEL and norm1_g.shape[0] == 1
    lambda_init = 0.8 - 0.6 * math.exp(-0.3 * 0)
    out = _layer(x[0], norm1_g[0], w_in[0], shift_mu[0], rwkv_w0[0], w_decay_up[0], rwkv_a0[0],
                 w_iclr_up[0], w_gate_up[0], k_k[0], k_a[0], r_k[0].reshape(-1), lnx_g[0],
                 lnx_b[0], lam_q1[0], lam_k1[0], lam_q2[0], lam_k2[0], subln_g[0], w_proj_a[0],
                 w_proj_b[0], w_out[0], norm2_g[0], peer_wq[0], peer_sub_keys[0], peer_u[0],
                 peer_v[0], final_g, lambda_init)
    return out[None]
```

```python
import functools
import math

import jax
import jax.numpy as jnp
from jax import lax
from jax.experimental import pallas as pl
from jax.experimental.pallas import tpu as pltpu

F32 = jnp.float32
BF16 = jnp.bfloat16
HIGHEST = lax.Precision.HIGHEST

LANES = 128
SUBLANES = 8
BF16_SUBLANES = 16

D_MODEL = 2048
RWKV_HEADS = 16
RWKV_HEAD_DIM = 64
RWKV_WIDTH = RWKV_HEADS * RWKV_HEAD_DIM
DECAY_LORA = 96
ICLR_LORA = 96
GATE_LORA = 256
LORA_PAD = 128
RWKV_COLS = 3 * RWKV_WIDTH + DECAY_LORA + ICLR_LORA + GATE_LORA
RWKV_COLS_PAD = 3 * RWKV_WIDTH + 2 * LORA_PAD + GATE_LORA
RWKV_CHUNK = 64
RWKV_BLOCK = 256
RWKV_GROUP = 4
RWKV_GROUP_LANES = RWKV_GROUP * RWKV_HEAD_DIM
DIFF_HEADS = 8
DIFF_HEAD_DIM = 64
DIFF_V_DIM = 2 * DIFF_HEAD_DIM
DIFF_WIDTH = DIFF_HEADS * DIFF_V_DIM
DIFF_COLS = 3 * DIFF_WIDTH
ATTN_K_BLOCK = 512
ATTN_Q_BLOCK = 1024
GATE_COLS = 2 * D_MODEL
PEER_HEADS = 8
PEER_HALF = 128
N_KEYS = 128
N_EXPERTS = N_KEYS * N_KEYS
PEER_TOPK = 16
NORM_EPS = 1e-6
LN_X_EPS = 64e-5
SUBLN_EPS = 1e-5
NEG_INF = -1e30

MM_ROW_BLOCK = 1024
MM_COL_BLOCK = 1024
RWKV_COL_BLOCK = RWKV_COLS_PAD // 4
VT_ROW_BLOCK = 1024
MERGE_ROW_BLOCK = 256
PEER_SCORE_BLOCK = 128
PEER_TOKEN_BLOCK = 512
PEER_EXPERT_BLOCK = 1024
MIB = 1024 * 1024
VMEM_LIMIT_RESIDENT = 56 * MIB
VMEM_LIMIT_ATTENTION = 40 * MIB


def _dot(a, b, precision=None):
    return jnp.dot(a, b, preferred_element_type=F32, precision=precision)


def _dot_nt(a, b, precision=None):
    return lax.dot_general(a, b, (((1,), (1,)), ((), ())), preferred_element_type=F32,
                           precision=precision)


def _const_spec(shape):
    nd = len(shape)
    return pl.BlockSpec(shape, lambda *_: (0,) * nd)


def _dot_bf16(a, b):
    return _dot(a.astype(BF16), b.astype(BF16))


def _mm_kernel(x_ref, w_ref, o_ref):
    o_ref[...] = _dot(x_ref[...], w_ref[...]).astype(o_ref.dtype)


def _matmul(x, w, out_dtype, tm, tn, name):
    s, k = x.shape
    n = w.shape[1]
    return pl.pallas_call(
        _mm_kernel,
        grid=(s // tm, n // tn),
        in_specs=[pl.BlockSpec((tm, k), lambda i, j: (i, 0)),
                  pl.BlockSpec((k, tn), lambda i, j: (0, j))],
        out_specs=pl.BlockSpec((tm, tn), lambda i, j: (i, j)),
        out_shape=jax.ShapeDtypeStruct((s, n), out_dtype),
        compiler_params=pltpu.CompilerParams(dimension_semantics=("parallel", "parallel")),
        name=name,
    )(x, w)


def _norm_mm_kernel(x_ref, g_ref, w_ref, o_ref, xn_ref, *, eps):
    @pl.when(pl.program_id(1) == 0)
    def _():
        x = x_ref[...]
        y = x * lax.rsqrt(jnp.mean(x * x, axis=-1, keepdims=True) + eps) * g_ref[...]
        xn_ref[...] = y.astype(xn_ref.dtype)

    o_ref[...] = _dot(xn_ref[...], w_ref[...]).astype(o_ref.dtype)


def _norm_matmul(x, g, w, eps, out_dtype, tm, tn, name):
    s, k = x.shape
    n = w.shape[1]
    return pl.pallas_call(
        functools.partial(_norm_mm_kernel, eps=eps),
        grid=(s // tm, n // tn),
        in_specs=[pl.BlockSpec((tm, k), lambda i, j: (i, 0)), _const_spec((1, k)),
                  pl.BlockSpec((k, tn), lambda i, j: (0, j))],
        out_specs=[pl.BlockSpec((tm, tn), lambda i, j: (i, j)),
                   pl.BlockSpec((tm, k), lambda i, j: (i, 0))],
        out_shape=[jax.ShapeDtypeStruct((s, n), out_dtype), jax.ShapeDtypeStruct((s, k), w.dtype)],
        compiler_params=pltpu.CompilerParams(dimension_semantics=("parallel", "arbitrary")),
        name=name,
    )(x, g.reshape(1, k), w)


def _mm_nt_kernel(w_ref, x_ref, o_ref):
    o_ref[...] = _dot_nt(w_ref[...], x_ref[...]).astype(o_ref.dtype)


def _matmul_nt(w, x, out_dtype, tn, tm, name):
    n, k = w.shape
    s = x.shape[0]
    return pl.pallas_call(
        _mm_nt_kernel,
        grid=(s // tm, n // tn),
        in_specs=[pl.BlockSpec((tn, k), lambda i, j: (j, 0)),
                  pl.BlockSpec((tm, k), lambda i, j: (i, 0))],
        out_specs=pl.BlockSpec((tn, tm), lambda i, j: (j, i)),
        out_shape=jax.ShapeDtypeStruct((n, s), out_dtype),
        compiler_params=pltpu.CompilerParams(dimension_semantics=("parallel", "parallel")),
        name=name,
    )(w, x)


def _transpose_cast_kernel(x_ref, o_ref):
    o_ref[0] = x_ref[...].T.astype(o_ref.dtype)


def _transpose_cast_blocks(x, out_dtype, tb):
    r, c = x.shape
    return pl.pallas_call(
        _transpose_cast_kernel,
        grid=(r // tb, c // tb),
        in_specs=[pl.BlockSpec((tb, tb), lambda i, j: (i, j))],
        out_specs=pl.BlockSpec((1, tb, tb), lambda i, j: (i, j, 0)),
        out_shape=jax.ShapeDtypeStruct((r // tb, c, tb), out_dtype),
        compiler_params=pltpu.CompilerParams(dimension_semantics=("parallel", "parallel")),
        name="transpose_cast",
    )(x)


def _head_sum(x, ones_bd):
    xb = x.astype(BF16)
    width = ones_bd.shape[0]
    tiles = [_dot(xb[:, c * width:(c + 1) * width], ones_bd)
             for c in range(x.shape[1] // width)]
    return jnp.concatenate(tiles, axis=1)


def _softplus(x):
    return jnp.maximum(x, 0.0) + jnp.log1p(jnp.exp(-jnp.abs(x)))


def _rwkv_kernel(p_ref, pprev_ref, mu_ref, w0_ref, wd_ref, a0_ref, wa_ref, wg_ref, kk_ref,
                 ka_ref, rk_ref, lng_ref, lnb_ref, o_ref, state_ref, y_ref):
    L = RWKV_CHUNK
    N = RWKV_HEAD_DIM
    W = RWKV_WIDTH
    G = RWKV_GROUP
    GL = RWKV_GROUP_LANES
    step = pl.program_id(0)

    @pl.when(step == 0)
    def _():
        state_ref[...] = jnp.zeros_like(state_ref)

    TB = p_ref.shape[0]
    row = lax.broadcasted_iota(jnp.int32, (TB, 1), 0)
    carry_on = jnp.where(step == 0, 0.0, 1.0)

    def shifted(c0, c1):
        p = p_ref[:, c0:c1]
        last = pprev_ref[SUBLANES - 1:SUBLANES, c0:c1] * carry_on
        prev = jnp.where(row == 0, last, pltpu.roll(p, 1, axis=0))
        return p + (prev - p) * mu_ref[:, c0:c1]

    r = shifted(0, W)
    k = shifted(W, 2 * W)
    v = shifted(2 * W, 3 * W)
    xw = shifted(3 * W, 3 * W + LORA_PAD)
    xa = shifted(3 * W + LORA_PAD, 3 * W + 2 * LORA_PAD)
    xg = shifted(3 * W + 2 * LORA_PAD, 3 * W + 2 * LORA_PAD + GATE_LORA)

    z = w0_ref[...] + _dot_bf16(jnp.tanh(xw), wd_ref[...])
    w_log = -_softplus(-z) - 0.5
    lw = -jnp.exp(w_log)
    a = jax.nn.sigmoid(a0_ref[...] + _dot_bf16(xa, wa_ref[...]))
    g = _dot_bf16(jax.nn.sigmoid(xg), wg_ref[...])

    bi = lax.broadcasted_iota(jnp.int32, (GL, GL), 0) // N
    bj = lax.broadcasted_iota(jnp.int32, (GL, GL), 1) // N
    bd_mask = bi == bj
    ones_bd = jnp.where(bd_mask, 1.0, 0.0).astype(BF16)

    kk = k * kk_ref[...]
    kk = kk / jnp.maximum(jnp.sqrt(_head_sum(kk * kk, ones_bd)), 1e-12)
    k = k * (1.0 + (a - 1.0) * ka_ref[...])

    ti = lax.broadcasted_iota(jnp.int32, (TB, TB), 0)
    tj = lax.broadcasted_iota(jnp.int32, (TB, TB), 1)
    tril = jnp.where((tj <= ti) & (tj // L == ti // L), 1.0, 0.0).astype(BF16)
    gi = lax.broadcasted_iota(jnp.int32, (L, GL), 0)
    gj = lax.broadcasted_iota(jnp.int32, (L, GL), 1) % L
    incl = gj <= gi
    strict = gj < gi
    eye4 = jnp.where(gj == gi, 1.0, 0.0).astype(F32)
    def bd(x):
        return jnp.where(bd_mask, jnp.concatenate([x] * G, axis=0), jnp.zeros((), x.dtype))

    lw_hi = lw.astype(BF16)
    lw_rest = lw - lw_hi.astype(F32)
    lw_mid = lw_rest.astype(BF16)
    lw_lo = (lw_rest - lw_mid.astype(F32)).astype(BF16)
    cum = _dot(tril, lw_hi) + (_dot(tril, lw_mid) + _dot(tril, lw_lo))
    e_inv = jnp.exp(-cum)
    r_t = (r * jnp.exp(cum)).astype(BF16)
    a_t = (-kk * jnp.exp(cum - lw)).astype(BF16)
    b = kk * a
    b_t = (b * e_inv).astype(BF16)
    k_t = (k * e_inv).astype(BF16)
    v_b = v.astype(BF16)
    eye4_b = eye4.astype(BF16)
    n_grp = RWKV_HEADS // G
    states = [state_ref[grp] for grp in range(n_grp)]

    n_chunks = TB // L
    pairs = [(c, grp) for c in range(n_chunks) for grp in range(n_grp)]
    rows_of = lambda c: slice(c * L, (c + 1) * L)
    lanes_of = lambda grp: slice(grp * GL, (grp + 1) * GL)
    cum_last = [cum[(c + 1) * L - 1:(c + 1) * L, :] for c in range(n_chunks)]

    a_ab, a_ak, a_rb, a_rk = {}, {}, {}, {}
    for c, grp in pairs:
        rs, gs = rows_of(c), lanes_of(grp)
        m = _dot_nt(jnp.concatenate([a_t[rs, gs], r_t[rs, gs]], axis=0),
                    jnp.concatenate([bd(b_t[rs, gs]), bd(k_t[rs, gs])], axis=0))
        a_ab[c, grp] = jnp.where(strict, m[:L, :GL], 0.0)
        a_ak[c, grp] = jnp.where(strict, m[:L, GL:], 0.0).astype(BF16)
        a_rb[c, grp] = jnp.where(incl, m[L:, :GL], 0.0).astype(BF16)
        a_rk[c, grp] = jnp.where(incl, m[L:, GL:], 0.0).astype(BF16)

    inv = {p: eye4 + a_ab[p] for p in pairs}
    pw = {}
    for p in pairs:
        pw_b = a_ab[p].astype(BF16)
        pw[p] = _dot(pw_b, bd(pw_b))
    for _ in range(4):
        for p in pairs:
            pw_b = pw[p].astype(BF16)
            both = _dot(jnp.concatenate([pw_b, inv[p].astype(BF16)], axis=0), bd(pw_b))
            pw[p], inv[p] = both[:L], inv[p] + both[L:]
    for p in pairs:
        inv[p] = (inv[p] + _dot(inv[p].astype(BF16), bd(pw[p].astype(BF16)))).astype(BF16)

    bk = {}
    for c in range(n_chunks):
        rs = rows_of(c)
        e_tail = jnp.exp(cum_last[c] - cum[rs])
        b_w = (b[rs] * e_tail).astype(BF16)
        k_w = (k[rs] * e_tail).astype(BF16)
        for grp in range(n_grp):
            gs = lanes_of(grp)
            bk[c, grp] = _dot_nt(
                eye4_b, jnp.concatenate([bd(b_w[:, gs]), bd(k_w[:, gs])], axis=0)).astype(BF16)

    for c in range(n_chunks):
        rs = rows_of(c)
        w_last = jnp.exp(cum_last[c])
        bd_z = [bd(states[grp].astype(BF16)) for grp in range(n_grp)]
        bd_v = [bd(v_b[rs, lanes_of(grp)]) for grp in range(n_grp)]
        x = [_dot(jnp.concatenate([a_t[rs, lanes_of(grp)], a_ak[c, grp]], axis=1),
                  jnp.concatenate([bd_z[grp], bd_v[grp]], axis=0)) for grp in range(n_grp)]
        u = [_dot(inv[c, grp], bd(x[grp].astype(BF16))) for grp in range(n_grp)]
        for grp in range(n_grp):
            gs = lanes_of(grp)
            lhs = jnp.concatenate(
                [jnp.concatenate([r_t[rs, gs], a_rb[c, grp], a_rk[c, grp]], axis=1),
                 jnp.concatenate([(eye4 * w_last[:, gs]).astype(BF16), bk[c, grp]], axis=1)],
                axis=0)
            yz = _dot(lhs, jnp.concatenate([bd_z[grp], bd(u[grp].astype(BF16)), bd_v[grp]],
                                           axis=0))
            y_ref[rs, gs] = yz[:L]
            states[grp] = yz[L:]
    for grp in range(n_grp):
        state_ref[grp] = states[grp]

    y = y_ref[...]
    mean = _head_sum(y, ones_bd) * (1.0 / N)
    yc = y - mean
    var = _head_sum(yc * yc, ones_bd) * (1.0 / N)
    yn = yc * lax.rsqrt(var + LN_X_EPS) * lng_ref[...] + lnb_ref[...]
    bonus = _head_sum(r * k * rk_ref[...], ones_bd) * v
    o_ref[...] = ((yn + bonus) * g).astype(o_ref.dtype)


def _rwkv_time_mix(p, mu, w0, wd, a0, wa, wg, k_k, k_a, r_k, lnx_g, lnx_b):
    s = p.shape[0]
    L = RWKV_CHUNK
    W = RWKV_WIDTH
    row = lambda v: v.reshape(1, -1)
    consts = [row(mu), row(w0), wd, row(a0), wa, wg, row(k_k), row(k_a), row(r_k), row(lnx_g),
              row(lnx_b)]
    tb = RWKV_BLOCK
    return pl.pallas_call(
        _rwkv_kernel,
        grid=(s // tb,),
        in_specs=[pl.BlockSpec((tb, RWKV_COLS_PAD), lambda i: (i, 0)),
                  pl.BlockSpec((SUBLANES, RWKV_COLS_PAD),
                               lambda i: (jnp.maximum(i * (tb // SUBLANES) - 1, 0), 0))]
                 + [_const_spec(c.shape) for c in consts],
        out_specs=pl.BlockSpec((tb, W), lambda i: (i, 0)),
        out_shape=jax.ShapeDtypeStruct((s, W), BF16),
        scratch_shapes=[pltpu.VMEM((RWKV_HEADS // RWKV_GROUP, RWKV_HEAD_DIM, RWKV_GROUP_LANES),
                                   F32),
                        pltpu.VMEM((tb, W), F32)],
        compiler_params=pltpu.CompilerParams(dimension_semantics=("arbitrary",)),
        name="rwkv7",
    )(p, p, *consts)


def _diffattn_kernel(q_ref, k_ref, vt_ref, lq1_ref, lk1_ref, lq2_ref, lk2_ref, g_ref, o_ref,
                     sa_ref, sb_ref, mx_ref, m_ref, acc_ref, *, lambda_init):
    TQ = ATTN_Q_BLOCK
    TK = ATTN_K_BLOCK
    DV = DIFF_V_DIM
    qi = pl.program_id(1)
    q = q_ref[...]
    lane = lax.broadcasted_iota(jnp.int32, q.shape, 1)
    zero = jnp.zeros_like(q)
    q_halves = (jnp.where(lane < DIFF_HEAD_DIM, q, zero), jnp.where(lane >= DIFF_HEAD_DIM, q, zero))
    ones = jnp.ones((BF16_SUBLANES, TK), BF16)
    key_i = lax.broadcasted_iota(jnp.int32, (TK, TQ), 0)
    qry_i = lax.broadcasted_iota(jnp.int32, (TK, TQ), 1)

    s_bufs = (sa_ref, sb_ref)

    def scores(j, buf, key_offset=None):
        kb = k_ref[pl.ds(pl.multiple_of(j * TK, TK), TK), :]
        for idx in range(2):
            s = _dot_nt(kb, q_halves[idx])
            if key_offset is not None:
                s = jnp.where(key_i + key_offset <= qry_i, s, NEG_INF)
            s_bufs[buf][idx] = s
            mx_ref[2 * buf + idx] = jnp.max(s, axis=0, keepdims=True)

    def probs(buf):
        out = []
        for idx in range(2):
            m_old = m_ref[idx]
            m_new = jnp.maximum(m_old, mx_ref[2 * buf + idx])
            m_ref[idx] = m_new
            out.append((jnp.exp2(s_bufs[buf][idx] - m_new).astype(BF16),
                        jnp.exp2(m_old - m_new)))
        return out

    def accumulate(j, pa):
        vt = vt_ref[:, pl.ds(pl.multiple_of(j * TK, TK), TK)]
        vext = jnp.concatenate([vt, ones], axis=0)
        for idx, (p, alpha) in enumerate(pa):
            acc_ref[idx] = alpha * acc_ref[idx] + _dot(vext, p)

    def step(j, buf, next_scores):
        pa = probs(buf)
        next_scores()
        accumulate(j, pa)

    m_ref[...] = jnp.full_like(m_ref, NEG_INF)
    acc_ref[...] = jnp.zeros_like(acc_ref)

    @pl.when(qi > 0)
    def _():
        scores(0, 0)

    @pl.loop(0, qi - 1)
    def _(i):
        step(2 * i, 0, lambda: scores(2 * i + 1, 1))
        step(2 * i + 1, 1, lambda: scores(2 * i + 2, 0))

    @pl.when(qi > 0)
    def _():
        step(2 * qi - 2, 0, lambda: scores(2 * qi - 1, 1))
        step(2 * qi - 1, 1, lambda: scores(2 * qi, 0, key_offset=0))

    @pl.when(qi == 0)
    def _():
        scores(0, 0, key_offset=0)

    step(2 * qi, 0, lambda: scores(2 * qi + 1, 1, key_offset=TK))
    step(2 * qi + 1, 1, lambda: None)

    lam = (jnp.exp(jnp.sum(lq1_ref[...] * lk1_ref[...], axis=-1, keepdims=True))
           - jnp.exp(jnp.sum(lq2_ref[...] * lk2_ref[...], axis=-1, keepdims=True))
           + lambda_init)
    o = (acc_ref[0, :DV, :] / acc_ref[0, DV:DV + 1, :]
         - lam * (acc_ref[1, :DV, :] / acc_ref[1, DV:DV + 1, :]))
    o = o * lax.rsqrt(jnp.mean(o * o, axis=0, keepdims=True) + SUBLN_EPS) * g_ref[...]
    o_ref[...] = (o * (1.0 - lambda_init)).T.astype(o_ref.dtype)


def _diff_attention(qk, vt, lq1, lk1, lq2, lk2, subln_g, lambda_init):
    s = qk.shape[0]
    H = DIFF_HEADS
    TQ = ATTN_Q_BLOCK
    TK = ATTN_K_BLOCK
    DV = DIFF_V_DIM
    assert TQ == 2 * TK and s % TQ == 0
    row = lambda v: v.reshape(1, -1)
    lam_specs = [_const_spec((1, DIFF_HEAD_DIM))] * 4
    return pl.pallas_call(
        functools.partial(_diffattn_kernel, lambda_init=lambda_init),
        grid=(H, s // TQ),
        in_specs=[pl.BlockSpec((TQ, DV), lambda h, qi: (qi, h)),
                  pl.BlockSpec((s, DV), lambda h, qi: (0, H + h)),
                  pl.BlockSpec((DV, s), lambda h, qi: (h, 0))]
                 + lam_specs + [_const_spec((DV, 1))],
        out_specs=pl.BlockSpec((TQ, DV), lambda h, qi: (qi, h)),
        out_shape=jax.ShapeDtypeStruct((s, DIFF_WIDTH), BF16),
        scratch_shapes=[pltpu.VMEM((2, TK, TQ), F32), pltpu.VMEM((2, TK, TQ), F32),
                        pltpu.VMEM((4, 1, TQ), F32), pltpu.VMEM((2, 1, TQ), F32),
                        pltpu.VMEM((2, DV + BF16_SUBLANES, TQ), F32)],
        compiler_params=pltpu.CompilerParams(dimension_semantics=("parallel", "arbitrary"),
                                             vmem_limit_bytes=VMEM_LIMIT_ATTENTION),
        name="diff_attention",
    )(qk, qk, vt, row(lq1), row(lk1), row(lq2), row(lk2), subln_g.reshape(DV, 1))


def _merge_kernel(ya_ref, yb_ref, gate_ref, x_ref, wa_ref, wb_ref, wo_ref, g2_ref,
                  h_ref, ht_ref):
    pa = _dot(ya_ref[...], wa_ref[...])
    pb = _dot(yb_ref[...], wb_ref[...])
    ga = jax.nn.sigmoid(gate_ref[:, :D_MODEL].astype(F32))
    gb = jax.nn.sigmoid(gate_ref[:, D_MODEL:].astype(F32))
    merged = ga * pa + gb * pb
    h = x_ref[...] + _dot(merged.astype(BF16), wo_ref[...])
    h_ref[...] = h
    hn = h * lax.rsqrt(jnp.mean(h * h, axis=-1, keepdims=True) + NORM_EPS) * g2_ref[...]
    ht_ref[...] = hn.T.astype(BF16)


def _merge(ya, yb, gate, x, wa, wb, wo, g2, tm):
    s = x.shape[0]
    D = D_MODEL
    rows = lambda w: pl.BlockSpec((tm, w), lambda i: (i, 0))
    single = lambda shape: pl.BlockSpec(shape, lambda i: (0, 0), pipeline_mode=pl.Buffered(1))
    return pl.pallas_call(
        _merge_kernel,
        grid=(s // tm,),
        in_specs=[rows(RWKV_WIDTH), rows(DIFF_WIDTH), rows(GATE_COLS), rows(D),
                  single((RWKV_WIDTH, D)), single((DIFF_WIDTH, D)), single((D, D)),
                  _const_spec((1, D))],
        out_specs=[rows(D), pl.BlockSpec((D, tm), lambda i: (0, i))],
        out_shape=[jax.ShapeDtypeStruct((s, D), F32), jax.ShapeDtypeStruct((D, s), BF16)],
        compiler_params=pltpu.CompilerParams(dimension_semantics=("parallel",),
                                             vmem_limit_bytes=VMEM_LIMIT_RESIDENT),
        name="merge_out_proj",
    )(ya, yb, gate, x, wa, wb, wo, g2.reshape(1, D))


def _cmp_exchange(xs, i, l, descending):
    hi = jnp.maximum(xs[i], xs[l])
    lo = jnp.minimum(xs[i], xs[l])
    xs[i], xs[l] = (hi, lo) if descending else (lo, hi)


def _bitonic_merge_desc(xs):
    xs = list(xs)
    n = len(xs)
    j = n // 2
    while j >= 1:
        for i in range(n):
            l = i ^ j
            if l > i:
                _cmp_exchange(xs, i, l, True)
        j //= 2
    return xs


def _bitonic_sort_desc(xs):
    xs = list(xs)
    n = len(xs)
    k = 2
    while k <= n:
        j = k // 2
        while j >= 1:
            for i in range(n):
                l = i ^ j
                if l > i:
                    _cmp_exchange(xs, i, l, (i & k) == 0)
            j //= 2
        k *= 2
    return xs


def _merge_top(a, b):
    n = len(a)
    return _bitonic_merge_desc([jnp.maximum(a[i], b[n - 1 - i]) for i in range(n)])


def _top16_over_rows(s):
    groups = [s[g * SUBLANES:(g + 1) * SUBLANES, :] for g in range(s.shape[0] // SUBLANES)]
    top = _bitonic_sort_desc(groups)
    for shift in (4, 2, 1):
        top = _merge_top(top, [pltpu.roll(x, shift, axis=0) for x in top])
    return top


def _prefix_count(rows, pred):
    def pick(conds, cands):
        if not conds:
            return cands[0]
        half = len(cands) // 2
        return jnp.where(conds[0], pick(conds[1:], cands[half:]), pick(conds[1:], cands[:half]))

    n = len(rows)
    conds = []
    count = None
    step = n // 2
    while step >= 1:
        cands = [rows[lo + step - 1] for lo in range(0, n, 2 * step)]
        c = pred(pick(conds, cands))
        inc = jnp.where(c, float(step), 0.0)
        count = inc if count is None else count + inc
        conds.append(c)
        step //= 2
    return jnp.where(pred(rows[n - 1]), float(n), count)


def _peer_score_kernel(qt_ref, keys_ref, r2_ref, e2_ref, n_ref, d_ref):
    K = PEER_TOPK
    T = qt_ref.shape[1]
    H = PEER_HEADS
    scores = []
    tops = []
    for hp in range(2 * H):
        s = _dot(keys_ref[hp], qt_ref[hp * PEER_HALF:(hp + 1) * PEER_HALF, :], HIGHEST)
        scores.append(s)
        tops.append(_top16_over_rows(s))
    sub = lax.broadcasted_iota(jnp.int32, (SUBLANES, T), 0)

    def by_head(p, i):
        out = tops[p][i]
        for h in range(1, H):
            out = jnp.where(sub == h, tops[2 * h + p][i], out)
        return out

    aa = [by_head(0, i) for i in range(K)]
    bb = [by_head(1, i) for i in range(K)]
    cands = [aa[i] + bb[j] for i in range(K) for j in range(K) if (i + 1) * (j + 1) <= K]
    cands += [jnp.full_like(cands[0], -jnp.inf)] * (-len(cands) % K)
    best = _bitonic_sort_desc(cands[:K])
    for c in range(K, len(cands), K):
        best = _merge_top(best, _bitonic_sort_desc(cands[c:c + K]))
    thr = best[K - 1]
    zsum = jnp.zeros_like(thr)
    for t in best:
        zsum = zsum + jnp.exp(t - best[0])
    inv_z = 1.0 / zsum
    for h in range(H):
        hs = slice(h, h + 1)
        s1, s2 = scores[2 * h], scores[2 * h + 1]
        thr_h = thr[hs]
        b_rows = [bb[m][hs] for m in range(K)]
        cnt = _prefix_count(b_rows, lambda b: s1 + b >= thr_h)
        rank = _prefix_count(b_rows, lambda b: b > s2)
        n_ref[h] = cnt
        r2_ref[h] = rank.astype(BF16)
        d_ref[h] = jnp.exp(s1 - aa[0][hs]) * inv_z[hs]
        e2_ref[h] = jnp.exp(s2 - bb[0][hs]).astype(BF16)


def _peer_scores(qt, keys, tt):
    nq, s = qt.shape
    H = PEER_HEADS
    out = lambda dtype: jax.ShapeDtypeStruct((H, N_KEYS, s), dtype)
    ospec = pl.BlockSpec((H, N_KEYS, tt), lambda i: (0, 0, i))
    return pl.pallas_call(
        _peer_score_kernel,
        grid=(s // tt,),
        in_specs=[pl.BlockSpec((nq, tt), lambda i: (0, i)),
                  _const_spec((2 * H, N_KEYS, PEER_HALF))],
        out_specs=[ospec] * 4,
        out_shape=[out(BF16), out(BF16), out(F32), out(F32)],
        compiler_params=pltpu.CompilerParams(dimension_semantics=("parallel",)),
        name="peer_scores",
    )(qt, keys)


def _peer_expert_kernel(hnt_ref, h1_ref, r2_ref, e2_ref, n_ref, d_ref, u_ref, vt_ref, fg_ref,
                        o_ref, acc_ref):
    e = pl.program_id(1)
    eb = u_ref.shape[0]
    tt = hnt_ref.shape[1]
    ni = eb // N_KEYS
    slab = BF16_SUBLANES
    strip = 2 * LANES

    groups = []
    for ii in range(ni):
        i = e * ni + ii
        rows = slice(ii * N_KEYS, (ii + 1) * N_KEYS)
        pre = _dot(u_ref[rows, :], hnt_ref[...])
        act = (0.5 * pre * (1.0 + lax.erf(pre * math.sqrt(0.5)))).astype(BF16)
        strips = []
        for t0 in range(0, tt, strip):
            ts = slice(t0, t0 + strip)
            nb = [jnp.broadcast_to(n_ref[h, pl.ds(i, 1), ts], (slab, strip)).astype(BF16)
                  for h in range(PEER_HEADS)]
            db = [jnp.broadcast_to(d_ref[h, pl.ds(i, 1), ts], (slab, strip)).astype(BF16)
                  for h in range(PEER_HEADS)]
            slabs = []
            for j0 in range(0, N_KEYS, slab):
                js = slice(j0, j0 + slab)
                gate = None
                for h in range(PEER_HEADS):
                    term = jnp.where(r2_ref[h, js, ts] < nb[h], e2_ref[h, js, ts],
                                     jnp.zeros((), BF16)) * db[h]
                    gate = term if gate is None else gate + term
                slabs.append(gate * act[js, ts])
            strips.append(jnp.concatenate(slabs, axis=0))
        groups.append(jnp.concatenate(strips, axis=1))
    prev = jnp.where(e == 0, 0.0, acc_ref[...])
    acc_ref[...] = prev + _dot(vt_ref[0], jnp.concatenate(groups, axis=0))

    @pl.when(e == pl.num_programs(1) - 1)
    def _():
        h = h1_ref[...] + acc_ref[...].T
        o_ref[...] = h * lax.rsqrt(jnp.mean(h * h, axis=-1, keepdims=True) + NORM_EPS) * fg_ref[...]


def _peer_experts(hnt, h1, r2, e2, n, d, u, vt, final_g, tt, eb):
    s = hnt.shape[1]
    D = D_MODEL
    H = PEER_HEADS
    sel = pl.BlockSpec((H, N_KEYS, tt), lambda i, e: (0, 0, i), pipeline_mode=pl.Buffered(1))
    return pl.pallas_call(
        _peer_expert_kernel,
        grid=(s // tt, N_EXPERTS // eb),
        in_specs=[pl.BlockSpec((D, tt), lambda i, e: (0, i)),
                  pl.BlockSpec((tt, D), lambda i, e: (i, 0), pipeline_mode=pl.Buffered(1)),
                  sel, sel, sel, sel,
                  pl.BlockSpec((eb, D), lambda i, e: (e, 0)),
                  pl.BlockSpec((1, D, eb), lambda i, e: (e, 0, 0)),
                  pl.BlockSpec((1, D), lambda i, e: (0, 0))],
        out_specs=pl.BlockSpec((tt, D), lambda i, e: (i, 0)),
        out_shape=jax.ShapeDtypeStruct((s, D), F32),
        scratch_shapes=[pltpu.VMEM((D, tt), F32)],
        compiler_params=pltpu.CompilerParams(dimension_semantics=("parallel", "arbitrary"),
                                             vmem_limit_bytes=VMEM_LIMIT_RESIDENT),
        name="peer_experts",
    )(hnt, h1, r2, e2, n, d, u, vt, final_g.reshape(1, D))


def _pad_rows(w, rows):
    return jnp.pad(w, ((0, rows - w.shape[0]), (0, 0)))


def _layer(h, norm1_g, w_in, shift_mu, rwkv_w0, w_decay_up, rwkv_a0, w_iclr_up, w_gate_up,
           k_k, k_a, r_k, lnx_g, lnx_b, lam_q1, lam_k1, lam_q2, lam_k2, subln_g, w_proj_a,
           w_proj_b, w_out, norm2_g, peer_wq, peer_sub_keys, peer_u, peer_v, out_g, lambda_init):
    s = h.shape[0]
    W = RWKV_WIDTH
    tmm = min(s, MM_ROW_BLOCK)
    tn = MM_COL_BLOCK

    c0, c1, c2 = 3 * W, 3 * W + DECAY_LORA, 3 * W + DECAY_LORA + ICLR_LORA
    pad_cols = lambda m, n: jnp.pad(m, ((0, 0), (0, n - m.shape[1])))
    w_rwkv = jnp.concatenate([w_in[:, :c0], pad_cols(w_in[:, c0:c1], LORA_PAD),
                              pad_cols(w_in[:, c1:c2], LORA_PAD), w_in[:, c2:RWKV_COLS]], axis=1)
    mu2 = shift_mu.reshape(1, -1)
    mu = jnp.concatenate([mu2[:, :c0], pad_cols(mu2[:, c0:c1], LORA_PAD),
                          pad_cols(mu2[:, c1:c2], LORA_PAD), mu2[:, c2:]], axis=1)
    d0 = RWKV_COLS
    q_scale = DIFF_HEAD_DIM ** -0.5 * math.log2(math.e)
    w_qk = jnp.concatenate([w_in[:, d0:d0 + DIFF_WIDTH] * q_scale,
                            w_in[:, d0 + DIFF_WIDTH:d0 + 2 * DIFF_WIDTH]], axis=1)
    w_vt = w_in[:, d0 + 2 * DIFF_WIDTH:d0 + DIFF_COLS].T
    w_gate = w_in[:, d0 + DIFF_COLS:]

    p_rwkv, xn = _norm_matmul(h, norm1_g, w_rwkv.astype(BF16), NORM_EPS, F32, tmm,
                              RWKV_COL_BLOCK, "in_proj_rwkv")
    p_qk = _matmul(xn, w_qk.astype(BF16), BF16, tmm, tn, "in_proj_qk")
    p_vt = _matmul_nt(w_vt.astype(BF16), xn, BF16, VT_ROW_BLOCK, tmm, "in_proj_vt")
    p_gate = _matmul(xn, w_gate.astype(BF16), BF16, tmm, tn, "in_proj_gate")

    y_a = _rwkv_time_mix(p_rwkv, mu, rwkv_w0, _pad_rows(w_decay_up, LORA_PAD), rwkv_a0,
                         _pad_rows(w_iclr_up, LORA_PAD), w_gate_up, k_k, k_a, r_k, lnx_g, lnx_b)
    y_b = _diff_attention(p_qk, p_vt, lam_q1, lam_k1, lam_q2, lam_k2, subln_g, lambda_init)
    h1, hn_t = _merge(y_a, y_b, p_gate, h, w_proj_a.astype(BF16), w_proj_b.astype(BF16),
                      w_out.astype(BF16), norm2_g, min(s, MERGE_ROW_BLOCK))

    keys = peer_sub_keys.reshape(2 * PEER_HEADS, N_KEYS, PEER_HALF)
    q_t = _matmul(peer_wq.T.astype(BF16), hn_t, F32, MM_ROW_BLOCK, tn, "peer_query")
    r2, e2, n, d = _peer_scores(q_t, keys, min(s, PEER_SCORE_BLOCK))
    eb = PEER_EXPERT_BLOCK
    return _peer_experts(hn_t, h1, r2, e2, n, d, peer_u.astype(BF16),
                         _transpose_cast_blocks(peer_v, BF16, eb), out_g,
                         min(s, PEER_TOKEN_BLOCK), eb)


def kernel(x, norm1_g, w_in, shift_mu, rwkv_w0, w_decay_up, rwkv_a0, w_iclr_up, w_gate_up, k_k, k_a, r_k, lnx_g, lnx_b, lam_q1, lam_k1, lam_q2, lam_k2, subln_g, w_proj_a, w_proj_b, w_out, norm2_g, peer_wq, peer_sub_keys, peer_u, peer_v, final_g):
    B, S, D = x.shape
    assert B == 1 and D == D_MODEL and norm1_g.shape[0] == 1
    lambda_init = 0.8 - 0.6 * math.exp(-0.3 * 0)
    out = _layer(x[0], norm1_g[0], w_in[0], shift_mu[0], rwkv_w0[0], w_decay_up[0], rwkv_a0[0],
                 w_iclr_up[0], w_gate_up[0], k_k[0], k_a[0], r_k[0].reshape(-1), lnx_g[0],
                 lnx_b[0], lam_q1[0], lam_k1[0], lam_q2[0], lam_k2[0], subln_g[0], w_proj_a[0],
                 w_proj_b[0], w_out[0], norm2_g[0], peer_wq[0], peer_sub_keys[0], peer_u[0],
                 peer_v[0], final_g, lambda_init)
    return out[None]
```

```python
import functools
import math

import jax
import jax.numpy as jnp
from jax import lax
from jax.experimental import pallas as pl
from jax.experimental.pallas import tpu as pltpu

F32 = jnp.float32
BF16 = jnp.bfloat16
HIGHEST = lax.Precision.HIGHEST

LANES = 128
SUBLANES = 8
BF16_SUBLANES = 16

D_MODEL = 2048
RWKV_HEADS = 16
RWKV_HEAD_DIM = 64
RWKV_WIDTH = RWKV_HEADS * RWKV_HEAD_DIM
DECAY_LORA = 96
ICLR_LORA = 96
GATE_LORA = 256
LORA_PAD = 128
RWKV_COLS = 3 * RWKV_WIDTH + DECAY_LORA + ICLR_LORA + GATE_LORA
RWKV_COLS_PAD = 3 * RWKV_WIDTH + 2 * LORA_PAD + GATE_LORA
RWKV_CHUNK = 64
RWKV_BLOCK = 256
RWKV_GROUP = 4
RWKV_GROUP_LANES = RWKV_GROUP * RWKV_HEAD_DIM
DIFF_HEADS = 8
DIFF_HEAD_DIM = 64
DIFF_V_DIM = 2 * DIFF_HEAD_DIM
DIFF_WIDTH = DIFF_HEADS * DIFF_V_DIM
DIFF_COLS = 3 * DIFF_WIDTH
ATTN_K_BLOCK = 512
ATTN_Q_BLOCK = 1024
GATE_COLS = 2 * D_MODEL
PEER_HEADS = 8
PEER_HALF = 128
N_KEYS = 128
N_EXPERTS = N_KEYS * N_KEYS
PEER_TOPK = 16
NORM_EPS = 1e-6
LN_X_EPS = 64e-5
SUBLN_EPS = 1e-5
NEG_INF = -1e30

MM_ROW_BLOCK = 1024
MM_ROW_BLOCK_BF16 = 2048
MM_COL_BLOCK = 1024
RWKV_COL_BLOCK = RWKV_COLS_PAD // 4
VT_ROW_BLOCK = 1024
MERGE_ROW_BLOCK = 256
PEER_SCORE_BLOCK = 128
PEER_TOKEN_BLOCK = 512
PEER_EXPERT_BLOCK = 1024
MIB = 1024 * 1024
VMEM_LIMIT_RESIDENT = 56 * MIB
VMEM_LIMIT_ATTENTION = 40 * MIB


def _dot(a, b, precision=None):
    return jnp.dot(a, b, preferred_element_type=F32, precision=precision)


def _dot_nt(a, b, precision=None):
    return lax.dot_general(a, b, (((1,), (1,)), ((), ())), preferred_element_type=F32,
                           precision=precision)


def _const_spec(shape):
    nd = len(shape)
    return pl.BlockSpec(shape, lambda *_: (0,) * nd)


def _dot_bf16(a, b):
    return _dot(a.astype(BF16), b.astype(BF16))


def _mm_kernel(x_ref, w_ref, o_ref):
    o_ref[...] = _dot(x_ref[...], w_ref[...]).astype(o_ref.dtype)


def _matmul(x, w, out_dtype, tm, tn, name):
    s, k = x.shape
    n = w.shape[1]
    return pl.pallas_call(
        _mm_kernel,
        grid=(s // tm, n // tn),
        in_specs=[pl.BlockSpec((tm, k), lambda i, j: (i, 0)),
                  pl.BlockSpec((k, tn), lambda i, j: (0, j))],
        out_specs=pl.BlockSpec((tm, tn), lambda i, j: (i, j)),
        out_shape=jax.ShapeDtypeStruct((s, n), out_dtype),
        compiler_params=pltpu.CompilerParams(dimension_semantics=("parallel", "parallel")),
        name=name,
    )(x, w)


def _norm_mm_kernel(x_ref, g_ref, w_ref, o_ref, xn_ref, *, eps):
    @pl.when(pl.program_id(1) == 0)
    def _():
        x = x_ref[...]
        y = x * lax.rsqrt(jnp.mean(x * x, axis=-1, keepdims=True) + eps) * g_ref[...]
        xn_ref[...] = y.astype(xn_ref.dtype)

    o_ref[...] = _dot(xn_ref[...], w_ref[...]).astype(o_ref.dtype)


def _norm_matmul(x, g, w, eps, out_dtype, tm, tn, name):
    s, k = x.shape
    n = w.shape[1]
    return pl.pallas_call(
        functools.partial(_norm_mm_kernel, eps=eps),
        grid=(s // tm, n // tn),
        in_specs=[pl.BlockSpec((tm, k), lambda i, j: (i, 0)), _const_spec((1, k)),
                  pl.BlockSpec((k, tn), lambda i, j: (0, j))],
        out_specs=[pl.BlockSpec((tm, tn), lambda i, j: (i, j)),
                   pl.BlockSpec((tm, k), lambda i, j: (i, 0))],
        out_shape=[jax.ShapeDtypeStruct((s, n), out_dtype), jax.ShapeDtypeStruct((s, k), w.dtype)],
        compiler_params=pltpu.CompilerParams(dimension_semantics=("parallel", "arbitrary")),
        name=name,
    )(x, g.reshape(1, k), w)


def _mm_nt_kernel(w_ref, x_ref, o_ref):
    o_ref[...] = _dot_nt(w_ref[...], x_ref[...]).astype(o_ref.dtype)


def _matmul_nt(w, x, out_dtype, tn, tm, name):
    n, k = w.shape
    s = x.shape[0]
    return pl.pallas_call(
        _mm_nt_kernel,
        grid=(s // tm, n // tn),
        in_specs=[pl.BlockSpec((tn, k), lambda i, j: (j, 0)),
                  pl.BlockSpec((tm, k), lambda i, j: (i, 0))],
        out_specs=pl.BlockSpec((tn, tm), lambda i, j: (j, i)),
        out_shape=jax.ShapeDtypeStruct((n, s), out_dtype),
        compiler_params=pltpu.CompilerParams(dimension_semantics=("parallel", "parallel")),
        name=name,
    )(w, x)


def _transpose_cast_kernel(x_ref, o_ref):
    o_ref[0] = x_ref[...].T.astype(o_ref.dtype)


def _transpose_cast_blocks(x, out_dtype, tb):
    r, c = x.shape
    return pl.pallas_call(
        _transpose_cast_kernel,
        grid=(r // tb, c // tb),
        in_specs=[pl.BlockSpec((tb, tb), lambda i, j: (i, j))],
        out_specs=pl.BlockSpec((1, tb, tb), lambda i, j: (i, j, 0)),
        out_shape=jax.ShapeDtypeStruct((r // tb, c, tb), out_dtype),
        compiler_params=pltpu.CompilerParams(dimension_semantics=("parallel", "parallel")),
        name="transpose_cast",
    )(x)


def _head_sum(x, ones_bd):
    xb = x.astype(BF16)
    width = ones_bd.shape[0]
    tiles = [_dot(xb[:, c * width:(c + 1) * width], ones_bd)
             for c in range(x.shape[1] // width)]
    return jnp.concatenate(tiles, axis=1)


def _softplus(x):
    return jnp.maximum(x, 0.0) + jnp.log1p(jnp.exp(-jnp.abs(x)))


def _rwkv_kernel(p_ref, pprev_ref, mu_ref, w0_ref, wd_ref, a0_ref, wa_ref, wg_ref, kk_ref,
                 ka_ref, rk_ref, lng_ref, lnb_ref, o_ref, state_ref, y_ref):
    L = RWKV_CHUNK
    N = RWKV_HEAD_DIM
    W = RWKV_WIDTH
    G = RWKV_GROUP
    GL = RWKV_GROUP_LANES
    step = pl.program_id(0)

    @pl.when(step == 0)
    def _():
        state_ref[...] = jnp.zeros_like(state_ref)

    TB = p_ref.shape[0]
    row = lax.broadcasted_iota(jnp.int32, (TB, 1), 0)
    carry_on = jnp.where(step == 0, 0.0, 1.0)

    def shifted(c0, c1):
        p = p_ref[:, c0:c1]
        last = pprev_ref[SUBLANES - 1:SUBLANES, c0:c1] * carry_on
        prev = jnp.where(row == 0, last, pltpu.roll(p, 1, axis=0))
        return p + (prev - p) * mu_ref[:, c0:c1]

    r = shifted(0, W)
    k = shifted(W, 2 * W)
    v = shifted(2 * W, 3 * W)
    xw = shifted(3 * W, 3 * W + LORA_PAD)
    xa = shifted(3 * W + LORA_PAD, 3 * W + 2 * LORA_PAD)
    xg = shifted(3 * W + 2 * LORA_PAD, 3 * W + 2 * LORA_PAD + GATE_LORA)

    z = w0_ref[...] + _dot_bf16(jnp.tanh(xw), wd_ref[...])
    w_log = -_softplus(-z) - 0.5
    lw = -jnp.exp(w_log)
    a = jax.nn.sigmoid(a0_ref[...] + _dot_bf16(xa, wa_ref[...]))
    g = _dot_bf16(jax.nn.sigmoid(xg), wg_ref[...])

    bi = lax.broadcasted_iota(jnp.int32, (GL, GL), 0) // N
    bj = lax.broadcasted_iota(jnp.int32, (GL, GL), 1) // N
    bd_mask = bi == bj
    ones_bd = jnp.where(bd_mask, 1.0, 0.0).astype(BF16)

    kk = k * kk_ref[...]
    kk = kk / jnp.maximum(jnp.sqrt(_head_sum(kk * kk, ones_bd)), 1e-12)
    k = k * (1.0 + (a - 1.0) * ka_ref[...])

    ti = lax.broadcasted_iota(jnp.int32, (TB, TB), 0)
    tj = lax.broadcasted_iota(jnp.int32, (TB, TB), 1)
    tril = jnp.where((tj <= ti) & (tj // L == ti // L), 1.0, 0.0).astype(BF16)
    gi = lax.broadcasted_iota(jnp.int32, (L, GL), 0)
    gj = lax.broadcasted_iota(jnp.int32, (L, GL), 1) % L
    incl = gj <= gi
    strict = gj < gi
    eye4 = jnp.where(gj == gi, 1.0, 0.0).astype(F32)
    def bd(x):
        return jnp.where(bd_mask, jnp.concatenate([x] * G, axis=0), jnp.zeros((), x.dtype))

    lw_hi = lw.astype(BF16)
    lw_rest = lw - lw_hi.astype(F32)
    lw_mid = lw_rest.astype(BF16)
    lw_lo = (lw_rest - lw_mid.astype(F32)).astype(BF16)
    cum = _dot(tril, lw_hi) + (_dot(tril, lw_mid) + _dot(tril, lw_lo))
    e_inv = jnp.exp(-cum)
    r_t = (r * jnp.exp(cum)).astype(BF16)
    a_t = (-kk * jnp.exp(cum - lw)).astype(BF16)
    b = kk * a
    b_t = (b * e_inv).astype(BF16)
    k_t = (k * e_inv).astype(BF16)
    v_b = v.astype(BF16)
    eye4_b = eye4.astype(BF16)
    n_grp = RWKV_HEADS // G
    states = [state_ref[grp] for grp in range(n_grp)]

    n_chunks = TB // L
    pairs = [(c, grp) for c in range(n_chunks) for grp in range(n_grp)]
    rows_of = lambda c: slice(c * L, (c + 1) * L)
    lanes_of = lambda grp: slice(grp * GL, (grp + 1) * GL)
    cum_last = [cum[(c + 1) * L - 1:(c + 1) * L, :] for c in range(n_chunks)]

    a_ab, a_ak, a_rb, a_rk = {}, {}, {}, {}
    for c, grp in pairs:
        rs, gs = rows_of(c), lanes_of(grp)
        m = _dot_nt(jnp.concatenate([a_t[rs, gs], r_t[rs, gs]], axis=0),
                    jnp.concatenate([bd(b_t[rs, gs]), bd(k_t[rs, gs])], axis=0))
        a_ab[c, grp] = jnp.where(strict, m[:L, :GL], 0.0)
        a_ak[c, grp] = jnp.where(strict, m[:L, GL:], 0.0).astype(BF16)
        a_rb[c, grp] = jnp.where(incl, m[L:, :GL], 0.0).astype(BF16)
        a_rk[c, grp] = jnp.where(incl, m[L:, GL:], 0.0).astype(BF16)

    inv = {p: eye4 + a_ab[p] for p in pairs}
    pw = {}
    for p in pairs:
        pw_b = a_ab[p].astype(BF16)
        pw[p] = _dot(pw_b, bd(pw_b))
    for _ in range(4):
        for p in pairs:
            pw_b = pw[p].astype(BF16)
            both = _dot(jnp.concatenate([pw_b, inv[p].astype(BF16)], axis=0), bd(pw_b))
            pw[p], inv[p] = both[:L], inv[p] + both[L:]
    for p in pairs:
        inv[p] = (inv[p] + _dot(inv[p].astype(BF16), bd(pw[p].astype(BF16)))).astype(BF16)

    bk = {}
    for c in range(n_chunks):
        rs = rows_of(c)
        e_tail = jnp.exp(cum_last[c] - cum[rs])
        b_w = (b[rs] * e_tail).astype(BF16)
        k_w = (k[rs] * e_tail).astype(BF16)
        for grp in range(n_grp):
            gs = lanes_of(grp)
            bk[c, grp] = _dot_nt(
                eye4_b, jnp.concatenate([bd(b_w[:, gs]), bd(k_w[:, gs])], axis=0)).astype(BF16)

    for c in range(n_chunks):
        rs = rows_of(c)
        w_last = jnp.exp(cum_last[c])
        bd_z = [bd(states[grp].astype(BF16)) for grp in range(n_grp)]
        bd_v = [bd(v_b[rs, lanes_of(grp)]) for grp in range(n_grp)]
        x = [_dot(jnp.concatenate([a_t[rs, lanes_of(grp)], a_ak[c, grp]], axis=1),
                  jnp.concatenate([bd_z[grp], bd_v[grp]], axis=0)) for grp in range(n_grp)]
        u = [_dot(inv[c, grp], bd(x[grp].astype(BF16))) for grp in range(n_grp)]
        for grp in range(n_grp):
            gs = lanes_of(grp)
            lhs = jnp.concatenate(
                [jnp.concatenate([r_t[rs, gs], a_rb[c, grp], a_rk[c, grp]], axis=1),
                 jnp.concatenate([(eye4 * w_last[:, gs]).astype(BF16), bk[c, grp]], axis=1)],
                axis=0)
            yz = _dot(lhs, jnp.concatenate([bd_z[grp], bd(u[grp].astype(BF16)), bd_v[grp]],
                                           axis=0))
            y_ref[rs, gs] = yz[:L]
            states[grp] = yz[L:]
    for grp in range(n_grp):
        state_ref[grp] = states[grp]

    y = y_ref[...]
    mean = _head_sum(y, ones_bd) * (1.0 / N)
    yc = y - mean
    var = _head_sum(yc * yc, ones_bd) * (1.0 / N)
    yn = yc * lax.rsqrt(var + LN_X_EPS) * lng_ref[...] + lnb_ref[...]
    bonus = _head_sum(r * k * rk_ref[...], ones_bd) * v
    o_ref[...] = ((yn + bonus) * g).astype(o_ref.dtype)


def _rwkv_time_mix(p, mu, w0, wd, a0, wa, wg, k_k, k_a, r_k, lnx_g, lnx_b):
    s = p.shape[0]
    L = RWKV_CHUNK
    W = RWKV_WIDTH
    row = lambda v: v.reshape(1, -1)
    consts = [row(mu), row(w0), wd, row(a0), wa, wg, row(k_k), row(k_a), row(r_k), row(lnx_g),
              row(lnx_b)]
    tb = RWKV_BLOCK
    return pl.pallas_call(
        _rwkv_kernel,
        grid=(s // tb,),
        in_specs=[pl.BlockSpec((tb, RWKV_COLS_PAD), lambda i: (i, 0)),
                  pl.BlockSpec((SUBLANES, RWKV_COLS_PAD),
                               lambda i: (jnp.maximum(i * (tb // SUBLANES) - 1, 0), 0))]
                 + [_const_spec(c.shape) for c in consts],
        out_specs=pl.BlockSpec((tb, W), lambda i: (i, 0)),
        out_shape=jax.ShapeDtypeStruct((s, W), BF16),
        scratch_shapes=[pltpu.VMEM((RWKV_HEADS // RWKV_GROUP, RWKV_HEAD_DIM, RWKV_GROUP_LANES),
                                   F32),
                        pltpu.VMEM((tb, W), F32)],
        compiler_params=pltpu.CompilerParams(dimension_semantics=("arbitrary",)),
        name="rwkv7",
    )(p, p, *consts)


def _diffattn_kernel(q_ref, k_ref, vt_ref, lq1_ref, lk1_ref, lq2_ref, lk2_ref, g_ref, o_ref,
                     sa_ref, sb_ref, mx_ref, m_ref, acc_ref, *, lambda_init):
    TQ = ATTN_Q_BLOCK
    TK = ATTN_K_BLOCK
    DV = DIFF_V_DIM
    qi = pl.program_id(1)
    q = q_ref[...]
    lane = lax.broadcasted_iota(jnp.int32, q.shape, 1)
    zero = jnp.zeros_like(q)
    q_halves = (jnp.where(lane < DIFF_HEAD_DIM, q, zero), jnp.where(lane >= DIFF_HEAD_DIM, q, zero))
    ones = jnp.ones((BF16_SUBLANES, TK), BF16)
    key_i = lax.broadcasted_iota(jnp.int32, (TK, TQ), 0)
    qry_i = lax.broadcasted_iota(jnp.int32, (TK, TQ), 1)

    s_bufs = (sa_ref, sb_ref)

    def scores(j, buf, key_offset=None):
        kb = k_ref[pl.ds(pl.multiple_of(j * TK, TK), TK), :]
        for idx in range(2):
            s = _dot_nt(kb, q_halves[idx])
            if key_offset is not None:
                s = jnp.where(key_i + key_offset <= qry_i, s, NEG_INF)
            s_bufs[buf][idx] = s
            mx_ref[2 * buf + idx] = jnp.max(s, axis=0, keepdims=True)

    def probs(buf):
        out = []
        for idx in range(2):
            m_old = m_ref[idx]
            m_new = jnp.maximum(m_old, mx_ref[2 * buf + idx])
            m_ref[idx] = m_new
            out.append((jnp.exp2(s_bufs[buf][idx] - m_new).astype(BF16),
                        jnp.exp2(m_old - m_new)))
        return out

    def accumulate(j, pa):
        vt = vt_ref[:, pl.ds(pl.multiple_of(j * TK, TK), TK)]
        vext = jnp.concatenate([vt, ones], axis=0)
        for idx, (p, alpha) in enumerate(pa):
            acc_ref[idx] = alpha * acc_ref[idx] + _dot(vext, p)

    def step(j, buf, next_scores):
        pa = probs(buf)
        next_scores()
        accumulate(j, pa)

    m_ref[...] = jnp.full_like(m_ref, NEG_INF)
    acc_ref[...] = jnp.zeros_like(acc_ref)

    @pl.when(qi > 0)
    def _():
        scores(0, 0)

    @pl.loop(0, qi - 1)
    def _(i):
        step(2 * i, 0, lambda: scores(2 * i + 1, 1))
        step(2 * i + 1, 1, lambda: scores(2 * i + 2, 0))

    @pl.when(qi > 0)
    def _():
        step(2 * qi - 2, 0, lambda: scores(2 * qi - 1, 1))
        step(2 * qi - 1, 1, lambda: scores(2 * qi, 0, key_offset=0))

    @pl.when(qi == 0)
    def _():
        scores(0, 0, key_offset=0)

    step(2 * qi, 0, lambda: scores(2 * qi + 1, 1, key_offset=TK))
    step(2 * qi + 1, 1, lambda: None)

    lam = (jnp.exp(jnp.sum(lq1_ref[...] * lk1_ref[...], axis=-1, keepdims=True))
           - jnp.exp(jnp.sum(lq2_ref[...] * lk2_ref[...], axis=-1, keepdims=True))
           + lambda_init)
    o = (acc_ref[0, :DV, :] / acc_ref[0, DV:DV + 1, :]
         - lam * (acc_ref[1, :DV, :] / acc_ref[1, DV:DV + 1, :]))
    o = o * lax.rsqrt(jnp.mean(o * o, axis=0, keepdims=True) + SUBLN_EPS) * g_ref[...]
    o_ref[...] = (o * (1.0 - lambda_init)).T.astype(o_ref.dtype)


def _diff_attention(qk, vt, lq1, lk1, lq2, lk2, subln_g, lambda_init):
    s = qk.shape[0]
    H = DIFF_HEADS
    TQ = ATTN_Q_BLOCK
    TK = ATTN_K_BLOCK
    DV = DIFF_V_DIM
    assert TQ == 2 * TK and s % TQ == 0
    row = lambda v: v.reshape(1, -1)
    lam_specs = [_const_spec((1, DIFF_HEAD_DIM))] * 4
    return pl.pallas_call(
        functools.partial(_diffattn_kernel, lambda_init=lambda_init),
        grid=(H, s // TQ),
        in_specs=[pl.BlockSpec((TQ, DV), lambda h, qi: (qi, h)),
                  pl.BlockSpec((s, DV), lambda h, qi: (0, H + h)),
                  pl.BlockSpec((DV, s), lambda h, qi: (h, 0))]
                 + lam_specs + [_const_spec((DV, 1))],
        out_specs=pl.BlockSpec((TQ, DV), lambda h, qi: (qi, h)),
        out_shape=jax.ShapeDtypeStruct((s, DIFF_WIDTH), BF16),
        scratch_shapes=[pltpu.VMEM((2, TK, TQ), F32), pltpu.VMEM((2, TK, TQ), F32),
                        pltpu.VMEM((4, 1, TQ), F32), pltpu.VMEM((2, 1, TQ), F32),
                        pltpu.VMEM((2, DV + BF16_SUBLANES, TQ), F32)],
        compiler_params=pltpu.CompilerParams(dimension_semantics=("parallel", "arbitrary"),
                                             vmem_limit_bytes=VMEM_LIMIT_ATTENTION),
        name="diff_attention",
    )(qk, qk, vt, row(lq1), row(lk1), row(lq2), row(lk2), subln_g.reshape(DV, 1))


def _merge_kernel(ya_ref, yb_ref, gate_ref, x_ref, wa_ref, wb_ref, wo_ref, g2_ref,
                  h_ref, ht_ref):
    pa = _dot(ya_ref[...], wa_ref[...])
    pb = _dot(yb_ref[...], wb_ref[...])
    ga = jax.nn.sigmoid(gate_ref[:, :D_MODEL].astype(F32))
    gb = jax.nn.sigmoid(gate_ref[:, D_MODEL:].astype(F32))
    merged = ga * pa + gb * pb
    h = x_ref[...] + _dot(merged.astype(BF16), wo_ref[...])
    h_ref[...] = h
    hn = h * lax.rsqrt(jnp.mean(h * h, axis=-1, keepdims=True) + NORM_EPS) * g2_ref[...]
    ht_ref[...] = hn.T.astype(BF16)


def _merge(ya, yb, gate, x, wa, wb, wo, g2, tm):
    s = x.shape[0]
    D = D_MODEL
    rows = lambda w: pl.BlockSpec((tm, w), lambda i: (i, 0))
    single = lambda shape: pl.BlockSpec(shape, lambda i: (0, 0), pipeline_mode=pl.Buffered(1))
    return pl.pallas_call(
        _merge_kernel,
        grid=(s // tm,),
        in_specs=[rows(RWKV_WIDTH), rows(DIFF_WIDTH), rows(GATE_COLS), rows(D),
                  single((RWKV_WIDTH, D)), single((DIFF_WIDTH, D)), single((D, D)),
                  _const_spec((1, D))],
        out_specs=[rows(D), pl.BlockSpec((D, tm), lambda i: (0, i))],
        out_shape=[jax.ShapeDtypeStruct((s, D), F32), jax.ShapeDtypeStruct((D, s), BF16)],
        compiler_params=pltpu.CompilerParams(dimension_semantics=("parallel",),
                                             vmem_limit_bytes=VMEM_LIMIT_RESIDENT),
        name="merge_out_proj",
    )(ya, yb, gate, x, wa, wb, wo, g2.reshape(1, D))


def _cmp_exchange(xs, i, l, descending):
    hi = jnp.maximum(xs[i], xs[l])
    lo = jnp.minimum(xs[i], xs[l])
    xs[i], xs[l] = (hi, lo) if descending else (lo, hi)


def _bitonic_merge_desc(xs):
    xs = list(xs)
    n = len(xs)
    j = n // 2
    while j >= 1:
        for i in range(n):
            l = i ^ j
            if l > i:
                _cmp_exchange(xs, i, l, True)
        j //= 2
    return xs


def _bitonic_sort_desc(xs):
    xs = list(xs)
    n = len(xs)
    k = 2
    while k <= n:
        j = k // 2
        while j >= 1:
            for i in range(n):
                l = i ^ j
                if l > i:
                    _cmp_exchange(xs, i, l, (i & k) == 0)
            j //= 2
        k *= 2
    return xs


def _merge_top(a, b):
    n = len(a)
    return _bitonic_merge_desc([jnp.maximum(a[i], b[n - 1 - i]) for i in range(n)])


def _top16_over_rows(s):
    groups = [s[g * SUBLANES:(g + 1) * SUBLANES, :] for g in range(s.shape[0] // SUBLANES)]
    top = _bitonic_sort_desc(groups)
    for shift in (4, 2, 1):
        top = _merge_top(top, [pltpu.roll(x, shift, axis=0) for x in top])
    return top


def _prefix_count(rows, pred):
    def pick(conds, cands):
        if not conds:
            return cands[0]
        half = len(cands) // 2
        return jnp.where(conds[0], pick(conds[1:], cands[half:]), pick(conds[1:], cands[:half]))

    n = len(rows)
    conds = []
    count = None
    step = n // 2
    while step >= 1:
        cands = [rows[lo + step - 1] for lo in range(0, n, 2 * step)]
        c = pred(pick(conds, cands))
        inc = jnp.where(c, float(step), 0.0)
        count = inc if count is None else count + inc
        conds.append(c)
        step //= 2
    return jnp.where(pred(rows[n - 1]), float(n), count)


def _peer_score_kernel(qt_ref, keys_ref, r2_ref, e2_ref, n_ref, d_ref):
    K = PEER_TOPK
    T = qt_ref.shape[1]
    H = PEER_HEADS
    scores = []
    tops = []
    for hp in range(2 * H):
        s = _dot(keys_ref[hp], qt_ref[hp * PEER_HALF:(hp + 1) * PEER_HALF, :], HIGHEST)
        scores.append(s)
        tops.append(_top16_over_rows(s))
    sub = lax.broadcasted_iota(jnp.int32, (SUBLANES, T), 0)

    def by_head(p, i):
        out = tops[p][i]
        for h in range(1, H):
            out = jnp.where(sub == h, tops[2 * h + p][i], out)
        return out

    aa = [by_head(0, i) for i in range(K)]
    bb = [by_head(1, i) for i in range(K)]
    cands = [aa[i] + bb[j] for i in range(K) for j in range(K) if (i + 1) * (j + 1) <= K]
    cands += [jnp.full_like(cands[0], -jnp.inf)] * (-len(cands) % K)
    best = _bitonic_sort_desc(cands[:K])
    for c in range(K, len(cands), K):
        best = _merge_top(best, _bitonic_sort_desc(cands[c:c + K]))
    thr = best[K - 1]
    zsum = jnp.zeros_like(thr)
    for t in best:
        zsum = zsum + jnp.exp(t - best[0])
    inv_z = 1.0 / zsum
    for h in range(H):
        hs = slice(h, h + 1)
        s1, s2 = scores[2 * h], scores[2 * h + 1]
        thr_h = thr[hs]
        b_rows = [bb[m][hs] for m in range(K)]
        cnt = _prefix_count(b_rows, lambda b: s1 + b >= thr_h)
        rank = _prefix_count(b_rows, lambda b: b > s2)
        n_ref[h] = cnt
        r2_ref[h] = rank.astype(BF16)
        d_ref[h] = jnp.exp(s1 - aa[0][hs]) * inv_z[hs]
        e2_ref[h] = jnp.exp(s2 - bb[0][hs]).astype(BF16)


def _peer_scores(qt, keys, tt):
    nq, s = qt.shape
    H = PEER_HEADS
    out = lambda dtype: jax.ShapeDtypeStruct((H, N_KEYS, s), dtype)
    ospec = pl.BlockSpec((H, N_KEYS, tt), lambda i: (0, 0, i))
    return pl.pallas_call(
        _peer_score_kernel,
        grid=(s // tt,),
        in_specs=[pl.BlockSpec((nq, tt), lambda i: (0, i)),
                  _const_spec((2 * H, N_KEYS, PEER_HALF))],
        out_specs=[ospec] * 4,
        out_shape=[out(BF16), out(BF16), out(F32), out(F32)],
        compiler_params=pltpu.CompilerParams(dimension_semantics=("parallel",)),
        name="peer_scores",
    )(qt, keys)


def _peer_expert_kernel(hnt_ref, h1_ref, r2_ref, e2_ref, n_ref, d_ref, u_ref, vt_ref, fg_ref,
                        o_ref, acc_ref):
    e = pl.program_id(1)
    eb = u_ref.shape[0]
    tt = hnt_ref.shape[1]
    ni = eb // N_KEYS
    slab = BF16_SUBLANES
    strip = 2 * LANES

    groups = []
    for ii in range(ni):
        i = e * ni + ii
        rows = slice(ii * N_KEYS, (ii + 1) * N_KEYS)
        pre = _dot(u_ref[rows, :], hnt_ref[...])
        act = (0.5 * pre * (1.0 + lax.erf(pre * math.sqrt(0.5)))).astype(BF16)
        strips = []
        for t0 in range(0, tt, strip):
            ts = slice(t0, t0 + strip)
            nb = [jnp.broadcast_to(n_ref[h, pl.ds(i, 1), ts], (slab, strip)).astype(BF16)
                  for h in range(PEER_HEADS)]
            db = [jnp.broadcast_to(d_ref[h, pl.ds(i, 1), ts], (slab, strip)).astype(BF16)
                  for h in range(PEER_HEADS)]
            slabs = []
            for j0 in range(0, N_KEYS, slab):
                js = slice(j0, j0 + slab)
                gate = None
                for h in range(PEER_HEADS):
                    term = jnp.where(r2_ref[h, js, ts] < nb[h], e2_ref[h, js, ts],
                                     jnp.zeros((), BF16)) * db[h]
                    gate = term if gate is None else gate + term
                slabs.append(gate * act[js, ts])
            strips.append(jnp.concatenate(slabs, axis=0))
        groups.append(jnp.concatenate(strips, axis=1))
    prev = jnp.where(e == 0, 0.0, acc_ref[...])
    acc_ref[...] = prev + _dot(vt_ref[0], jnp.concatenate(groups, axis=0))

    @pl.when(e == pl.num_programs(1) - 1)
    def _():
        h = h1_ref[...] + acc_ref[...].T
        o_ref[...] = h * lax.rsqrt(jnp.mean(h * h, axis=-1, keepdims=True) + NORM_EPS) * fg_ref[...]


def _peer_experts(hnt, h1, r2, e2, n, d, u, vt, final_g, tt, eb):
    s = hnt.shape[1]
    D = D_MODEL
    H = PEER_HEADS
    sel = pl.BlockSpec((H, N_KEYS, tt), lambda i, e: (0, 0, i), pipeline_mode=pl.Buffered(1))
    return pl.pallas_call(
        _peer_expert_kernel,
        grid=(s // tt, N_EXPERTS // eb),
        in_specs=[pl.BlockSpec((D, tt), lambda i, e: (0, i)),
                  pl.BlockSpec((tt, D), lambda i, e: (i, 0), pipeline_mode=pl.Buffered(1)),
                  sel, sel, sel, sel,
                  pl.BlockSpec((eb, D), lambda i, e: (e, 0)),
                  pl.BlockSpec((1, D, eb), lambda i, e: (e, 0, 0)),
                  pl.BlockSpec((1, D), lambda i, e: (0, 0))],
        out_specs=pl.BlockSpec((tt, D), lambda i, e: (i, 0)),
        out_shape=jax.ShapeDtypeStruct((s, D), F32),
        scratch_shapes=[pltpu.VMEM((D, tt), F32)],
        compiler_params=pltpu.CompilerParams(dimension_semantics=("parallel", "arbitrary"),
                                             vmem_limit_bytes=VMEM_LIMIT_RESIDENT),
        name="peer_experts",
    )(hnt, h1, r2, e2, n, d, u, vt, final_g.reshape(1, D))


def _pad_rows(w, rows):
    return jnp.pad(w, ((0, rows - w.shape[0]), (0, 0)))


def _layer(h, norm1_g, w_in, shift_mu, rwkv_w0, w_decay_up, rwkv_a0, w_iclr_up, w_gate_up,
           k_k, k_a, r_k, lnx_g, lnx_b, lam_q1, lam_k1, lam_q2, lam_k2, subln_g, w_proj_a,
           w_proj_b, w_out, norm2_g, peer_wq, peer_sub_keys, peer_u, peer_v, out_g, lambda_init):
    s = h.shape[0]
    W = RWKV_WIDTH
    tmm = min(s, MM_ROW_BLOCK)
    tmb = min(s, MM_ROW_BLOCK_BF16)
    tn = MM_COL_BLOCK

    c0, c1, c2 = 3 * W, 3 * W + DECAY_LORA, 3 * W + DECAY_LORA + ICLR_LORA
    pad_cols = lambda m, n: jnp.pad(m, ((0, 0), (0, n - m.shape[1])))
    w_rwkv = jnp.concatenate([w_in[:, :c0], pad_cols(w_in[:, c0:c1], LORA_PAD),
                              pad_cols(w_in[:, c1:c2], LORA_PAD), w_in[:, c2:RWKV_COLS]], axis=1)
    mu2 = shift_mu.reshape(1, -1)
    mu = jnp.concatenate([mu2[:, :c0], pad_cols(mu2[:, c0:c1], LORA_PAD),
                          pad_cols(mu2[:, c1:c2], LORA_PAD), mu2[:, c2:]], axis=1)
    d0 = RWKV_COLS
    q_scale = DIFF_HEAD_DIM ** -0.5 * math.log2(math.e)
    w_qk = jnp.concatenate([w_in[:, d0:d0 + DIFF_WIDTH] * q_scale,
                            w_in[:, d0 + DIFF_WIDTH:d0 + 2 * DIFF_WIDTH]], axis=1)
    w_vt = w_in[:, d0 + 2 * DIFF_WIDTH:d0 + DIFF_COLS].T
    w_gate = w_in[:, d0 + DIFF_COLS:]

    p_rwkv, xn = _norm_matmul(h, norm1_g, w_rwkv.astype(BF16), NORM_EPS, F32, tmm,
                              RWKV_COL_BLOCK, "in_proj_rwkv")
    p_qk = _matmul(xn, w_qk.astype(BF16), BF16, tmb, tn, "in_proj_qk")
    p_vt = _matmul_nt(w_vt.astype(BF16), xn, BF16, VT_ROW_BLOCK, tmb, "in_proj_vt")
    p_gate = _matmul(xn, w_gate.astype(BF16), BF16, tmb, tn, "in_proj_gate")

    y_a = _rwkv_time_mix(p_rwkv, mu, rwkv_w0, _pad_rows(w_decay_up, LORA_PAD), rwkv_a0,
                         _pad_rows(w_iclr_up, LORA_PAD), w_gate_up, k_k, k_a, r_k, lnx_g, lnx_b)
    y_b = _diff_attention(p_qk, p_vt, lam_q1, lam_k1, lam_q2, lam_k2, subln_g, lambda_init)
    h1, hn_t = _merge(y_a, y_b, p_gate, h, w_proj_a.astype(BF16), w_proj_b.astype(BF16),
                      w_out.astype(BF16), norm2_g, min(s, MERGE_ROW_BLOCK))

    keys = peer_sub_keys.reshape(2 * PEER_HEADS, N_KEYS, PEER_HALF)
    q_t = _matmul(peer_wq.T.astype(BF16), hn_t, F32, MM_ROW_BLOCK_BF16, tn, "peer_query")
    r2, e2, n, d = _peer_scores(q_t, keys, min(s, PEER_SCORE_BLOCK))
    eb = PEER_EXPERT_BLOCK
    return _peer_experts(hn_t, h1, r2, e2, n, d, peer_u.astype(BF16),
                         _transpose_cast_blocks(peer_v, BF16, eb), out_g,
                         min(s, PEER_TOKEN_BLOCK), eb)


def kernel(x, norm1_g, w_in, shift_mu, rwkv_w0, w_decay_up, rwkv_a0, w_iclr_up, w_gate_up, k_k, k_a, r_k, lnx_g, lnx_b, lam_q1, lam_k1, lam_q2, lam_k2, subln_g, w_proj_a, w_proj_b, w_out, norm2_g, peer_wq, peer_sub_keys, peer_u, peer_v, final_g):
    B, S, D = x.shape
    assert B == 1 and D == D_MODEL and norm1_g.shape[0] == 1
    lambda_init = 0.8 - 0.6 * math.exp(-0.3 * 0)
    out = _layer(x[0], norm1_g[0], w_in[0], shift_mu[0], rwkv_w0[0], w_decay_up[0], rwkv_a0[0],
                 w_iclr_up[0], w_gate_up[0], k_k[0], k_a[0], r_k[0].reshape(-1), lnx_g[0],
                 lnx_b[0], lam_q1[0], lam_k1[0], lam_q2[0], lam_k2[0], subln_g[0], w_proj_a[0],
                 w_proj_b[0], w_out[0], norm2_g[0], peer_wq[0], peer_sub_keys[0], peer_u[0],
                 peer_v[0], final_g, lambda_init)
    return out[None]
```

```python
import functools
import math

import jax
import jax.numpy as jnp
from jax import lax
from jax.experimental import pallas as pl
from jax.experimental.pallas import tpu as pltpu

F32 = jnp.float32
BF16 = jnp.bfloat16
HIGHEST = lax.Precision.HIGHEST

LANES = 128
SUBLANES = 8
BF16_SUBLANES = 16

D_MODEL = 2048
RWKV_HEADS = 16
RWKV_HEAD_DIM = 64
RWKV_WIDTH = RWKV_HEADS * RWKV_HEAD_DIM
DECAY_LORA = 96
ICLR_LORA = 96
GATE_LORA = 256
LORA_PAD = 128
RWKV_COLS = 3 * RWKV_WIDTH + DECAY_LORA + ICLR_LORA + GATE_LORA
RWKV_COLS_PAD = 3 * RWKV_WIDTH + 2 * LORA_PAD + GATE_LORA
RWKV_CHUNK = 64
RWKV_BLOCK = 256
RWKV_GROUP = 4
RWKV_GROUP_LANES = RWKV_GROUP * RWKV_HEAD_DIM
DIFF_HEADS = 8
DIFF_HEAD_DIM = 64
DIFF_V_DIM = 2 * DIFF_HEAD_DIM
DIFF_WIDTH = DIFF_HEADS * DIFF_V_DIM
DIFF_COLS = 3 * DIFF_WIDTH
ATTN_K_BLOCK = 512
ATTN_Q_BLOCK = 1024
GATE_COLS = 2 * D_MODEL
PEER_HEADS = 8
PEER_HALF = 128
N_KEYS = 128
N_EXPERTS = N_KEYS * N_KEYS
PEER_TOPK = 16
NORM_EPS = 1e-6
LN_X_EPS = 64e-5
SUBLN_EPS = 1e-5
NEG_INF = -1e30

MM_ROW_BLOCK = 1024
MM_ROW_BLOCK_BF16 = 2048
MM_COL_BLOCK = 1024
RWKV_COL_BLOCK = RWKV_COLS_PAD // 4
VT_ROW_BLOCK = 1024
MERGE_ROW_BLOCK = 256
PEER_SCORE_BLOCK = 128
PEER_TOKEN_BLOCK = 512
PEER_EXPERT_BLOCK = 1024
MIB = 1024 * 1024
VMEM_LIMIT_RESIDENT = 61 * MIB
VMEM_LIMIT_ATTENTION = 40 * MIB


def _dot(a, b, precision=None):
    return jnp.dot(a, b, preferred_element_type=F32, precision=precision)


def _dot_nt(a, b, precision=None):
    return lax.dot_general(a, b, (((1,), (1,)), ((), ())), preferred_element_type=F32,
                           precision=precision)


def _const_spec(shape):
    nd = len(shape)
    return pl.BlockSpec(shape, lambda *_: (0,) * nd)


def _dot_bf16(a, b):
    return _dot(a.astype(BF16), b.astype(BF16))


def _mm_kernel(x_ref, w_ref, o_ref):
    o_ref[...] = _dot(x_ref[...], w_ref[...]).astype(o_ref.dtype)


def _matmul(x, w, out_dtype, tm, tn, name):
    s, k = x.shape
    n = w.shape[1]
    return pl.pallas_call(
        _mm_kernel,
        grid=(s // tm, n // tn),
        in_specs=[pl.BlockSpec((tm, k), lambda i, j: (i, 0)),
                  pl.BlockSpec((k, tn), lambda i, j: (0, j))],
        out_specs=pl.BlockSpec((tm, tn), lambda i, j: (i, j)),
        out_shape=jax.ShapeDtypeStruct((s, n), out_dtype),
        compiler_params=pltpu.CompilerParams(dimension_semantics=("parallel", "parallel")),
        name=name,
    )(x, w)


def _norm_mm_kernel(x_ref, g_ref, w_ref, o_ref, xn_ref, *, eps):
    @pl.when(pl.program_id(1) == 0)
    def _():
        x = x_ref[...]
        y = x * lax.rsqrt(jnp.mean(x * x, axis=-1, keepdims=True) + eps) * g_ref[...]
        xn_ref[...] = y.astype(xn_ref.dtype)

    o_ref[...] = _dot(xn_ref[...], w_ref[...]).astype(o_ref.dtype)


def _norm_matmul(x, g, w, eps, out_dtype, tm, tn, name):
    s, k = x.shape
    n = w.shape[1]
    return pl.pallas_call(
        functools.partial(_norm_mm_kernel, eps=eps),
        grid=(s // tm, n // tn),
        in_specs=[pl.BlockSpec((tm, k), lambda i, j: (i, 0)), _const_spec((1, k)),
                  pl.BlockSpec((k, tn), lambda i, j: (0, j))],
        out_specs=[pl.BlockSpec((tm, tn), lambda i, j: (i, j)),
                   pl.BlockSpec((tm, k), lambda i, j: (i, 0))],
        out_shape=[jax.ShapeDtypeStruct((s, n), out_dtype), jax.ShapeDtypeStruct((s, k), w.dtype)],
        compiler_params=pltpu.CompilerParams(dimension_semantics=("parallel", "arbitrary")),
        name=name,
    )(x, g.reshape(1, k), w)


def _mm_nt_kernel(w_ref, x_ref, o_ref):
    o_ref[...] = _dot_nt(w_ref[...], x_ref[...]).astype(o_ref.dtype)


def _matmul_nt(w, x, out_dtype, tn, tm, name):
    n, k = w.shape
    s = x.shape[0]
    return pl.pallas_call(
        _mm_nt_kernel,
        grid=(s // tm, n // tn),
        in_specs=[pl.BlockSpec((tn, k), lambda i, j: (j, 0)),
                  pl.BlockSpec((tm, k), lambda i, j: (i, 0))],
        out_specs=pl.BlockSpec((tn, tm), lambda i, j: (j, i)),
        out_shape=jax.ShapeDtypeStruct((n, s), out_dtype),
        compiler_params=pltpu.CompilerParams(dimension_semantics=("parallel", "parallel")),
        name=name,
    )(w, x)


def _transpose_cast_kernel(x_ref, o_ref):
    o_ref[0] = x_ref[...].T.astype(o_ref.dtype)


def _transpose_cast_blocks(x, out_dtype, tb):
    r, c = x.shape
    return pl.pallas_call(
        _transpose_cast_kernel,
        grid=(r // tb, c // tb),
        in_specs=[pl.BlockSpec((tb, tb), lambda i, j: (i, j))],
        out_specs=pl.BlockSpec((1, tb, tb), lambda i, j: (i, j, 0)),
        out_shape=jax.ShapeDtypeStruct((r // tb, c, tb), out_dtype),
        compiler_params=pltpu.CompilerParams(dimension_semantics=("parallel", "parallel")),
        name="transpose_cast",
    )(x)


def _head_sum(x, ones_bd):
    xb = x.astype(BF16)
    width = ones_bd.shape[0]
    tiles = [_dot(xb[:, c * width:(c + 1) * width], ones_bd)
             for c in range(x.shape[1] // width)]
    return jnp.concatenate(tiles, axis=1)


def _softplus(x):
    return jnp.maximum(x, 0.0) + jnp.log1p(jnp.exp(-jnp.abs(x)))


def _rwkv_kernel(p_ref, pprev_ref, mu_ref, w0_ref, wd_ref, a0_ref, wa_ref, wg_ref, kk_ref,
                 ka_ref, rk_ref, lng_ref, lnb_ref, o_ref, state_ref, y_ref):
    L = RWKV_CHUNK
    N = RWKV_HEAD_DIM
    W = RWKV_WIDTH
    G = RWKV_GROUP
    GL = RWKV_GROUP_LANES
    step = pl.program_id(0)

    @pl.when(step == 0)
    def _():
        state_ref[...] = jnp.zeros_like(state_ref)

    TB = p_ref.shape[0]
    row = lax.broadcasted_iota(jnp.int32, (TB, 1), 0)
    carry_on = jnp.where(step == 0, 0.0, 1.0)

    def shifted(c0, c1):
        p = p_ref[:, c0:c1]
        last = pprev_ref[SUBLANES - 1:SUBLANES, c0:c1] * carry_on
        prev = jnp.where(row == 0, last, pltpu.roll(p, 1, axis=0))
        return p + (prev - p) * mu_ref[:, c0:c1]

    r = shifted(0, W)
    k = shifted(W, 2 * W)
    v = shifted(2 * W, 3 * W)
    xw = shifted(3 * W, 3 * W + LORA_PAD)
    xa = shifted(3 * W + LORA_PAD, 3 * W + 2 * LORA_PAD)
    xg = shifted(3 * W + 2 * LORA_PAD, 3 * W + 2 * LORA_PAD + GATE_LORA)

    z = w0_ref[...] + _dot_bf16(jnp.tanh(xw), wd_ref[...])
    w_log = -_softplus(-z) - 0.5
    lw = -jnp.exp(w_log)
    a = jax.nn.sigmoid(a0_ref[...] + _dot_bf16(xa, wa_ref[...]))
    g = _dot_bf16(jax.nn.sigmoid(xg), wg_ref[...])

    bi = lax.broadcasted_iota(jnp.int32, (GL, GL), 0) // N
    bj = lax.broadcasted_iota(jnp.int32, (GL, GL), 1) // N
    bd_mask = bi == bj
    ones_bd = jnp.where(bd_mask, 1.0, 0.0).astype(BF16)

    kk = k * kk_ref[...]
    kk = kk / jnp.maximum(jnp.sqrt(_head_sum(kk * kk, ones_bd)), 1e-12)
    k = k * (1.0 + (a - 1.0) * ka_ref[...])

    ti = lax.broadcasted_iota(jnp.int32, (TB, TB), 0)
    tj = lax.broadcasted_iota(jnp.int32, (TB, TB), 1)
    tril = jnp.where((tj <= ti) & (tj // L == ti // L), 1.0, 0.0).astype(BF16)
    gi = lax.broadcasted_iota(jnp.int32, (L, GL), 0)
    gj = lax.broadcasted_iota(jnp.int32, (L, GL), 1) % L
    incl = gj <= gi
    strict = gj < gi
    eye4 = jnp.where(gj == gi, 1.0, 0.0).astype(F32)
    def bd(x):
        return jnp.where(bd_mask, jnp.concatenate([x] * G, axis=0), jnp.zeros((), x.dtype))

    lw_hi = lw.astype(BF16)
    lw_rest = lw - lw_hi.astype(F32)
    lw_mid = lw_rest.astype(BF16)
    lw_lo = (lw_rest - lw_mid.astype(F32)).astype(BF16)
    cum = _dot(tril, lw_hi) + (_dot(tril, lw_mid) + _dot(tril, lw_lo))
    e_inv = jnp.exp(-cum)
    r_t = (r * jnp.exp(cum)).astype(BF16)
    a_t = (-kk * jnp.exp(cum - lw)).astype(BF16)
    b = kk * a
    b_t = (b * e_inv).astype(BF16)
    k_t = (k * e_inv).astype(BF16)
    v_b = v.astype(BF16)
    eye4_b = eye4.astype(BF16)
    n_grp = RWKV_HEADS // G
    states = [state_ref[grp] for grp in range(n_grp)]

    n_chunks = TB // L
    pairs = [(c, grp) for c in range(n_chunks) for grp in range(n_grp)]
    rows_of = lambda c: slice(c * L, (c + 1) * L)
    lanes_of = lambda grp: slice(grp * GL, (grp + 1) * GL)
    cum_last = [cum[(c + 1) * L - 1:(c + 1) * L, :] for c in range(n_chunks)]

    a_ab, a_ak, a_rb, a_rk = {}, {}, {}, {}
    for c, grp in pairs:
        rs, gs = rows_of(c), lanes_of(grp)
        m = _dot_nt(jnp.concatenate([a_t[rs, gs], r_t[rs, gs]], axis=0),
                    jnp.concatenate([bd(b_t[rs, gs]), bd(k_t[rs, gs])], axis=0))
        a_ab[c, grp] = jnp.where(strict, m[:L, :GL], 0.0)
        a_ak[c, grp] = jnp.where(strict, m[:L, GL:], 0.0).astype(BF16)
        a_rb[c, grp] = jnp.where(incl, m[L:, :GL], 0.0).astype(BF16)
        a_rk[c, grp] = jnp.where(incl, m[L:, GL:], 0.0).astype(BF16)

    inv = {p: eye4 + a_ab[p] for p in pairs}
    pw = {}
    for p in pairs:
        pw_b = a_ab[p].astype(BF16)
        pw[p] = _dot(pw_b, bd(pw_b))
    for _ in range(4):
        for p in pairs:
            pw_b = pw[p].astype(BF16)
            both = _dot(jnp.concatenate([pw_b, inv[p].astype(BF16)], axis=0), bd(pw_b))
            pw[p], inv[p] = both[:L], inv[p] + both[L:]
    for p in pairs:
        inv[p] = (inv[p] + _dot(inv[p].astype(BF16), bd(pw[p].astype(BF16)))).astype(BF16)

    bk = {}
    for c in range(n_chunks):
        rs = rows_of(c)
        e_tail = jnp.exp(cum_last[c] - cum[rs])
        b_w = (b[rs] * e_tail).astype(BF16)
        k_w = (k[rs] * e_tail).astype(BF16)
        for grp in range(n_grp):
            gs = lanes_of(grp)
            bk[c, grp] = _dot_nt(
                eye4_b, jnp.concatenate([bd(b_w[:, gs]), bd(k_w[:, gs])], axis=0)).astype(BF16)

    for c in range(n_chunks):
        rs = rows_of(c)
        w_last = jnp.exp(cum_last[c])
        bd_z = [bd(states[grp].astype(BF16)) for grp in range(n_grp)]
        bd_v = [bd(v_b[rs, lanes_of(grp)]) for grp in range(n_grp)]
        x = [_dot(jnp.concatenate([a_t[rs, lanes_of(grp)], a_ak[c, grp]], axis=1),
                  jnp.concatenate([bd_z[grp], bd_v[grp]], axis=0)) for grp in range(n_grp)]
        u = [_dot(inv[c, grp], bd(x[grp].astype(BF16))) for grp in range(n_grp)]
        for grp in range(n_grp):
            gs = lanes_of(grp)
            lhs = jnp.concatenate(
                [jnp.concatenate([r_t[rs, gs], a_rb[c, grp], a_rk[c, grp]], axis=1),
                 jnp.concatenate([(eye4 * w_last[:, gs]).astype(BF16), bk[c, grp]], axis=1)],
                axis=0)
            yz = _dot(lhs, jnp.concatenate([bd_z[grp], bd(u[grp].astype(BF16)), bd_v[grp]],
                                           axis=0))
            y_ref[rs, gs] = yz[:L]
            states[grp] = yz[L:]
    for grp in range(n_grp):
        state_ref[grp] = states[grp]

    y = y_ref[...]
    mean = _head_sum(y, ones_bd) * (1.0 / N)
    yc = y - mean
    var = _head_sum(yc * yc, ones_bd) * (1.0 / N)
    yn = yc * lax.rsqrt(var + LN_X_EPS) * lng_ref[...] + lnb_ref[...]
    bonus = _head_sum(r * k * rk_ref[...], ones_bd) * v
    o_ref[...] = ((yn + bonus) * g).astype(o_ref.dtype)


def _rwkv_time_mix(p, mu, w0, wd, a0, wa, wg, k_k, k_a, r_k, lnx_g, lnx_b):
    s = p.shape[0]
    L = RWKV_CHUNK
    W = RWKV_WIDTH
    row = lambda v: v.reshape(1, -1)
    consts = [row(mu), row(w0), wd, row(a0), wa, wg, row(k_k), row(k_a), row(r_k), row(lnx_g),
              row(lnx_b)]
    tb = RWKV_BLOCK
    return pl.pallas_call(
        _rwkv_kernel,
        grid=(s // tb,),
        in_specs=[pl.BlockSpec((tb, RWKV_COLS_PAD), lambda i: (i, 0)),
                  pl.BlockSpec((SUBLANES, RWKV_COLS_PAD),
                               lambda i: (jnp.maximum(i * (tb // SUBLANES) - 1, 0), 0))]
                 + [_const_spec(c.shape) for c in consts],
        out_specs=pl.BlockSpec((tb, W), lambda i: (i, 0)),
        out_shape=jax.ShapeDtypeStruct((s, W), BF16),
        scratch_shapes=[pltpu.VMEM((RWKV_HEADS // RWKV_GROUP, RWKV_HEAD_DIM, RWKV_GROUP_LANES),
                                   F32),
                        pltpu.VMEM((tb, W), F32)],
        compiler_params=pltpu.CompilerParams(dimension_semantics=("arbitrary",)),
        name="rwkv7",
    )(p, p, *consts)


def _diffattn_kernel(q_ref, k_ref, vt_ref, lq1_ref, lk1_ref, lq2_ref, lk2_ref, g_ref, o_ref,
                     sa_ref, sb_ref, mx_ref, m_ref, acc_ref, *, lambda_init):
    TQ = ATTN_Q_BLOCK
    TK = ATTN_K_BLOCK
    DV = DIFF_V_DIM
    qi = pl.program_id(1)
    q = q_ref[...]
    lane = lax.broadcasted_iota(jnp.int32, q.shape, 1)
    zero = jnp.zeros_like(q)
    q_halves = (jnp.where(lane < DIFF_HEAD_DIM, q, zero), jnp.where(lane >= DIFF_HEAD_DIM, q, zero))
    ones = jnp.ones((BF16_SUBLANES, TK), BF16)
    key_i = lax.broadcasted_iota(jnp.int32, (TK, TQ), 0)
    qry_i = lax.broadcasted_iota(jnp.int32, (TK, TQ), 1)

    s_bufs = (sa_ref, sb_ref)

    def scores(j, buf, key_offset=None):
        kb = k_ref[pl.ds(pl.multiple_of(j * TK, TK), TK), :]
        for idx in range(2):
            s = _dot_nt(kb, q_halves[idx])
            if key_offset is not None:
                s = jnp.where(key_i + key_offset <= qry_i, s, NEG_INF)
            s_bufs[buf][idx] = s
            mx_ref[2 * buf + idx] = jnp.max(s, axis=0, keepdims=True)

    def probs(buf):
        out = []
        for idx in range(2):
            m_old = m_ref[idx]
            m_new = jnp.maximum(m_old, mx_ref[2 * buf + idx])
            m_ref[idx] = m_new
            out.append((jnp.exp2(s_bufs[buf][idx] - m_new).astype(BF16),
                        jnp.exp2(m_old - m_new)))
        return out

    def accumulate(j, pa):
        vt = vt_ref[:, pl.ds(pl.multiple_of(j * TK, TK), TK)]
        vext = jnp.concatenate([vt, ones], axis=0)
        for idx, (p, alpha) in enumerate(pa):
            acc_ref[idx] = alpha * acc_ref[idx] + _dot(vext, p)

    def step(j, buf, next_scores):
        pa = probs(buf)
        next_scores()
        accumulate(j, pa)

    m_ref[...] = jnp.full_like(m_ref, NEG_INF)
    acc_ref[...] = jnp.zeros_like(acc_ref)

    @pl.when(qi > 0)
    def _():
        scores(0, 0)

    @pl.loop(0, qi - 1)
    def _(i):
        step(2 * i, 0, lambda: scores(2 * i + 1, 1))
        step(2 * i + 1, 1, lambda: scores(2 * i + 2, 0))

    @pl.when(qi > 0)
    def _():
        step(2 * qi - 2, 0, lambda: scores(2 * qi - 1, 1))
        step(2 * qi - 1, 1, lambda: scores(2 * qi, 0, key_offset=0))

    @pl.when(qi == 0)
    def _():
        scores(0, 0, key_offset=0)

    step(2 * qi, 0, lambda: scores(2 * qi + 1, 1, key_offset=TK))
    step(2 * qi + 1, 1, lambda: None)

    lam = (jnp.exp(jnp.sum(lq1_ref[...] * lk1_ref[...], axis=-1, keepdims=True))
           - jnp.exp(jnp.sum(lq2_ref[...] * lk2_ref[...], axis=-1, keepdims=True))
           + lambda_init)
    o = (acc_ref[0, :DV, :] / acc_ref[0, DV:DV + 1, :]
         - lam * (acc_ref[1, :DV, :] / acc_ref[1, DV:DV + 1, :]))
    o = o * lax.rsqrt(jnp.mean(o * o, axis=0, keepdims=True) + SUBLN_EPS) * g_ref[...]
    o_ref[...] = (o * (1.0 - lambda_init)).T.astype(o_ref.dtype)


def _diff_attention(qk, vt, lq1, lk1, lq2, lk2, subln_g, lambda_init):
    s = qk.shape[0]
    H = DIFF_HEADS
    TQ = ATTN_Q_BLOCK
    TK = ATTN_K_BLOCK
    DV = DIFF_V_DIM
    assert TQ == 2 * TK and s % TQ == 0
    row = lambda v: v.reshape(1, -1)
    lam_specs = [_const_spec((1, DIFF_HEAD_DIM))] * 4
    return pl.pallas_call(
        functools.partial(_diffattn_kernel, lambda_init=lambda_init),
        grid=(H, s // TQ),
        in_specs=[pl.BlockSpec((TQ, DV), lambda h, qi: (qi, h)),
                  pl.BlockSpec((s, DV), lambda h, qi: (0, H + h)),
                  pl.BlockSpec((DV, s), lambda h, qi: (h, 0))]
                 + lam_specs + [_const_spec((DV, 1))],
        out_specs=pl.BlockSpec((TQ, DV), lambda h, qi: (qi, h)),
        out_shape=jax.ShapeDtypeStruct((s, DIFF_WIDTH), BF16),
        scratch_shapes=[pltpu.VMEM((2, TK, TQ), F32), pltpu.VMEM((2, TK, TQ), F32),
                        pltpu.VMEM((4, 1, TQ), F32), pltpu.VMEM((2, 1, TQ), F32),
                        pltpu.VMEM((2, DV + BF16_SUBLANES, TQ), F32)],
        compiler_params=pltpu.CompilerParams(dimension_semantics=("parallel", "arbitrary"),
                                             vmem_limit_bytes=VMEM_LIMIT_ATTENTION),
        name="diff_attention",
    )(qk, qk, vt, row(lq1), row(lk1), row(lq2), row(lk2), subln_g.reshape(DV, 1))


def _merge_kernel(ya_ref, yb_ref, gate_ref, x_ref, wa_ref, wb_ref, wo_ref, g2_ref,
                  h_ref, ht_ref):
    pa = _dot(ya_ref[...], wa_ref[...])
    pb = _dot(yb_ref[...], wb_ref[...])
    ga = jax.nn.sigmoid(gate_ref[:, :D_MODEL].astype(F32))
    gb = jax.nn.sigmoid(gate_ref[:, D_MODEL:].astype(F32))
    merged = ga * pa + gb * pb
    h = x_ref[...] + _dot(merged.astype(BF16), wo_ref[...])
    h_ref[...] = h
    hn = h * lax.rsqrt(jnp.mean(h * h, axis=-1, keepdims=True) + NORM_EPS) * g2_ref[...]
    ht_ref[...] = hn.T.astype(BF16)


def _merge(ya, yb, gate, x, wa, wb, wo, g2, tm):
    s = x.shape[0]
    D = D_MODEL
    rows = lambda w: pl.BlockSpec((tm, w), lambda i: (i, 0))
    single = lambda shape: pl.BlockSpec(shape, lambda i: (0, 0), pipeline_mode=pl.Buffered(1))
    return pl.pallas_call(
        _merge_kernel,
        grid=(s // tm,),
        in_specs=[rows(RWKV_WIDTH), rows(DIFF_WIDTH), rows(GATE_COLS), rows(D),
                  single((RWKV_WIDTH, D)), single((DIFF_WIDTH, D)), single((D, D)),
                  _const_spec((1, D))],
        out_specs=[rows(D), pl.BlockSpec((D, tm), lambda i: (0, i))],
        out_shape=[jax.ShapeDtypeStruct((s, D), F32), jax.ShapeDtypeStruct((D, s), BF16)],
        compiler_params=pltpu.CompilerParams(dimension_semantics=("parallel",),
                                             vmem_limit_bytes=VMEM_LIMIT_RESIDENT),
        name="merge_out_proj",
    )(ya, yb, gate, x, wa, wb, wo, g2.reshape(1, D))


def _cmp_exchange(xs, i, l, descending):
    hi = jnp.maximum(xs[i], xs[l])
    lo = jnp.minimum(xs[i], xs[l])
    xs[i], xs[l] = (hi, lo) if descending else (lo, hi)


def _bitonic_merge_desc(xs):
    xs = list(xs)
    n = len(xs)
    j = n // 2
    while j >= 1:
        for i in range(n):
            l = i ^ j
            if l > i:
                _cmp_exchange(xs, i, l, True)
        j //= 2
    return xs


def _bitonic_sort_desc(xs):
    xs = list(xs)
    n = len(xs)
    k = 2
    while k <= n:
        j = k // 2
        while j >= 1:
            for i in range(n):
                l = i ^ j
                if l > i:
                    _cmp_exchange(xs, i, l, (i & k) == 0)
            j //= 2
        k *= 2
    return xs


def _merge_top(a, b):
    n = len(a)
    return _bitonic_merge_desc([jnp.maximum(a[i], b[n - 1 - i]) for i in range(n)])


def _top16_over_rows(s):
    groups = [s[g * SUBLANES:(g + 1) * SUBLANES, :] for g in range(s.shape[0] // SUBLANES)]
    top = _bitonic_sort_desc(groups)
    for shift in (4, 2, 1):
        top = _merge_top(top, [pltpu.roll(x, shift, axis=0) for x in top])
    return top


def _prefix_count(rows, pred):
    def pick(conds, cands):
        if not conds:
            return cands[0]
        half = len(cands) // 2
        return jnp.where(conds[0], pick(conds[1:], cands[half:]), pick(conds[1:], cands[:half]))

    n = len(rows)
    conds = []
    count = None
    step = n // 2
    while step >= 1:
        cands = [rows[lo + step - 1] for lo in range(0, n, 2 * step)]
        c = pred(pick(conds, cands))
        inc = jnp.where(c, float(step), 0.0)
        count = inc if count is None else count + inc
        conds.append(c)
        step //= 2
    return jnp.where(pred(rows[n - 1]), float(n), count)


def _peer_score_kernel(qt_ref, keys_ref, r2_ref, e2_ref, n_ref, d_ref):
    K = PEER_TOPK
    T = qt_ref.shape[1]
    H = PEER_HEADS
    scores = []
    tops = []
    for hp in range(2 * H):
        s = _dot(keys_ref[hp], qt_ref[hp * PEER_HALF:(hp + 1) * PEER_HALF, :], HIGHEST)
        scores.append(s)
        tops.append(_top16_over_rows(s))
    sub = lax.broadcasted_iota(jnp.int32, (SUBLANES, T), 0)

    def by_head(p, i):
        out = tops[p][i]
        for h in range(1, H):
            out = jnp.where(sub == h, tops[2 * h + p][i], out)
        return out

    aa = [by_head(0, i) for i in range(K)]
    bb = [by_head(1, i) for i in range(K)]
    cands = [aa[i] + bb[j] for i in range(K) for j in range(K) if (i + 1) * (j + 1) <= K]
    cands += [jnp.full_like(cands[0], -jnp.inf)] * (-len(cands) % K)
    best = _bitonic_sort_desc(cands[:K])
    for c in range(K, len(cands), K):
        best = _merge_top(best, _bitonic_sort_desc(cands[c:c + K]))
    thr = best[K - 1]
    zsum = jnp.zeros_like(thr)
    for t in best:
        zsum = zsum + jnp.exp(t - best[0])
    inv_z = 1.0 / zsum
    for h in range(H):
        hs = slice(h, h + 1)
        s1, s2 = scores[2 * h], scores[2 * h + 1]
        thr_h = thr[hs]
        b_rows = [bb[m][hs] for m in range(K)]
        cnt = _prefix_count(b_rows, lambda b: s1 + b >= thr_h)
        rank = _prefix_count(b_rows, lambda b: b > s2)
        n_ref[h] = cnt
        r2_ref[h] = rank.astype(BF16)
        d_ref[h] = jnp.exp(s1 - aa[0][hs]) * inv_z[hs]
        e2_ref[h] = jnp.exp(s2 - bb[0][hs]).astype(BF16)


def _peer_scores(qt, keys, tt):
    nq, s = qt.shape
    H = PEER_HEADS
    out = lambda dtype: jax.ShapeDtypeStruct((H, N_KEYS, s), dtype)
    ospec = pl.BlockSpec((H, N_KEYS, tt), lambda i: (0, 0, i))
    return pl.pallas_call(
        _peer_score_kernel,
        grid=(s // tt,),
        in_specs=[pl.BlockSpec((nq, tt), lambda i: (0, i)),
                  _const_spec((2 * H, N_KEYS, PEER_HALF))],
        out_specs=[ospec] * 4,
        out_shape=[out(BF16), out(BF16), out(F32), out(F32)],
        compiler_params=pltpu.CompilerParams(dimension_semantics=("parallel",)),
        name="peer_scores",
    )(qt, keys)


def _peer_expert_kernel(hnt_ref, h1_ref, r2_ref, e2_ref, n_ref, d_ref, u_ref, vt_ref, fg_ref,
                        o_ref, acc_ref):
    e = pl.program_id(1)
    eb = u_ref.shape[0]
    tt = hnt_ref.shape[1]
    ni = eb // N_KEYS
    slab = BF16_SUBLANES
    strip = 2 * LANES

    groups = []
    for ii in range(ni):
        i = e * ni + ii
        rows = slice(ii * N_KEYS, (ii + 1) * N_KEYS)
        pre = _dot(u_ref[rows, :], hnt_ref[...])
        act = (0.5 * pre * (1.0 + lax.erf(pre * math.sqrt(0.5)))).astype(BF16)
        strips = []
        for t0 in range(0, tt, strip):
            ts = slice(t0, t0 + strip)
            nb = [jnp.broadcast_to(n_ref[h, pl.ds(i, 1), ts], (slab, strip)).astype(BF16)
                  for h in range(PEER_HEADS)]
            db = [jnp.broadcast_to(d_ref[h, pl.ds(i, 1), ts], (slab, strip)).astype(BF16)
                  for h in range(PEER_HEADS)]
            slabs = []
            for j0 in range(0, N_KEYS, slab):
                js = slice(j0, j0 + slab)
                gate = None
                for h in range(PEER_HEADS):
                    term = jnp.where(r2_ref[h, js, ts] < nb[h], e2_ref[h, js, ts],
                                     jnp.zeros((), BF16)) * db[h]
                    gate = term if gate is None else gate + term
                slabs.append(gate * act[js, ts])
            strips.append(jnp.concatenate(slabs, axis=0))
        groups.append(jnp.concatenate(strips, axis=1))
    prev = jnp.where(e == 0, 0.0, acc_ref[...])
    acc_ref[...] = prev + _dot(vt_ref[0], jnp.concatenate(groups, axis=0))

    @pl.when(e == pl.num_programs(1) - 1)
    def _():
        h = h1_ref[...] + acc_ref[...].T
        o_ref[...] = h * lax.rsqrt(jnp.mean(h * h, axis=-1, keepdims=True) + NORM_EPS) * fg_ref[...]


def _peer_experts(hnt, h1, r2, e2, n, d, u, vt, final_g, tt, eb):
    s = hnt.shape[1]
    D = D_MODEL
    H = PEER_HEADS
    sel = pl.BlockSpec((H, N_KEYS, tt), lambda i, e: (0, 0, i), pipeline_mode=pl.Buffered(1))
    return pl.pallas_call(
        _peer_expert_kernel,
        grid=(s // tt, N_EXPERTS // eb),
        in_specs=[pl.BlockSpec((D, tt), lambda i, e: (0, i)),
                  pl.BlockSpec((tt, D), lambda i, e: (i, 0)),
                  sel, sel, sel, sel,
                  pl.BlockSpec((eb, D), lambda i, e: (e, 0)),
                  pl.BlockSpec((1, D, eb), lambda i, e: (e, 0, 0)),
                  pl.BlockSpec((1, D), lambda i, e: (0, 0))],
        out_specs=pl.BlockSpec((tt, D), lambda i, e: (i, 0)),
        out_shape=jax.ShapeDtypeStruct((s, D), F32),
        scratch_shapes=[pltpu.VMEM((D, tt), F32)],
        compiler_params=pltpu.CompilerParams(dimension_semantics=("parallel", "arbitrary"),
                                             vmem_limit_bytes=VMEM_LIMIT_RESIDENT),
        name="peer_experts",
    )(hnt, h1, r2, e2, n, d, u, vt, final_g.reshape(1, D))


def _pad_rows(w, rows):
    return jnp.pad(w, ((0, rows - w.shape[0]), (0, 0)))


def _layer(h, norm1_g, w_in, shift_mu, rwkv_w0, w_decay_up, rwkv_a0, w_iclr_up, w_gate_up,
           k_k, k_a, r_k, lnx_g, lnx_b, lam_q1, lam_k1, lam_q2, lam_k2, subln_g, w_proj_a,
           w_proj_b, w_out, norm2_g, peer_wq, peer_sub_keys, peer_u, peer_v, out_g, lambda_init):
    s = h.shape[0]
    W = RWKV_WIDTH
    tmm = min(s, MM_ROW_BLOCK)
    tmb = min(s, MM_ROW_BLOCK_BF16)
    tn = MM_COL_BLOCK

    c0, c1, c2 = 3 * W, 3 * W + DECAY_LORA, 3 * W + DECAY_LORA + ICLR_LORA
    pad_cols = lambda m, n: jnp.pad(m, ((0, 0), (0, n - m.shape[1])))
    w_rwkv = jnp.concatenate([w_in[:, :c0], pad_cols(w_in[:, c0:c1], LORA_PAD),
                              pad_cols(w_in[:, c1:c2], LORA_PAD), w_in[:, c2:RWKV_COLS]], axis=1)
    mu2 = shift_mu.reshape(1, -1)
    mu = jnp.concatenate([mu2[:, :c0], pad_cols(mu2[:, c0:c1], LORA_PAD),
                          pad_cols(mu2[:, c1:c2], LORA_PAD), mu2[:, c2:]], axis=1)
    d0 = RWKV_COLS
    q_scale = DIFF_HEAD_DIM ** -0.5 * math.log2(math.e)
    w_qk = jnp.concatenate([w_in[:, d0:d0 + DIFF_WIDTH] * q_scale,
                            w_in[:, d0 + DIFF_WIDTH:d0 + 2 * DIFF_WIDTH]], axis=1)
    w_vt = w_in[:, d0 + 2 * DIFF_WIDTH:d0 + DIFF_COLS].T
    w_gate = w_in[:, d0 + DIFF_COLS:]

    p_rwkv, xn = _norm_matmul(h, norm1_g, w_rwkv.astype(BF16), NORM_EPS, F32, tmm,
                              RWKV_COL_BLOCK, "in_proj_rwkv")
    p_qk = _matmul(xn, w_qk.astype(BF16), BF16, tmb, tn, "in_proj_qk")
    p_vt = _matmul_nt(w_vt.astype(BF16), xn, BF16, VT_ROW_BLOCK, tmb, "in_proj_vt")
    p_gate = _matmul(xn, w_gate.astype(BF16), BF16, tmb, tn, "in_proj_gate")

    y_a = _rwkv_time_mix(p_rwkv, mu, rwkv_w0, _pad_rows(w_decay_up, LORA_PAD), rwkv_a0,
                         _pad_rows(w_iclr_up, LORA_PAD), w_gate_up, k_k, k_a, r_k, lnx_g, lnx_b)
    y_b = _diff_attention(p_qk, p_vt, lam_q1, lam_k1, lam_q2, lam_k2, subln_g, lambda_init)
    h1, hn_t = _merge(y_a, y_b, p_gate, h, w_proj_a.astype(BF16), w_proj_b.astype(BF16),
                      w_out.astype(BF16), norm2_g, min(s, MERGE_ROW_BLOCK))

    keys = peer_sub_keys.reshape(2 * PEER_HEADS, N_KEYS, PEER_HALF)
    q_t = _matmul(peer_wq.T.astype(BF16), hn_t, F32, MM_ROW_BLOCK_BF16, tn, "peer_query")
    r2, e2, n, d = _peer_scores(q_t, keys, min(s, PEER_SCORE_BLOCK))
    eb = PEER_EXPERT_BLOCK
    return _peer_experts(hn_t, h1, r2, e2, n, d, peer_u.astype(BF16),
                         _transpose_cast_blocks(peer_v, BF16, eb), out_g,
                         min(s, PEER_TOKEN_BLOCK), eb)


def kernel(x, norm1_g, w_in, shift_mu, rwkv_w0, w_decay_up, rwkv_a0, w_iclr_up, w_gate_up, k_k, k_a, r_k, lnx_g, lnx_b, lam_q1, lam_k1, lam_q2, lam_k2, subln_g, w_proj_a, w_proj_b, w_out, norm2_g, peer_wq, peer_sub_keys, peer_u, peer_v, final_g):
    B, S, D = x.shape
    assert B == 1 and D == D_MODEL and norm1_g.shape[0] == 1
    lambda_init = 0.8 - 0.6 * math.exp(-0.3 * 0)
    out = _layer(x[0], norm1_g[0], w_in[0], shift_mu[0], rwkv_w0[0], w_decay_up[0], rwkv_a0[0],
                 w_iclr_up[0], w_gate_up[0], k_k[0], k_a[0], r_k[0].reshape(-1), lnx_g[0],
                 lnx_b[0], lam_q1[0], lam_k1[0], lam_q2[0], lam_k2[0], subln_g[0], w_proj_a[0],
                 w_proj_b[0], w_out[0], norm2_g[0], peer_wq[0], peer_sub_keys[0], peer_u[0],
                 peer_v[0], final_g, lambda_init)
    return out[None]
```

```python
import functools
import math

import jax
import jax.numpy as jnp
from jax import lax
from jax.experimental import pallas as pl
from jax.experimental.pallas import tpu as pltpu

F32 = jnp.float32
BF16 = jnp.bfloat16
HIGHEST = lax.Precision.HIGHEST

LANES = 128
SUBLANES = 8
BF16_SUBLANES = 16

D_MODEL = 2048
RWKV_HEADS = 16
RWKV_HEAD_DIM = 64
RWKV_WIDTH = RWKV_HEADS * RWKV_HEAD_DIM
DECAY_LORA = 96
ICLR_LORA = 96
GATE_LORA = 256
LORA_PAD = 128
RWKV_COLS = 3 * RWKV_WIDTH + DECAY_LORA + ICLR_LORA + GATE_LORA
RWKV_COLS_PAD = 3 * RWKV_WIDTH + 2 * LORA_PAD + GATE_LORA
RWKV_CHUNK = 64
RWKV_BLOCK = 256
RWKV_GROUP = 4
RWKV_GROUP_LANES = RWKV_GROUP * RWKV_HEAD_DIM
DIFF_HEADS = 8
DIFF_HEAD_DIM = 64
DIFF_V_DIM = 2 * DIFF_HEAD_DIM
DIFF_WIDTH = DIFF_HEADS * DIFF_V_DIM
DIFF_COLS = 3 * DIFF_WIDTH
ATTN_K_BLOCK = 512
ATTN_Q_BLOCK = 1024
GATE_COLS = 2 * D_MODEL
PEER_HEADS = 8
PEER_HALF = 128
N_KEYS = 128
N_EXPERTS = N_KEYS * N_KEYS
PEER_TOPK = 16
NORM_EPS = 1e-6
LN_X_EPS = 64e-5
SUBLN_EPS = 1e-5
NEG_INF = -1e30

MM_ROW_BLOCK = 1024
MM_ROW_BLOCK_BF16 = 2048
MM_COL_BLOCK = 1024
RWKV_COL_BLOCK = RWKV_COLS_PAD // 4
VT_ROW_BLOCK = 1024
MERGE_ROW_BLOCK = 256
PEER_SCORE_BLOCK = 128
PEER_TOKEN_BLOCK = 512
PEER_EXPERT_BLOCK = 1024
MIB = 1024 * 1024
VMEM_LIMIT_RESIDENT = 61 * MIB
VMEM_LIMIT_ATTENTION = 40 * MIB


def _dot(a, b, precision=None):
    return jnp.dot(a, b, preferred_element_type=F32, precision=precision)


def _dot_nt(a, b, precision=None):
    return lax.dot_general(a, b, (((1,), (1,)), ((), ())), preferred_element_type=F32,
                           precision=precision)


def _const_spec(shape):
    nd = len(shape)
    return pl.BlockSpec(shape, lambda *_: (0,) * nd)


def _dot_bf16(a, b):
    return _dot(a.astype(BF16), b.astype(BF16))


def _mm_kernel(x_ref, w_ref, o_ref):
    o_ref[...] = _dot(x_ref[...], w_ref[...]).astype(o_ref.dtype)


def _matmul(x, w, out_dtype, tm, tn, name):
    s, k = x.shape
    n = w.shape[1]
    return pl.pallas_call(
        _mm_kernel,
        grid=(s // tm, n // tn),
        in_specs=[pl.BlockSpec((tm, k), lambda i, j: (i, 0)),
                  pl.BlockSpec((k, tn), lambda i, j: (0, j))],
        out_specs=pl.BlockSpec((tm, tn), lambda i, j: (i, j)),
        out_shape=jax.ShapeDtypeStruct((s, n), out_dtype),
        compiler_params=pltpu.CompilerParams(dimension_semantics=("parallel", "parallel")),
        name=name,
    )(x, w)


def _norm_mm_kernel(x_ref, g_ref, w_ref, o_ref, xn_ref, *, eps):
    @pl.when(pl.program_id(1) == 0)
    def _():
        x = x_ref[...]
        y = x * lax.rsqrt(jnp.mean(x * x, axis=-1, keepdims=True) + eps) * g_ref[...]
        xn_ref[...] = y.astype(xn_ref.dtype)

    o_ref[...] = _dot(xn_ref[...], w_ref[...]).astype(o_ref.dtype)


def _norm_matmul(x, g, w, eps, out_dtype, tm, tn, name):
    s, k = x.shape
    n = w.shape[1]
    return pl.pallas_call(
        functools.partial(_norm_mm_kernel, eps=eps),
        grid=(s // tm, n // tn),
        in_specs=[pl.BlockSpec((tm, k), lambda i, j: (i, 0)), _const_spec((1, k)),
                  pl.BlockSpec((k, tn), lambda i, j: (0, j))],
        out_specs=[pl.BlockSpec((tm, tn), lambda i, j: (i, j)),
                   pl.BlockSpec((tm, k), lambda i, j: (i, 0))],
        out_shape=[jax.ShapeDtypeStruct((s, n), out_dtype), jax.ShapeDtypeStruct((s, k), w.dtype)],
        compiler_params=pltpu.CompilerParams(dimension_semantics=("parallel", "arbitrary")),
        name=name,
    )(x, g.reshape(1, k), w)


def _mm_nt_kernel(w_ref, x_ref, o_ref):
    o_ref[...] = _dot_nt(w_ref[...], x_ref[...]).astype(o_ref.dtype)


def _matmul_nt(w, x, out_dtype, tn, tm, name):
    n, k = w.shape
    s = x.shape[0]
    return pl.pallas_call(
        _mm_nt_kernel,
        grid=(s // tm, n // tn),
        in_specs=[pl.BlockSpec((tn, k), lambda i, j: (j, 0)),
                  pl.BlockSpec((tm, k), lambda i, j: (i, 0))],
        out_specs=pl.BlockSpec((tn, tm), lambda i, j: (j, i)),
        out_shape=jax.ShapeDtypeStruct((n, s), out_dtype),
        compiler_params=pltpu.CompilerParams(dimension_semantics=("parallel", "parallel")),
        name=name,
    )(w, x)


def _transpose_cast_kernel(x_ref, o_ref):
    o_ref[0] = x_ref[...].T.astype(o_ref.dtype)


def _transpose_cast_blocks(x, out_dtype, tb):
    r, c = x.shape
    return pl.pallas_call(
        _transpose_cast_kernel,
        grid=(r // tb, c // tb),
        in_specs=[pl.BlockSpec((tb, tb), lambda i, j: (i, j))],
        out_specs=pl.BlockSpec((1, tb, tb), lambda i, j: (i, j, 0)),
        out_shape=jax.ShapeDtypeStruct((r // tb, c, tb), out_dtype),
        compiler_params=pltpu.CompilerParams(dimension_semantics=("parallel", "parallel")),
        name="transpose_cast",
    )(x)


def _head_sum(x, ones_bd):
    xb = x.astype(BF16)
    width = ones_bd.shape[0]
    tiles = [_dot(xb[:, c * width:(c + 1) * width], ones_bd)
             for c in range(x.shape[1] // width)]
    return jnp.concatenate(tiles, axis=1)


def _softplus(x):
    return jnp.maximum(x, 0.0) + jnp.log1p(jnp.exp(-jnp.abs(x)))


def _rwkv_kernel(p_ref, pprev_ref, mu_ref, w0_ref, wd_ref, a0_ref, wa_ref, wg_ref, kk_ref,
                 ka_ref, rk_ref, lng_ref, lnb_ref, o_ref, state_ref, y_ref):
    L = RWKV_CHUNK
    N = RWKV_HEAD_DIM
    W = RWKV_WIDTH
    G = RWKV_GROUP
    GL = RWKV_GROUP_LANES
    step = pl.program_id(0)

    @pl.when(step == 0)
    def _():
        state_ref[...] = jnp.zeros_like(state_ref)

    TB = p_ref.shape[0]
    row = lax.broadcasted_iota(jnp.int32, (TB, 1), 0)
    carry_on = jnp.where(step == 0, 0.0, 1.0)

    def shifted(c0, c1):
        p = p_ref[:, c0:c1]
        last = pprev_ref[SUBLANES - 1:SUBLANES, c0:c1] * carry_on
        prev = jnp.where(row == 0, last, pltpu.roll(p, 1, axis=0))
        return p + (prev - p) * mu_ref[:, c0:c1]

    r = shifted(0, W)
    k = shifted(W, 2 * W)
    v = shifted(2 * W, 3 * W)
    xw = shifted(3 * W, 3 * W + LORA_PAD)
    xa = shifted(3 * W + LORA_PAD, 3 * W + 2 * LORA_PAD)
    xg = shifted(3 * W + 2 * LORA_PAD, 3 * W + 2 * LORA_PAD + GATE_LORA)

    z = w0_ref[...] + _dot_bf16(jnp.tanh(xw), wd_ref[...])
    w_log = -_softplus(-z) - 0.5
    lw = -jnp.exp(w_log)
    a = jax.nn.sigmoid(a0_ref[...] + _dot_bf16(xa, wa_ref[...]))
    g = _dot_bf16(jax.nn.sigmoid(xg), wg_ref[...])

    bi = lax.broadcasted_iota(jnp.int32, (GL, GL), 0) // N
    bj = lax.broadcasted_iota(jnp.int32, (GL, GL), 1) // N
    bd_mask = bi == bj
    ones_bd = jnp.where(bd_mask, 1.0, 0.0).astype(BF16)

    kk = k * kk_ref[...]
    kk = kk / jnp.maximum(jnp.sqrt(_head_sum(kk * kk, ones_bd)), 1e-12)
    k = k * (1.0 + (a - 1.0) * ka_ref[...])

    ti = lax.broadcasted_iota(jnp.int32, (TB, TB), 0)
    tj = lax.broadcasted_iota(jnp.int32, (TB, TB), 1)
    tril = jnp.where((tj <= ti) & (tj // L == ti // L), 1.0, 0.0).astype(BF16)
    gi = lax.broadcasted_iota(jnp.int32, (L, GL), 0)
    gj = lax.broadcasted_iota(jnp.int32, (L, GL), 1) % L
    incl = gj <= gi
    strict = gj < gi
    eye4 = jnp.where(gj == gi, 1.0, 0.0).astype(F32)
    def bd(x):
        return jnp.where(bd_mask, jnp.concatenate([x] * G, axis=0), jnp.zeros((), x.dtype))

    lw_hi = lw.astype(BF16)
    lw_rest = lw - lw_hi.astype(F32)
    lw_mid = lw_rest.astype(BF16)
    lw_lo = (lw_rest - lw_mid.astype(F32)).astype(BF16)
    cum = _dot(tril, lw_hi) + (_dot(tril, lw_mid) + _dot(tril, lw_lo))
    e_inv = jnp.exp(-cum)
    r_t = (r * jnp.exp(cum)).astype(BF16)
    a_t = (-kk * jnp.exp(cum - lw)).astype(BF16)
    b = kk * a
    b_t = (b * e_inv).astype(BF16)
    k_t = (k * e_inv).astype(BF16)
    v_b = v.astype(BF16)
    eye4_b = eye4.astype(BF16)
    n_grp = RWKV_HEADS // G
    states = [state_ref[grp] for grp in range(n_grp)]

    n_chunks = TB // L
    pairs = [(c, grp) for c in range(n_chunks) for grp in range(n_grp)]
    rows_of = lambda c: slice(c * L, (c + 1) * L)
    lanes_of = lambda grp: slice(grp * GL, (grp + 1) * GL)
    cum_last = [cum[(c + 1) * L - 1:(c + 1) * L, :] for c in range(n_chunks)]

    a_ab, a_ak, a_rb, a_rk = {}, {}, {}, {}
    for c, grp in pairs:
        rs, gs = rows_of(c), lanes_of(grp)
        m = _dot_nt(jnp.concatenate([a_t[rs, gs], r_t[rs, gs]], axis=0),
                    jnp.concatenate([bd(b_t[rs, gs]), bd(k_t[rs, gs])], axis=0))
        a_ab[c, grp] = jnp.where(strict, m[:L, :GL], 0.0)
        a_ak[c, grp] = jnp.where(strict, m[:L, GL:], 0.0).astype(BF16)
        a_rb[c, grp] = jnp.where(incl, m[L:, :GL], 0.0).astype(BF16)
        a_rk[c, grp] = jnp.where(incl, m[L:, GL:], 0.0).astype(BF16)

    inv = {p: eye4 + a_ab[p] for p in pairs}
    pw = {}
    for p in pairs:
        pw_b = a_ab[p].astype(BF16)
        pw[p] = _dot(pw_b, bd(pw_b))
    for _ in range(4):
        for p in pairs:
            pw_b = pw[p].astype(BF16)
            both = _dot(jnp.concatenate([pw_b, inv[p].astype(BF16)], axis=0), bd(pw_b))
            pw[p], inv[p] = both[:L], inv[p] + both[L:]
    for p in pairs:
        inv[p] = (inv[p] + _dot(inv[p].astype(BF16), bd(pw[p].astype(BF16)))).astype(BF16)

    bk = {}
    for c in range(n_chunks):
        rs = rows_of(c)
        e_tail = jnp.exp(cum_last[c] - cum[rs])
        b_w = (b[rs] * e_tail).astype(BF16)
        k_w = (k[rs] * e_tail).astype(BF16)
        for grp in range(n_grp):
            gs = lanes_of(grp)
            bk[c, grp] = _dot_nt(
                eye4_b, jnp.concatenate([bd(b_w[:, gs]), bd(k_w[:, gs])], axis=0)).astype(BF16)

    for c in range(n_chunks):
        rs = rows_of(c)
        w_last = jnp.exp(cum_last[c])
        bd_z = [bd(states[grp].astype(BF16)) for grp in range(n_grp)]
        bd_v = [bd(v_b[rs, lanes_of(grp)]) for grp in range(n_grp)]
        x = [_dot(jnp.concatenate([a_t[rs, lanes_of(grp)], a_ak[c, grp]], axis=1),
                  jnp.concatenate([bd_z[grp], bd_v[grp]], axis=0)) for grp in range(n_grp)]
        u = [_dot(inv[c, grp], bd(x[grp].astype(BF16))) for grp in range(n_grp)]
        for grp in range(n_grp):
            gs = lanes_of(grp)
            lhs = jnp.concatenate(
                [jnp.concatenate([r_t[rs, gs], a_rb[c, grp], a_rk[c, grp]], axis=1),
                 jnp.concatenate([(eye4 * w_last[:, gs]).astype(BF16), bk[c, grp]], axis=1)],
                axis=0)
            yz = _dot(lhs, jnp.concatenate([bd_z[grp], bd(u[grp].astype(BF16)), bd_v[grp]],
                                           axis=0))
            y_ref[rs, gs] = yz[:L]
            states[grp] = yz[L:]
    for grp in range(n_grp):
        state_ref[grp] = states[grp]

    y = y_ref[...]
    mean = _head_sum(y, ones_bd) * (1.0 / N)
    yc = y - mean
    var = _head_sum(yc * yc, ones_bd) * (1.0 / N)
    yn = yc * lax.rsqrt(var + LN_X_EPS) * lng_ref[...] + lnb_ref[...]
    bonus = _head_sum(r * k * rk_ref[...], ones_bd) * v
    o_ref[...] = ((yn + bonus) * g).astype(o_ref.dtype)


def _rwkv_time_mix(p, mu, w0, wd, a0, wa, wg, k_k, k_a, r_k, lnx_g, lnx_b):
    s = p.shape[0]
    L = RWKV_CHUNK
    W = RWKV_WIDTH
    row = lambda v: v.reshape(1, -1)
    consts = [row(mu), row(w0), wd, row(a0), wa, wg, row(k_k), row(k_a), row(r_k), row(lnx_g),
              row(lnx_b)]
    tb = RWKV_BLOCK
    return pl.pallas_call(
        _rwkv_kernel,
        grid=(s // tb,),
        in_specs=[pl.BlockSpec((tb, RWKV_COLS_PAD), lambda i: (i, 0)),
                  pl.BlockSpec((SUBLANES, RWKV_COLS_PAD),
                               lambda i: (jnp.maximum(i * (tb // SUBLANES) - 1, 0), 0))]
                 + [_const_spec(c.shape) for c in consts],
        out_specs=pl.BlockSpec((tb, W), lambda i: (i, 0)),
        out_shape=jax.ShapeDtypeStruct((s, W), BF16),
        scratch_shapes=[pltpu.VMEM((RWKV_HEADS // RWKV_GROUP, RWKV_HEAD_DIM, RWKV_GROUP_LANES),
                                   F32),
                        pltpu.VMEM((tb, W), F32)],
        compiler_params=pltpu.CompilerParams(dimension_semantics=("arbitrary",)),
        name="rwkv7",
    )(p, p, *consts)


def _diffattn_kernel(q_ref, k_ref, vt_ref, lq1_ref, lk1_ref, lq2_ref, lk2_ref, g_ref, o_ref,
                     sa_ref, sb_ref, mx_ref, m_ref, acc_ref, *, lambda_init):
    TQ = ATTN_Q_BLOCK
    TK = ATTN_K_BLOCK
    DV = DIFF_V_DIM
    qi = pl.program_id(1)
    q = q_ref[...]
    lane = lax.broadcasted_iota(jnp.int32, q.shape, 1)
    zero = jnp.zeros_like(q)
    q_halves = (jnp.where(lane < DIFF_HEAD_DIM, q, zero), jnp.where(lane >= DIFF_HEAD_DIM, q, zero))
    ones = jnp.ones((BF16_SUBLANES, TK), BF16)
    key_i = lax.broadcasted_iota(jnp.int32, (TK, TQ), 0)
    qry_i = lax.broadcasted_iota(jnp.int32, (TK, TQ), 1)

    s_bufs = (sa_ref, sb_ref)

    def scores(j, buf, key_offset=None):
        kb = k_ref[pl.ds(pl.multiple_of(j * TK, TK), TK), :]
        for idx in range(2):
            s = _dot_nt(kb, q_halves[idx])
            if key_offset is not None:
                s = jnp.where(key_i + key_offset <= qry_i, s, NEG_INF)
            s_bufs[buf][idx] = s
            mx_ref[2 * buf + idx] = jnp.max(s, axis=0, keepdims=True)

    def probs(buf):
        out = []
        for idx in range(2):
            m_old = m_ref[idx]
            m_new = jnp.maximum(m_old, mx_ref[2 * buf + idx])
            m_ref[idx] = m_new
            out.append((jnp.exp2(s_bufs[buf][idx] - m_new).astype(BF16),
                        jnp.exp2(m_old - m_new)))
        return out

    def accumulate(j, pa):
        vt = vt_ref[:, pl.ds(pl.multiple_of(j * TK, TK), TK)]
        vext = jnp.concatenate([vt, ones], axis=0)
        for idx, (p, alpha) in enumerate(pa):
            acc_ref[idx] = alpha * acc_ref[idx] + _dot(vext, p)

    def step(j, buf, next_scores):
        pa = probs(buf)
        next_scores()
        accumulate(j, pa)

    m_ref[...] = jnp.full_like(m_ref, NEG_INF)
    acc_ref[...] = jnp.zeros_like(acc_ref)

    @pl.when(qi > 0)
    def _():
        scores(0, 0)

    @pl.loop(0, qi - 1)
    def _(i):
        step(2 * i, 0, lambda: scores(2 * i + 1, 1))
        step(2 * i + 1, 1, lambda: scores(2 * i + 2, 0))

    @pl.when(qi > 0)
    def _():
        step(2 * qi - 2, 0, lambda: scores(2 * qi - 1, 1))
        step(2 * qi - 1, 1, lambda: scores(2 * qi, 0, key_offset=0))

    @pl.when(qi == 0)
    def _():
        scores(0, 0, key_offset=0)

    step(2 * qi, 0, lambda: scores(2 * qi + 1, 1, key_offset=TK))
    step(2 * qi + 1, 1, lambda: None)

    lam = (jnp.exp(jnp.sum(lq1_ref[...] * lk1_ref[...], axis=-1, keepdims=True))
           - jnp.exp(jnp.sum(lq2_ref[...] * lk2_ref[...], axis=-1, keepdims=True))
           + lambda_init)
    o = (acc_ref[0, :DV, :] / acc_ref[0, DV:DV + 1, :]
         - lam * (acc_ref[1, :DV, :] / acc_ref[1, DV:DV + 1, :]))
    o = o * lax.rsqrt(jnp.mean(o * o, axis=0, keepdims=True) + SUBLN_EPS) * g_ref[...]
    o_ref[...] = (o * (1.0 - lambda_init)).T.astype(o_ref.dtype)


def _diff_attention(qk, vt, lq1, lk1, lq2, lk2, subln_g, lambda_init):
    s = qk.shape[0]
    H = DIFF_HEADS
    TQ = ATTN_Q_BLOCK
    TK = ATTN_K_BLOCK
    DV = DIFF_V_DIM
    assert TQ == 2 * TK and s % TQ == 0
    row = lambda v: v.reshape(1, -1)
    lam_specs = [_const_spec((1, DIFF_HEAD_DIM))] * 4
    return pl.pallas_call(
        functools.partial(_diffattn_kernel, lambda_init=lambda_init),
        grid=(H, s // TQ),
        in_specs=[pl.BlockSpec((TQ, DV), lambda h, qi: (qi, h)),
                  pl.BlockSpec((s, DV), lambda h, qi: (0, H + h)),
                  pl.BlockSpec((DV, s), lambda h, qi: (h, 0))]
                 + lam_specs + [_const_spec((DV, 1))],
        out_specs=pl.BlockSpec((TQ, DV), lambda h, qi: (qi, h)),
        out_shape=jax.ShapeDtypeStruct((s, DIFF_WIDTH), BF16),
        scratch_shapes=[pltpu.VMEM((2, TK, TQ), F32), pltpu.VMEM((2, TK, TQ), F32),
                        pltpu.VMEM((4, 1, TQ), F32), pltpu.VMEM((2, 1, TQ), F32),
                        pltpu.VMEM((2, DV + BF16_SUBLANES, TQ), F32)],
        compiler_params=pltpu.CompilerParams(dimension_semantics=("parallel", "arbitrary"),
                                             vmem_limit_bytes=VMEM_LIMIT_ATTENTION),
        name="diff_attention",
    )(qk, qk, vt, row(lq1), row(lk1), row(lq2), row(lk2), subln_g.reshape(DV, 1))


def _merge_kernel(ya_ref, yb_ref, gate_ref, x_ref, wa_ref, wb_ref, wo_ref, g2_ref,
                  h_ref, ht_ref):
    pa = _dot(ya_ref[...], wa_ref[...])
    pb = _dot(yb_ref[...], wb_ref[...])
    ga = jax.nn.sigmoid(gate_ref[:, :D_MODEL].astype(F32))
    gb = jax.nn.sigmoid(gate_ref[:, D_MODEL:].astype(F32))
    merged = ga * pa + gb * pb
    h = x_ref[...] + _dot(merged.astype(BF16), wo_ref[...])
    h_ref[...] = h
    hn = h * lax.rsqrt(jnp.mean(h * h, axis=-1, keepdims=True) + NORM_EPS) * g2_ref[...]
    ht_ref[...] = hn.T.astype(BF16)


def _merge(ya, yb, gate, x, wa, wb, wo, g2, tm):
    s = x.shape[0]
    D = D_MODEL
    rows = lambda w: pl.BlockSpec((tm, w), lambda i: (i, 0))
    single = lambda shape: pl.BlockSpec(shape, lambda i: (0, 0), pipeline_mode=pl.Buffered(1))
    return pl.pallas_call(
        _merge_kernel,
        grid=(s // tm,),
        in_specs=[rows(RWKV_WIDTH), rows(DIFF_WIDTH), rows(GATE_COLS), rows(D),
                  single((RWKV_WIDTH, D)), single((DIFF_WIDTH, D)), single((D, D)),
                  _const_spec((1, D))],
        out_specs=[rows(D), pl.BlockSpec((D, tm), lambda i: (0, i))],
        out_shape=[jax.ShapeDtypeStruct((s, D), F32), jax.ShapeDtypeStruct((D, s), BF16)],
        compiler_params=pltpu.CompilerParams(dimension_semantics=("parallel",),
                                             vmem_limit_bytes=VMEM_LIMIT_RESIDENT),
        name="merge_out_proj",
    )(ya, yb, gate, x, wa, wb, wo, g2.reshape(1, D))


def _cmp_exchange(xs, i, l, descending):
    hi = jnp.maximum(xs[i], xs[l])
    lo = jnp.minimum(xs[i], xs[l])
    xs[i], xs[l] = (hi, lo) if descending else (lo, hi)


def _bitonic_merge_desc(xs):
    xs = list(xs)
    n = len(xs)
    j = n // 2
    while j >= 1:
        for i in range(n):
            l = i ^ j
            if l > i:
                _cmp_exchange(xs, i, l, True)
        j //= 2
    return xs


def _bitonic_sort_desc(xs):
    xs = list(xs)
    n = len(xs)
    k = 2
    while k <= n:
        j = k // 2
        while j >= 1:
            for i in range(n):
                l = i ^ j
                if l > i:
                    _cmp_exchange(xs, i, l, (i & k) == 0)
            j //= 2
        k *= 2
    return xs


def _merge_top(a, b):
    n = len(a)
    return _bitonic_merge_desc([jnp.maximum(a[i], b[n - 1 - i]) for i in range(n)])


def _top16_over_rows(s):
    groups = [s[g * SUBLANES:(g + 1) * SUBLANES, :] for g in range(s.shape[0] // SUBLANES)]
    top = _bitonic_sort_desc(groups)
    for shift in (4, 2, 1):
        top = _merge_top(top, [pltpu.roll(x, shift, axis=0) for x in top])
    return top


def _prefix_count(rows, pred):
    def pick(conds, cands):
        if not conds:
            return cands[0]
        half = len(cands) // 2
        return jnp.where(conds[0], pick(conds[1:], cands[half:]), pick(conds[1:], cands[:half]))

    n = len(rows)
    conds = []
    count = None
    step = n // 2
    while step >= 1:
        cands = [rows[lo + step - 1] for lo in range(0, n, 2 * step)]
        c = pred(pick(conds, cands))
        inc = jnp.where(c, float(step), 0.0)
        count = inc if count is None else count + inc
        conds.append(c)
        step //= 2
    return jnp.where(pred(rows[n - 1]), float(n), count)


def _peer_score_kernel(qt_ref, keys_ref, r2_ref, e2_ref, n_ref, d_ref):
    K = PEER_TOPK
    T = qt_ref.shape[1]
    H = PEER_HEADS
    scores = []
    tops = []
    for hp in range(2 * H):
        s = _dot(keys_ref[hp], qt_ref[hp * PEER_HALF:(hp + 1) * PEER_HALF, :], HIGHEST)
        scores.append(s)
        tops.append(_top16_over_rows(s))
    sub = lax.broadcasted_iota(jnp.int32, (SUBLANES, T), 0)

    def by_head(p, i):
        out = tops[p][i]
        for h in range(1, H):
            out = jnp.where(sub == h, tops[2 * h + p][i], out)
        return out

    aa = [by_head(0, i) for i in range(K)]
    bb = [by_head(1, i) for i in range(K)]
    cands = [aa[i] + bb[j] for i in range(K) for j in range(K) if (i + 1) * (j + 1) <= K]
    cands += [jnp.full_like(cands[0], -jnp.inf)] * (-len(cands) % K)
    best = _bitonic_sort_desc(cands[:K])
    for c in range(K, len(cands), K):
        best = _merge_top(best, _bitonic_sort_desc(cands[c:c + K]))
    thr = best[K - 1]
    zsum = jnp.zeros_like(thr)
    for t in best:
        zsum = zsum + jnp.exp(t - best[0])
    inv_z = 1.0 / zsum
    for h in range(H):
        hs = slice(h, h + 1)
        s1, s2 = scores[2 * h], scores[2 * h + 1]
        thr_h = thr[hs]
        b_rows = [bb[m][hs] for m in range(K)]
        cnt = _prefix_count(b_rows, lambda b: s1 + b >= thr_h)
        rank = _prefix_count(b_rows, lambda b: b > s2)
        n_ref[h] = cnt
        r2_ref[h] = rank.astype(BF16)
        d_ref[h] = jnp.exp(s1 - aa[0][hs]) * inv_z[hs]
        e2_ref[h] = jnp.exp(s2 - bb[0][hs]).astype(BF16)


def _peer_scores(qt, keys, tt):
    nq, s = qt.shape
    H = PEER_HEADS
    out = lambda dtype: jax.ShapeDtypeStruct((H, N_KEYS, s), dtype)
    ospec = pl.BlockSpec((H, N_KEYS, tt), lambda i: (0, 0, i))
    return pl.pallas_call(
        _peer_score_kernel,
        grid=(s // tt,),
        in_specs=[pl.BlockSpec((nq, tt), lambda i: (0, i)),
                  _const_spec((2 * H, N_KEYS, PEER_HALF))],
        out_specs=[ospec] * 4,
        out_shape=[out(BF16), out(BF16), out(F32), out(F32)],
        compiler_params=pltpu.CompilerParams(dimension_semantics=("parallel",)),
        name="peer_scores",
    )(qt, keys)


def _peer_expert_kernel(hnt_ref, h1_ref, r2_ref, e2_ref, n_ref, d_ref, u_ref, vt_ref, fg_ref,
                        o_ref, acc_ref):
    e = pl.program_id(1)
    eb = u_ref.shape[0]
    tt = hnt_ref.shape[1]
    ni = eb // N_KEYS
    slab = BF16_SUBLANES
    strip = 2 * LANES

    groups = []
    for ii in range(ni):
        i = e * ni + ii
        rows = slice(ii * N_KEYS, (ii + 1) * N_KEYS)
        pre = _dot(u_ref[rows, :], hnt_ref[...])
        act = (0.5 * pre * (1.0 + lax.erf(pre * math.sqrt(0.5)))).astype(BF16)
        strips = []
        for t0 in range(0, tt, strip):
            ts = slice(t0, t0 + strip)
            nb = [jnp.broadcast_to(n_ref[h, pl.ds(i, 1), ts], (slab, strip)).astype(BF16)
                  for h in range(PEER_HEADS)]
            db = [jnp.broadcast_to(d_ref[h, pl.ds(i, 1), ts], (slab, strip)).astype(BF16)
                  for h in range(PEER_HEADS)]
            slabs = []
            for j0 in range(0, N_KEYS, slab):
                js = slice(j0, j0 + slab)
                gate = None
                for h in range(PEER_HEADS):
                    term = jnp.where(r2_ref[h, js, ts] < nb[h], e2_ref[h, js, ts],
                                     jnp.zeros((), BF16)) * db[h]
                    gate = term if gate is None else gate + term
                slabs.append(gate * act[js, ts])
            strips.append(jnp.concatenate(slabs, axis=0))
        groups.append(jnp.concatenate(strips, axis=1))
    prev = jnp.where(e == 0, 0.0, acc_ref[...])
    acc_ref[...] = prev + _dot(vt_ref[0], jnp.concatenate(groups, axis=0))

    @pl.when(e == pl.num_programs(1) - 1)
    def _():
        h = h1_ref[...] + acc_ref[...].T
        o_ref[...] = h * lax.rsqrt(jnp.mean(h * h, axis=-1, keepdims=True) + NORM_EPS) * fg_ref[...]


def _peer_experts(hnt, h1, r2, e2, n, d, u, vt, final_g, tt, eb):
    s = hnt.shape[1]
    D = D_MODEL
    H = PEER_HEADS
    sel = pl.BlockSpec((H, N_KEYS, tt), lambda i, e: (0, 0, i), pipeline_mode=pl.Buffered(1))
    sel2 = pl.BlockSpec((H, N_KEYS, tt), lambda i, e: (0, 0, i))
    return pl.pallas_call(
        _peer_expert_kernel,
        grid=(s // tt, N_EXPERTS // eb),
        in_specs=[pl.BlockSpec((D, tt), lambda i, e: (0, i)),
                  pl.BlockSpec((tt, D), lambda i, e: (i, 0)),
                  sel2, sel2, sel, sel,
                  pl.BlockSpec((eb, D), lambda i, e: (e, 0)),
                  pl.BlockSpec((1, D, eb), lambda i, e: (e, 0, 0)),
                  pl.BlockSpec((1, D), lambda i, e: (0, 0))],
        out_specs=pl.BlockSpec((tt, D), lambda i, e: (i, 0)),
        out_shape=jax.ShapeDtypeStruct((s, D), F32),
        scratch_shapes=[pltpu.VMEM((D, tt), F32)],
        compiler_params=pltpu.CompilerParams(dimension_semantics=("parallel", "arbitrary"),
                                             vmem_limit_bytes=VMEM_LIMIT_RESIDENT),
        name="peer_experts",
    )(hnt, h1, r2, e2, n, d, u, vt, final_g.reshape(1, D))


def _pad_rows(w, rows):
    return jnp.pad(w, ((0, rows - w.shape[0]), (0, 0)))


def _layer(h, norm1_g, w_in, shift_mu, rwkv_w0, w_decay_up, rwkv_a0, w_iclr_up, w_gate_up,
           k_k, k_a, r_k, lnx_g, lnx_b, lam_q1, lam_k1, lam_q2, lam_k2, subln_g, w_proj_a,
           w_proj_b, w_out, norm2_g, peer_wq, peer_sub_keys, peer_u, peer_v, out_g, lambda_init):
    s = h.shape[0]
    W = RWKV_WIDTH
    tmm = min(s, MM_ROW_BLOCK)
    tmb = min(s, MM_ROW_BLOCK_BF16)
    tn = MM_COL_BLOCK

    c0, c1, c2 = 3 * W, 3 * W + DECAY_LORA, 3 * W + DECAY_LORA + ICLR_LORA
    pad_cols = lambda m, n: jnp.pad(m, ((0, 0), (0, n - m.shape[1])))
    w_rwkv = jnp.concatenate([w_in[:, :c0], pad_cols(w_in[:, c0:c1], LORA_PAD),
                              pad_cols(w_in[:, c1:c2], LORA_PAD), w_in[:, c2:RWKV_COLS]], axis=1)
    mu2 = shift_mu.reshape(1, -1)
    mu = jnp.concatenate([mu2[:, :c0], pad_cols(mu2[:, c0:c1], LORA_PAD),
                          pad_cols(mu2[:, c1:c2], LORA_PAD), mu2[:, c2:]], axis=1)
    d0 = RWKV_COLS
    q_scale = DIFF_HEAD_DIM ** -0.5 * math.log2(math.e)
    w_qk = jnp.concatenate([w_in[:, d0:d0 + DIFF_WIDTH] * q_scale,
                            w_in[:, d0 + DIFF_WIDTH:d0 + 2 * DIFF_WIDTH]], axis=1)
    w_vt = w_in[:, d0 + 2 * DIFF_WIDTH:d0 + DIFF_COLS].T
    w_gate = w_in[:, d0 + DIFF_COLS:]

    p_rwkv, xn = _norm_matmul(h, norm1_g, w_rwkv.astype(BF16), NORM_EPS, F32, tmm,
                              RWKV_COL_BLOCK, "in_proj_rwkv")
    p_qk = _matmul(xn, w_qk.astype(BF16), BF16, tmb, tn, "in_proj_qk")
    p_vt = _matmul_nt(w_vt.astype(BF16), xn, BF16, VT_ROW_BLOCK, tmb, "in_proj_vt")
    p_gate = _matmul(xn, w_gate.astype(BF16), BF16, tmb, tn, "in_proj_gate")

    y_a = _rwkv_time_mix(p_rwkv, mu, rwkv_w0, _pad_rows(w_decay_up, LORA_PAD), rwkv_a0,
                         _pad_rows(w_iclr_up, LORA_PAD), w_gate_up, k_k, k_a, r_k, lnx_g, lnx_b)
    y_b = _diff_attention(p_qk, p_vt, lam_q1, lam_k1, lam_q2, lam_k2, subln_g, lambda_init)
    h1, hn_t = _merge(y_a, y_b, p_gate, h, w_proj_a.astype(BF16), w_proj_b.astype(BF16),
                      w_out.astype(BF16), norm2_g, min(s, MERGE_ROW_BLOCK))

    keys = peer_sub_keys.reshape(2 * PEER_HEADS, N_KEYS, PEER_HALF)
    q_t = _matmul(peer_wq.T.astype(BF16), hn_t, F32, MM_ROW_BLOCK_BF16, tn, "peer_query")
    r2, e2, n, d = _peer_scores(q_t, keys, min(s, PEER_SCORE_BLOCK))
    eb = PEER_EXPERT_BLOCK
    return _peer_experts(hn_t, h1, r2, e2, n, d, peer_u.astype(BF16),
                         _transpose_cast_blocks(peer_v, BF16, eb), out_g,
                         min(s, PEER_TOKEN_BLOCK), eb)


def kernel(x, norm1_g, w_in, shift_mu, rwkv_w0, w_decay_up, rwkv_a0, w_iclr_up, w_gate_up, k_k, k_a, r_k, lnx_g, lnx_b, lam_q1, lam_k1, lam_q2, lam_k2, subln_g, w_proj_a, w_proj_b, w_out, norm2_g, peer_wq, peer_sub_keys, peer_u, peer_v, final_g):
    B, S, D = x.shape
    assert B == 1 and D == D_MODEL and norm1_g.shape[0] == 1
    lambda_init = 0.8 - 0.6 * math.exp(-0.3 * 0)
    out = _layer(x[0], norm1_g[0], w_in[0], shift_mu[0], rwkv_w0[0], w_decay_up[0], rwkv_a0[0],
                 w_iclr_up[0], w_gate_up[0], k_k[0], k_a[0], r_k[0].reshape(-1), lnx_g[0],
                 lnx_b[0], lam_q1[0], lam_k1[0], lam_q2[0], lam_k2[0], subln_g[0], w_proj_a[0],
                 w_proj_b[0], w_out[0], norm2_g[0], peer_wq[0], peer_sub_keys[0], peer_u[0],
                 peer_v[0], final_g, lambda_init)
    return out[None]
```
